```python
import math
import jax
import jax.numpy as jnp
from jax import lax
import numpy as np

D_MODEL = 2048
BATCH = 16
SEQ = 256
DEPTH = 4
DEC_BATCH = 2
DEC_SEQ = 4096
PAST_LEN = 256

GRID_W = 64
N_BRANCH = 4
BRANCH_W = 512
H_A = 4
D_A = 64
H_B = 8
D_B = 64
NA_ROWS = 8
NA_COLS = 16
H_C = 4
Q_RANK = 512
KV_RANK = 256
NOPE_DIM = 128
ROPE_DIM = 64
V_DIM_C = 128
H_D = 8
G_D = 2
D_D = 64
ROPE_BASE = 10000.0
Q_BLOCK = 128
EPS = 1e-6
ALPHA = (2 * DEPTH) ** 0.25
BETA = (8 * DEPTH) ** -0.25
IN_SIZES = (H_A * 2 * D_A, H_A * 2 * D_A, H_A * 2 * D_A,
            H_B * D_B, H_B * D_B, H_B * D_B,
            Q_RANK, KV_RANK, ROPE_DIM,
            H_D * D_D, G_D * D_D, G_D * D_D,
            N_BRANCH * BRANCH_W, N_BRANCH * D_MODEL)
IN_COLS = sum(IN_SIZES)

kernel_name = 'hybrid_diffusion_gated_quad_attention_step'


def rmsnorm(x, g):
    x32 = x.astype(jnp.float32)
    y = x32 * lax.rsqrt(jnp.mean(x32 * x32, axis=-1, keepdims=True) + EPS)
    return (y * g.astype(jnp.float32)).astype(x.dtype)


def layernorm(x, g, b):
    x32 = x.astype(jnp.float32)
    mu = jnp.mean(x32, axis=-1, keepdims=True)
    var = jnp.mean(jnp.square(x32 - mu), axis=-1, keepdims=True)
    y = (x32 - mu) * lax.rsqrt(var + EPS) * g.astype(jnp.float32) + b.astype(jnp.float32)
    return y.astype(x.dtype)


def softmax32(s):
    return jax.nn.softmax(s.astype(jnp.float32), axis=-1)


def split_columns(y, sizes):
    offs = [0]
    for s in sizes:
        offs.append(offs[-1] + s)
    return [y[..., offs[i]:offs[i + 1]] for i in range(len(sizes))]


def sweep_query_blocks(fn, *qs):
    B, T = qs[0].shape[:2]
    nb = T // Q_BLOCK
    blocks = tuple(jnp.moveaxis(a.reshape((B, nb, Q_BLOCK) + a.shape[2:]), 1, 0) for a in qs)
    out = lax.map(lambda args: fn(*args), blocks)
    return jnp.moveaxis(out, 0, 1).reshape((B, T) + out.shape[3:])


def axial_rope_tables(n_tokens, dim, dtype):
    t = jnp.arange(n_tokens)
    row = (t // GRID_W).astype(jnp.float32)
    col = (t % GRID_W).astype(jnp.float32)
    quarter = dim // 4
    inv_freq = ROPE_BASE ** (-jnp.arange(quarter, dtype=jnp.float32) / quarter)
    ar = row[:, None] * inv_freq
    ac = col[:, None] * inv_freq
    ang = jnp.concatenate([ar, ar, ac, ac], axis=-1)
    return jnp.cos(ang).astype(dtype), jnp.sin(ang).astype(dtype)


def apply_axial_rope(x, cos, sin):
    shape = (cos.shape[0],) + (1,) * (x.ndim - 3) + (cos.shape[1],)
    x1, x2, x3, x4 = jnp.split(x, 4, axis=-1)
    rot = jnp.concatenate([-x2, x1, -x4, x3], axis=-1)
    return x * cos.reshape(shape) + rot * sin.reshape(shape)


def adaln(cond, w, b):
    mod = jax.nn.silu(cond) @ w + b
    shift, scale, gate = jnp.split(mod, 3, axis=-1)
    return shift[..., None, :], scale[..., None, :], gate[..., None, :]


def diff_lambda(lam_a, lam_init):
    l32 = lam_a.astype(jnp.float32)
    return jnp.exp(jnp.sum(l32[0] * l32[1])) - jnp.exp(jnp.sum(l32[2] * l32[3])) + lam_init


def diff_attention(q, k, v, lam, subln_g, lam_init):
    B, T = q.shape[:2]
    scale = D_A ** -0.5

    def block(qb):
        s = jnp.einsum('bqhcd,bkhcd->bchqk', qb, k) * scale
        p = softmax32(s)
        pm = p[:, 0] - lam * p[:, 1]
        return jnp.einsum('bhqk,bkhe->bqhe', pm.astype(v.dtype), v)

    o = sweep_query_blocks(block, q)
    o = rmsnorm(o, subln_g) * (1.0 - lam_init)
    return o.reshape(B, T, H_A * 2 * D_A)


def grouped_attention(q, k, v):
    B, T, H, d = q.shape
    G = k.shape[2]
    qg = q.reshape(B, T, G, H // G, d)
    scale = d ** -0.5

    def block(qb):
        s = jnp.einsum('bqgrd,bkgd->bgrqk', qb, k) * scale
        p = softmax32(s).astype(v.dtype)
        return jnp.einsum('bgrqk,bkgd->bqgrd', p, v)

    return sweep_query_blocks(block, qg).reshape(B, T, H * d)


def mla_expand(c_kv, w_ukv):
    B, S, _ = c_kv.shape
    kv = (c_kv @ w_ukv).reshape(B, S, H_C, NOPE_DIM + V_DIM_C)
    return kv[..., :NOPE_DIM], kv[..., NOPE_DIM:]


def mla_attention(qn, qr, kn, kr, v):
    B, T = qn.shape[:2]
    scale = (NOPE_DIM + ROPE_DIM) ** -0.5

    def block(qn_b, qr_b):
        s = (jnp.einsum('bqhd,bkhd->bhqk', qn_b, kn) + jnp.einsum('bqhd,bkd->bhqk', qr_b, kr)) * scale
        p = softmax32(s).astype(v.dtype)
        return jnp.einsum('bhqk,bkhd->bqhd', p, v)

    return sweep_query_blocks(block, qn, qr).reshape(B, T, H_C * V_DIM_C)


def neighbourhood_attention(q, k, v, k_ctx, v_ctx, rpb, rows):
    B, T, H, d = q.shape
    kr = min(NA_ROWS, rows)
    n_nb = kr * NA_COLS
    scale = d ** -0.5
    qg = q.reshape(B, rows, GRID_W, H, d)
    kg = k.reshape(B, rows, GRID_W, H, d)
    vg = v.reshape(B, rows, GRID_W, H, d)
    cols = jnp.arange(GRID_W)
    col_idx = jnp.clip(cols - NA_COLS // 2, 0, GRID_W - NA_COLS)[:, None] + jnp.arange(NA_COLS)[None, :]
    col_bias_idx = col_idx - cols[:, None] + (NA_COLS - 1)

    def row_block(r):
        r0 = jnp.clip(r - kr // 2, 0, rows - kr)
        row_bias_idx = r0 + jnp.arange(kr) - r + (NA_ROWS - 1)
        bias = rpb[:, row_bias_idx][:, :, col_bias_idx]
        bias = jnp.transpose(bias, (0, 2, 1, 3)).reshape(H, GRID_W, n_nb)
        k_rows = lax.dynamic_slice_in_dim(kg, r0, kr, axis=1)[:, :, col_idx]
        v_rows = lax.dynamic_slice_in_dim(vg, r0, kr, axis=1)[:, :, col_idx]
        k_nb = jnp.transpose(k_rows, (0, 2, 1, 3, 4, 5)).reshape(B, GRID_W, n_nb, H, d)
        v_nb = jnp.transpose(v_rows, (0, 2, 1, 3, 4, 5)).reshape(B, GRID_W, n_nb, H, d)
        q_r = lax.dynamic_index_in_dim(qg, r, axis=1, keepdims=False)
        s_nb = jnp.einsum('bwhd,bwnhd->bhwn', q_r, k_nb) * scale + bias
        s_ctx = jnp.einsum('bwhd,bphd->bhwp', q_r, k_ctx) * scale
        p = softmax32(jnp.concatenate([s_nb, s_ctx], axis=-1)).astype(v.dtype)
        return (jnp.einsum('bhwn,bwnhd->bwhd', p[..., :n_nb], v_nb)
                + jnp.einsum('bhwp,bphd->bwhd', p[..., n_nb:], v_ctx))

    out = lax.map(row_block, jnp.arange(rows))
    return jnp.moveaxis(out, 0, 1).reshape(B, T, H * d)


def project_inputs(h, P):
    B, T, _ = h.shape
    aq, ak, av, bq, bk, bv, cq, ckv, ckr, dq, dk, dv, z, g = split_columns(h @ P['w_in'], IN_SIZES)
    q_c = (rmsnorm(cq, P['c_q_norm']) @ P['w_c_uq']).reshape(B, T, H_C, NOPE_DIM + ROPE_DIM)
    return {
        'qa': aq.reshape(B, T, H_A, 2, D_A), 'ka': ak.reshape(B, T, H_A, 2, D_A), 'va': av.reshape(B, T, H_A, 2 * D_A),
        'qb': bq.reshape(B, T, H_B, D_B), 'kb': bk.reshape(B, T, H_B, D_B), 'vb': bv.reshape(B, T, H_B, D_B),
        'qn': q_c[..., :NOPE_DIM], 'qr': q_c[..., NOPE_DIM:],
        'ckv': rmsnorm(ckv, P['c_kv_norm']), 'kr': ckr,
        'qd': rmsnorm(dq.reshape(B, T, H_D, D_D), P['d_q_norm']),
        'kd': rmsnorm(dk.reshape(B, T, G_D, D_D), P['d_k_norm']),
        'vd': dv.reshape(B, T, G_D, D_D),
        'z': z, 'g': g,
    }


def merge_branches(outs, z, g, w_br, w_out):
    zs = jnp.split(z, N_BRANCH, axis=-1)
    gs = jnp.split(g, N_BRANCH, axis=-1)
    merged = None
    for i in range(N_BRANCH):
        term = jax.nn.sigmoid(gs[i]) * ((outs[i] * jax.nn.silu(zs[i])) @ w_br[i])
        merged = term if merged is None else merged + term
    return merged @ w_out


def context_layer(x, cond, P, lam_init):
    B, S, _ = x.shape
    shift, scale, gate = adaln(cond, P['w_ada'], P['b_ada'])
    h = x * (1.0 + scale) + shift
    m = project_inputs(h, P)
    lam = diff_lambda(P['lam_a'], lam_init)
    o_a = diff_attention(m['qa'], m['ka'], m['va'], lam, P['a_subln'], lam_init)
    o_b = grouped_attention(m['qb'], m['kb'], m['vb'])
    kn, vc = mla_expand(m['ckv'], P['w_c_ukv'])
    o_c = mla_attention(m['qn'], m['qr'], kn, m['kr'], vc)
    o_d = grouped_attention(m['qd'], m['kd'], m['vd'])
    y = merge_branches([o_a, o_b, o_c, o_d], m['z'], m['g'], P['w_br'], P['w_out'])
    x_new = layernorm(ALPHA * x + gate * y, P['ln_g'], P['ln_b'])
    ctx_tensors = (m['ka'].reshape(B, S, H_A, 2 * D_A), m['va'], m['kb'], m['vb'],
                   m['ckv'], m['kr'], m['kd'], m['vd'])
    return x_new, ctx_tensors


def latent_layer(x, cond, cached, P, lam_init, cos, sin, rows):
    B, T, _ = x.shape
    ck_a, cv_a, ck_b, cv_b, c_kv, c_kr, ck_d, cv_d = cached
    n_ctx = ck_a.shape[1]
    shift, scale, gate = adaln(cond, P['w_ada'], P['b_ada'])
    h = x * (1.0 + scale) + shift
    m = project_inputs(h, P)
    lam = diff_lambda(P['lam_a'], lam_init)
    qa = apply_axial_rope(m['qa'], cos, sin)
    ka = jnp.concatenate([apply_axial_rope(m['ka'], cos, sin), ck_a.reshape(B, n_ctx, H_A, 2, D_A)], axis=1)
    va = jnp.concatenate([m['va'], cv_a], axis=1)
    o_a = diff_attention(qa, ka, va, lam, P['a_subln'], lam_init)
    o_b = neighbourhood_attention(m['qb'], m['kb'], m['vb'], ck_b, cv_b, P['b_rpb'], rows)
    kn, vc = mla_expand(jnp.concatenate([m['ckv'], c_kv], axis=1), P['w_c_ukv'])
    kr = jnp.concatenate([apply_axial_rope(m['kr'], cos, sin), c_kr], axis=1)
    o_c = mla_attention(m['qn'], apply_axial_rope(m['qr'], cos, sin), kn, kr, vc)
    kd = jnp.concatenate([apply_axial_rope(m['kd'], cos, sin), ck_d], axis=1)
    vd = jnp.concatenate([m['vd'], cv_d], axis=1)
    o_d = grouped_attention(apply_axial_rope(m['qd'], cos, sin), kd, vd)
    y = merge_branches([o_a, o_b, o_c, o_d], m['z'], m['g'], P['w_br'], P['w_out'])
    return layernorm(ALPHA * x + gate * y, P['ln_g'], P['ln_b'])


def setup_inputs(seed: int = 0) -> dict:
    key = jax.random.key(seed)
    ks = jax.random.split(key, 32)

    def nrm(i, shape, s=1.0):
        return jax.random.normal(ks[i], shape, jnp.float32) * s

    cb = (DEC_BATCH, DEPTH, PAST_LEN)
    return {
        'x_prompt': nrm(0, (BATCH, SEQ, D_MODEL)),
        'x_sample': nrm(1, (DEC_BATCH, DEC_SEQ, D_MODEL)),
        'cache_a_k': nrm(2, cb + (H_A, 2 * D_A)),
        'cache_a_v': nrm(3, cb + (H_A, 2 * D_A)),
        'cache_b_k': nrm(4, cb + (H_B, D_B)),
        'cache_b_v': nrm(5, cb + (H_B, D_B)),
        'cache_c_kv': nrm(6, cb + (KV_RANK,)),
        'cache_c_kr': nrm(7, cb + (ROPE_DIM,)),
        'cache_d_k': nrm(8, cb + (G_D, D_D)),
        'cache_d_v': nrm(9, cb + (G_D, D_D)),
        'c': nrm(10, (DEC_BATCH, D_MODEL)),
        'c_ctx': nrm(11, (D_MODEL,)),
        'w_ada': nrm(12, (DEPTH, D_MODEL, 3 * D_MODEL), 0.5 * D_MODEL ** -0.5),
        'b_ada': nrm(13, (DEPTH, 3 * D_MODEL), 0.01),
        'w_in': nrm(14, (DEPTH, D_MODEL, IN_COLS), D_MODEL ** -0.5),
        'lam_a': nrm(15, (DEPTH, 4, D_A), 0.1),
        'a_subln': 1.0 + nrm(16, (DEPTH, 2 * D_A), 0.02),
        'b_rpb': nrm(17, (DEPTH, H_B, 2 * NA_ROWS - 1, 2 * NA_COLS - 1), 0.1),
        'c_q_norm': 1.0 + nrm(18, (DEPTH, Q_RANK), 0.02),
        'c_kv_norm': 1.0 + nrm(19, (DEPTH, KV_RANK), 0.02),
        'w_c_uq': nrm(20, (DEPTH, Q_RANK, H_C * (NOPE_DIM + ROPE_DIM)), Q_RANK ** -0.5),
        'w_c_ukv': nrm(21, (DEPTH, KV_RANK, H_C * (NOPE_DIM + V_DIM_C)), KV_RANK ** -0.5),
        'd_q_norm': 1.0 + nrm(22, (DEPTH, D_D), 0.02),
        'd_k_norm': 1.0 + nrm(23, (DEPTH, D_D), 0.02),
        'w_br': nrm(24, (DEPTH, N_BRANCH, BRANCH_W, D_MODEL), BETA * BRANCH_W ** -0.5),
        'w_out': nrm(25, (DEPTH, D_MODEL, D_MODEL), BETA * D_MODEL ** -0.5),
        'ln_g': 1.0 + nrm(26, (DEPTH, D_MODEL), 0.02),
        'ln_b': nrm(27, (DEPTH, D_MODEL), 0.02),
    }


def reference(x_prompt, x_sample, cache_a_k, cache_a_v, cache_b_k, cache_b_v, cache_c_kv, cache_c_kr,
              cache_d_k, cache_d_v, c, c_ctx, w_ada, b_ada, w_in, lam_a, a_subln, b_rpb, c_q_norm, c_kv_norm,
              w_c_uq, w_c_ukv, d_q_norm, d_k_norm, w_br, w_out, ln_g, ln_b):
    n_lat = x_sample.shape[1]
    rows = n_lat // GRID_W
    cos, sin = axial_rope_tables(n_lat, D_A, x_sample.dtype)
    xp = x_prompt
    xs = x_sample
    collected = [[] for _ in range(8)]
    for l in range(DEPTH):
        P = {'w_ada': w_ada[l], 'b_ada': b_ada[l], 'w_in': w_in[l], 'lam_a': lam_a[l], 'a_subln': a_subln[l],
             'b_rpb': b_rpb[l], 'c_q_norm': c_q_norm[l], 'c_kv_norm': c_kv_norm[l], 'w_c_uq': w_c_uq[l],
             'w_c_ukv': w_c_ukv[l], 'd_q_norm': d_q_norm[l], 'd_k_norm': d_k_norm[l], 'w_br': w_br[l],
             'w_out': w_out[l], 'ln_g': ln_g[l], 'ln_b': ln_b[l]}
        lam_init = 0.8 - 0.6 * math.exp(-0.3 * l)
        xp, ctx_tensors = context_layer(xp, c_ctx, P, lam_init)
        for store, t in zip(collected, ctx_tensors):
            store.append(t)
        cached = (cache_a_k[:, l], cache_a_v[:, l], cache_b_k[:, l], cache_b_v[:, l],
                  cache_c_kv[:, l], cache_c_kr[:, l], cache_d_k[:, l], cache_d_v[:, l])
        xs = latent_layer(xs, c, cached, P, lam_init, cos, sin, rows)
    new_a_k = jnp.stack(collected[0], axis=1)
    new_a_v = jnp.stack(collected[1], axis=1)
    new_b_k = jnp.stack(collected[2], axis=1)
    new_b_v = jnp.stack(collected[3], axis=1)
    new_c_kv = jnp.stack(collected[4], axis=1)
    new_c_kr = jnp.stack(collected[5], axis=1)
    new_d_k = jnp.stack(collected[6], axis=1)
    new_d_v = jnp.stack(collected[7], axis=1)
    return (xp, xs, new_a_k, new_a_v, new_b_k, new_b_v, new_c_kv, new_c_kr, new_d_k, new_d_v)
```

```python
import functools
import math

import jax
import jax.numpy as jnp
import numpy as np
from jax import lax
from jax.experimental import pallas as pl
from jax.experimental.pallas import tpu as pltpu

D_MODEL = 2048
BATCH = 16
SEQ = 256
DEPTH = 4
DEC_BATCH = 2
DEC_SEQ = 4096
PAST_LEN = 256
GRID_W = 64
ROWS = DEC_SEQ // GRID_W
N_BRANCH = 4
BRANCH_W = 512
H_A, D_A = 4, 64
H_B, D_B = 8, 64
NA_ROWS, NA_COLS = 8, 16
H_C, Q_RANK, KV_RANK, NOPE_DIM, ROPE_DIM, V_DIM_C = 4, 512, 256, 128, 64, 128
H_D, G_D, D_D = 8, 2, 64
ROPE_BASE = 10000.0
EPS = 1e-6
ALPHA = (2 * DEPTH) ** 0.25
IN_SIZES = (512, 512, 512, 512, 512, 512, Q_RANK, KV_RANK, ROPE_DIM, 512, 128, 128,
            N_BRANCH * BRANCH_W, N_BRANCH * D_MODEL)

BF = jnp.bfloat16
F32 = jnp.float32
LANES = 128
VMEM_LIMIT = 56 * 1024 * 1024
NEG = -1e30

N_CTX = BATCH * SEQ
N_LAT = DEC_BATCH * DEC_SEQ
NQ = 4736
NZG = 10240
GQA_PERM = (0, 4, 1, 5, 2, 6, 3, 7)

O_AQ, O_AK, O_AV, O_BQ, O_BK, O_BV = 0, 512, 1024, 1536, 2048, 2560
O_CQ, O_CKV, O_DQ, O_DK, O_DV, O_KR = 3072, 3584, 3840, 4352, 4480, 4608

TM_PROJ = 256
TM_ZG = 512
TN_ZG = 1024
TM_MERGE = 256
TQ = 256
NB_ROWS = 4
NB_KROWS = 12


def _cparams(sem):
    return pltpu.CompilerParams(dimension_semantics=sem, vmem_limit_bytes=VMEM_LIMIT)


def _dot(a, b):
    return jnp.dot(a, b, preferred_element_type=F32)


def _dot_nt(a, b):
    return lax.dot_general(a, b, (((1,), (1,)), ((), ())), preferred_element_type=F32)


def _sigmoid(x):
    return 1.0 / (1.0 + jnp.exp(-x))


def _lane_lt64(shape):
    return lax.broadcasted_iota(jnp.int32, shape, len(shape) - 1) < 64


def _adaln_body(c_ref, w_ref, b_ref, o_ref):
    c = c_ref[...]
    s = (c * _sigmoid(c)).astype(BF)
    o_ref[...] = _dot(s, w_ref[...].astype(BF)) + b_ref[...]


def _adaln(cond8, w_ada, b_ada):
    tn = 1536
    return pl.pallas_call(
        _adaln_body,
        grid=(DEPTH, 3 * D_MODEL // tn),
        in_specs=[pl.BlockSpec((8, D_MODEL), lambda l, j: (0, 0)),
                  pl.BlockSpec((None, D_MODEL, tn), lambda l, j: (l, 0, j)),
                  pl.BlockSpec((None, 1, tn), lambda l, j: (l, 0, j))],
        out_specs=pl.BlockSpec((None, 8, tn), lambda l, j: (l, 0, j)),
        out_shape=jax.ShapeDtypeStruct((DEPTH, 8, 3 * D_MODEL), F32),
        compiler_params=_cparams(("arbitrary", "arbitrary")),
        name="adaln",
    )(cond8, w_ada, b_ada.reshape(DEPTH, 1, 3 * D_MODEL))


def _bias_table_body(rpb_ref, o_ref):
    lh = pl.program_id(0)
    qc = lax.broadcasted_iota(jnp.int32, (GRID_W, LANES), 0)
    lane = lax.broadcasted_iota(jnp.int32, (GRID_W, LANES), 1)
    kc = jnp.bitwise_and(lane, 63)
    c0 = jnp.clip(qc - NA_COLS // 2, 0, GRID_W - NA_COLS)
    col_ok = (kc >= c0) & (kc < c0 + NA_COLS)
    dc = kc - qc + (NA_COLS - 1)
    right = lane >= 64
    n_dr, n_dc = 2 * NA_ROWS - 1, 2 * NA_COLS - 1
    for u in range(n_dr + 1):
        val = jnp.full((GRID_W, LANES), NEG, F32)
        for half, sel in ((0, ~right), (1, right)):
            dr = u - 1 + half
            if 0 <= dr < n_dr:
                for d in range(n_dc):
                    r = rpb_ref[(lh * n_dr + dr) * n_dc + d]
                    val = jnp.where(sel & col_ok & (dc == d), r, val)
        o_ref[u] = val


def _bias_table(b_rpb):
    n = DEPTH * H_B
    return pl.pallas_call(
        _bias_table_body,
        grid_spec=pltpu.PrefetchScalarGridSpec(
            num_scalar_prefetch=1, grid=(n,),
            in_specs=[],
            out_specs=pl.BlockSpec((None, 2 * NA_ROWS, GRID_W, LANES), lambda i, r: (i, 0, 0, 0))),
        out_shape=jax.ShapeDtypeStruct((n, 2 * NA_ROWS, GRID_W, LANES), F32),
        compiler_params=_cparams(("arbitrary",)),
        name="nb_bias_table",
    )(b_rpb.reshape(-1))


def _cache_mla_body(ckv_ref, kr_ref, wn_ref, wv_ref, k_ref, v_ref):
    ckv = ckv_ref[...].astype(BF)
    kn = _dot(ckv, wn_ref[...])
    kr = kr_ref[...]
    k_ref[...] = jnp.concatenate(
        [t for h in range(H_C) for t in (kn[:, h * LANES:(h + 1) * LANES], kr)], axis=1).astype(BF)
    v_ref[...] = _dot(ckv, wv_ref[...]).astype(BF)


def _cache_mla(cache_c_kv, cache_c_kr, wukv_n, wukv_v):
    return pl.pallas_call(
        _cache_mla_body,
        grid=(DEPTH, DEC_BATCH),
        in_specs=[pl.BlockSpec((None, None, PAST_LEN, KV_RANK), lambda l, b: (b, l, 0, 0)),
                  pl.BlockSpec((None, None, PAST_LEN, LANES), lambda l, b: (b, l, 0, 0)),
                  pl.BlockSpec((None, KV_RANK, 512), lambda l, b: (l, 0, 0)),
                  pl.BlockSpec((None, KV_RANK, 512), lambda l, b: (l, 0, 0))],
        out_specs=[pl.BlockSpec((None, None, PAST_LEN, 1024), lambda l, b: (l, b, 0, 0)),
                   pl.BlockSpec((None, None, PAST_LEN, 512), lambda l, b: (l, b, 0, 0))],
        out_shape=[jax.ShapeDtypeStruct((DEPTH, DEC_BATCH, PAST_LEN, 1024), BF),
                   jax.ShapeDtypeStruct((DEPTH, DEC_BATCH, PAST_LEN, 512), BF)],
        compiler_params=_cparams(("arbitrary", "arbitrary")),
        name="cache_mla",
    )(cache_c_kv, jnp.pad(cache_c_kr, ((0, 0), (0, 0), (0, 0), (0, LANES - ROPE_DIM))), wukv_n, wukv_v)


def _rope_tiles(x, cos, sa, sb):
    outs = []
    for j in range(x.shape[1] // LANES):
        t = x[:, j * LANES:(j + 1) * LANES]
        outs.append(t * cos + pltpu.roll(t, LANES - 16, 1) * sa + pltpu.roll(t, 16, 1) * sb)
    return outs[0] if len(outs) == 1 else jnp.concatenate(outs, axis=1)


def _group64_rms(x, g):
    w = x.shape[1]
    r = lax.shift_right_logical(lax.broadcasted_iota(jnp.int32, (w, w), 0), 6)
    c = lax.shift_right_logical(lax.broadcasted_iota(jnp.int32, (w, w), 1), 6)
    bd = jnp.where(r == c, 1.0, 0.0).astype(BF)
    x2 = x * x
    hi = x2.astype(BF)
    lo = (x2 - hi.astype(F32)).astype(BF)
    ms = (_dot(hi, bd) + _dot(lo, bd)) * (1.0 / 64)
    return x * lax.rsqrt(ms + EPS) * g


def _full_rms(x, g):
    ms = jnp.mean(x * x, axis=-1, keepdims=True)
    return x * lax.rsqrt(ms + EPS) * g


def _proj_body(l_ref, x_ref, mod_ref, w_ref, wuq_ref, wun_ref, wuv_ref, cqn_ref, ckvn_ref, dqn_ref, dkn_ref,
               *refs, rope):
    if rope:
        cos_ref, sa_ref, sb_ref = refs[:3]
        refs = refs[3:]
        cos, sa, sb = cos_ref[...], sa_ref[...], sb_ref[...]
        rp = lambda t: _rope_tiles(t, cos, sa, sb)
    else:
        rp = lambda t: t
    x = x_ref[...]
    shift = mod_ref[:, 0:D_MODEL]
    scale = mod_ref[:, D_MODEL:2 * D_MODEL]
    h = (x * (1.0 + scale) + shift).astype(BF)
    acc = _dot(h, w_ref[...])

    def col(o, n):
        return acc[:, o:o + n]

    qa = rp(col(O_AQ, 512)) * (D_A ** -0.5)
    ka = rp(col(O_AK, 512))
    va = col(O_AV, 512)
    qb = col(O_BQ, 512) * (D_B ** -0.5)
    kb = col(O_BK, 512)
    vb = col(O_BV, 512)
    cq = _full_rms(col(O_CQ, Q_RANK), cqn_ref[...]).astype(BF)
    qc_raw = _dot(cq, wuq_ref[...])
    qc_scale = (NOPE_DIM + ROPE_DIM) ** -0.5
    qc = jnp.concatenate(
        [t for hh in range(H_C) for t in (qc_raw[:, 2 * hh * LANES:(2 * hh + 1) * LANES],
                                          rp(qc_raw[:, (2 * hh + 1) * LANES:(2 * hh + 2) * LANES]))],
        axis=1) * qc_scale
    ckv = _full_rms(col(O_CKV, KV_RANK), ckvn_ref[...])
    ckv_b = ckv.astype(BF)
    kn = _dot(ckv_b, wun_ref[...])
    vc = _dot(ckv_b, wuv_ref[...])
    kr_raw = col(O_KR, LANES)
    kr = rp(kr_raw)
    kc = jnp.concatenate([t for hh in range(H_C) for t in (kn[:, hh * LANES:(hh + 1) * LANES], kr)], axis=1)
    qd = rp(_group64_rms(col(O_DQ, 512), dqn_ref[...])) * (D_D ** -0.5)
    kd_n = _group64_rms(col(O_DK, LANES), dkn_ref[...])
    kd = rp(kd_n)
    vd = col(O_DV, LANES)

    if rope:
        (h_o, qa_o, ka_o, va_o, qb_o, kb_o, vb_o, qc_o, kc_o, vc_o, qd_o, kd_o, vd_o) = refs
        ka_o[...] = ka.astype(BF)
        va_o[...] = va.astype(BF)
        kb_o[...] = kb.astype(BF)
        vb_o[...] = vb.astype(BF)
        kd_o[...] = kd.astype(BF)
        vd_o[...] = vd.astype(BF)
    else:
        (h_o, qa_o, qb_o, qc_o, kc_o, vc_o, qd_o,
         ka_o, va_o, kb_o, vb_o, ckv_o, kr_o, kd_o, vd_o) = refs
        ka_o[...] = ka
        va_o[...] = va
        kb_o[...] = kb
        vb_o[...] = vb
        ckv_o[...] = ckv
        kr_o[...] = kr_raw[:, 0:ROPE_DIM]
        kd_o[...] = kd_n
        vd_o[...] = vd
    h_o[...] = h
    qa_o[...] = qa.astype(BF)
    qb_o[...] = qb.astype(BF)
    qc_o[...] = qc.astype(BF)
    kc_o[...] = kc.astype(BF)
    vc_o[...] = vc.astype(BF)
    qd_o[...] = qd.astype(BF)


def _proj(l, x, mod4, W, rope_tabs, *, rope):
    m = x.shape[0]
    tm = TM_PROJ
    per_b = DEC_SEQ // tm
    cond = (lambda i: 1 + i // per_b) if rope else (lambda i: 0)
    row = lambda w: pl.BlockSpec((tm, w), lambda i, lr: (i, 0))
    wfull = lambda a: pl.BlockSpec((None,) + a.shape[1:], lambda i, lr: (lr[0],) + (0,) * (a.ndim - 1),
                                   pipeline_mode=pl.Buffered(1))
    weights = [W["w_qkv"], W["wuq"], W["wukv_n"], W["wukv_v"], W["cqn"], W["ckvn"], W["dqn"], W["dkn"]]
    in_specs = [row(D_MODEL),
                pl.BlockSpec((None, None, 1, 3 * D_MODEL), lambda i, lr: (lr[0], cond(i), 0, 0))]
    in_specs += [wfull(a) for a in weights]
    args = [x, mod4] + weights
    if rope:
        in_specs += [pl.BlockSpec((tm, LANES), lambda i, lr: (i % per_b, 0))] * 3
        args += list(rope_tabs)
        widths = [(D_MODEL, BF), (512, BF), (512, BF), (512, BF), (512, BF), (512, BF), (512, BF),
                  (1024, BF), (1024, BF), (512, BF), (512, BF), (LANES, BF), (LANES, BF)]
    else:
        widths = [(D_MODEL, BF), (512, BF), (512, BF), (1024, BF), (1024, BF), (512, BF), (512, BF),
                  (512, F32), (512, F32), (512, F32), (512, F32), (KV_RANK, F32), (ROPE_DIM, F32),
                  (LANES, F32), (LANES, F32)]
    return pl.pallas_call(
        functools.partial(_proj_body, rope=rope),
        grid_spec=pltpu.PrefetchScalarGridSpec(
            num_scalar_prefetch=1, grid=(m // tm,), in_specs=in_specs,
            out_specs=[row(w) for w, _ in widths]),
        out_shape=[jax.ShapeDtypeStruct((m, w), d) for w, d in widths],
        compiler_params=_cparams(("arbitrary",)),
        name="proj_lat" if rope else "proj_ctx",
    )(l, *args)


def _zg_body(l_ref, h_ref, w_ref, o_ref):
    a = _dot(h_ref[...], w_ref[...])
    s = _sigmoid(a)
    is_z = pl.program_id(1) < (N_BRANCH * BRANCH_W) // TN_ZG

    @pl.when(is_z)
    def _():
        o_ref[...] = (a * s).astype(BF)

    @pl.when(jnp.logical_not(is_z))
    def _():
        o_ref[...] = s.astype(BF)


def _zg(l, h, w_zg):
    m = h.shape[0]
    return pl.pallas_call(
        _zg_body,
        grid_spec=pltpu.PrefetchScalarGridSpec(
            num_scalar_prefetch=1, grid=(m // TM_ZG, NZG // TN_ZG),
            in_specs=[pl.BlockSpec((TM_ZG, D_MODEL), lambda i, j, lr: (i, 0)),
                      pl.BlockSpec((None, D_MODEL, TN_ZG), lambda i, j, lr: (lr[0], 0, j))],
            out_specs=pl.BlockSpec((TM_ZG, TN_ZG), lambda i, j, lr: (i, j))),
        out_shape=jax.ShapeDtypeStruct((m, NZG), BF),
        compiler_params=_cparams(("arbitrary", "arbitrary")),
        name="zg_proj",
    )(l, h, w_zg)


def _softmax_pv(scores, values):
    m = None
    for s in scores:
        sm = jnp.max(s, axis=-1, keepdims=True)
        m = sm if m is None else jnp.maximum(m, sm)
    acc, den = None, None
    for s, v in zip(scores, values):
        p = jnp.exp(s - m)
        d = jnp.sum(p, axis=-1, keepdims=True)
        o = _dot(p.astype(BF), v)
        acc = o if acc is None else acc + o
        den = d if den is None else den + d
    return acc / den


def _diff_lambda(lam_ref, lam_init):
    la = lam_ref[...]
    s01 = jnp.sum(la[0:1] * la[1:2], axis=-1, keepdims=True)
    s23 = jnp.sum(la[2:3] * la[3:4], axis=-1, keepdims=True)
    return jnp.exp(s01) - jnp.exp(s23) + lam_init


def _diff_head(q, ks, vs, lam, subln, lam_init):
    lt = _lane_lt64(q.shape)
    zero = jnp.zeros_like(q)
    o = []
    for qm in (jnp.where(lt, q, zero), jnp.where(lt, zero, q)):
        o.append(_softmax_pv([_dot_nt(qm, k) for k in ks], vs))
    d = o[0] - lam * o[1]
    ms = jnp.mean(d * d, axis=-1, keepdims=True)
    return d * lax.rsqrt(ms + EPS) * subln * (1.0 - lam_init)


def _pair_heads(q, ks, vs, bias=None):
    lt = _lane_lt64(q.shape)
    zero = jnp.zeros_like(q)
    o = []
    for qm in (jnp.where(lt, q, zero), jnp.where(lt, zero, q)):
        sc = [_dot_nt(qm, k) for k in ks]
        if bias is not None:
            sc[0] = sc[0] + bias
        o.append(_softmax_pv(sc, vs))
    return jnp.where(_lane_lt64(o[0].shape), o[0], o[1])


def _ctx_attn_body(l_ref, li_ref, qa_ref, ka_ref, va_ref, qb_ref, kb_ref, vb_ref, qc_ref, kc_ref, vc_ref,
                   qd_ref, kd_ref, vd_ref, lam_ref, subln_ref, oa_ref, ob_ref, oc_ref, od_ref):
    lam_init = li_ref[l_ref[0]]
    lam = _diff_lambda(lam_ref, lam_init)
    subln = subln_ref[...]
    for h in range(H_A):
        sl = slice(h * LANES, (h + 1) * LANES)
        oa_ref[:, sl] = _diff_head(qa_ref[:, sl], [ka_ref[:, sl].astype(BF)], [va_ref[:, sl].astype(BF)],
                                   lam, subln, lam_init).astype(BF)
    for j in range(H_B // 2):
        sl = slice(j * LANES, (j + 1) * LANES)
        ob_ref[:, sl] = _pair_heads(qb_ref[:, sl], [kb_ref[:, sl].astype(BF)],
                                    [vb_ref[:, sl].astype(BF)]).astype(BF)
    for h in range(H_C):
        oc_ref[:, h * LANES:(h + 1) * LANES] = _softmax_pv(
            [_dot_nt(qc_ref[:, 2 * h * LANES:(2 * h + 2) * LANES], kc_ref[:, 2 * h * LANES:(2 * h + 2) * LANES])],
            [vc_ref[:, h * LANES:(h + 1) * LANES]]).astype(BF)
    kd = kd_ref[...].astype(BF)
    vd = vd_ref[...].astype(BF)
    for j in range(H_D // 2):
        sl = slice(j * LANES, (j + 1) * LANES)
        od_ref[:, sl] = _pair_heads(qd_ref[:, sl], [kd], [vd]).astype(BF)


def _ctx_attn(l, lam_tab, pc, lam_a, a_subln):
    row = lambda w: pl.BlockSpec((SEQ, w), lambda b, lr, li: (b, 0))
    ins = [pc["qa"], pc["ka"], pc["va"], pc["qb"], pc["kb"], pc["vb"], pc["qc"], pc["kc"], pc["vc"],
           pc["qd"], pc["kd"], pc["vd"]]
    in_specs = [row(a.shape[1]) for a in ins]
    in_specs += [pl.BlockSpec((None, 4, D_A), lambda b, lr, li: (lr[0], 0, 0)),
                 pl.BlockSpec((None, 1, 2 * D_A), lambda b, lr, li: (lr[0], 0, 0))]
    return pl.pallas_call(
        _ctx_attn_body,
        grid_spec=pltpu.PrefetchScalarGridSpec(
            num_scalar_prefetch=2, grid=(BATCH,), in_specs=in_specs,
            out_specs=[row(512)] * 4),
        out_shape=[jax.ShapeDtypeStruct((N_CTX, 512), BF)] * 4,
        compiler_params=_cparams(("arbitrary",)),
        name="ctx_attn",
    )(l, lam_tab, *ins, lam_a, a_subln)


def _lat_a_body(l_ref, li_ref, q_ref, k_ref, v_ref, kc_ref, vc_ref, lam_ref, subln_ref, o_ref):
    lam_init = li_ref[l_ref[0]]
    lam = _diff_lambda(lam_ref, lam_init)
    o_ref[...] = _diff_head(q_ref[...], [k_ref[...], kc_ref[...].astype(BF)],
                            [v_ref[...], vc_ref[...].astype(BF)], lam, subln_ref[...], lam_init).astype(BF)


def _lat_a(l, lam_tab, q, k, v, cache_k, cache_v, lam_a, a_subln):
    kv = pl.BlockSpec((None, DEC_SEQ, LANES), lambda b, h, i, lr, li: (b, 0, h))
    cache = pl.BlockSpec((None, None, PAST_LEN, LANES), lambda b, h, i, lr, li: (b, lr[0], 0, h))
    qo = pl.BlockSpec((None, TQ, LANES), lambda b, h, i, lr, li: (b, i, h))
    return pl.pallas_call(
        _lat_a_body,
        grid_spec=pltpu.PrefetchScalarGridSpec(
            num_scalar_prefetch=2, grid=(DEC_BATCH, H_A, DEC_SEQ // TQ),
            in_specs=[qo, kv, kv, cache, cache,
                      pl.BlockSpec((None, 4, D_A), lambda b, h, i, lr, li: (lr[0], 0, 0)),
                      pl.BlockSpec((None, 1, 2 * D_A), lambda b, h, i, lr, li: (lr[0], 0, 0))],
            out_specs=qo),
        out_shape=jax.ShapeDtypeStruct((DEC_BATCH, DEC_SEQ, 512), BF),
        compiler_params=_cparams(("arbitrary",) * 3),
        name="lat_attn_a",
    )(l, lam_tab, q, k, v, cache_k, cache_v, lam_a, a_subln)


def _lat_c_body(l_ref, q_ref, k_ref, v_ref, kc_ref, vc_ref, o_ref):
    q = q_ref[...]
    o_ref[...] = _softmax_pv([_dot_nt(q, k_ref[...]), _dot_nt(q, kc_ref[...])],
                             [v_ref[...], vc_ref[...]]).astype(BF)


def _lat_c(l, q, k, v, cache_k, cache_v):
    im = lambda f: (lambda b, h, i, lr: f(b, h, i, lr))
    return pl.pallas_call(
        _lat_c_body,
        grid_spec=pltpu.PrefetchScalarGridSpec(
            num_scalar_prefetch=1, grid=(DEC_BATCH, H_C, DEC_SEQ // TQ),
            in_specs=[pl.BlockSpec((None, TQ, 2 * LANES), im(lambda b, h, i, lr: (b, i, h))),
                      pl.BlockSpec((None, DEC_SEQ, 2 * LANES), im(lambda b, h, i, lr: (b, 0, h))),
                      pl.BlockSpec((None, DEC_SEQ, LANES), im(lambda b, h, i, lr: (b, 0, h))),
                      pl.BlockSpec((None, None, PAST_LEN, 2 * LANES), im(lambda b, h, i, lr: (lr[0], b, 0, h))),
                      pl.BlockSpec((None, None, PAST_LEN, LANES), im(lambda b, h, i, lr: (lr[0], b, 0, h)))],
            out_specs=pl.BlockSpec((None, TQ, LANES), im(lambda b, h, i, lr: (b, i, h)))),
        out_shape=jax.ShapeDtypeStruct((DEC_BATCH, DEC_SEQ, 512), BF),
        compiler_params=_cparams(("arbitrary",) * 3),
        name="lat_attn_c",
    )(l, q, k, v, cache_k, cache_v)


def _lat_d_body(l_ref, q_ref, k_ref, v_ref, kc_ref, vc_ref, o_ref):
    ks = [k_ref[...], kc_ref[...].astype(BF)]
    vs = [v_ref[...], vc_ref[...].astype(BF)]
    for j in range(H_D // 2):
        sl = slice(j * LANES, (j + 1) * LANES)
        o_ref[:, sl] = _pair_heads(q_ref[:, sl], ks, vs).astype(BF)


def _lat_d(l, q, k, v, cache_k, cache_v):
    kv = pl.BlockSpec((None, DEC_SEQ, LANES), lambda b, i, lr: (b, 0, 0))
    cache = pl.BlockSpec((None, None, PAST_LEN, LANES), lambda b, i, lr: (b, lr[0], 0, 0))
    qo = pl.BlockSpec((None, TQ, 512), lambda b, i, lr: (b, i, 0))
    return pl.pallas_call(
        _lat_d_body,
        grid_spec=pltpu.PrefetchScalarGridSpec(
            num_scalar_prefetch=1, grid=(DEC_BATCH, DEC_SEQ // TQ),
            in_specs=[qo, kv, kv, cache, cache], out_specs=qo),
        out_shape=jax.ShapeDtypeStruct((DEC_BATCH, DEC_SEQ, 512), BF),
        compiler_params=_cparams(("arbitrary",) * 2),
        name="lat_attn_d",
    )(l, q, k, v, cache_k, cache_v)


def _lat_b_body(l_ref, q_ref, k_ref, v_ref, kc_ref, vc_ref, tab_ref, o_ref):
    i = pl.program_id(1)
    qr0 = i * NB_ROWS
    kr0 = jnp.clip(qr0 - NA_ROWS // 2, 0, ROWS - NB_KROWS)
    start = pl.multiple_of(kr0 * GRID_W, GRID_W)
    n_keys = NB_KROWS * GRID_W
    lt = _lane_lt64((GRID_W, LANES))
    for j in range(H_B // 2):
        sl = slice(j * LANES, (j + 1) * LANES)
        kwin = k_ref[pl.ds(start, n_keys), sl]
        vwin = v_ref[pl.ds(start, n_keys), sl]
        ks = [kwin, kc_ref[:, sl].astype(BF)]
        vs = [vwin, vc_ref[:, sl].astype(BF)]
        q = q_ref[:, sl]
        ltq = _lane_lt64(q.shape)
        zero = jnp.zeros_like(q)
        outs = []
        for half, qm in ((0, jnp.where(ltq, q, zero)), (1, jnp.where(ltq, zero, q))):
            head = 2 * j + half
            rows = []
            for a in range(NB_ROWS):
                qr = qr0 + a
                r0 = jnp.clip(qr - NA_ROWS // 2, 0, ROWS - NA_ROWS)
                tiles = []
                for p in range(NB_KROWS // 2):
                    kr_l = kr0 + 2 * p
                    u = jnp.clip(kr_l - qr + NA_ROWS, 0, 2 * NA_ROWS - 1)
                    pen_l = jnp.where((kr_l >= r0) & (kr_l < r0 + NA_ROWS), 0.0, NEG)
                    pen_r = jnp.where((kr_l + 1 >= r0) & (kr_l + 1 < r0 + NA_ROWS), 0.0, NEG)
                    tiles.append(tab_ref[head, u] + jnp.where(lt, pen_l, pen_r))
                rows.append(jnp.concatenate(tiles, axis=1))
            bias = jnp.concatenate(rows, axis=0)
            sc = [_dot_nt(qm, ks[0]) + bias, _dot_nt(qm, ks[1])]
            outs.append(_softmax_pv(sc, vs))
        o_ref[:, sl] = jnp.where(ltq, outs[0], outs[1]).astype(BF)


def _lat_b(l, q, k, v, cache_k, cache_v, tab):
    nq = NB_ROWS * GRID_W
    kv = pl.BlockSpec((None, DEC_SEQ, 512), lambda b, i, lr: (b, 0, 0))
    cache = pl.BlockSpec((None, None, PAST_LEN, 512), lambda b, i, lr: (b, lr[0], 0, 0))
    qo = pl.BlockSpec((None, nq, 512), lambda b, i, lr: (b, i, 0))
    return pl.pallas_call(
        _lat_b_body,
        grid_spec=pltpu.PrefetchScalarGridSpec(
            num_scalar_prefetch=1, grid=(DEC_BATCH, DEC_SEQ // nq),
            in_specs=[qo, kv, kv, cache, cache,
                      pl.BlockSpec((None, H_B, 2 * NA_ROWS, GRID_W, LANES), lambda b, i, lr: (lr[0], 0, 0, 0, 0))],
            out_specs=qo),
        out_shape=jax.ShapeDtypeStruct((DEC_BATCH, DEC_SEQ, 512), BF),
        compiler_params=_cparams(("arbitrary",) * 2),
        name="lat_attn_b",
    )(l, q, k, v, cache_k, cache_v, tab)


def _merge_body(l_ref, x_ref, mod_ref, oa_ref, ob_ref, oc_ref, od_ref, z_ref, g0_ref, g1_ref, g2_ref, g3_ref,
                wbr_ref, wout_ref, lng_ref, lnb_ref, o_ref):
    merged = None
    for i, (o_r, g_r) in enumerate(zip((oa_ref, ob_ref, oc_ref, od_ref), (g0_ref, g1_ref, g2_ref, g3_ref))):
        u = (o_r[...].astype(F32) * z_ref[:, i * BRANCH_W:(i + 1) * BRANCH_W].astype(F32)).astype(BF)
        term = g_r[...].astype(F32) * _dot(u, wbr_ref[i])
        merged = term if merged is None else merged + term
    y = _dot(merged.astype(BF), wout_ref[...])
    gate = mod_ref[:, 2 * D_MODEL:3 * D_MODEL]
    r = ALPHA * x_ref[...] + gate * y
    mu = jnp.mean(r, axis=-1, keepdims=True)
    d = r - mu
    var = jnp.mean(d * d, axis=-1, keepdims=True)
    o_ref[...] = d * lax.rsqrt(var + EPS) * lng_ref[...] + lnb_ref[...]


def _merge(l, x, mod4, outs, zg, W, *, latent):
    m = x.shape[0]
    tm = TM_MERGE
    per_b = DEC_SEQ // tm
    cond = (lambda i: 1 + i // per_b) if latent else (lambda i: 0)
    row = lambda w: pl.BlockSpec((tm, w), lambda i, lr: (i, 0))
    zgb = lambda j: pl.BlockSpec((tm, D_MODEL), lambda i, lr: (i, j))
    wfull = lambda a: pl.BlockSpec((None,) + a.shape[1:], lambda i, lr: (lr[0],) + (0,) * (a.ndim - 1),
                                   pipeline_mode=pl.Buffered(1))
    in_specs = [row(D_MODEL),
                pl.BlockSpec((None, None, 1, 3 * D_MODEL), lambda i, lr: (lr[0], cond(i), 0, 0)),
                row(512), row(512), row(512), row(512),
                zgb(0), zgb(1), zgb(2), zgb(3), zgb(4),
                wfull(W["w_br"]), wfull(W["w_out"]), wfull(W["ln_g"]), wfull(W["ln_b"])]
    return pl.pallas_call(
        _merge_body,
        grid_spec=pltpu.PrefetchScalarGridSpec(
            num_scalar_prefetch=1, grid=(m // tm,), in_specs=in_specs, out_specs=row(D_MODEL)),
        out_shape=jax.ShapeDtypeStruct((m, D_MODEL), F32),
        compiler_params=_cparams(("arbitrary",)),
        name="merge_lat" if latent else "merge_ctx",
    )(l, x, mod4, *outs, zg, zg, zg, zg, zg, W["w_br"], W["w_out"], W["ln_g"], W["ln_b"])


def _rope_tables():
    t = jnp.arange(DEC_SEQ)
    row = (t // GRID_W).astype(F32)
    col = (t % GRID_W).astype(F32)
    quarter = D_A // 4
    inv_freq = ROPE_BASE ** (-jnp.arange(quarter, dtype=F32) / quarter)
    ar = row[:, None] * inv_freq
    ac = col[:, None] * inv_freq
    ang = jnp.concatenate([ar, ar, ac, ac], axis=-1)
    cos, sin = jnp.cos(ang), jnp.sin(ang)
    even = (jnp.arange(D_A) // quarter) % 2 == 0
    sa = jnp.where(even, -sin, 0.0)
    sb = jnp.where(even, 0.0, sin)
    tile2 = lambda a: jnp.concatenate([a, a], axis=-1)
    return tile2(cos), tile2(sa), tile2(sb)


def _perm_heads(a, axis):
    shp = a.shape
    a = a.reshape(shp[:axis] + (H_D, D_D) + shp[axis + 1:])
    a = jnp.take(a, jnp.array(GQA_PERM), axis=axis)
    return a.reshape(shp)


def _prep_weights(w_in, c_q_norm, c_kv_norm, w_c_uq, w_c_ukv, d_q_norm, d_k_norm, w_br, w_out, ln_g, ln_b):
    offs = np.concatenate([[0], np.cumsum(IN_SIZES)])
    part = lambda i: w_in[:, :, offs[i]:offs[i + 1]]
    aq, ak, av, bq, bk, bv, cq, ckv, ckr, dq, dk, dv, z, g = (part(i) for i in range(14))
    w_qkv = jnp.concatenate(
        [aq, ak, av, bq, bk, bv, cq, ckv, _perm_heads(dq, 2), dk, dv, ckr,
         jnp.zeros((DEPTH, D_MODEL, LANES - ROPE_DIM), F32)], axis=2).astype(BF)
    z3 = _perm_heads(z[:, :, 3 * BRANCH_W:], 2)
    w_zg = jnp.concatenate([z[:, :, :3 * BRANCH_W], z3, g], axis=2).astype(BF)
    uq = w_c_uq.reshape(DEPTH, Q_RANK, H_C, NOPE_DIM + ROPE_DIM)
    wuq = jnp.concatenate([uq, jnp.zeros((DEPTH, Q_RANK, H_C, 2 * LANES - NOPE_DIM - ROPE_DIM), F32)],
                          axis=3).reshape(DEPTH, Q_RANK, H_C * 2 * LANES).astype(BF)
    ukv = w_c_ukv.reshape(DEPTH, KV_RANK, H_C, NOPE_DIM + V_DIM_C)
    wukv_n = ukv[..., :NOPE_DIM].reshape(DEPTH, KV_RANK, H_C * NOPE_DIM).astype(BF)
    wukv_v = ukv[..., NOPE_DIM:].reshape(DEPTH, KV_RANK, H_C * V_DIM_C).astype(BF)
    wbr = jnp.concatenate([w_br[:, :3], _perm_heads(w_br[:, 3:], 2)], axis=1).astype(BF)
    return {
        "w_qkv": w_qkv, "w_zg": w_zg, "wuq": wuq, "wukv_n": wukv_n, "wukv_v": wukv_v,
        "cqn": c_q_norm.reshape(DEPTH, 1, Q_RANK), "ckvn": c_kv_norm.reshape(DEPTH, 1, KV_RANK),
        "dqn": jnp.tile(d_q_norm, (1, H_D)).reshape(DEPTH, 1, H_D * D_D),
        "dkn": jnp.tile(d_k_norm, (1, G_D)).reshape(DEPTH, 1, G_D * D_D),
        "w_br": wbr, "w_out": w_out.astype(BF),
        "ln_g": ln_g.reshape(DEPTH, 1, D_MODEL), "ln_b": ln_b.reshape(DEPTH, 1, D_MODEL),
    }


def kernel(x_prompt, x_sample, cache_a_k, cache_a_v, cache_b_k, cache_b_v, cache_c_kv, cache_c_kr, cache_d_k,
           cache_d_v, c, c_ctx, w_ada, b_ada, w_in, lam_a, a_subln, b_rpb, c_q_norm, c_kv_norm, w_c_uq, w_c_ukv,
           d_q_norm, d_k_norm, w_br, w_out, ln_g, ln_b):
    W = _prep_weights(w_in, c_q_norm, c_kv_norm, w_c_uq, w_c_ukv, d_q_norm, d_k_norm, w_br, w_out, ln_g, ln_b)
    cond8 = jnp.concatenate([c_ctx[None], c, jnp.zeros((5, D_MODEL), F32)], axis=0)
    mod4 = _adaln(cond8, w_ada, b_ada).reshape(DEPTH, 8, 1, 3 * D_MODEL)
    nb_tab = _bias_table(b_rpb).reshape(DEPTH, H_B, 2 * NA_ROWS, GRID_W, LANES)
    kc_cache, vc_cache = _cache_mla(cache_c_kv, cache_c_kr, W["wukv_n"], W["wukv_v"])
    rope_tabs = _rope_tables()
    lam_tab = jnp.array([0.8 - 0.6 * math.exp(-0.3 * l) for l in range(DEPTH)], F32)
    subln = a_subln.reshape(DEPTH, 1, 2 * D_A)
    flat = lambda a: a.reshape(DEC_BATCH, DEPTH, PAST_LEN, -1)
    ca_k, ca_v, cb_k, cb_v, cd_k, cd_v = (flat(a) for a in (cache_a_k, cache_a_v, cache_b_k, cache_b_v,
                                                               cache_d_k, cache_d_v))
    lat3 = lambda a: a.reshape(DEC_BATCH, DEC_SEQ, a.shape[-1])

    def layer(carry, li):
        xp, xs = carry
        l = li.reshape(1)
        (h_c, qa, qb, qc, kc, vc, qd, ka, va, kb, vb, ckv, kr, kd, vd) = _proj(l, xp, mod4, W, None, rope=False)
        pc = dict(qa=qa, ka=ka, va=va, qb=qb, kb=kb, vb=vb, qc=qc, kc=kc, vc=vc, qd=qd, kd=kd, vd=vd)
        outs_c = _ctx_attn(l, lam_tab, pc, lam_a, subln)
        zg_c = _zg(l, h_c, W["w_zg"])
        xp_new = _merge(l, xp, mod4, outs_c, zg_c, W, latent=False)
        (h_l, lqa, lka, lva, lqb, lkb, lvb, lqc, lkc, lvc, lqd, lkd, lvd) = _proj(l, xs, mod4, W, rope_tabs,
                                                                                   rope=True)
        o_a = _lat_a(l, lam_tab, lat3(lqa), lat3(lka), lat3(lva), ca_k, ca_v, lam_a, subln)
        o_b = _lat_b(l, lat3(lqb), lat3(lkb), lat3(lvb), cb_k, cb_v, nb_tab)
        o_c = _lat_c(l, lat3(lqc), lat3(lkc), lat3(lvc), kc_cache, vc_cache)
        o_d = _lat_d(l, lat3(lqd), lat3(lkd), lat3(lvd), cd_k, cd_v)
        outs_l = [o.reshape(N_LAT, 512) for o in (o_a, o_b, o_c, o_d)]
        zg_l = _zg(l, h_l, W["w_zg"])
        xs_new = _merge(l, xs, mod4, outs_l, zg_l, W, latent=True)
        return (xp_new, xs_new), (ka, va, kb, vb, ckv, kr, kd, vd)

    (xp, xs), caches = lax.scan(layer, (x_prompt.reshape(N_CTX, D_MODEL), x_sample.reshape(N_LAT, D_MODEL)),
                                jnp.arange(DEPTH, dtype=jnp.int32))
    ka, va, kb, vb, ckv, kr, kd, vd = caches

    def out(a, tail):
        return jnp.transpose(a.reshape((DEPTH, BATCH, SEQ) + tail), (1, 0, 2) + tuple(range(3, 3 + len(tail))))

    return (xp.reshape(BATCH, SEQ, D_MODEL), xs.reshape(DEC_BATCH, DEC_SEQ, D_MODEL),
            out(ka, (H_A, 2 * D_A)), out(va, (H_A, 2 * D_A)), out(kb, (H_B, D_B)), out(vb, (H_B, D_B)),
            out(ckv, (KV_RANK,)), out(kr, (ROPE_DIM,)), out(kd, (G_D, D_D)), out(vd, (G_D, D_D)))
```

```python
import functools
import math

import jax
import jax.numpy as jnp
import numpy as np
from jax import lax
from jax.experimental import pallas as pl
from jax.experimental.pallas import tpu as pltpu

D_MODEL = 2048
BATCH = 16
SEQ = 256
DEPTH = 4
DEC_BATCH = 2
DEC_SEQ = 4096
PAST_LEN = 256
GRID_W = 64
ROWS = DEC_SEQ // GRID_W
N_BRANCH = 4
BRANCH_W = 512
H_A, D_A = 4, 64
H_B, D_B = 8, 64
NA_ROWS, NA_COLS = 8, 16
H_C, Q_RANK, KV_RANK, NOPE_DIM, ROPE_DIM, V_DIM_C = 4, 512, 256, 128, 64, 128
H_D, G_D, D_D = 8, 2, 64
ROPE_BASE = 10000.0
EPS = 1e-6
ALPHA = (2 * DEPTH) ** 0.25
IN_SIZES = (512, 512, 512, 512, 512, 512, Q_RANK, KV_RANK, ROPE_DIM, 512, 128, 128,
            N_BRANCH * BRANCH_W, N_BRANCH * D_MODEL)

BF = jnp.bfloat16
F32 = jnp.float32
LANES = 128
MXU_N = 256
LOG2E = 1.4426950408889634
VMEM_LIMIT = 56 * 1024 * 1024
NEG = -1e30

N_CTX = BATCH * SEQ
N_LAT = DEC_BATCH * DEC_SEQ
NQ = 4736
NZG = 10240
GQA_PERM = (0, 4, 1, 5, 2, 6, 3, 7)

O_AQ, O_AK, O_AV, O_BQ, O_BK, O_BV = 0, 512, 1024, 1536, 2048, 2560
O_CQ, O_CKV, O_DQ, O_DK, O_DV, O_KR = 3072, 3584, 3840, 4352, 4480, 4608

TM_PROJ = 256
TM_ZG = 512
TN_ZG = 1024
TM_MERGE = 256
ROW_TILE = 256
TQ_A = 256
TQ_C = 512
TQ_D = 512
NB_ROWS = 4
NB_KROWS = 12


def _cparams(sem):
    return pltpu.CompilerParams(dimension_semantics=sem, vmem_limit_bytes=VMEM_LIMIT)


def _dot(a, b):
    return jnp.dot(a, b, preferred_element_type=F32)


def _dot_nt(a, b):
    return lax.dot_general(a, b, (((1,), (1,)), ((), ())), preferred_element_type=F32)


def _sigmoid(x):
    return 1.0 / (1.0 + jnp.exp(-x))


def _lane_lt64(shape):
    return lax.broadcasted_iota(jnp.int32, shape, len(shape) - 1) < 64


def _adaln_body(c_ref, w_ref, b_ref, o_ref):
    c = c_ref[...]
    s = (c * _sigmoid(c)).astype(BF)
    o_ref[...] = _dot(s, w_ref[...].astype(BF)) + b_ref[...]


def _adaln(cond8, w_ada, b_ada):
    tn = 1536
    return pl.pallas_call(
        _adaln_body,
        grid=(DEPTH, 3 * D_MODEL // tn),
        in_specs=[pl.BlockSpec((8, D_MODEL), lambda l, j: (0, 0)),
                  pl.BlockSpec((None, D_MODEL, tn), lambda l, j: (l, 0, j)),
                  pl.BlockSpec((None, 1, tn), lambda l, j: (l, 0, j))],
        out_specs=pl.BlockSpec((None, 8, tn), lambda l, j: (l, 0, j)),
        out_shape=jax.ShapeDtypeStruct((DEPTH, 8, 3 * D_MODEL), F32),
        compiler_params=_cparams(("arbitrary", "arbitrary")),
        name="adaln",
    )(cond8, w_ada, b_ada.reshape(DEPTH, 1, 3 * D_MODEL))


def _bias_table_body(rpb_ref, o_ref):
    lh = pl.program_id(0)
    qc = lax.broadcasted_iota(jnp.int32, (GRID_W, LANES), 0)
    lane = lax.broadcasted_iota(jnp.int32, (GRID_W, LANES), 1)
    kc = jnp.bitwise_and(lane, 63)
    c0 = jnp.clip(qc - NA_COLS // 2, 0, GRID_W - NA_COLS)
    col_ok = (kc >= c0) & (kc < c0 + NA_COLS)
    dc = kc - qc + (NA_COLS - 1)
    right = lane >= 64
    n_dr, n_dc = 2 * NA_ROWS - 1, 2 * NA_COLS - 1
    for u in range(n_dr + 1):
        val = jnp.full((GRID_W, LANES), NEG, F32)
        for half, sel in ((0, ~right), (1, right)):
            dr = u - 1 + half
            if 0 <= dr < n_dr:
                for d in range(n_dc):
                    r = rpb_ref[(lh * n_dr + dr) * n_dc + d] * LOG2E
                    val = jnp.where(sel & col_ok & (dc == d), r, val)
        o_ref[u] = val


def _bias_table(b_rpb):
    n = DEPTH * H_B
    return pl.pallas_call(
        _bias_table_body,
        grid_spec=pltpu.PrefetchScalarGridSpec(
            num_scalar_prefetch=1, grid=(n,),
            in_specs=[],
            out_specs=pl.BlockSpec((None, 2 * NA_ROWS, GRID_W, LANES), lambda i, r: (i, 0, 0, 0))),
        out_shape=jax.ShapeDtypeStruct((n, 2 * NA_ROWS, GRID_W, LANES), F32),
        compiler_params=_cparams(("arbitrary",)),
        name="nb_bias_table",
    )(b_rpb.reshape(-1))


def _cache_mla_body(ckv_ref, kr_ref, wn_ref, wv_ref, k_ref, v_ref):
    ckv = ckv_ref[...].astype(BF)
    kn = _dot(ckv, wn_ref[...])
    kr = kr_ref[...]
    k_ref[...] = jnp.concatenate(
        [t for h in range(H_C) for t in (kn[:, h * LANES:(h + 1) * LANES], kr)], axis=1).astype(BF)
    v_ref[...] = _dot(ckv, wv_ref[...]).astype(BF)


def _cache_mla(cache_c_kv, cache_c_kr, wukv_n, wukv_v):
    return pl.pallas_call(
        _cache_mla_body,
        grid=(DEPTH, DEC_BATCH),
        in_specs=[pl.BlockSpec((None, None, PAST_LEN, KV_RANK), lambda l, b: (b, l, 0, 0)),
                  pl.BlockSpec((None, None, PAST_LEN, LANES), lambda l, b: (b, l, 0, 0)),
                  pl.BlockSpec((None, KV_RANK, 512), lambda l, b: (l, 0, 0)),
                  pl.BlockSpec((None, KV_RANK, 512), lambda l, b: (l, 0, 0))],
        out_specs=[pl.BlockSpec((None, None, PAST_LEN, 1024), lambda l, b: (l, b, 0, 0)),
                   pl.BlockSpec((None, None, PAST_LEN, 512), lambda l, b: (l, b, 0, 0))],
        out_shape=[jax.ShapeDtypeStruct((DEPTH, DEC_BATCH, PAST_LEN, 1024), BF),
                   jax.ShapeDtypeStruct((DEPTH, DEC_BATCH, PAST_LEN, 512), BF)],
        compiler_params=_cparams(("arbitrary", "arbitrary")),
        name="cache_mla",
    )(cache_c_kv, jnp.pad(cache_c_kr, ((0, 0), (0, 0), (0, 0), (0, LANES - ROPE_DIM))), wukv_n, wukv_v)


def _rope_tiles(x, cos, sa, sb):
    outs = []
    for j in range(x.shape[1] // LANES):
        t = x[:, j * LANES:(j + 1) * LANES]
        outs.append(t * cos + pltpu.roll(t, LANES - 16, 1) * sa + pltpu.roll(t, 16, 1) * sb)
    return outs[0] if len(outs) == 1 else jnp.concatenate(outs, axis=1)


def _group64_rms(x, g):
    w = x.shape[1]
    r = lax.shift_right_logical(lax.broadcasted_iota(jnp.int32, (w, w), 0), 6)
    c = lax.shift_right_logical(lax.broadcasted_iota(jnp.int32, (w, w), 1), 6)
    bd = jnp.where(r == c, 1.0, 0.0).astype(BF)
    x2 = x * x
    hi = x2.astype(BF)
    lo = (x2 - hi.astype(F32)).astype(BF)
    ms = (_dot(hi, bd) + _dot(lo, bd)) * (1.0 / 64)
    return x * lax.rsqrt(ms + EPS) * g


def _full_rms(x, g):
    ms = jnp.mean(x * x, axis=-1, keepdims=True)
    return x * lax.rsqrt(ms + EPS) * g


def _proj_body(l_ref, x_ref, mod_ref, w_ref, wuq_ref, wun_ref, wuv_ref, cqn_ref, ckvn_ref, dqn_ref, dkn_ref,
               *refs, rope):
    if rope:
        cos_ref, sa_ref, sb_ref = refs[:3]
        refs = refs[3:]
        cos, sa, sb = cos_ref[...], sa_ref[...], sb_ref[...]
        rp = lambda t: _rope_tiles(t, cos, sa, sb)
    else:
        rp = lambda t: t
    x = x_ref[...]
    shift = mod_ref[:, 0:D_MODEL]
    scale = mod_ref[:, D_MODEL:2 * D_MODEL]
    h = (x * (1.0 + scale) + shift).astype(BF)

    acc = _dot(h, w_ref[...])

    def col(o, n):
        return acc[:, o:o + n]

    def with_ones(v):
        ones = jnp.ones((v.shape[0], LANES), BF)
        return jnp.concatenate([t for j in range(v.shape[1] // LANES)
                                for t in (v[:, j * LANES:(j + 1) * LANES].astype(BF), ones)], axis=1)

    qa = rp(col(O_AQ, 512)) * (D_A ** -0.5 * LOG2E)
    ka = rp(col(O_AK, 512))
    va = col(O_AV, 512)
    qb = col(O_BQ, 512) * (D_B ** -0.5 * LOG2E)
    kb = col(O_BK, 512)
    vb = col(O_BV, 512)
    cq = _full_rms(col(O_CQ, Q_RANK), cqn_ref[...]).astype(BF)
    qc_raw = _dot(cq, wuq_ref[...])
    qc_scale = (NOPE_DIM + ROPE_DIM) ** -0.5 * LOG2E
    qc = jnp.concatenate(
        [t for hh in range(H_C) for t in (qc_raw[:, 2 * hh * LANES:(2 * hh + 1) * LANES],
                                          rp(qc_raw[:, (2 * hh + 1) * LANES:(2 * hh + 2) * LANES]))],
        axis=1) * qc_scale
    ckv = _full_rms(col(O_CKV, KV_RANK), ckvn_ref[...])
    ckv_b = ckv.astype(BF)
    kn = _dot(ckv_b, wun_ref[...])
    vc = _dot(ckv_b, wuv_ref[...])
    kr_raw = col(O_KR, LANES)
    kr = rp(kr_raw)
    kc = jnp.concatenate([t for hh in range(H_C) for t in (kn[:, hh * LANES:(hh + 1) * LANES], kr)], axis=1)
    qd = rp(_group64_rms(col(O_DQ, 512), dqn_ref[...])) * (D_D ** -0.5 * LOG2E)
    kd_n = _group64_rms(col(O_DK, LANES), dkn_ref[...])
    kd = rp(kd_n)
    vd = col(O_DV, LANES)

    if rope:
        (h_o, qa_o, ka_o, va_o, qb_o, kb_o, vb_o, qc_o, kc_o, vc_o, qd_o, kd_o, vd_o) = refs
        ka_o[...] = ka.astype(BF)
        va_o[...] = with_ones(va)
        kb_o[...] = kb.astype(BF)
        vb_o[...] = with_ones(vb)
        kd_o[...] = kd.astype(BF)
        vd_o[...] = with_ones(vd)
        vc_o[...] = with_ones(vc)
    else:
        (h_o, qa_o, qb_o, qc_o, kc_o, vc_o, qd_o,
         ka_o, va_o, kb_o, vb_o, ckv_o, kr_o, kd_o, vd_o) = refs
        ka_o[...] = ka
        va_o[...] = va
        kb_o[...] = kb
        vb_o[...] = vb
        ckv_o[...] = ckv
        kr_o[...] = kr_raw[:, 0:ROPE_DIM]
        kd_o[...] = kd_n
        vd_o[...] = vd
        vc_o[...] = vc.astype(BF)
    h_o[...] = h
    qa_o[...] = qa.astype(BF)
    qb_o[...] = qb.astype(BF)
    qc_o[...] = qc.astype(BF)
    kc_o[...] = kc.astype(BF)
    qd_o[...] = qd.astype(BF)


def _proj(l, x, mod4, W, rope_tabs, *, rope):
    m = x.shape[0]
    tm = TM_PROJ
    per_b = DEC_SEQ // tm
    cond = (lambda i: 1 + i // per_b) if rope else (lambda i: 0)
    row = lambda w: pl.BlockSpec((tm, w), lambda i, lr: (i, 0))
    wfull = lambda a: pl.BlockSpec((None,) + a.shape[1:], lambda i, lr: (lr[0],) + (0,) * (a.ndim - 1),
                                   pipeline_mode=pl.Buffered(1))
    weights = [W["w_qkv"], W["wuq"], W["wukv_n"], W["wukv_v"], W["cqn"], W["ckvn"], W["dqn"], W["dkn"]]
    in_specs = [row(D_MODEL),
                pl.BlockSpec((None, None, 1, 3 * D_MODEL), lambda i, lr: (lr[0], cond(i), 0, 0))]
    in_specs += [wfull(a) for a in weights]
    args = [x, mod4] + weights
    if rope:
        in_specs += [pl.BlockSpec((tm, LANES), lambda i, lr: (i % per_b, 0))] * 3
        args += list(rope_tabs)
        widths = [(D_MODEL, BF), (512, BF), (512, BF), (1024, BF), (512, BF), (512, BF), (1024, BF),
                  (1024, BF), (1024, BF), (1024, BF), (512, BF), (LANES, BF), (2 * LANES, BF)]
    else:
        widths = [(D_MODEL, BF), (512, BF), (512, BF), (1024, BF), (1024, BF), (512, BF), (512, BF),
                  (512, F32), (512, F32), (512, F32), (512, F32), (KV_RANK, F32), (ROPE_DIM, F32),
                  (LANES, F32), (LANES, F32)]
    return pl.pallas_call(
        functools.partial(_proj_body, rope=rope),
        grid_spec=pltpu.PrefetchScalarGridSpec(
            num_scalar_prefetch=1, grid=(m // tm,), in_specs=in_specs,
            out_specs=[row(w) for w, _ in widths]),
        out_shape=[jax.ShapeDtypeStruct((m, w), d) for w, d in widths],
        compiler_params=_cparams(("arbitrary",)),
        name="proj_lat" if rope else "proj_ctx",
    )(l, *args)


def _zg_body(l_ref, h_ref, w_ref, o_ref):
    is_z = pl.program_id(1) < (N_BRANCH * BRANCH_W) // TN_ZG

    def run(silu):
        h = h_ref[...]
        for n in range(TN_ZG // MXU_N):
            sl = slice(n * MXU_N, (n + 1) * MXU_N)
            a = _dot(h, w_ref[:, sl])
            s = _sigmoid(a)
            o_ref[:, sl] = ((a * s) if silu else s).astype(BF)

    @pl.when(is_z)
    def _():
        run(True)

    @pl.when(jnp.logical_not(is_z))
    def _():
        run(False)


def _zg(l, h, w_zg):
    m = h.shape[0]
    return pl.pallas_call(
        _zg_body,
        grid_spec=pltpu.PrefetchScalarGridSpec(
            num_scalar_prefetch=1, grid=(m // TM_ZG, NZG // TN_ZG),
            in_specs=[pl.BlockSpec((TM_ZG, D_MODEL), lambda i, j, lr: (i, 0)),
                      pl.BlockSpec((None, D_MODEL, TN_ZG), lambda i, j, lr: (lr[0], 0, j))],
            out_specs=pl.BlockSpec((TM_ZG, TN_ZG), lambda i, j, lr: (i, j))),
        out_shape=jax.ShapeDtypeStruct((m, NZG), BF),
        compiler_params=_cparams(("arbitrary", "arbitrary")),
        name="zg_proj",
    )(l, h, w_zg)


def _softmax_pv(scores, values):
    m = None
    for s in scores:
        sm = jnp.max(s, axis=-1, keepdims=True)
        m = sm if m is None else jnp.maximum(m, sm)
    acc = None
    for s, v in zip(scores, values):
        o = _dot(jnp.exp2((s - m).astype(BF)), v)
        acc = o if acc is None else acc + o
    return acc[:, :LANES] / acc[:, LANES:]


def _with_ones(v):
    return jnp.concatenate([v, jnp.ones_like(v)], axis=1)


def _diff_lambda(lam_ref, lam_init):
    la = lam_ref[...]
    s01 = jnp.sum(la[0:1] * la[1:2], axis=-1, keepdims=True)
    s23 = jnp.sum(la[2:3] * la[3:4], axis=-1, keepdims=True)
    return jnp.exp(s01) - jnp.exp(s23) + lam_init


def _diff_head(q, ks, vs, lam, subln, lam_init):
    lt = _lane_lt64(q.shape)
    zero = jnp.zeros_like(q)
    o = []
    for qm in (jnp.where(lt, q, zero), jnp.where(lt, zero, q)):
        o.append(_softmax_pv([_dot_nt(qm, k) for k in ks], vs))
    d = o[0] - lam * o[1]
    ms = jnp.mean(d * d, axis=-1, keepdims=True)
    return d * lax.rsqrt(ms + EPS) * subln * (1.0 - lam_init)


def _pair_heads(q, ks, vs, bias=None):
    lt = _lane_lt64(q.shape)
    zero = jnp.zeros_like(q)
    o = []
    for qm in (jnp.where(lt, q, zero), jnp.where(lt, zero, q)):
        sc = [_dot_nt(qm, k) for k in ks]
        if bias is not None:
            sc[0] = sc[0] + bias
        o.append(_softmax_pv(sc, vs))
    return jnp.where(_lane_lt64(o[0].shape), o[0], o[1])


def _ctx_attn_body(l_ref, li_ref, qa_ref, ka_ref, va_ref, qb_ref, kb_ref, vb_ref, qc_ref, kc_ref, vc_ref,
                   qd_ref, kd_ref, vd_ref, lam_ref, subln_ref, oa_ref, ob_ref, oc_ref, od_ref):
    lam_init = li_ref[l_ref[0]]
    lam = _diff_lambda(lam_ref, lam_init)
    subln = subln_ref[...]
    for h in range(H_A):
        sl = slice(h * LANES, (h + 1) * LANES)
        oa_ref[:, sl] = _diff_head(qa_ref[:, sl], [ka_ref[:, sl].astype(BF)],
                                   [_with_ones(va_ref[:, sl].astype(BF))], lam, subln, lam_init).astype(BF)
    for j in range(H_B // 2):
        sl = slice(j * LANES, (j + 1) * LANES)
        ob_ref[:, sl] = _pair_heads(qb_ref[:, sl], [kb_ref[:, sl].astype(BF)],
                                    [_with_ones(vb_ref[:, sl].astype(BF))]).astype(BF)
    for h in range(H_C):
        oc_ref[:, h * LANES:(h + 1) * LANES] = _softmax_pv(
            [_dot_nt(qc_ref[:, 2 * h * LANES:(2 * h + 2) * LANES], kc_ref[:, 2 * h * LANES:(2 * h + 2) * LANES])],
            [_with_ones(vc_ref[:, h * LANES:(h + 1) * LANES])]).astype(BF)
    kd = kd_ref[...].astype(BF)
    vd = _with_ones(vd_ref[...].astype(BF))
    for j in range(H_D // 2):
        sl = slice(j * LANES, (j + 1) * LANES)
        od_ref[:, sl] = _pair_heads(qd_ref[:, sl], [kd], [vd]).astype(BF)


def _ctx_attn(l, lam_tab, pc, lam_a, a_subln):
    row = lambda w: pl.BlockSpec((SEQ, w), lambda b, lr, li: (b, 0))
    ins = [pc["qa"], pc["ka"], pc["va"], pc["qb"], pc["kb"], pc["vb"], pc["qc"], pc["kc"], pc["vc"],
           pc["qd"], pc["kd"], pc["vd"]]
    in_specs = [row(a.shape[1]) for a in ins]
    in_specs += [pl.BlockSpec((None, 4, D_A), lambda b, lr, li: (lr[0], 0, 0)),
                 pl.BlockSpec((None, 1, 2 * D_A), lambda b, lr, li: (lr[0], 0, 0))]
    return pl.pallas_call(
        _ctx_attn_body,
        grid_spec=pltpu.PrefetchScalarGridSpec(
            num_scalar_prefetch=2, grid=(BATCH,), in_specs=in_specs,
            out_specs=[row(512)] * 4),
        out_shape=[jax.ShapeDtypeStruct((N_CTX, 512), BF)] * 4,
        compiler_params=_cparams(("arbitrary",)),
        name="ctx_attn",
    )(l, lam_tab, *ins, lam_a, a_subln)


def _lat_a_body(l_ref, li_ref, q_ref, k_ref, v_ref, kc_ref, vc_ref, lam_ref, subln_ref, o_ref):
    lam_init = li_ref[l_ref[0]]
    lam = _diff_lambda(lam_ref, lam_init)
    subln = subln_ref[...]
    for h in range(H_A):
        sl = slice(h * LANES, (h + 1) * LANES)
        vsl = slice(2 * h * LANES, (2 * h + 2) * LANES)
        o_ref[:, sl] = _diff_head(q_ref[:, sl], [k_ref[:, sl], kc_ref[:, sl].astype(BF)],
                                  [v_ref[:, vsl], _with_ones(vc_ref[:, sl].astype(BF))], lam, subln,
                                  lam_init).astype(BF)


def _lat_specs(tq, q_w, k_w, v_w, ck_w, cv_w, cache_layer_first):
    def im(f):
        return lambda b, i, *pre: f(b, i, pre[0][0])
    cidx = (lambda b, i, l: (l, b, 0, 0)) if cache_layer_first else (lambda b, i, l: (b, l, 0, 0))
    qo = lambda w: pl.BlockSpec((None, tq, w), im(lambda b, i, l: (b, i, 0)))
    kv = lambda w: pl.BlockSpec((None, DEC_SEQ, w), im(lambda b, i, l: (b, 0, 0)), pipeline_mode=pl.Buffered(1))
    cache = lambda w: pl.BlockSpec((None, None, PAST_LEN, w), im(cidx))
    return qo, [qo(q_w), kv(k_w), kv(v_w), cache(ck_w), cache(cv_w)]


def _lat_a(l, lam_tab, q, k, v, cache_k, cache_v, lam_a, a_subln):
    qo, in_specs = _lat_specs(TQ_A, 512, 512, 1024, 512, 512, False)
    in_specs += [pl.BlockSpec((None, 4, D_A), lambda b, i, lr, li: (lr[0], 0, 0)),
                 pl.BlockSpec((None, 1, 2 * D_A), lambda b, i, lr, li: (lr[0], 0, 0))]
    return pl.pallas_call(
        _lat_a_body,
        grid_spec=pltpu.PrefetchScalarGridSpec(
            num_scalar_prefetch=2, grid=(DEC_BATCH, DEC_SEQ // TQ_A), in_specs=in_specs, out_specs=qo(512)),
        out_shape=jax.ShapeDtypeStruct((DEC_BATCH, DEC_SEQ, 512), BF),
        compiler_params=_cparams(("arbitrary",) * 2),
        name="lat_attn_a",
    )(l, lam_tab, q, k, v, cache_k, cache_v, lam_a, a_subln)


def _lat_c_body(l_ref, q_ref, k_ref, v_ref, kc_ref, vc_ref, o_ref):
    for h in range(H_C):
        sl = slice(h * LANES, (h + 1) * LANES)
        sl2 = slice(2 * h * LANES, (2 * h + 2) * LANES)
        vs = [v_ref[:, sl2], _with_ones(vc_ref[:, sl])]
        for r in range(TQ_C // ROW_TILE):
            rows = slice(r * ROW_TILE, (r + 1) * ROW_TILE)
            q = q_ref[rows, sl2]
            o_ref[rows, sl] = _softmax_pv([_dot_nt(q, k_ref[:, sl2]), _dot_nt(q, kc_ref[:, sl2])],
                                          vs).astype(BF)


def _lat_c(l, q, k, v, cache_k, cache_v):
    qo, in_specs = _lat_specs(TQ_C, 1024, 1024, 1024, 1024, 512, True)
    return pl.pallas_call(
        _lat_c_body,
        grid_spec=pltpu.PrefetchScalarGridSpec(
            num_scalar_prefetch=1, grid=(DEC_BATCH, DEC_SEQ // TQ_C), in_specs=in_specs, out_specs=qo(512)),
        out_shape=jax.ShapeDtypeStruct((DEC_BATCH, DEC_SEQ, 512), BF),
        compiler_params=_cparams(("arbitrary",) * 2),
        name="lat_attn_c",
    )(l, q, k, v, cache_k, cache_v)


def _lat_d_body(l_ref, q_ref, k_ref, v_ref, kc_ref, vc_ref, o_ref):
    ks = [k_ref[...], kc_ref[...].astype(BF)]
    vs = [v_ref[...], _with_ones(vc_ref[...].astype(BF))]
    for r in range(TQ_D // ROW_TILE):
        rows = slice(r * ROW_TILE, (r + 1) * ROW_TILE)
        for j in range(H_D // 2):
            sl = slice(j * LANES, (j + 1) * LANES)
            o_ref[rows, sl] = _pair_heads(q_ref[rows, sl], ks, vs).astype(BF)


def _lat_d(l, q, k, v, cache_k, cache_v):
    qo, in_specs = _lat_specs(TQ_D, 512, LANES, 2 * LANES, LANES, LANES, False)
    return pl.pallas_call(
        _lat_d_body,
        grid_spec=pltpu.PrefetchScalarGridSpec(
            num_scalar_prefetch=1, grid=(DEC_BATCH, DEC_SEQ // TQ_D), in_specs=in_specs, out_specs=qo(512)),
        out_shape=jax.ShapeDtypeStruct((DEC_BATCH, DEC_SEQ, 512), BF),
        compiler_params=_cparams(("arbitrary",) * 2),
        name="lat_attn_d",
    )(l, q, k, v, cache_k, cache_v)


def _lat_b_body(l_ref, q_ref, k_ref, v_ref, kc_ref, vc_ref, tab_ref, o_ref):
    i = pl.program_id(1)
    qr0 = i * NB_ROWS
    kr0 = jnp.clip(qr0 - NA_ROWS // 2, 0, ROWS - NB_KROWS)
    start = pl.multiple_of(kr0 * GRID_W, GRID_W)
    n_keys = NB_KROWS * GRID_W
    lt = _lane_lt64((GRID_W, LANES))
    for j in range(H_B // 2):
        sl = slice(j * LANES, (j + 1) * LANES)
        kwin = k_ref[pl.ds(start, n_keys), sl]
        vwin = v_ref[pl.ds(start, n_keys), 2 * j * LANES:(2 * j + 2) * LANES]
        ks = [kwin, kc_ref[:, sl].astype(BF)]
        vs = [vwin, _with_ones(vc_ref[:, sl].astype(BF))]
        q = q_ref[:, sl]
        ltq = _lane_lt64(q.shape)
        zero = jnp.zeros_like(q)
        outs = []
        for half, qm in ((0, jnp.where(ltq, q, zero)), (1, jnp.where(ltq, zero, q))):
            head = 2 * j + half
            rows = []
            for a in range(NB_ROWS):
                qr = qr0 + a
                r0 = jnp.clip(qr - NA_ROWS // 2, 0, ROWS - NA_ROWS)
                tiles = []
                for p in range(NB_KROWS // 2):
                    kr_l = kr0 + 2 * p
                    u = jnp.clip(kr_l - qr + NA_ROWS, 0, 2 * NA_ROWS - 1)
                    pen_l = jnp.where((kr_l >= r0) & (kr_l < r0 + NA_ROWS), 0.0, NEG)
                    pen_r = jnp.where((kr_l + 1 >= r0) & (kr_l + 1 < r0 + NA_ROWS), 0.0, NEG)
                    tiles.append(tab_ref[head, u] + jnp.where(lt, pen_l, pen_r))
                rows.append(jnp.concatenate(tiles, axis=1))
            bias = jnp.concatenate(rows, axis=0)
            sc = [_dot_nt(qm, ks[0]) + bias, _dot_nt(qm, ks[1])]
            outs.append(_softmax_pv(sc, vs))
        o_ref[:, sl] = jnp.where(ltq, outs[0], outs[1]).astype(BF)


def _lat_b(l, q, k, v, cache_k, cache_v, tab):
    nq = NB_ROWS * GRID_W
    kv = pl.BlockSpec((None, DEC_SEQ, 512), lambda b, i, lr: (b, 0, 0))
    vv = pl.BlockSpec((None, DEC_SEQ, 1024), lambda b, i, lr: (b, 0, 0))
    cache = pl.BlockSpec((None, None, PAST_LEN, 512), lambda b, i, lr: (b, lr[0], 0, 0))
    qo = pl.BlockSpec((None, nq, 512), lambda b, i, lr: (b, i, 0))
    return pl.pallas_call(
        _lat_b_body,
        grid_spec=pltpu.PrefetchScalarGridSpec(
            num_scalar_prefetch=1, grid=(DEC_BATCH, DEC_SEQ // nq),
            in_specs=[qo, kv, vv, cache, cache,
                      pl.BlockSpec((None, H_B, 2 * NA_ROWS, GRID_W, LANES), lambda b, i, lr: (lr[0], 0, 0, 0, 0))],
            out_specs=qo),
        out_shape=jax.ShapeDtypeStruct((DEC_BATCH, DEC_SEQ, 512), BF),
        compiler_params=_cparams(("arbitrary",) * 2),
        name="lat_attn_b",
    )(l, q, k, v, cache_k, cache_v, tab)


def _merge_body(l_ref, x_ref, mod_ref, oa_ref, ob_ref, oc_ref, od_ref, z_ref, g0_ref, g1_ref, g2_ref, g3_ref,
                wbr_ref, wout_ref, lng_ref, lnb_ref, o_ref):
    merged = None
    for i, (o_r, g_r) in enumerate(zip((oa_ref, ob_ref, oc_ref, od_ref), (g0_ref, g1_ref, g2_ref, g3_ref))):
        u = (o_r[...].astype(F32) * z_ref[:, i * BRANCH_W:(i + 1) * BRANCH_W].astype(F32)).astype(BF)
        term = g_r[...].astype(F32) * _dot(u, wbr_ref[i])
        merged = term if merged is None else merged + term
    y = _dot(merged.astype(BF), wout_ref[...])
    gate = mod_ref[:, 2 * D_MODEL:3 * D_MODEL]
    r = ALPHA * x_ref[...] + gate * y
    mu = jnp.mean(r, axis=-1, keepdims=True)
    d = r - mu
    var = jnp.mean(d * d, axis=-1, keepdims=True)
    o_ref[...] = d * lax.rsqrt(var + EPS) * lng_ref[...] + lnb_ref[...]


def _merge(l, x, mod4, outs, zg, W, *, latent):
    m = x.shape[0]
    tm = TM_MERGE
    per_b = DEC_SEQ // tm
    cond = (lambda i: 1 + i // per_b) if latent else (lambda i: 0)
    row = lambda w: pl.BlockSpec((tm, w), lambda i, lr: (i, 0))
    zgb = lambda j: pl.BlockSpec((tm, D_MODEL), lambda i, lr: (i, j))
    wfull = lambda a: pl.BlockSpec((None,) + a.shape[1:], lambda i, lr: (lr[0],) + (0,) * (a.ndim - 1),
                                   pipeline_mode=pl.Buffered(1))
    in_specs = [row(D_MODEL),
                pl.BlockSpec((None, None, 1, 3 * D_MODEL), lambda i, lr: (lr[0], cond(i), 0, 0)),
                row(512), row(512), row(512), row(512),
                zgb(0), zgb(1), zgb(2), zgb(3), zgb(4),
                wfull(W["w_br"]), wfull(W["w_out"]), wfull(W["ln_g"]), wfull(W["ln_b"])]
    return pl.pallas_call(
        _merge_body,
        grid_spec=pltpu.PrefetchScalarGridSpec(
            num_scalar_prefetch=1, grid=(m // tm,), in_specs=in_specs, out_specs=row(D_MODEL)),
        out_shape=jax.ShapeDtypeStruct((m, D_MODEL), F32),
        compiler_params=_cparams(("arbitrary",)),
        name="merge_lat" if latent else "merge_ctx",
    )(l, x, mod4, *outs, zg, zg, zg, zg, zg, W["w_br"], W["w_out"], W["ln_g"], W["ln_b"])


def _rope_tables():
    t = jnp.arange(DEC_SEQ)
    row = (t // GRID_W).astype(F32)
    col = (t % GRID_W).astype(F32)
    quarter = D_A // 4
    inv_freq = ROPE_BASE ** (-jnp.arange(quarter, dtype=F32) / quarter)
    ar = row[:, None] * inv_freq
    ac = col[:, None] * inv_freq
    ang = jnp.concatenate([ar, ar, ac, ac], axis=-1)
    cos, sin = jnp.cos(ang), jnp.sin(ang)
    even = (jnp.arange(D_A) // quarter) % 2 == 0
    sa = jnp.where(even, -sin, 0.0)
    sb = jnp.where(even, 0.0, sin)
    tile2 = lambda a: jnp.concatenate([a, a], axis=-1)
    return tile2(cos), tile2(sa), tile2(sb)


def _perm_heads(a, axis):
    shp = a.shape
    a = a.reshape(shp[:axis] + (H_D, D_D) + shp[axis + 1:])
    a = jnp.take(a, jnp.array(GQA_PERM), axis=axis)
    return a.reshape(shp)


def _prep_weights(w_in, c_q_norm, c_kv_norm, w_c_uq, w_c_ukv, d_q_norm, d_k_norm, w_br, w_out, ln_g, ln_b):
    offs = np.concatenate([[0], np.cumsum(IN_SIZES)])
    part = lambda i: w_in[:, :, offs[i]:offs[i + 1]]
    aq, ak, av, bq, bk, bv, cq, ckv, ckr, dq, dk, dv, z, g = (part(i) for i in range(14))
    w_qkv = jnp.concatenate(
        [aq, ak, av, bq, bk, bv, cq, ckv, _perm_heads(dq, 2), dk, dv, ckr,
         jnp.zeros((DEPTH, D_MODEL, LANES - ROPE_DIM), F32)], axis=2).astype(BF)
    z3 = _perm_heads(z[:, :, 3 * BRANCH_W:], 2)
    w_zg = jnp.concatenate([z[:, :, :3 * BRANCH_W], z3, g], axis=2).astype(BF)
    uq = w_c_uq.reshape(DEPTH, Q_RANK, H_C, NOPE_DIM + ROPE_DIM)
    wuq = jnp.concatenate([uq, jnp.zeros((DEPTH, Q_RANK, H_C, 2 * LANES - NOPE_DIM - ROPE_DIM), F32)],
                          axis=3).reshape(DEPTH, Q_RANK, H_C * 2 * LANES).astype(BF)
    ukv = w_c_ukv.reshape(DEPTH, KV_RANK, H_C, NOPE_DIM + V_DIM_C)
    wukv_n = ukv[..., :NOPE_DIM].reshape(DEPTH, KV_RANK, H_C * NOPE_DIM).astype(BF)
    wukv_v = ukv[..., NOPE_DIM:].reshape(DEPTH, KV_RANK, H_C * V_DIM_C).astype(BF)
    wbr = jnp.concatenate([w_br[:, :3], _perm_heads(w_br[:, 3:], 2)], axis=1).astype(BF)
    return {
        "w_qkv": w_qkv, "w_zg": w_zg, "wuq": wuq, "wukv_n": wukv_n, "wukv_v": wukv_v,
        "cqn": c_q_norm.reshape(DEPTH, 1, Q_RANK), "ckvn": c_kv_norm.reshape(DEPTH, 1, KV_RANK),
        "dqn": jnp.tile(d_q_norm, (1, H_D)).reshape(DEPTH, 1, H_D * D_D),
        "dkn": jnp.tile(d_k_norm, (1, G_D)).reshape(DEPTH, 1, G_D * D_D),
        "w_br": wbr, "w_out": w_out.astype(BF),
        "ln_g": ln_g.reshape(DEPTH, 1, D_MODEL), "ln_b": ln_b.reshape(DEPTH, 1, D_MODEL),
    }


def kernel(x_prompt, x_sample, cache_a_k, cache_a_v, cache_b_k, cache_b_v, cache_c_kv, cache_c_kr, cache_d_k,
           cache_d_v, c, c_ctx, w_ada, b_ada, w_in, lam_a, a_subln, b_rpb, c_q_norm, c_kv_norm, w_c_uq, w_c_ukv,
           d_q_norm, d_k_norm, w_br, w_out, ln_g, ln_b):
    W = _prep_weights(w_in, c_q_norm, c_kv_norm, w_c_uq, w_c_ukv, d_q_norm, d_k_norm, w_br, w_out, ln_g, ln_b)
    cond8 = jnp.concatenate([c_ctx[None], c, jnp.zeros((5, D_MODEL), F32)], axis=0)
    mod4 = _adaln(cond8, w_ada, b_ada).reshape(DEPTH, 8, 1, 3 * D_MODEL)
    nb_tab = _bias_table(b_rpb).reshape(DEPTH, H_B, 2 * NA_ROWS, GRID_W, LANES)
    kc_cache, vc_cache = _cache_mla(cache_c_kv, cache_c_kr, W["wukv_n"], W["wukv_v"])
    rope_tabs = _rope_tables()
    lam_tab = jnp.array([0.8 - 0.6 * math.exp(-0.3 * l) for l in range(DEPTH)], F32)
    subln = a_subln.reshape(DEPTH, 1, 2 * D_A)
    flat = lambda a: a.reshape(DEC_BATCH, DEPTH, PAST_LEN, -1)
    ca_k, ca_v, cb_k, cb_v, cd_k, cd_v = (flat(a) for a in (cache_a_k, cache_a_v, cache_b_k, cache_b_v,
                                                               cache_d_k, cache_d_v))
    lat3 = lambda a: a.reshape(DEC_BATCH, DEC_SEQ, a.shape[-1])

    def layer(carry, li):
        xp, xs = carry
        l = li.reshape(1)
        (h_c, qa, qb, qc, kc, vc, qd, ka, va, kb, vb, ckv, kr, kd, vd) = _proj(l, xp, mod4, W, None, rope=False)
        pc = dict(qa=qa, ka=ka, va=va, qb=qb, kb=kb, vb=vb, qc=qc, kc=kc, vc=vc, qd=qd, kd=kd, vd=vd)
        outs_c = _ctx_attn(l, lam_tab, pc, lam_a, subln)
        zg_c = _zg(l, h_c, W["w_zg"])
        xp_new = _merge(l, xp, mod4, outs_c, zg_c, W, latent=False)
        (h_l, lqa, lka, lva, lqb, lkb, lvb, lqc, lkc, lvc, lqd, lkd, lvd) = _proj(l, xs, mod4, W, rope_tabs,
                                                                                   rope=True)
        o_a = _lat_a(l, lam_tab, lat3(lqa), lat3(lka), lat3(lva), ca_k, ca_v, lam_a, subln)
        o_b = _lat_b(l, lat3(lqb), lat3(lkb), lat3(lvb), cb_k, cb_v, nb_tab)
        o_c = _lat_c(l, lat3(lqc), lat3(lkc), lat3(lvc), kc_cache, vc_cache)
        o_d = _lat_d(l, lat3(lqd), lat3(lkd), lat3(lvd), cd_k, cd_v)
        outs_l = [o.reshape(N_LAT, 512) for o in (o_a, o_b, o_c, o_d)]
        zg_l = _zg(l, h_l, W["w_zg"])
        xs_new = _merge(l, xs, mod4, outs_l, zg_l, W, latent=True)
        return (xp_new, xs_new), (ka, va, kb, vb, ckv, kr, kd, vd)

    (xp, xs), caches = lax.scan(layer, (x_prompt.reshape(N_CTX, D_MODEL), x_sample.reshape(N_LAT, D_MODEL)),
                                jnp.arange(DEPTH, dtype=jnp.int32))
    ka, va, kb, vb, ckv, kr, kd, vd = caches

    def out(a, tail):
        return jnp.transpose(a.reshape((DEPTH, BATCH, SEQ) + tail), (1, 0, 2) + tuple(range(3, 3 + len(tail))))

    return (xp.reshape(BATCH, SEQ, D_MODEL), xs.reshape(DEC_BATCH, DEC_SEQ, D_MODEL),
            out(ka, (H_A, 2 * D_A)), out(va, (H_A, 2 * D_A)), out(kb, (H_B, D_B)), out(vb, (H_B, D_B)),
            out(ckv, (KV_RANK,)), out(kr, (ROPE_DIM,)), out(kd, (G_D, D_D)), out(vd, (G_D, D_D)))
```

```python
import functools
import math

import jax
import jax.numpy as jnp
import numpy as np
from jax import lax
from jax.experimental import pallas as pl
from jax.experimental.pallas import tpu as pltpu

D_MODEL = 2048
BATCH = 16
SEQ = 256
DEPTH = 4
DEC_BATCH = 2
DEC_SEQ = 4096
PAST_LEN = 256
GRID_W = 64
ROWS = DEC_SEQ // GRID_W
N_BRANCH = 4
BRANCH_W = 512
H_A, D_A = 4, 64
H_B, D_B = 8, 64
NA_ROWS, NA_COLS = 8, 16
H_C, Q_RANK, KV_RANK, NOPE_DIM, ROPE_DIM, V_DIM_C = 4, 512, 256, 128, 64, 128
H_D, G_D, D_D = 8, 2, 64
ROPE_BASE = 10000.0
EPS = 1e-6
ALPHA = (2 * DEPTH) ** 0.25
IN_SIZES = (512, 512, 512, 512, 512, 512, Q_RANK, KV_RANK, ROPE_DIM, 512, 128, 128,
            N_BRANCH * BRANCH_W, N_BRANCH * D_MODEL)

BF = jnp.bfloat16
F32 = jnp.float32
LANES = 128
MXU_N = 256
LOG2E = 1.4426950408889634
VMEM_LIMIT = 56 * 1024 * 1024
NEG = -1e30

N_CTX = BATCH * SEQ
N_LAT = DEC_BATCH * DEC_SEQ
NQ = 4736
NZG = 10240
GQA_PERM = (0, 4, 1, 5, 2, 6, 3, 7)

O_AQ, O_AK, O_AV, O_BQ, O_BK, O_BV = 0, 512, 1024, 1536, 2048, 2560
O_CQ, O_CKV, O_DQ, O_DK, O_DV, O_KR = 3072, 3584, 3840, 4352, 4480, 4608

TM_PROJ = 256
TM_ZG = 1024
TN_ZG = 2048
TM_MERGE = 256
ROW_TILE = 256
TQ_A = 256
TQ_C = 512
TQ_D = 512
NB_ROWS = 4
NB_KROWS = 12


def _cparams(sem):
    return pltpu.CompilerParams(dimension_semantics=sem, vmem_limit_bytes=VMEM_LIMIT)


def _dot(a, b):
    return jnp.dot(a, b, preferred_element_type=F32)


def _dot_nt(a, b):
    return lax.dot_general(a, b, (((1,), (1,)), ((), ())), preferred_element_type=F32)


def _sigmoid(x):
    return 1.0 / (1.0 + jnp.exp(-x))


def _lane_lt64(shape):
    return lax.broadcasted_iota(jnp.int32, shape, len(shape) - 1) < 64


def _adaln_body(c_ref, w_ref, b_ref, o_ref):
    c = c_ref[...]
    s = (c * _sigmoid(c)).astype(BF)
    o_ref[...] = _dot(s, w_ref[...].astype(BF)) + b_ref[...]


def _adaln(cond8, w_ada, b_ada):
    tn = 1536
    return pl.pallas_call(
        _adaln_body,
        grid=(DEPTH, 3 * D_MODEL // tn),
        in_specs=[pl.BlockSpec((8, D_MODEL), lambda l, j: (0, 0)),
                  pl.BlockSpec((None, D_MODEL, tn), lambda l, j: (l, 0, j)),
                  pl.BlockSpec((None, 1, tn), lambda l, j: (l, 0, j))],
        out_specs=pl.BlockSpec((None, 8, tn), lambda l, j: (l, 0, j)),
        out_shape=jax.ShapeDtypeStruct((DEPTH, 8, 3 * D_MODEL), F32),
        compiler_params=_cparams(("arbitrary", "arbitrary")),
        name="adaln",
    )(cond8, w_ada, b_ada.reshape(DEPTH, 1, 3 * D_MODEL))


def _bias_table_body(rpb_ref, o_ref):
    lh = pl.program_id(0)
    qc = lax.broadcasted_iota(jnp.int32, (GRID_W, LANES), 0)
    lane = lax.broadcasted_iota(jnp.int32, (GRID_W, LANES), 1)
    kc = jnp.bitwise_and(lane, 63)
    c0 = jnp.clip(qc - NA_COLS // 2, 0, GRID_W - NA_COLS)
    col_ok = (kc >= c0) & (kc < c0 + NA_COLS)
    dc = kc - qc + (NA_COLS - 1)
    right = lane >= 64
    n_dr, n_dc = 2 * NA_ROWS - 1, 2 * NA_COLS - 1
    for u in range(n_dr + 1):
        val = jnp.full((GRID_W, LANES), NEG, F32)
        for half, sel in ((0, ~right), (1, right)):
            dr = u - 1 + half
            if 0 <= dr < n_dr:
                for d in range(n_dc):
                    r = rpb_ref[(lh * n_dr + dr) * n_dc + d] * LOG2E
                    val = jnp.where(sel & col_ok & (dc == d), r, val)
        o_ref[u] = val


def _bias_table(b_rpb):
    n = DEPTH * H_B
    return pl.pallas_call(
        _bias_table_body,
        grid_spec=pltpu.PrefetchScalarGridSpec(
            num_scalar_prefetch=1, grid=(n,),
            in_specs=[],
            out_specs=pl.BlockSpec((None, 2 * NA_ROWS, GRID_W, LANES), lambda i, r: (i, 0, 0, 0))),
        out_shape=jax.ShapeDtypeStruct((n, 2 * NA_ROWS, GRID_W, LANES), F32),
        compiler_params=_cparams(("arbitrary",)),
        name="nb_bias_table",
    )(b_rpb.reshape(-1))


def _cache_mla_body(ckv_ref, kr_ref, wn_ref, wv_ref, k_ref, v_ref):
    ckv = ckv_ref[...].astype(BF)
    kn = _dot(ckv, wn_ref[...])
    kr = kr_ref[...]
    k_ref[...] = jnp.concatenate(
        [t for h in range(H_C) for t in (kn[:, h * LANES:(h + 1) * LANES], kr)], axis=1).astype(BF)
    v_ref[...] = _dot(ckv, wv_ref[...]).astype(BF)


def _cache_mla(cache_c_kv, cache_c_kr, wukv_n, wukv_v):
    return pl.pallas_call(
        _cache_mla_body,
        grid=(DEPTH, DEC_BATCH),
        in_specs=[pl.BlockSpec((None, None, PAST_LEN, KV_RANK), lambda l, b: (b, l, 0, 0)),
                  pl.BlockSpec((None, None, PAST_LEN, LANES), lambda l, b: (b, l, 0, 0)),
                  pl.BlockSpec((None, KV_RANK, 512), lambda l, b: (l, 0, 0)),
                  pl.BlockSpec((None, KV_RANK, 512), lambda l, b: (l, 0, 0))],
        out_specs=[pl.BlockSpec((None, None, PAST_LEN, 1024), lambda l, b: (l, b, 0, 0)),
                   pl.BlockSpec((None, None, PAST_LEN, 512), lambda l, b: (l, b, 0, 0))],
        out_shape=[jax.ShapeDtypeStruct((DEPTH, DEC_BATCH, PAST_LEN, 1024), BF),
                   jax.ShapeDtypeStruct((DEPTH, DEC_BATCH, PAST_LEN, 512), BF)],
        compiler_params=_cparams(("arbitrary", "arbitrary")),
        name="cache_mla",
    )(cache_c_kv, jnp.pad(cache_c_kr, ((0, 0), (0, 0), (0, 0), (0, LANES - ROPE_DIM))), wukv_n, wukv_v)


def _rope_tiles(x, cos, sa, sb):
    outs = []
    for j in range(x.shape[1] // LANES):
        t = x[:, j * LANES:(j + 1) * LANES]
        outs.append(t * cos + pltpu.roll(t, LANES - 16, 1) * sa + pltpu.roll(t, 16, 1) * sb)
    return outs[0] if len(outs) == 1 else jnp.concatenate(outs, axis=1)


def _group64_rms(x, g):
    w = x.shape[1]
    r = lax.shift_right_logical(lax.broadcasted_iota(jnp.int32, (w, w), 0), 6)
    c = lax.shift_right_logical(lax.broadcasted_iota(jnp.int32, (w, w), 1), 6)
    bd = jnp.where(r == c, 1.0, 0.0).astype(BF)
    x2 = x * x
    hi = x2.astype(BF)
    lo = (x2 - hi.astype(F32)).astype(BF)
    ms = (_dot(hi, bd) + _dot(lo, bd)) * (1.0 / 64)
    return x * lax.rsqrt(ms + EPS) * g


def _full_rms(x, g):
    ms = jnp.mean(x * x, axis=-1, keepdims=True)
    return x * lax.rsqrt(ms + EPS) * g


def _proj_body(l_ref, x_ref, mod_ref, w_ref, wuq_ref, wun_ref, wuv_ref, cqn_ref, ckvn_ref, dqn_ref, dkn_ref,
               *refs, rope):
    if rope:
        cos_ref, sa_ref, sb_ref = refs[:3]
        refs = refs[3:]
        cos, sa, sb = cos_ref[...], sa_ref[...], sb_ref[...]
        rp = lambda t: _rope_tiles(t, cos, sa, sb)
    else:
        rp = lambda t: t
    x = x_ref[...]
    shift = mod_ref[:, 0:D_MODEL]
    scale = mod_ref[:, D_MODEL:2 * D_MODEL]
    h = (x * (1.0 + scale) + shift).astype(BF)

    acc = _dot(h, w_ref[...])

    def col(o, n):
        return acc[:, o:o + n]

    def with_ones(v):
        ones = jnp.ones((v.shape[0], LANES), BF)
        return jnp.concatenate([t for j in range(v.shape[1] // LANES)
                                for t in (v[:, j * LANES:(j + 1) * LANES].astype(BF), ones)], axis=1)

    qa = rp(col(O_AQ, 512)) * (D_A ** -0.5 * LOG2E)
    ka = rp(col(O_AK, 512))
    va = col(O_AV, 512)
    qb = col(O_BQ, 512) * (D_B ** -0.5 * LOG2E)
    kb = col(O_BK, 512)
    vb = col(O_BV, 512)
    cq = _full_rms(col(O_CQ, Q_RANK), cqn_ref[...]).astype(BF)
    qc_raw = _dot(cq, wuq_ref[...])
    qc_scale = (NOPE_DIM + ROPE_DIM) ** -0.5 * LOG2E
    qc = jnp.concatenate(
        [t for hh in range(H_C) for t in (qc_raw[:, 2 * hh * LANES:(2 * hh + 1) * LANES],
                                          rp(qc_raw[:, (2 * hh + 1) * LANES:(2 * hh + 2) * LANES]))],
        axis=1) * qc_scale
    ckv = _full_rms(col(O_CKV, KV_RANK), ckvn_ref[...])
    ckv_b = ckv.astype(BF)
    kn = _dot(ckv_b, wun_ref[...])
    vc = _dot(ckv_b, wuv_ref[...])
    kr_raw = col(O_KR, LANES)
    kr = rp(kr_raw)
    kc = jnp.concatenate([t for hh in range(H_C) for t in (kn[:, hh * LANES:(hh + 1) * LANES], kr)], axis=1)
    qd = rp(_group64_rms(col(O_DQ, 512), dqn_ref[...])) * (D_D ** -0.5 * LOG2E)
    kd_n = _group64_rms(col(O_DK, LANES), dkn_ref[...])
    kd = rp(kd_n)
    vd = col(O_DV, LANES)

    if rope:
        (h_o, qa_o, ka_o, va_o, qb_o, kb_o, vb_o, qc_o, kc_o, vc_o, qd_o, kd_o, vd_o) = refs
        ka_o[...] = ka.astype(BF)
        va_o[...] = with_ones(va)
        kb_o[...] = kb.astype(BF)
        vb_o[...] = with_ones(vb)
        kd_o[...] = kd.astype(BF)
        vd_o[...] = with_ones(vd)
        vc_o[...] = with_ones(vc)
    else:
        (h_o, qa_o, qb_o, qc_o, kc_o, vc_o, qd_o,
         ka_o, va_o, kb_o, vb_o, ckv_o, kr_o, kd_o, vd_o) = refs[len(CACHE_WIDTHS):]
        ka_o[...] = ka
        va_o[...] = va
        kb_o[...] = kb
        vb_o[...] = vb
        ckv_o[...] = ckv
        kr_o[...] = kr_raw[:, 0:ROPE_DIM]
        kd_o[...] = kd_n
        vd_o[...] = vd
        vc_o[...] = vc.astype(BF)
    h_o[...] = h
    qa_o[...] = qa.astype(BF)
    qb_o[...] = qb.astype(BF)
    qc_o[...] = qc.astype(BF)
    kc_o[...] = kc.astype(BF)
    qd_o[...] = qd.astype(BF)


CACHE_WIDTHS = (512, 512, 512, 512, KV_RANK, ROPE_DIM, LANES, LANES)


def _proj(l, x, mod4, W, extra, *, rope):
    m = x.shape[0]
    tm = TM_PROJ
    per_b = DEC_SEQ // tm
    cond = (lambda i: 1 + i // per_b) if rope else (lambda i: 0)
    row = lambda w: pl.BlockSpec((tm, w), lambda i, lr: (i, 0))
    wfull = lambda a: pl.BlockSpec((None,) + a.shape[1:], lambda i, lr: (lr[0],) + (0,) * (a.ndim - 1),
                                   pipeline_mode=pl.Buffered(1))
    weights = [W["w_qkv"], W["wuq"], W["wukv_n"], W["wukv_v"], W["cqn"], W["ckvn"], W["dqn"], W["dkn"]]
    in_specs = [row(D_MODEL),
                pl.BlockSpec((None, None, 1, 3 * D_MODEL), lambda i, lr: (lr[0], cond(i), 0, 0))]
    in_specs += [wfull(a) for a in weights]
    args = [x, mod4] + weights
    if rope:
        in_specs += [pl.BlockSpec((tm, LANES), lambda i, lr: (i % per_b, 0))] * 3
        widths = [(D_MODEL, BF), (512, BF), (512, BF), (1024, BF), (512, BF), (512, BF), (1024, BF),
                  (1024, BF), (1024, BF), (1024, BF), (512, BF), (LANES, BF), (2 * LANES, BF)]
        aliases = {}
    else:
        assert tm == SEQ
        in_specs += [pl.BlockSpec(memory_space=pl.ANY)] * len(CACHE_WIDTHS)
        widths = [(D_MODEL, BF), (512, BF), (512, BF), (1024, BF), (1024, BF), (512, BF), (512, BF)]
        aliases = {1 + len(args) + k: len(widths) + k for k in range(len(CACHE_WIDTHS))}
    args += list(extra)
    out_specs = [row(w) for w, _ in widths]
    out_shape = [jax.ShapeDtypeStruct((m, w), d) for w, d in widths]
    if not rope:
        out_specs += [pl.BlockSpec((None, None, SEQ, w), lambda i, lr: (i, lr[0], 0, 0)) for w in CACHE_WIDTHS]
        out_shape += [jax.ShapeDtypeStruct((BATCH, DEPTH, SEQ, w), F32) for w in CACHE_WIDTHS]
    return pl.pallas_call(
        functools.partial(_proj_body, rope=rope),
        grid_spec=pltpu.PrefetchScalarGridSpec(
            num_scalar_prefetch=1, grid=(m // tm,), in_specs=in_specs, out_specs=out_specs),
        out_shape=out_shape,
        input_output_aliases=aliases,
        compiler_params=_cparams(("arbitrary",)),
        name="proj_lat" if rope else "proj_ctx",
    )(l, *args)


def _zg_body(l_ref, h_ref, w_ref, o_ref):
    is_z = pl.program_id(0) < (N_BRANCH * BRANCH_W) // TN_ZG

    def run(silu):
        h = h_ref[...]
        for n in range(TN_ZG // MXU_N):
            sl = slice(n * MXU_N, (n + 1) * MXU_N)
            a = _dot(h, w_ref[:, sl])
            s = _sigmoid(a)
            o_ref[:, sl] = ((a * s) if silu else s).astype(BF)

    @pl.when(is_z)
    def _():
        run(True)

    @pl.when(jnp.logical_not(is_z))
    def _():
        run(False)


def _zg(l, h, w_zg):
    m = h.shape[0]
    return pl.pallas_call(
        _zg_body,
        grid_spec=pltpu.PrefetchScalarGridSpec(
            num_scalar_prefetch=1, grid=(NZG // TN_ZG, m // TM_ZG),
            in_specs=[pl.BlockSpec((TM_ZG, D_MODEL), lambda j, i, lr: (i, 0)),
                      pl.BlockSpec((None, D_MODEL, TN_ZG), lambda j, i, lr: (lr[0], 0, j))],
            out_specs=pl.BlockSpec((TM_ZG, TN_ZG), lambda j, i, lr: (i, j))),
        out_shape=jax.ShapeDtypeStruct((m, NZG), BF),
        compiler_params=_cparams(("arbitrary", "arbitrary")),
        name="zg_proj",
    )(l, h, w_zg)


def _softmax_pv(scores, values):
    m = None
    for s in scores:
        sm = jnp.max(s, axis=-1, keepdims=True)
        m = sm if m is None else jnp.maximum(m, sm)
    acc = None
    for s, v in zip(scores, values):
        o = _dot(jnp.exp2((s - m).astype(BF)), v)
        acc = o if acc is None else acc + o
    return acc[:, :LANES] / acc[:, LANES:]


def _with_ones(v):
    return jnp.concatenate([v, jnp.ones_like(v)], axis=1)


def _diff_lambda(lam_ref, lam_init):
    la = lam_ref[...]
    s01 = jnp.sum(la[0:1] * la[1:2], axis=-1, keepdims=True)
    s23 = jnp.sum(la[2:3] * la[3:4], axis=-1, keepdims=True)
    return jnp.exp(s01) - jnp.exp(s23) + lam_init


def _diff_head(q, ks, vs, lam, subln, lam_init):
    lt = _lane_lt64(q.shape)
    zero = jnp.zeros_like(q)
    o = []
    for qm in (jnp.where(lt, q, zero), jnp.where(lt, zero, q)):
        o.append(_softmax_pv([_dot_nt(qm, k) for k in ks], vs))
    d = o[0] - lam * o[1]
    ms = jnp.mean(d * d, axis=-1, keepdims=True)
    return d * lax.rsqrt(ms + EPS) * subln * (1.0 - lam_init)


def _pair_heads(q, ks, vs, bias=None):
    lt = _lane_lt64(q.shape)
    zero = jnp.zeros_like(q)
    o = []
    for qm in (jnp.where(lt, q, zero), jnp.where(lt, zero, q)):
        sc = [_dot_nt(qm, k) for k in ks]
        if bias is not None:
            sc[0] = sc[0] + bias
        o.append(_softmax_pv(sc, vs))
    return jnp.where(_lane_lt64(o[0].shape), o[0], o[1])


def _ctx_attn_body(l_ref, li_ref, qa_ref, ka_ref, va_ref, qb_ref, kb_ref, vb_ref, qc_ref, kc_ref, vc_ref,
                   qd_ref, kd_ref, vd_ref, lam_ref, subln_ref, oa_ref, ob_ref, oc_ref, od_ref):
    lam_init = li_ref[l_ref[0]]
    lam = _diff_lambda(lam_ref, lam_init)
    subln = subln_ref[...]
    for h in range(H_A):
        sl = slice(h * LANES, (h + 1) * LANES)
        oa_ref[:, sl] = _diff_head(qa_ref[:, sl], [ka_ref[:, sl].astype(BF)],
                                   [_with_ones(va_ref[:, sl].astype(BF))], lam, subln, lam_init).astype(BF)
    for j in range(H_B // 2):
        sl = slice(j * LANES, (j + 1) * LANES)
        ob_ref[:, sl] = _pair_heads(qb_ref[:, sl], [kb_ref[:, sl].astype(BF)],
                                    [_with_ones(vb_ref[:, sl].astype(BF))]).astype(BF)
    for h in range(H_C):
        oc_ref[:, h * LANES:(h + 1) * LANES] = _softmax_pv(
            [_dot_nt(qc_ref[:, 2 * h * LANES:(2 * h + 2) * LANES], kc_ref[:, 2 * h * LANES:(2 * h + 2) * LANES])],
            [_with_ones(vc_ref[:, h * LANES:(h + 1) * LANES])]).astype(BF)
    kd = kd_ref[...].astype(BF)
    vd = _with_ones(vd_ref[...].astype(BF))
    for j in range(H_D // 2):
        sl = slice(j * LANES, (j + 1) * LANES)
        od_ref[:, sl] = _pair_heads(qd_ref[:, sl], [kd], [vd]).astype(BF)


def _ctx_attn(l, lam_tab, pc, lam_a, a_subln):
    row = lambda w: pl.BlockSpec((SEQ, w), lambda b, lr, li: (b, 0))
    ins = [pc["qa"], pc["ka"], pc["va"], pc["qb"], pc["kb"], pc["vb"], pc["qc"], pc["kc"], pc["vc"],
           pc["qd"], pc["kd"], pc["vd"]]
    layer_row = lambda w: pl.BlockSpec((None, None, SEQ, w), lambda b, lr, li: (b, lr[0], 0, 0))
    in_specs = [row(a.shape[1]) if a.ndim == 2 else layer_row(a.shape[3]) for a in ins]
    in_specs += [pl.BlockSpec((None, 4, D_A), lambda b, lr, li: (lr[0], 0, 0)),
                 pl.BlockSpec((None, 1, 2 * D_A), lambda b, lr, li: (lr[0], 0, 0))]
    return pl.pallas_call(
        _ctx_attn_body,
        grid_spec=pltpu.PrefetchScalarGridSpec(
            num_scalar_prefetch=2, grid=(BATCH,), in_specs=in_specs,
            out_specs=[row(512)] * 4),
        out_shape=[jax.ShapeDtypeStruct((N_CTX, 512), BF)] * 4,
        compiler_params=_cparams(("arbitrary",)),
        name="ctx_attn",
    )(l, lam_tab, *ins, lam_a, a_subln)


def _lat_a_body(l_ref, li_ref, q_ref, k_ref, v_ref, kc_ref, vc_ref, lam_ref, subln_ref, o_ref):
    lam_init = li_ref[l_ref[0]]
    lam = _diff_lambda(lam_ref, lam_init)
    subln = subln_ref[...]
    for h in range(H_A):
        sl = slice(h * LANES, (h + 1) * LANES)
        vsl = slice(2 * h * LANES, (2 * h + 2) * LANES)
        o_ref[:, sl] = _diff_head(q_ref[:, sl], [k_ref[:, sl], kc_ref[:, sl].astype(BF)],
                                  [v_ref[:, vsl], _with_ones(vc_ref[:, sl].astype(BF))], lam, subln,
                                  lam_init).astype(BF)


def _lat_specs(tq, q_w, k_w, v_w, ck_w, cv_w, cache_layer_first):
    def im(f):
        return lambda b, i, *pre: f(b, i, pre[0][0])
    cidx = (lambda b, i, l: (l, b, 0, 0)) if cache_layer_first else (lambda b, i, l: (b, l, 0, 0))
    qo = lambda w: pl.BlockSpec((None, tq, w), im(lambda b, i, l: (b, i, 0)))
    kv = lambda w: pl.BlockSpec((None, DEC_SEQ, w), im(lambda b, i, l: (b, 0, 0)), pipeline_mode=pl.Buffered(1))
    cache = lambda w: pl.BlockSpec((None, None, PAST_LEN, w), im(cidx))
    return qo, [qo(q_w), kv(k_w), kv(v_w), cache(ck_w), cache(cv_w)]


def _lat_a(l, lam_tab, q, k, v, cache_k, cache_v, lam_a, a_subln):
    qo, in_specs = _lat_specs(TQ_A, 512, 512, 1024, 512, 512, False)
    in_specs += [pl.BlockSpec((None, 4, D_A), lambda b, i, lr, li: (lr[0], 0, 0)),
                 pl.BlockSpec((None, 1, 2 * D_A), lambda b, i, lr, li: (lr[0], 0, 0))]
    return pl.pallas_call(
        _lat_a_body,
        grid_spec=pltpu.PrefetchScalarGridSpec(
            num_scalar_prefetch=2, grid=(DEC_BATCH, DEC_SEQ // TQ_A), in_specs=in_specs, out_specs=qo(512)),
        out_shape=jax.ShapeDtypeStruct((DEC_BATCH, DEC_SEQ, 512), BF),
        compiler_params=_cparams(("arbitrary",) * 2),
        name="lat_attn_a",
    )(l, lam_tab, q, k, v, cache_k, cache_v, lam_a, a_subln)


def _lat_c_body(l_ref, q_ref, k_ref, v_ref, kc_ref, vc_ref, o_ref):
    for h in range(H_C):
        sl = slice(h * LANES, (h + 1) * LANES)
        sl2 = slice(2 * h * LANES, (2 * h + 2) * LANES)
        vs = [v_ref[:, sl2], _with_ones(vc_ref[:, sl])]
        for r in range(TQ_C // ROW_TILE):
            rows = slice(r * ROW_TILE, (r + 1) * ROW_TILE)
            q = q_ref[rows, sl2]
            o_ref[rows, sl] = _softmax_pv([_dot_nt(q, k_ref[:, sl2]), _dot_nt(q, kc_ref[:, sl2])],
                                          vs).astype(BF)


def _lat_c(l, q, k, v, cache_k, cache_v):
    qo, in_specs = _lat_specs(TQ_C, 1024, 1024, 1024, 1024, 512, True)
    return pl.pallas_call(
        _lat_c_body,
        grid_spec=pltpu.PrefetchScalarGridSpec(
            num_scalar_prefetch=1, grid=(DEC_BATCH, DEC_SEQ // TQ_C), in_specs=in_specs, out_specs=qo(512)),
        out_shape=jax.ShapeDtypeStruct((DEC_BATCH, DEC_SEQ, 512), BF),
        compiler_params=_cparams(("arbitrary",) * 2),
        name="lat_attn_c",
    )(l, q, k, v, cache_k, cache_v)


def _lat_d_body(l_ref, q_ref, k_ref, v_ref, kc_ref, vc_ref, o_ref):
    ks = [k_ref[...], kc_ref[...].astype(BF)]
    vs = [v_ref[...], _with_ones(vc_ref[...].astype(BF))]
    for r in range(TQ_D // ROW_TILE):
        rows = slice(r * ROW_TILE, (r + 1) * ROW_TILE)
        for j in range(H_D // 2):
            sl = slice(j * LANES, (j + 1) * LANES)
            o_ref[rows, sl] = _pair_heads(q_ref[rows, sl], ks, vs).astype(BF)


def _lat_d(l, q, k, v, cache_k, cache_v):
    qo, in_specs = _lat_specs(TQ_D, 512, LANES, 2 * LANES, LANES, LANES, False)
    return pl.pallas_call(
        _lat_d_body,
        grid_spec=pltpu.PrefetchScalarGridSpec(
            num_scalar_prefetch=1, grid=(DEC_BATCH, DEC_SEQ // TQ_D), in_specs=in_specs, out_specs=qo(512)),
        out_shape=jax.ShapeDtypeStruct((DEC_BATCH, DEC_SEQ, 512), BF),
        compiler_params=_cparams(("arbitrary",) * 2),
        name="lat_attn_d",
    )(l, q, k, v, cache_k, cache_v)


def _lat_b_body(l_ref, q_ref, k_ref, v_ref, kc_ref, vc_ref, tab_ref, o_ref):
    i = pl.program_id(1)
    qr0 = i * NB_ROWS
    kr0 = jnp.clip(qr0 - NA_ROWS // 2, 0, ROWS - NB_KROWS)
    start = pl.multiple_of(kr0 * GRID_W, GRID_W)
    n_keys = NB_KROWS * GRID_W
    lt = _lane_lt64((GRID_W, LANES))
    for j in range(H_B // 2):
        sl = slice(j * LANES, (j + 1) * LANES)
        kwin = k_ref[pl.ds(start, n_keys), sl]
        vwin = v_ref[pl.ds(start, n_keys), 2 * j * LANES:(2 * j + 2) * LANES]
        ks = [kwin, kc_ref[:, sl].astype(BF)]
        vs = [vwin, _with_ones(vc_ref[:, sl].astype(BF))]
        q = q_ref[:, sl]
        ltq = _lane_lt64(q.shape)
        zero = jnp.zeros_like(q)
        outs = []
        for half, qm in ((0, jnp.where(ltq, q, zero)), (1, jnp.where(ltq, zero, q))):
            head = 2 * j + half
            rows = []
            for a in range(NB_ROWS):
                qr = qr0 + a
                r0 = jnp.clip(qr - NA_ROWS // 2, 0, ROWS - NA_ROWS)
                tiles = []
                for p in range(NB_KROWS // 2):
                    kr_l = kr0 + 2 * p
                    u = jnp.clip(kr_l - qr + NA_ROWS, 0, 2 * NA_ROWS - 1)
                    pen_l = jnp.where((kr_l >= r0) & (kr_l < r0 + NA_ROWS), 0.0, NEG)
                    pen_r = jnp.where((kr_l + 1 >= r0) & (kr_l + 1 < r0 + NA_ROWS), 0.0, NEG)
                    tiles.append(tab_ref[head, u] + jnp.where(lt, pen_l, pen_r))
                rows.append(jnp.concatenate(tiles, axis=1))
            bias = jnp.concatenate(rows, axis=0)
            sc = [_dot_nt(qm, ks[0]) + bias, _dot_nt(qm, ks[1])]
            outs.append(_softmax_pv(sc, vs))
        o_ref[:, sl] = jnp.where(ltq, outs[0], outs[1]).astype(BF)


def _lat_b(l, q, k, v, cache_k, cache_v, tab):
    nq = NB_ROWS * GRID_W
    kv = pl.BlockSpec((None, DEC_SEQ, 512), lambda b, i, lr: (b, 0, 0))
    vv = pl.BlockSpec((None, DEC_SEQ, 1024), lambda b, i, lr: (b, 0, 0))
    cache = pl.BlockSpec((None, None, PAST_LEN, 512), lambda b, i, lr: (b, lr[0], 0, 0))
    qo = pl.BlockSpec((None, nq, 512), lambda b, i, lr: (b, i, 0))
    return pl.pallas_call(
        _lat_b_body,
        grid_spec=pltpu.PrefetchScalarGridSpec(
            num_scalar_prefetch=1, grid=(DEC_BATCH, DEC_SEQ // nq),
            in_specs=[qo, kv, vv, cache, cache,
                      pl.BlockSpec((None, H_B, 2 * NA_ROWS, GRID_W, LANES), lambda b, i, lr: (lr[0], 0, 0, 0, 0))],
            out_specs=qo),
        out_shape=jax.ShapeDtypeStruct((DEC_BATCH, DEC_SEQ, 512), BF),
        compiler_params=_cparams(("arbitrary",) * 2),
        name="lat_attn_b",
    )(l, q, k, v, cache_k, cache_v, tab)


def _merge_body(l_ref, x_ref, mod_ref, oa_ref, ob_ref, oc_ref, od_ref, z_ref, g0_ref, g1_ref, g2_ref, g3_ref,
                wbr_ref, wout_ref, lng_ref, lnb_ref, o_ref):
    merged = None
    for i, (o_r, g_r) in enumerate(zip((oa_ref, ob_ref, oc_ref, od_ref), (g0_ref, g1_ref, g2_ref, g3_ref))):
        u = (o_r[...].astype(F32) * z_ref[:, i * BRANCH_W:(i + 1) * BRANCH_W].astype(F32)).astype(BF)
        term = g_r[...].astype(F32) * _dot(u, wbr_ref[i])
        merged = term if merged is None else merged + term
    y = _dot(merged.astype(BF), wout_ref[...])
    gate = mod_ref[:, 2 * D_MODEL:3 * D_MODEL]
    r = ALPHA * x_ref[...] + gate * y
    mu = jnp.mean(r, axis=-1, keepdims=True)
    d = r - mu
    var = jnp.mean(d * d, axis=-1, keepdims=True)
    o_ref[...] = d * lax.rsqrt(var + EPS) * lng_ref[...] + lnb_ref[...]


def _merge(l, x, mod4, outs, zg, W, *, latent):
    m = x.shape[0]
    tm = TM_MERGE
    per_b = DEC_SEQ // tm
    cond = (lambda i: 1 + i // per_b) if latent else (lambda i: 0)
    row = lambda w: pl.BlockSpec((tm, w), lambda i, lr: (i, 0))
    zgb = lambda j: pl.BlockSpec((tm, D_MODEL), lambda i, lr: (i, j))
    wfull = lambda a: pl.BlockSpec((None,) + a.shape[1:], lambda i, lr: (lr[0],) + (0,) * (a.ndim - 1),
                                   pipeline_mode=pl.Buffered(1))
    in_specs = [row(D_MODEL),
                pl.BlockSpec((None, None, 1, 3 * D_MODEL), lambda i, lr: (lr[0], cond(i), 0, 0)),
                row(512), row(512), row(512), row(512),
                zgb(0), zgb(1), zgb(2), zgb(3), zgb(4),
                wfull(W["w_br"]), wfull(W["w_out"]), wfull(W["ln_g"]), wfull(W["ln_b"])]
    return pl.pallas_call(
        _merge_body,
        grid_spec=pltpu.PrefetchScalarGridSpec(
            num_scalar_prefetch=1, grid=(m // tm,), in_specs=in_specs, out_specs=row(D_MODEL)),
        out_shape=jax.ShapeDtypeStruct((m, D_MODEL), F32),
        input_output_aliases={1: 0},
        compiler_params=_cparams(("arbitrary",)),
        name="merge_lat" if latent else "merge_ctx",
    )(l, x, mod4, *outs, zg, zg, zg, zg, zg, W["w_br"], W["w_out"], W["ln_g"], W["ln_b"])


def _rope_tables():
    t = jnp.arange(DEC_SEQ)
    row = (t // GRID_W).astype(F32)
    col = (t % GRID_W).astype(F32)
    quarter = D_A // 4
    inv_freq = ROPE_BASE ** (-jnp.arange(quarter, dtype=F32) / quarter)
    ar = row[:, None] * inv_freq
    ac = col[:, None] * inv_freq
    ang = jnp.concatenate([ar, ar, ac, ac], axis=-1)
    cos, sin = jnp.cos(ang), jnp.sin(ang)
    even = (jnp.arange(D_A) // quarter) % 2 == 0
    sa = jnp.where(even, -sin, 0.0)
    sb = jnp.where(even, 0.0, sin)
    tile2 = lambda a: jnp.concatenate([a, a], axis=-1)
    return tile2(cos), tile2(sa), tile2(sb)


def _perm_heads(a, axis):
    shp = a.shape
    a = a.reshape(shp[:axis] + (H_D, D_D) + shp[axis + 1:])
    a = jnp.take(a, jnp.array(GQA_PERM), axis=axis)
    return a.reshape(shp)


def _prep_weights(w_in, c_q_norm, c_kv_norm, w_c_uq, w_c_ukv, d_q_norm, d_k_norm, w_br, w_out, ln_g, ln_b):
    offs = np.concatenate([[0], np.cumsum(IN_SIZES)])
    part = lambda i: w_in[:, :, offs[i]:offs[i + 1]]
    aq, ak, av, bq, bk, bv, cq, ckv, ckr, dq, dk, dv, z, g = (part(i) for i in range(14))
    w_qkv = jnp.concatenate(
        [aq, ak, av, bq, bk, bv, cq, ckv, _perm_heads(dq, 2), dk, dv, ckr,
         jnp.zeros((DEPTH, D_MODEL, LANES - ROPE_DIM), F32)], axis=2).astype(BF)
    z3 = _perm_heads(z[:, :, 3 * BRANCH_W:], 2)
    w_zg = jnp.concatenate([z[:, :, :3 * BRANCH_W], z3, g], axis=2).astype(BF)
    uq = w_c_uq.reshape(DEPTH, Q_RANK, H_C, NOPE_DIM + ROPE_DIM)
    wuq = jnp.concatenate([uq, jnp.zeros((DEPTH, Q_RANK, H_C, 2 * LANES - NOPE_DIM - ROPE_DIM), F32)],
                          axis=3).reshape(DEPTH, Q_RANK, H_C * 2 * LANES).astype(BF)
    ukv = w_c_ukv.reshape(DEPTH, KV_RANK, H_C, NOPE_DIM + V_DIM_C)
    wukv_n = ukv[..., :NOPE_DIM].reshape(DEPTH, KV_RANK, H_C * NOPE_DIM).astype(BF)
    wukv_v = ukv[..., NOPE_DIM:].reshape(DEPTH, KV_RANK, H_C * V_DIM_C).astype(BF)
    wbr = jnp.concatenate([w_br[:, :3], _perm_heads(w_br[:, 3:], 2)], axis=1).astype(BF)
    return {
        "w_qkv": w_qkv, "w_zg": w_zg, "wuq": wuq, "wukv_n": wukv_n, "wukv_v": wukv_v,
        "cqn": c_q_norm.reshape(DEPTH, 1, Q_RANK), "ckvn": c_kv_norm.reshape(DEPTH, 1, KV_RANK),
        "dqn": jnp.tile(d_q_norm, (1, H_D)).reshape(DEPTH, 1, H_D * D_D),
        "dkn": jnp.tile(d_k_norm, (1, G_D)).reshape(DEPTH, 1, G_D * D_D),
        "w_br": wbr, "w_out": w_out.astype(BF),
        "ln_g": ln_g.reshape(DEPTH, 1, D_MODEL), "ln_b": ln_b.reshape(DEPTH, 1, D_MODEL),
    }


def kernel(x_prompt, x_sample, cache_a_k, cache_a_v, cache_b_k, cache_b_v, cache_c_kv, cache_c_kr, cache_d_k,
           cache_d_v, c, c_ctx, w_ada, b_ada, w_in, lam_a, a_subln, b_rpb, c_q_norm, c_kv_norm, w_c_uq, w_c_ukv,
           d_q_norm, d_k_norm, w_br, w_out, ln_g, ln_b):
    W = _prep_weights(w_in, c_q_norm, c_kv_norm, w_c_uq, w_c_ukv, d_q_norm, d_k_norm, w_br, w_out, ln_g, ln_b)
    cond8 = jnp.concatenate([c_ctx[None], c, jnp.zeros((5, D_MODEL), F32)], axis=0)
    mod4 = _adaln(cond8, w_ada, b_ada).reshape(DEPTH, 8, 1, 3 * D_MODEL)
    nb_tab = _bias_table(b_rpb).reshape(DEPTH, H_B, 2 * NA_ROWS, GRID_W, LANES)
    kc_cache, vc_cache = _cache_mla(cache_c_kv, cache_c_kr, W["wukv_n"], W["wukv_v"])
    rope_tabs = _rope_tables()
    lam_tab = jnp.array([0.8 - 0.6 * math.exp(-0.3 * l) for l in range(DEPTH)], F32)
    subln = a_subln.reshape(DEPTH, 1, 2 * D_A)
    flat = lambda a: a.reshape(DEC_BATCH, DEPTH, PAST_LEN, -1)
    ca_k, ca_v, cb_k, cb_v, cd_k, cd_v = (flat(a) for a in (cache_a_k, cache_a_v, cache_b_k, cache_b_v,
                                                               cache_d_k, cache_d_v))
    lat3 = lambda a: a.reshape(DEC_BATCH, DEC_SEQ, a.shape[-1])

    def layer(carry, li):
        xp, xs, bufs = carry
        l = li.reshape(1)
        (h_c, qa, qb, qc, kc, vc, qd, ka, va, kb, vb, ckv, kr, kd, vd) = _proj(l, xp, mod4, W, bufs, rope=False)
        pc = dict(qa=qa, ka=ka, va=va, qb=qb, kb=kb, vb=vb, qc=qc, kc=kc, vc=vc, qd=qd, kd=kd, vd=vd)
        outs_c = _ctx_attn(l, lam_tab, pc, lam_a, subln)
        zg_c = _zg(l, h_c, W["w_zg"])
        xp_new = _merge(l, xp, mod4, outs_c, zg_c, W, latent=False)
        (h_l, lqa, lka, lva, lqb, lkb, lvb, lqc, lkc, lvc, lqd, lkd, lvd) = _proj(l, xs, mod4, W, rope_tabs,
                                                                                   rope=True)
        o_a = _lat_a(l, lam_tab, lat3(lqa), lat3(lka), lat3(lva), ca_k, ca_v, lam_a, subln)
        o_b = _lat_b(l, lat3(lqb), lat3(lkb), lat3(lvb), cb_k, cb_v, nb_tab)
        o_c = _lat_c(l, lat3(lqc), lat3(lkc), lat3(lvc), kc_cache, vc_cache)
        o_d = _lat_d(l, lat3(lqd), lat3(lkd), lat3(lvd), cd_k, cd_v)
        outs_l = [o.reshape(N_LAT, 512) for o in (o_a, o_b, o_c, o_d)]
        zg_l = _zg(l, h_l, W["w_zg"])
        xs_new = _merge(l, xs, mod4, outs_l, zg_l, W, latent=True)
        return (xp_new, xs_new, (ka, va, kb, vb, ckv, kr, kd, vd)), None

    bufs0 = tuple(jnp.zeros((BATCH, DEPTH, SEQ, w), F32) for w in CACHE_WIDTHS)
    (xp, xs, caches), _ = lax.scan(
        layer, (x_prompt.reshape(N_CTX, D_MODEL), x_sample.reshape(N_LAT, D_MODEL), bufs0),
        jnp.arange(DEPTH, dtype=jnp.int32))
    ka, va, kb, vb, ckv, kr, kd, vd = caches

    def out(a, tail):
        return a.reshape((BATCH, DEPTH, SEQ) + tail)

    return (xp.reshape(BATCH, SEQ, D_MODEL), xs.reshape(DEC_BATCH, DEC_SEQ, D_MODEL),
            out(ka, (H_A, 2 * D_A)), out(va, (H_A, 2 * D_A)), out(kb, (H_B, D_B)), out(vb, (H_B, D_B)),
            out(ckv, (KV_RANK,)), out(kr, (ROPE_DIM,)), out(kd, (G_D, D_D)), out(vd, (G_D, D_D)))
```

```python
import functools
import math

import jax
import jax.numpy as jnp
import numpy as np
from jax import lax
from jax.experimental import pallas as pl
from jax.experimental.pallas import tpu as pltpu

D_MODEL = 2048
BATCH = 16
SEQ = 256
DEPTH = 4
DEC_BATCH = 2
DEC_SEQ = 4096
PAST_LEN = 256
GRID_W = 64
ROWS = DEC_SEQ // GRID_W
N_BRANCH = 4
BRANCH_W = 512
H_A, D_A = 4, 64
H_B, D_B = 8, 64
NA_ROWS, NA_COLS = 8, 16
H_C, Q_RANK, KV_RANK, NOPE_DIM, ROPE_DIM, V_DIM_C = 4, 512, 256, 128, 64, 128
H_D, G_D, D_D = 8, 2, 64
ROPE_BASE = 10000.0
EPS = 1e-6
ALPHA = (2 * DEPTH) ** 0.25
IN_SIZES = (512, 512, 512, 512, 512, 512, Q_RANK, KV_RANK, ROPE_DIM, 512, 128, 128,
            N_BRANCH * BRANCH_W, N_BRANCH * D_MODEL)

BF = jnp.bfloat16
F32 = jnp.float32
LANES = 128
MXU_N = 256
LOG2E = 1.4426950408889634
VMEM_LIMIT = 56 * 1024 * 1024
NEG = -1e30

N_CTX = BATCH * SEQ
N_LAT = DEC_BATCH * DEC_SEQ
NQ = 4736
NZG = 10240
GQA_PERM = (0, 4, 1, 5, 2, 6, 3, 7)

O_AQ, O_AK, O_AV, O_BQ, O_BK, O_BV = 0, 512, 1024, 1536, 2048, 2560
O_CQ, O_CKV, O_DQ, O_DK, O_DV, O_KR = 3072, 3584, 3840, 4352, 4480, 4608

TM_PROJ = 256
TM_ZG = 1024
TN_ZG = 2048
TM_MERGE = 256
ROW_TILE = 256
TQ_A = 512
TQ_C = 512
TQ_D = 512
TR_REPACK = 128
NB_ROWS = 4
NB_KROWS = 12


def _cparams(sem):
    return pltpu.CompilerParams(dimension_semantics=sem, vmem_limit_bytes=VMEM_LIMIT)


def _dot(a, b):
    return jnp.dot(a, b, preferred_element_type=F32)


def _dot_nt(a, b):
    return lax.dot_general(a, b, (((1,), (1,)), ((), ())), preferred_element_type=F32)


def _sigmoid(x):
    return 1.0 / (1.0 + jnp.exp(-x))


def _lane_lt64(shape):
    return lax.broadcasted_iota(jnp.int32, shape, len(shape) - 1) < 64


def _adaln_body(c_ref, w_ref, b_ref, o_ref):
    c = c_ref[...]
    s = (c * _sigmoid(c)).astype(BF)
    o_ref[...] = _dot(s, w_ref[...].astype(BF)) + b_ref[...]


def _adaln(cond8, w_ada, b_ada):
    tn = 1536
    return pl.pallas_call(
        _adaln_body,
        grid=(DEPTH, 3 * D_MODEL // tn),
        in_specs=[pl.BlockSpec((8, D_MODEL), lambda l, j: (0, 0)),
                  pl.BlockSpec((None, D_MODEL, tn), lambda l, j: (l, 0, j)),
                  pl.BlockSpec((None, 1, tn), lambda l, j: (l, 0, j))],
        out_specs=pl.BlockSpec((None, 8, tn), lambda l, j: (l, 0, j)),
        out_shape=jax.ShapeDtypeStruct((DEPTH, 8, 3 * D_MODEL), F32),
        compiler_params=_cparams(("arbitrary", "arbitrary")),
        name="adaln",
    )(cond8, w_ada, b_ada.reshape(DEPTH, 1, 3 * D_MODEL))


def _bias_table_body(rpb_ref, o_ref):
    lh = pl.program_id(0)
    qc = lax.broadcasted_iota(jnp.int32, (GRID_W, LANES), 0)
    lane = lax.broadcasted_iota(jnp.int32, (GRID_W, LANES), 1)
    kc = jnp.bitwise_and(lane, 63)
    c0 = jnp.clip(qc - NA_COLS // 2, 0, GRID_W - NA_COLS)
    col_ok = (kc >= c0) & (kc < c0 + NA_COLS)
    dc = kc - qc + (NA_COLS - 1)
    right = lane >= 64
    n_dr, n_dc = 2 * NA_ROWS - 1, 2 * NA_COLS - 1
    for u in range(n_dr + 1):
        val = jnp.full((GRID_W, LANES), NEG, F32)
        for half, sel in ((0, ~right), (1, right)):
            dr = u - 1 + half
            if 0 <= dr < n_dr:
                for d in range(n_dc):
                    r = rpb_ref[(lh * n_dr + dr) * n_dc + d] * LOG2E
                    val = jnp.where(sel & col_ok & (dc == d), r, val)
        o_ref[u] = val


def _bias_table(b_rpb):
    n = DEPTH * H_B
    return pl.pallas_call(
        _bias_table_body,
        grid_spec=pltpu.PrefetchScalarGridSpec(
            num_scalar_prefetch=1, grid=(n,),
            in_specs=[],
            out_specs=pl.BlockSpec((None, 2 * NA_ROWS, GRID_W, LANES), lambda i, r: (i, 0, 0, 0))),
        out_shape=jax.ShapeDtypeStruct((n, 2 * NA_ROWS, GRID_W, LANES), F32),
        compiler_params=_cparams(("arbitrary",)),
        name="nb_bias_table",
    )(b_rpb.reshape(-1))


def _cache_mla_body(ckv_ref, kr_ref, wn_ref, wv_ref, k_ref, v_ref):
    ckv = ckv_ref[...].astype(BF)
    kn = _dot(ckv, wn_ref[...])
    kr = kr_ref[...]
    k_ref[...] = jnp.concatenate(
        [t for h in range(H_C) for t in (kn[:, h * LANES:(h + 1) * LANES], kr)], axis=1).astype(BF)
    v_ref[...] = _dot(ckv, wv_ref[...]).astype(BF)


def _cache_mla(cache_c_kv, cache_c_kr, wukv_n, wukv_v):
    return pl.pallas_call(
        _cache_mla_body,
        grid=(DEPTH, DEC_BATCH),
        in_specs=[pl.BlockSpec((None, None, PAST_LEN, KV_RANK), lambda l, b: (b, l, 0, 0)),
                  pl.BlockSpec((None, None, PAST_LEN, LANES), lambda l, b: (b, l, 0, 0)),
                  pl.BlockSpec((None, KV_RANK, 512), lambda l, b: (l, 0, 0)),
                  pl.BlockSpec((None, KV_RANK, 512), lambda l, b: (l, 0, 0))],
        out_specs=[pl.BlockSpec((None, None, PAST_LEN, 1024), lambda l, b: (l, b, 0, 0)),
                   pl.BlockSpec((None, None, PAST_LEN, 512), lambda l, b: (l, b, 0, 0))],
        out_shape=[jax.ShapeDtypeStruct((DEPTH, DEC_BATCH, PAST_LEN, 1024), BF),
                   jax.ShapeDtypeStruct((DEPTH, DEC_BATCH, PAST_LEN, 512), BF)],
        compiler_params=_cparams(("arbitrary", "arbitrary")),
        name="cache_mla",
    )(cache_c_kv, jnp.pad(cache_c_kr, ((0, 0), (0, 0), (0, 0), (0, LANES - ROPE_DIM))), wukv_n, wukv_v)


def _rope_tiles(x, cos, sa, sb):
    outs = []
    for j in range(x.shape[1] // LANES):
        t = x[:, j * LANES:(j + 1) * LANES]
        outs.append(t * cos + pltpu.roll(t, LANES - 16, 1) * sa + pltpu.roll(t, 16, 1) * sb)
    return outs[0] if len(outs) == 1 else jnp.concatenate(outs, axis=1)


def _group64_rms(x, g):
    w = x.shape[1]
    r = lax.shift_right_logical(lax.broadcasted_iota(jnp.int32, (w, w), 0), 6)
    c = lax.shift_right_logical(lax.broadcasted_iota(jnp.int32, (w, w), 1), 6)
    bd = jnp.where(r == c, 1.0, 0.0).astype(BF)
    x2 = x * x
    hi = x2.astype(BF)
    lo = (x2 - hi.astype(F32)).astype(BF)
    ms = (_dot(hi, bd) + _dot(lo, bd)) * (1.0 / 64)
    return x * lax.rsqrt(ms + EPS) * g


def _full_rms(x, g):
    ms = jnp.mean(x * x, axis=-1, keepdims=True)
    return x * lax.rsqrt(ms + EPS) * g


def _proj_body(l_ref, x_ref, mod_ref, w_ref, wuq_ref, wun_ref, wuv_ref, cqn_ref, ckvn_ref, dqn_ref, dkn_ref,
               *refs, rope):
    if rope:
        cos_ref, sa_ref, sb_ref = refs[:3]
        refs = refs[3:]
        cos, sa, sb = cos_ref[...], sa_ref[...], sb_ref[...]
        rp = lambda t: _rope_tiles(t, cos, sa, sb)
    else:
        rp = lambda t: t
    x = x_ref[...]
    shift = mod_ref[:, 0:D_MODEL]
    scale = mod_ref[:, D_MODEL:2 * D_MODEL]
    h = (x * (1.0 + scale) + shift).astype(BF)

    acc = _dot(h, w_ref[...])

    def col(o, n):
        return acc[:, o:o + n]

    def with_ones(v):
        ones = jnp.ones((v.shape[0], LANES), BF)
        return jnp.concatenate([t for j in range(v.shape[1] // LANES)
                                for t in (v[:, j * LANES:(j + 1) * LANES].astype(BF), ones)], axis=1)

    qa = rp(col(O_AQ, 512)) * (D_A ** -0.5 * LOG2E)
    ka = rp(col(O_AK, 512))
    va = col(O_AV, 512)
    qb = col(O_BQ, 512) * (D_B ** -0.5 * LOG2E)
    kb = col(O_BK, 512)
    vb = col(O_BV, 512)
    cq = _full_rms(col(O_CQ, Q_RANK), cqn_ref[...]).astype(BF)
    qc_raw = _dot(cq, wuq_ref[...])
    qc_scale = (NOPE_DIM + ROPE_DIM) ** -0.5 * LOG2E
    qc = jnp.concatenate(
        [t for hh in range(H_C) for t in (qc_raw[:, 2 * hh * LANES:(2 * hh + 1) * LANES],
                                          rp(qc_raw[:, (2 * hh + 1) * LANES:(2 * hh + 2) * LANES]))],
        axis=1) * qc_scale
    ckv = _full_rms(col(O_CKV, KV_RANK), ckvn_ref[...])
    ckv_b = ckv.astype(BF)
    kn = _dot(ckv_b, wun_ref[...])
    vc = _dot(ckv_b, wuv_ref[...])
    kr_raw = col(O_KR, LANES)
    kr = rp(kr_raw)
    kc = jnp.concatenate([t for hh in range(H_C) for t in (kn[:, hh * LANES:(hh + 1) * LANES], kr)], axis=1)
    qd = rp(_group64_rms(col(O_DQ, 512), dqn_ref[...])) * (D_D ** -0.5 * LOG2E)
    kd_n = _group64_rms(col(O_DK, LANES), dkn_ref[...])
    kd = rp(kd_n)
    vd = col(O_DV, LANES)

    if rope:
        (h_o, qa_o, ka_o, va_o, qb_o, kb_o, vb_o, qc_o, kc_o, vc_o, qd_o, kd_o, vd_o) = refs
        ka_o[...] = ka.astype(BF)
        va_o[...] = with_ones(va)
        kb_o[...] = kb.astype(BF)
        vb_o[...] = with_ones(vb)
        kd_o[...] = kd.astype(BF)
        vd_o[...] = with_ones(vd)
        vc_o[...] = with_ones(vc)
    else:
        (h_o, qa_o, qb_o, qc_o, kc_o, vc_o, qd_o,
         ka_o, va_o, kb_o, vb_o, ckv_o, kr_o, kd_o, vd_o) = refs[len(CACHE_WIDTHS):]
        ka_o[...] = ka
        va_o[...] = va
        kb_o[...] = kb
        vb_o[...] = vb
        ckv_o[...] = ckv
        kr_o[...] = kr_raw[:, 0:ROPE_DIM]
        kd_o[...] = kd_n
        vd_o[...] = vd
        vc_o[...] = vc.astype(BF)
    h_o[...] = h
    qa_o[...] = qa.astype(BF)
    qb_o[...] = qb.astype(BF)
    qc_o[...] = qc.astype(BF)
    kc_o[...] = kc.astype(BF)
    qd_o[...] = qd.astype(BF)


CACHE_WIDTHS = (512, 512, 512, 512, KV_RANK, ROPE_DIM, LANES, LANES)


def _proj(l, x, mod4, W, extra, *, rope):
    m = x.shape[0]
    tm = TM_PROJ
    per_b = DEC_SEQ // tm
    cond = (lambda i: 1 + i // per_b) if rope else (lambda i: 0)
    row = lambda w: pl.BlockSpec((tm, w), lambda i, lr: (i, 0))
    wfull = lambda a: pl.BlockSpec((None,) + a.shape[1:], lambda i, lr: (lr[0],) + (0,) * (a.ndim - 1),
                                   pipeline_mode=pl.Buffered(1))
    weights = [W["w_qkv"], W["wuq"], W["wukv_n"], W["wukv_v"], W["cqn"], W["ckvn"], W["dqn"], W["dkn"]]
    in_specs = [row(D_MODEL),
                pl.BlockSpec((None, None, 1, 3 * D_MODEL), lambda i, lr: (lr[0], cond(i), 0, 0))]
    in_specs += [wfull(a) for a in weights]
    args = [x, mod4] + weights
    if rope:
        in_specs += [pl.BlockSpec((tm, LANES), lambda i, lr: (i % per_b, 0))] * 3
        widths = [(D_MODEL, BF), (512, BF), (512, BF), (1024, BF), (512, BF), (512, BF), (1024, BF),
                  (1024, BF), (1024, BF), (1024, BF), (512, BF), (LANES, BF), (2 * LANES, BF)]
        aliases = {}
    else:
        assert tm == SEQ
        in_specs += [pl.BlockSpec(memory_space=pl.ANY)] * len(CACHE_WIDTHS)
        widths = [(D_MODEL, BF), (512, BF), (512, BF), (1024, BF), (1024, BF), (512, BF), (512, BF)]
        aliases = {1 + len(args) + k: len(widths) + k for k in range(len(CACHE_WIDTHS))}
    args += list(extra)
    out_specs = [row(w) for w, _ in widths]
    out_shape = [jax.ShapeDtypeStruct((m, w), d) for w, d in widths]
    if not rope:
        out_specs += [pl.BlockSpec((None, None, SEQ, w), lambda i, lr: (i, lr[0], 0, 0)) for w in CACHE_WIDTHS]
        out_shape += [jax.ShapeDtypeStruct((BATCH, DEPTH, SEQ, w), F32) for w in CACHE_WIDTHS]
    return pl.pallas_call(
        functools.partial(_proj_body, rope=rope),
        grid_spec=pltpu.PrefetchScalarGridSpec(
            num_scalar_prefetch=1, grid=(m // tm,), in_specs=in_specs, out_specs=out_specs),
        out_shape=out_shape,
        input_output_aliases=aliases,
        compiler_params=_cparams(("arbitrary",)),
        name="proj_lat" if rope else "proj_ctx",
    )(l, *args)


def _zg_body(l_ref, h_ref, w_ref, o_ref):
    is_z = pl.program_id(0) < (N_BRANCH * BRANCH_W) // TN_ZG

    def run(silu):
        h = h_ref[...]
        for n in range(TN_ZG // MXU_N):
            sl = slice(n * MXU_N, (n + 1) * MXU_N)
            a = _dot(h, w_ref[:, sl])
            s = _sigmoid(a)
            o_ref[:, sl] = ((a * s) if silu else s).astype(BF)

    @pl.when(is_z)
    def _():
        run(True)

    @pl.when(jnp.logical_not(is_z))
    def _():
        run(False)


def _zg(l, h, w_zg):
    m = h.shape[0]
    return pl.pallas_call(
        _zg_body,
        grid_spec=pltpu.PrefetchScalarGridSpec(
            num_scalar_prefetch=1, grid=(NZG // TN_ZG, m // TM_ZG),
            in_specs=[pl.BlockSpec((TM_ZG, D_MODEL), lambda j, i, lr: (i, 0)),
                      pl.BlockSpec((None, D_MODEL, TN_ZG), lambda j, i, lr: (lr[0], 0, j))],
            out_specs=pl.BlockSpec((TM_ZG, TN_ZG), lambda j, i, lr: (i, j))),
        out_shape=jax.ShapeDtypeStruct((m, NZG), BF),
        compiler_params=_cparams(("arbitrary", "arbitrary")),
        name="zg_proj",
    )(l, h, w_zg)


def _softmax_pv(scores, values):
    m = None
    for s in scores:
        sm = jnp.max(s, axis=-1, keepdims=True)
        m = sm if m is None else jnp.maximum(m, sm)
    acc = None
    for s, v in zip(scores, values):
        o = _dot(jnp.exp2((s - m).astype(BF)), v)
        acc = o if acc is None else acc + o
    return acc[:, :LANES] / acc[:, LANES:]


def _with_ones(v):
    return jnp.concatenate([v, jnp.ones_like(v)], axis=1)


def _diff_lambda(lam_ref, lam_init):
    la = lam_ref[...]
    s01 = jnp.sum(la[0:1] * la[1:2], axis=-1, keepdims=True)
    s23 = jnp.sum(la[2:3] * la[3:4], axis=-1, keepdims=True)
    return jnp.exp(s01) - jnp.exp(s23) + lam_init


def _diff_head(q, ks, vs, lam, subln, lam_init):
    lt = _lane_lt64(q.shape)
    zero = jnp.zeros_like(q)
    o = []
    for qm in (jnp.where(lt, q, zero), jnp.where(lt, zero, q)):
        o.append(_softmax_pv([_dot_nt(qm, k) for k in ks], vs))
    d = o[0] - lam * o[1]
    ms = jnp.mean(d * d, axis=-1, keepdims=True)
    return d * lax.rsqrt(ms + EPS) * subln * (1.0 - lam_init)


def _pair_heads(q, ks, vs, bias=None):
    lt = _lane_lt64(q.shape)
    zero = jnp.zeros_like(q)
    o = []
    for qm in (jnp.where(lt, q, zero), jnp.where(lt, zero, q)):
        sc = [_dot_nt(qm, k) for k in ks]
        if bias is not None:
            sc[0] = sc[0] + bias
        o.append(_softmax_pv(sc, vs))
    return jnp.where(_lane_lt64(o[0].shape), o[0], o[1])


def _ctx_attn_body(l_ref, li_ref, qa_ref, ka_ref, va_ref, qb_ref, kb_ref, vb_ref, qc_ref, kc_ref, vc_ref,
                   qd_ref, kd_ref, vd_ref, lam_ref, subln_ref, oa_ref, ob_ref, oc_ref, od_ref):
    lam_init = li_ref[l_ref[0]]
    lam = _diff_lambda(lam_ref, lam_init)
    subln = subln_ref[...]
    for h in range(H_A):
        sl = slice(h * LANES, (h + 1) * LANES)
        oa_ref[:, sl] = _diff_head(qa_ref[:, sl], [ka_ref[:, sl].astype(BF)],
                                   [_with_ones(va_ref[:, sl].astype(BF))], lam, subln, lam_init).astype(BF)
    for j in range(H_B // 2):
        sl = slice(j * LANES, (j + 1) * LANES)
        ob_ref[:, sl] = _pair_heads(qb_ref[:, sl], [kb_ref[:, sl].astype(BF)],
                                    [_with_ones(vb_ref[:, sl].astype(BF))]).astype(BF)
    for h in range(H_C):
        oc_ref[:, h * LANES:(h + 1) * LANES] = _softmax_pv(
            [_dot_nt(qc_ref[:, 2 * h * LANES:(2 * h + 2) * LANES], kc_ref[:, 2 * h * LANES:(2 * h + 2) * LANES])],
            [_with_ones(vc_ref[:, h * LANES:(h + 1) * LANES])]).astype(BF)
    kd = kd_ref[...].astype(BF)
    vd = _with_ones(vd_ref[...].astype(BF))
    for j in range(H_D // 2):
        sl = slice(j * LANES, (j + 1) * LANES)
        od_ref[:, sl] = _pair_heads(qd_ref[:, sl], [kd], [vd]).astype(BF)


def _ctx_attn(l, lam_tab, pc, lam_a, a_subln):
    row = lambda w: pl.BlockSpec((SEQ, w), lambda b, lr, li: (b, 0))
    ins = [pc["qa"], pc["ka"], pc["va"], pc["qb"], pc["kb"], pc["vb"], pc["qc"], pc["kc"], pc["vc"],
           pc["qd"], pc["kd"], pc["vd"]]
    layer_row = lambda w: pl.BlockSpec((None, None, SEQ, w), lambda b, lr, li: (b, lr[0], 0, 0))
    in_specs = [row(a.shape[1]) if a.ndim == 2 else layer_row(a.shape[3]) for a in ins]
    in_specs += [pl.BlockSpec((None, 4, D_A), lambda b, lr, li: (lr[0], 0, 0)),
                 pl.BlockSpec((None, 1, 2 * D_A), lambda b, lr, li: (lr[0], 0, 0))]
    return pl.pallas_call(
        _ctx_attn_body,
        grid_spec=pltpu.PrefetchScalarGridSpec(
            num_scalar_prefetch=2, grid=(BATCH,), in_specs=in_specs,
            out_specs=[row(512)] * 4),
        out_shape=[jax.ShapeDtypeStruct((N_CTX, 512), BF)] * 4,
        compiler_params=_cparams(("arbitrary",)),
        name="ctx_attn",
    )(l, lam_tab, *ins, lam_a, a_subln)


def _lat_a_body(l_ref, li_ref, q_ref, k_ref, v_ref, kc_ref, vc_ref, lam_ref, subln_ref, o_ref):
    lam_init = li_ref[l_ref[0]]
    lam = _diff_lambda(lam_ref, lam_init)
    subln = subln_ref[...]
    for h in range(H_A):
        sl = slice(h * LANES, (h + 1) * LANES)
        vsl = slice(2 * h * LANES, (2 * h + 2) * LANES)
        ks = [k_ref[:, sl], kc_ref[:, sl].astype(BF)]
        vs = [v_ref[:, vsl], _with_ones(vc_ref[:, sl].astype(BF))]
        for r in range(TQ_A // ROW_TILE):
            rows = slice(r * ROW_TILE, (r + 1) * ROW_TILE)
            o_ref[rows, sl] = _diff_head(q_ref[rows, sl], ks, vs, lam, subln, lam_init).astype(BF)


def _lat_specs(tq, q_w, k_w, v_w, ck_w, cv_w, cache_layer_first):
    def im(f):
        return lambda b, i, *pre: f(b, i, pre[0][0])
    cidx = (lambda b, i, l: (l, b, 0, 0)) if cache_layer_first else (lambda b, i, l: (b, l, 0, 0))
    qo = lambda w: pl.BlockSpec((None, tq, w), im(lambda b, i, l: (b, i, 0)))
    kv = lambda w: pl.BlockSpec((None, DEC_SEQ, w), im(lambda b, i, l: (b, 0, 0)), pipeline_mode=pl.Buffered(1))
    cache = lambda w: pl.BlockSpec((None, None, PAST_LEN, w), im(cidx))
    return qo, [qo(q_w), kv(k_w), kv(v_w), cache(ck_w), cache(cv_w)]


def _lat_a(l, lam_tab, q, k, v, cache_k, cache_v, lam_a, a_subln):
    qo, in_specs = _lat_specs(TQ_A, 512, 512, 1024, 512, 512, False)
    in_specs += [pl.BlockSpec((None, 4, D_A), lambda b, i, lr, li: (lr[0], 0, 0)),
                 pl.BlockSpec((None, 1, 2 * D_A), lambda b, i, lr, li: (lr[0], 0, 0))]
    return pl.pallas_call(
        _lat_a_body,
        grid_spec=pltpu.PrefetchScalarGridSpec(
            num_scalar_prefetch=2, grid=(DEC_BATCH, DEC_SEQ // TQ_A), in_specs=in_specs, out_specs=qo(512)),
        out_shape=jax.ShapeDtypeStruct((DEC_BATCH, DEC_SEQ, 512), BF),
        compiler_params=_cparams(("arbitrary",) * 2),
        name="lat_attn_a",
    )(l, lam_tab, q, k, v, cache_k, cache_v, lam_a, a_subln)


def _lat_c_body(l_ref, q_ref, k_ref, v_ref, kc_ref, vc_ref, o_ref):
    for h in range(H_C):
        sl = slice(h * LANES, (h + 1) * LANES)
        sl2 = slice(2 * h * LANES, (2 * h + 2) * LANES)
        vs = [v_ref[:, sl2], _with_ones(vc_ref[:, sl])]
        for r in range(TQ_C // ROW_TILE):
            rows = slice(r * ROW_TILE, (r + 1) * ROW_TILE)
            q = q_ref[rows, sl2]
            o_ref[rows, sl] = _softmax_pv([_dot_nt(q, k_ref[:, sl2]), _dot_nt(q, kc_ref[:, sl2])],
                                          vs).astype(BF)


def _lat_c(l, q, k, v, cache_k, cache_v):
    qo, in_specs = _lat_specs(TQ_C, 1024, 1024, 1024, 1024, 512, True)
    return pl.pallas_call(
        _lat_c_body,
        grid_spec=pltpu.PrefetchScalarGridSpec(
            num_scalar_prefetch=1, grid=(DEC_BATCH, DEC_SEQ // TQ_C), in_specs=in_specs, out_specs=qo(512)),
        out_shape=jax.ShapeDtypeStruct((DEC_BATCH, DEC_SEQ, 512), BF),
        compiler_params=_cparams(("arbitrary",) * 2),
        name="lat_attn_c",
    )(l, q, k, v, cache_k, cache_v)


def _lat_d_body(l_ref, q_ref, k_ref, v_ref, kc_ref, vc_ref, o_ref):
    ks = [k_ref[...], kc_ref[...].astype(BF)]
    vs = [v_ref[...], _with_ones(vc_ref[...].astype(BF))]
    for r in range(TQ_D // ROW_TILE):
        rows = slice(r * ROW_TILE, (r + 1) * ROW_TILE)
        for j in range(H_D // 2):
            sl = slice(j * LANES, (j + 1) * LANES)
            o_ref[rows, sl] = _pair_heads(q_ref[rows, sl], ks, vs).astype(BF)


def _lat_d(l, q, k, v, cache_k, cache_v):
    qo, in_specs = _lat_specs(TQ_D, 512, LANES, 2 * LANES, LANES, LANES, False)
    return pl.pallas_call(
        _lat_d_body,
        grid_spec=pltpu.PrefetchScalarGridSpec(
            num_scalar_prefetch=1, grid=(DEC_BATCH, DEC_SEQ // TQ_D), in_specs=in_specs, out_specs=qo(512)),
        out_shape=jax.ShapeDtypeStruct((DEC_BATCH, DEC_SEQ, 512), BF),
        compiler_params=_cparams(("arbitrary",) * 2),
        name="lat_attn_d",
    )(l, q, k, v, cache_k, cache_v)


def _lat_b_body(l_ref, q_ref, k_ref, v_ref, kc_ref, vc_ref, tab_ref, o_ref):
    i = pl.program_id(1)
    qr0 = i * NB_ROWS
    kr0 = jnp.clip(qr0 - NA_ROWS // 2, 0, ROWS - NB_KROWS)
    start = pl.multiple_of(kr0 * GRID_W, GRID_W)
    n_keys = NB_KROWS * GRID_W
    lt = _lane_lt64((GRID_W, LANES))
    for j in range(H_B // 2):
        sl = slice(j * LANES, (j + 1) * LANES)
        kwin = k_ref[pl.ds(start, n_keys), sl]
        vwin = v_ref[pl.ds(start, n_keys), 2 * j * LANES:(2 * j + 2) * LANES]
        ks = [kwin, kc_ref[:, sl].astype(BF)]
        vs = [vwin, _with_ones(vc_ref[:, sl].astype(BF))]
        q = q_ref[:, sl]
        ltq = _lane_lt64(q.shape)
        zero = jnp.zeros_like(q)
        outs = []
        for half, qm in ((0, jnp.where(ltq, q, zero)), (1, jnp.where(ltq, zero, q))):
            head = 2 * j + half
            rows = []
            for a in range(NB_ROWS):
                qr = qr0 + a
                r0 = jnp.clip(qr - NA_ROWS // 2, 0, ROWS - NA_ROWS)
                tiles = []
                for p in range(NB_KROWS // 2):
                    kr_l = kr0 + 2 * p
                    u = jnp.clip(kr_l - qr + NA_ROWS, 0, 2 * NA_ROWS - 1)
                    pen_l = jnp.where((kr_l >= r0) & (kr_l < r0 + NA_ROWS), 0.0, NEG)
                    pen_r = jnp.where((kr_l + 1 >= r0) & (kr_l + 1 < r0 + NA_ROWS), 0.0, NEG)
                    tiles.append(tab_ref[head, u] + jnp.where(lt, pen_l, pen_r))
                rows.append(jnp.concatenate(tiles, axis=1))
            bias = jnp.concatenate(rows, axis=0)
            sc = [_dot_nt(qm, ks[0]) + bias, _dot_nt(qm, ks[1])]
            outs.append(_softmax_pv(sc, vs))
        o_ref[:, sl] = jnp.where(ltq, outs[0], outs[1]).astype(BF)


def _lat_b(l, q, k, v, cache_k, cache_v, tab):
    nq = NB_ROWS * GRID_W
    kv = pl.BlockSpec((None, DEC_SEQ, 512), lambda b, i, lr: (b, 0, 0))
    vv = pl.BlockSpec((None, DEC_SEQ, 1024), lambda b, i, lr: (b, 0, 0))
    cache = pl.BlockSpec((None, None, PAST_LEN, 512), lambda b, i, lr: (b, lr[0], 0, 0))
    qo = pl.BlockSpec((None, nq, 512), lambda b, i, lr: (b, i, 0))
    return pl.pallas_call(
        _lat_b_body,
        grid_spec=pltpu.PrefetchScalarGridSpec(
            num_scalar_prefetch=1, grid=(DEC_BATCH, DEC_SEQ // nq),
            in_specs=[qo, kv, vv, cache, cache,
                      pl.BlockSpec((None, H_B, 2 * NA_ROWS, GRID_W, LANES), lambda b, i, lr: (lr[0], 0, 0, 0, 0))],
            out_specs=qo),
        out_shape=jax.ShapeDtypeStruct((DEC_BATCH, DEC_SEQ, 512), BF),
        compiler_params=_cparams(("arbitrary",) * 2),
        name="lat_attn_b",
    )(l, q, k, v, cache_k, cache_v, tab)


def _merge_body(l_ref, x_ref, mod_ref, oa_ref, ob_ref, oc_ref, od_ref, z_ref, g0_ref, g1_ref, g2_ref, g3_ref,
                wbr_ref, wout_ref, lng_ref, lnb_ref, o_ref):
    merged = None
    for i, (o_r, g_r) in enumerate(zip((oa_ref, ob_ref, oc_ref, od_ref), (g0_ref, g1_ref, g2_ref, g3_ref))):
        u = (o_r[...].astype(F32) * z_ref[:, i * BRANCH_W:(i + 1) * BRANCH_W].astype(F32)).astype(BF)
        term = g_r[...].astype(F32) * _dot(u, wbr_ref[i])
        merged = term if merged is None else merged + term
    y = _dot(merged.astype(BF), wout_ref[...])
    gate = mod_ref[:, 2 * D_MODEL:3 * D_MODEL]
    r = ALPHA * x_ref[...] + gate * y
    mu = jnp.mean(r, axis=-1, keepdims=True)
    d = r - mu
    var = jnp.mean(d * d, axis=-1, keepdims=True)
    o_ref[...] = d * lax.rsqrt(var + EPS) * lng_ref[...] + lnb_ref[...]


def _merge(l, x, mod4, outs, zg, W, *, latent):
    m = x.shape[0]
    tm = TM_MERGE
    per_b = DEC_SEQ // tm
    cond = (lambda i: 1 + i // per_b) if latent else (lambda i: 0)
    row = lambda w: pl.BlockSpec((tm, w), lambda i, lr: (i, 0))
    zgb = lambda j: pl.BlockSpec((tm, D_MODEL), lambda i, lr: (i, j))
    wfull = lambda a: pl.BlockSpec((None,) + a.shape[1:], lambda i, lr: (lr[0],) + (0,) * (a.ndim - 1),
                                   pipeline_mode=pl.Buffered(1))
    in_specs = [row(D_MODEL),
                pl.BlockSpec((None, None, 1, 3 * D_MODEL), lambda i, lr: (lr[0], cond(i), 0, 0)),
                row(512), row(512), row(512), row(512),
                zgb(0), zgb(1), zgb(2), zgb(3), zgb(4),
                wfull(W["w_br"]), wfull(W["w_out"]), wfull(W["ln_g"]), wfull(W["ln_b"])]
    return pl.pallas_call(
        _merge_body,
        grid_spec=pltpu.PrefetchScalarGridSpec(
            num_scalar_prefetch=1, grid=(m // tm,), in_specs=in_specs, out_specs=row(D_MODEL)),
        out_shape=jax.ShapeDtypeStruct((m, D_MODEL), F32),
        input_output_aliases={1: 0},
        compiler_params=_cparams(("arbitrary",)),
        name="merge_lat" if latent else "merge_ctx",
    )(l, x, mod4, *outs, zg, zg, zg, zg, zg, W["w_br"], W["w_out"], W["ln_g"], W["ln_b"])


def _rope_tables():
    t = jnp.arange(DEC_SEQ)
    row = (t // GRID_W).astype(F32)
    col = (t % GRID_W).astype(F32)
    quarter = D_A // 4
    inv_freq = ROPE_BASE ** (-jnp.arange(quarter, dtype=F32) / quarter)
    ar = row[:, None] * inv_freq
    ac = col[:, None] * inv_freq
    ang = jnp.concatenate([ar, ar, ac, ac], axis=-1)
    cos, sin = jnp.cos(ang), jnp.sin(ang)
    even = (jnp.arange(D_A) // quarter) % 2 == 0
    sa = jnp.where(even, -sin, 0.0)
    sb = jnp.where(even, 0.0, sin)
    tile2 = lambda a: jnp.concatenate([a, a], axis=-1)
    return tile2(cos), tile2(sa), tile2(sb)


def _perm_heads(a, axis):
    shp = a.shape
    a = a.reshape(shp[:axis] + (H_D, D_D) + shp[axis + 1:])
    a = jnp.take(a, jnp.array(GQA_PERM), axis=axis)
    return a.reshape(shp)


def _repack_plan():
    offs = [int(v) // 64 for v in np.concatenate([[0], np.cumsum(IN_SIZES)])]
    aq, ak, av, bq, bk, bv, cq, ckv, ckr, dq, dk, dv, z, g = offs[:14]
    halves = list(range(aq, ckr))
    halves += [dq + h for j in range(H_D // 2) for h in (j, H_D // 2 + j)]
    halves += [dk, dk + 1, dv, dv + 1, ckr, None]
    qkv = [(halves[2 * t], halves[2 * t + 1]) for t in range(NQ // LANES)]
    z3 = z + 3 * BRANCH_W // 64
    halves = list(range(z, z3)) + [z3 + h for j in range(H_D // 2) for h in (j, H_D // 2 + j)]
    halves += list(range(g, offs[14]))
    zg = [(halves[2 * t], halves[2 * t + 1]) for t in range(NZG // LANES)]
    return qkv, zg


N_IN = sum(IN_SIZES)
N_IN_FULL_TILES = N_IN // LANES


def _repack_body(w_ref, tail_ref, qkv_ref, zg_ref):
    lt = _lane_lt64((TR_REPACK, LANES))

    def src_tile(t):
        return tail_ref[...] if t == N_IN_FULL_TILES else w_ref[:, t * LANES:(t + 1) * LANES]

    def build(a, b):
        if a is not None and b == a + 1:
            if a % 2 == 0:
                return src_tile(a // 2)
            return pltpu.roll(jnp.where(lt, src_tile(b // 2), src_tile(a // 2)), 64, 1)
        left = jnp.zeros((TR_REPACK, LANES), F32) if a is None else src_tile(a // 2)
        right = jnp.zeros((TR_REPACK, LANES), F32) if b is None else src_tile(b // 2)
        if a is not None and a % 2 == 1:
            left = pltpu.roll(left, 64, 1)
        if b is not None and b % 2 == 0:
            right = pltpu.roll(right, 64, 1)
        return jnp.where(lt, left, right)

    qkv_plan, zg_plan = _repack_plan()
    for plan, o_ref in ((qkv_plan, qkv_ref), (zg_plan, zg_ref)):
        for t, (a, b) in enumerate(plan):
            o_ref[:, t * LANES:(t + 1) * LANES] = build(a, b).astype(BF)


def _repack_w_in(w_in):
    tail = jnp.pad(w_in[:, :, N_IN_FULL_TILES * LANES:], ((0, 0), (0, 0), (0, LANES - N_IN % LANES)))
    return pl.pallas_call(
        _repack_body,
        grid=(DEPTH, D_MODEL // TR_REPACK),
        in_specs=[pl.BlockSpec((None, TR_REPACK, N_IN_FULL_TILES * LANES), lambda l, i: (l, i, 0)),
                  pl.BlockSpec((None, TR_REPACK, LANES), lambda l, i: (l, i, 0))],
        out_specs=[pl.BlockSpec((None, TR_REPACK, NQ), lambda l, i: (l, i, 0)),
                   pl.BlockSpec((None, TR_REPACK, NZG), lambda l, i: (l, i, 0))],
        out_shape=[jax.ShapeDtypeStruct((DEPTH, D_MODEL, NQ), BF),
                   jax.ShapeDtypeStruct((DEPTH, D_MODEL, NZG), BF)],
        compiler_params=_cparams(("arbitrary", "arbitrary")),
        name="repack_w_in",
    )(w_in, tail)


def _prep_weights(w_in, c_q_norm, c_kv_norm, w_c_uq, w_c_ukv, d_q_norm, d_k_norm, w_br, w_out, ln_g, ln_b):
    w_qkv, w_zg = _repack_w_in(w_in)
    uq = w_c_uq.reshape(DEPTH, Q_RANK, H_C, NOPE_DIM + ROPE_DIM)
    wuq = jnp.concatenate([uq, jnp.zeros((DEPTH, Q_RANK, H_C, 2 * LANES - NOPE_DIM - ROPE_DIM), F32)],
                          axis=3).reshape(DEPTH, Q_RANK, H_C * 2 * LANES).astype(BF)
    ukv = w_c_ukv.reshape(DEPTH, KV_RANK, H_C, NOPE_DIM + V_DIM_C)
    wukv_n = ukv[..., :NOPE_DIM].reshape(DEPTH, KV_RANK, H_C * NOPE_DIM).astype(BF)
    wukv_v = ukv[..., NOPE_DIM:].reshape(DEPTH, KV_RANK, H_C * V_DIM_C).astype(BF)
    wbr = jnp.concatenate([w_br[:, :3], _perm_heads(w_br[:, 3:], 2)], axis=1).astype(BF)
    return {
        "w_qkv": w_qkv, "w_zg": w_zg, "wuq": wuq, "wukv_n": wukv_n, "wukv_v": wukv_v,
        "cqn": c_q_norm.reshape(DEPTH, 1, Q_RANK), "ckvn": c_kv_norm.reshape(DEPTH, 1, KV_RANK),
        "dqn": jnp.tile(d_q_norm, (1, H_D)).reshape(DEPTH, 1, H_D * D_D),
        "dkn": jnp.tile(d_k_norm, (1, G_D)).reshape(DEPTH, 1, G_D * D_D),
        "w_br": wbr, "w_out": w_out.astype(BF),
        "ln_g": ln_g.reshape(DEPTH, 1, D_MODEL), "ln_b": ln_b.reshape(DEPTH, 1, D_MODEL),
    }


def kernel(x_prompt, x_sample, cache_a_k, cache_a_v, cache_b_k, cache_b_v, cache_c_kv, cache_c_kr, cache_d_k,
           cache_d_v, c, c_ctx, w_ada, b_ada, w_in, lam_a, a_subln, b_rpb, c_q_norm, c_kv_norm, w_c_uq, w_c_ukv,
           d_q_norm, d_k_norm, w_br, w_out, ln_g, ln_b):
    W = _prep_weights(w_in, c_q_norm, c_kv_norm, w_c_uq, w_c_ukv, d_q_norm, d_k_norm, w_br, w_out, ln_g, ln_b)
    cond8 = jnp.concatenate([c_ctx[None], c, jnp.zeros((5, D_MODEL), F32)], axis=0)
    mod4 = _adaln(cond8, w_ada, b_ada).reshape(DEPTH, 8, 1, 3 * D_MODEL)
    nb_tab = _bias_table(b_rpb).reshape(DEPTH, H_B, 2 * NA_ROWS, GRID_W, LANES)
    kc_cache, vc_cache = _cache_mla(cache_c_kv, cache_c_kr, W["wukv_n"], W["wukv_v"])
    rope_tabs = _rope_tables()
    lam_tab = jnp.array([0.8 - 0.6 * math.exp(-0.3 * l) for l in range(DEPTH)], F32)
    subln = a_subln.reshape(DEPTH, 1, 2 * D_A)
    flat = lambda a: a.reshape(DEC_BATCH, DEPTH, PAST_LEN, -1)
    ca_k, ca_v, cb_k, cb_v, cd_k, cd_v = (flat(a) for a in (cache_a_k, cache_a_v, cache_b_k, cache_b_v,
                                                               cache_d_k, cache_d_v))
    lat3 = lambda a: a.reshape(DEC_BATCH, DEC_SEQ, a.shape[-1])

    def layer(carry, li):
        xp, xs, bufs = carry
        l = li.reshape(1)
        (h_c, qa, qb, qc, kc, vc, qd, ka, va, kb, vb, ckv, kr, kd, vd) = _proj(l, xp, mod4, W, bufs, rope=False)
        pc = dict(qa=qa, ka=ka, va=va, qb=qb, kb=kb, vb=vb, qc=qc, kc=kc, vc=vc, qd=qd, kd=kd, vd=vd)
        outs_c = _ctx_attn(l, lam_tab, pc, lam_a, subln)
        zg_c = _zg(l, h_c, W["w_zg"])
        xp_new = _merge(l, xp, mod4, outs_c, zg_c, W, latent=False)
        (h_l, lqa, lka, lva, lqb, lkb, lvb, lqc, lkc, lvc, lqd, lkd, lvd) = _proj(l, xs, mod4, W, rope_tabs,
                                                                                   rope=True)
        o_a = _lat_a(l, lam_tab, lat3(lqa), lat3(lka), lat3(lva), ca_k, ca_v, lam_a, subln)
        o_b = _lat_b(l, lat3(lqb), lat3(lkb), lat3(lvb), cb_k, cb_v, nb_tab)
        o_c = _lat_c(l, lat3(lqc), lat3(lkc), lat3(lvc), kc_cache, vc_cache)
        o_d = _lat_d(l, lat3(lqd), lat3(lkd), lat3(lvd), cd_k, cd_v)
        outs_l = [o.reshape(N_LAT, 512) for o in (o_a, o_b, o_c, o_d)]
        zg_l = _zg(l, h_l, W["w_zg"])
        xs_new = _merge(l, xs, mod4, outs_l, zg_l, W, latent=True)
        return (xp_new, xs_new, (ka, va, kb, vb, ckv, kr, kd, vd)), None

    bufs0 = tuple(jnp.zeros((BATCH, DEPTH, SEQ, w), F32) for w in CACHE_WIDTHS)
    (xp, xs, caches), _ = lax.scan(
        layer, (x_prompt.reshape(N_CTX, D_MODEL), x_sample.reshape(N_LAT, D_MODEL), bufs0),
        jnp.arange(DEPTH, dtype=jnp.int32))
    ka, va, kb, vb, ckv, kr, kd, vd = caches

    def out(a, tail):
        return a.reshape((BATCH, DEPTH, SEQ) + tail)

    return (xp.reshape(BATCH, SEQ, D_MODEL), xs.reshape(DEC_BATCH, DEC_SEQ, D_MODEL),
            out(ka, (H_A, 2 * D_A)), out(va, (H_A, 2 * D_A)), out(kb, (H_B, D_B)), out(vb, (H_B, D_B)),
            out(ckv, (KV_RANK,)), out(kr, (ROPE_DIM,)), out(kd, (G_D, D_D)), out(vd, (G_D, D_D)))
```

```python
import functools
import math

import jax
import jax.numpy as jnp
import numpy as np
from jax import lax
from jax.experimental import pallas as pl
from jax.experimental.pallas import tpu as pltpu

D_MODEL = 2048
BATCH = 16
SEQ = 256
DEPTH = 4
DEC_BATCH = 2
DEC_SEQ = 4096
PAST_LEN = 256
GRID_W = 64
ROWS = DEC_SEQ // GRID_W
N_BRANCH = 4
BRANCH_W = 512
H_A, D_A = 4, 64
H_B, D_B = 8, 64
NA_ROWS, NA_COLS = 8, 16
H_C, Q_RANK, KV_RANK, NOPE_DIM, ROPE_DIM, V_DIM_C = 4, 512, 256, 128, 64, 128
H_D, G_D, D_D = 8, 2, 64
ROPE_BASE = 10000.0
EPS = 1e-6
ALPHA = (2 * DEPTH) ** 0.25
IN_SIZES = (512, 512, 512, 512, 512, 512, Q_RANK, KV_RANK, ROPE_DIM, 512, 128, 128,
            N_BRANCH * BRANCH_W, N_BRANCH * D_MODEL)

BF = jnp.bfloat16
F32 = jnp.float32
LANES = 128
MXU_N = 256
LOG2E = 1.4426950408889634
VMEM_LIMIT = 56 * 1024 * 1024
NEG = -1e30

N_CTX = BATCH * SEQ
N_LAT = DEC_BATCH * DEC_SEQ
NQ = 4736
NZG = 10240
GQA_PERM = (0, 4, 1, 5, 2, 6, 3, 7)

O_AQ, O_AK, O_AV, O_BQ, O_BK, O_BV = 0, 512, 1024, 1536, 2048, 2560
O_CQ, O_CKV, O_DQ, O_DK, O_DV, O_KR = 3072, 3584, 3840, 4352, 4480, 4608

TM_PROJ = 256
TM_ZG = 1024
TN_ZG = 2048
TM_MERGE = 256
ROW_TILE = 256
TQ_A = 512
TQ_C = 512
TQ_D = 512
TC_REPACK = 256
NB_ROWS = 4
NB_KROWS = 12


def _cparams(sem):
    return pltpu.CompilerParams(dimension_semantics=sem, vmem_limit_bytes=VMEM_LIMIT)


def _dot(a, b):
    return jnp.dot(a, b, preferred_element_type=F32)


def _dot_nt(a, b):
    return lax.dot_general(a, b, (((1,), (1,)), ((), ())), preferred_element_type=F32)


def _sigmoid(x):
    return 1.0 / (1.0 + jnp.exp(-x))


def _lane_lt64(shape):
    return lax.broadcasted_iota(jnp.int32, shape, len(shape) - 1) < 64


def _adaln_body(c_ref, w_ref, b_ref, o_ref):
    c = c_ref[...]
    s = (c * _sigmoid(c)).astype(BF)
    o_ref[...] = _dot(s, w_ref[...].astype(BF)) + b_ref[...]


def _adaln(cond8, w_ada, b_ada):
    tn = 1536
    return pl.pallas_call(
        _adaln_body,
        grid=(DEPTH, 3 * D_MODEL // tn),
        in_specs=[pl.BlockSpec((8, D_MODEL), lambda l, j: (0, 0)),
                  pl.BlockSpec((None, D_MODEL, tn), lambda l, j: (l, 0, j)),
                  pl.BlockSpec((None, 1, tn), lambda l, j: (l, 0, j))],
        out_specs=pl.BlockSpec((None, 8, tn), lambda l, j: (l, 0, j)),
        out_shape=jax.ShapeDtypeStruct((DEPTH, 8, 3 * D_MODEL), F32),
        compiler_params=_cparams(("arbitrary", "arbitrary")),
        name="adaln",
    )(cond8, w_ada, b_ada.reshape(DEPTH, 1, 3 * D_MODEL))


def _bias_table_body(rpb_ref, o_ref):
    lh = pl.program_id(0)
    qc = lax.broadcasted_iota(jnp.int32, (GRID_W, LANES), 0)
    lane = lax.broadcasted_iota(jnp.int32, (GRID_W, LANES), 1)
    kc = jnp.bitwise_and(lane, 63)
    c0 = jnp.clip(qc - NA_COLS // 2, 0, GRID_W - NA_COLS)
    col_ok = (kc >= c0) & (kc < c0 + NA_COLS)
    dc = kc - qc + (NA_COLS - 1)
    right = lane >= 64
    n_dr, n_dc = 2 * NA_ROWS - 1, 2 * NA_COLS - 1
    for u in range(n_dr + 1):
        val = jnp.full((GRID_W, LANES), NEG, F32)
        for half, sel in ((0, ~right), (1, right)):
            dr = u - 1 + half
            if 0 <= dr < n_dr:
                for d in range(n_dc):
                    r = rpb_ref[(lh * n_dr + dr) * n_dc + d] * LOG2E
                    val = jnp.where(sel & col_ok & (dc == d), r, val)
        o_ref[u] = val


def _bias_table(b_rpb):
    n = DEPTH * H_B
    return pl.pallas_call(
        _bias_table_body,
        grid_spec=pltpu.PrefetchScalarGridSpec(
            num_scalar_prefetch=1, grid=(n,),
            in_specs=[],
            out_specs=pl.BlockSpec((None, 2 * NA_ROWS, GRID_W, LANES), lambda i, r: (i, 0, 0, 0))),
        out_shape=jax.ShapeDtypeStruct((n, 2 * NA_ROWS, GRID_W, LANES), F32),
        compiler_params=_cparams(("arbitrary",)),
        name="nb_bias_table",
    )(b_rpb.reshape(-1))


def _cache_mla_body(ckv_ref, kr_ref, wn_ref, wv_ref, k_ref, v_ref):
    ckv = ckv_ref[...].astype(BF)
    kn = _dot(ckv, wn_ref[...])
    kr = kr_ref[...]
    k_ref[...] = jnp.concatenate(
        [t for h in range(H_C) for t in (kn[:, h * LANES:(h + 1) * LANES], kr)], axis=1).astype(BF)
    v_ref[...] = _dot(ckv, wv_ref[...]).astype(BF)


def _cache_mla(cache_c_kv, cache_c_kr, wukv_n, wukv_v):
    return pl.pallas_call(
        _cache_mla_body,
        grid=(DEPTH, DEC_BATCH),
        in_specs=[pl.BlockSpec((None, None, PAST_LEN, KV_RANK), lambda l, b: (b, l, 0, 0)),
                  pl.BlockSpec((None, None, PAST_LEN, LANES), lambda l, b: (b, l, 0, 0)),
                  pl.BlockSpec((None, KV_RANK, 512), lambda l, b: (l, 0, 0)),
                  pl.BlockSpec((None, KV_RANK, 512), lambda l, b: (l, 0, 0))],
        out_specs=[pl.BlockSpec((None, None, PAST_LEN, 1024), lambda l, b: (l, b, 0, 0)),
                   pl.BlockSpec((None, None, PAST_LEN, 512), lambda l, b: (l, b, 0, 0))],
        out_shape=[jax.ShapeDtypeStruct((DEPTH, DEC_BATCH, PAST_LEN, 1024), BF),
                   jax.ShapeDtypeStruct((DEPTH, DEC_BATCH, PAST_LEN, 512), BF)],
        compiler_params=_cparams(("arbitrary", "arbitrary")),
        name="cache_mla",
    )(cache_c_kv, jnp.pad(cache_c_kr, ((0, 0), (0, 0), (0, 0), (0, LANES - ROPE_DIM))), wukv_n, wukv_v)


def _rope_tiles(x, cos, sa, sb):
    outs = []
    for j in range(x.shape[1] // LANES):
        t = x[:, j * LANES:(j + 1) * LANES]
        outs.append(t * cos + pltpu.roll(t, LANES - 16, 1) * sa + pltpu.roll(t, 16, 1) * sb)
    return outs[0] if len(outs) == 1 else jnp.concatenate(outs, axis=1)


def _group64_rms(x, g):
    w = x.shape[1]
    r = lax.shift_right_logical(lax.broadcasted_iota(jnp.int32, (w, w), 0), 6)
    c = lax.shift_right_logical(lax.broadcasted_iota(jnp.int32, (w, w), 1), 6)
    bd = jnp.where(r == c, 1.0, 0.0).astype(BF)
    x2 = x * x
    hi = x2.astype(BF)
    lo = (x2 - hi.astype(F32)).astype(BF)
    ms = (_dot(hi, bd) + _dot(lo, bd)) * (1.0 / 64)
    return x * lax.rsqrt(ms + EPS) * g


def _full_rms(x, g):
    ms = jnp.mean(x * x, axis=-1, keepdims=True)
    return x * lax.rsqrt(ms + EPS) * g


def _proj_body(l_ref, x_ref, mod_ref, w_ref, wuq_ref, wun_ref, wuv_ref, cqn_ref, ckvn_ref, dqn_ref, dkn_ref,
               *refs, rope):
    if rope:
        cos_ref, sa_ref, sb_ref = refs[:3]
        refs = refs[3:]
        cos, sa, sb = cos_ref[...], sa_ref[...], sb_ref[...]
        rp = lambda t: _rope_tiles(t, cos, sa, sb)
    else:
        rp = lambda t: t
    x = x_ref[...]
    shift = mod_ref[:, 0:D_MODEL]
    scale = mod_ref[:, D_MODEL:2 * D_MODEL]
    h = (x * (1.0 + scale) + shift).astype(BF)

    acc = _dot_nt(h, w_ref[...])

    def col(o, n):
        return acc[:, o:o + n]

    def with_ones(v):
        ones = jnp.ones((v.shape[0], LANES), BF)
        return jnp.concatenate([t for j in range(v.shape[1] // LANES)
                                for t in (v[:, j * LANES:(j + 1) * LANES].astype(BF), ones)], axis=1)

    qa = rp(col(O_AQ, 512)) * (D_A ** -0.5 * LOG2E)
    ka = rp(col(O_AK, 512))
    va = col(O_AV, 512)
    qb = col(O_BQ, 512) * (D_B ** -0.5 * LOG2E)
    kb = col(O_BK, 512)
    vb = col(O_BV, 512)
    cq = _full_rms(col(O_CQ, Q_RANK), cqn_ref[...]).astype(BF)
    qc_raw = _dot(cq, wuq_ref[...])
    qc_scale = (NOPE_DIM + ROPE_DIM) ** -0.5 * LOG2E
    qc = jnp.concatenate(
        [t for hh in range(H_C) for t in (qc_raw[:, 2 * hh * LANES:(2 * hh + 1) * LANES],
                                          rp(qc_raw[:, (2 * hh + 1) * LANES:(2 * hh + 2) * LANES]))],
        axis=1) * qc_scale
    ckv = _full_rms(col(O_CKV, KV_RANK), ckvn_ref[...])
    ckv_b = ckv.astype(BF)
    kn = _dot(ckv_b, wun_ref[...])
    vc = _dot(ckv_b, wuv_ref[...])
    kr_raw = col(O_KR, LANES)
    kr = rp(kr_raw)
    kc = jnp.concatenate([t for hh in range(H_C) for t in (kn[:, hh * LANES:(hh + 1) * LANES], kr)], axis=1)
    qd = rp(_group64_rms(col(O_DQ, 512), dqn_ref[...])) * (D_D ** -0.5 * LOG2E)
    kd_n = _group64_rms(col(O_DK, LANES), dkn_ref[...])
    kd = rp(kd_n)
    vd = col(O_DV, LANES)

    if rope:
        (h_o, qa_o, ka_o, va_o, qb_o, kb_o, vb_o, qc_o, kc_o, vc_o, qd_o, kd_o, vd_o) = refs
        ka_o[...] = ka.astype(BF)
        va_o[...] = with_ones(va)
        kb_o[...] = kb.astype(BF)
        vb_o[...] = with_ones(vb)
        kd_o[...] = kd.astype(BF)
        vd_o[...] = with_ones(vd)
        vc_o[...] = with_ones(vc)
    else:
        (h_o, qa_o, qb_o, qc_o, kc_o, vc_o, qd_o,
         ka_o, va_o, kb_o, vb_o, ckv_o, kr_o, kd_o, vd_o) = refs[len(CACHE_WIDTHS):]
        ka_o[...] = ka
        va_o[...] = va
        kb_o[...] = kb
        vb_o[...] = vb
        ckv_o[...] = ckv
        kr_o[...] = kr_raw[:, 0:ROPE_DIM]
        kd_o[...] = kd_n
        vd_o[...] = vd
        vc_o[...] = vc.astype(BF)
    h_o[...] = h
    qa_o[...] = qa.astype(BF)
    qb_o[...] = qb.astype(BF)
    qc_o[...] = qc.astype(BF)
    kc_o[...] = kc.astype(BF)
    qd_o[...] = qd.astype(BF)


CACHE_WIDTHS = (512, 512, 512, 512, KV_RANK, ROPE_DIM, LANES, LANES)


def _proj(l, x, mod4, W, extra, *, rope):
    m = x.shape[0]
    tm = TM_PROJ
    per_b = DEC_SEQ // tm
    cond = (lambda i: 1 + i // per_b) if rope else (lambda i: 0)
    row = lambda w: pl.BlockSpec((tm, w), lambda i, lr: (i, 0))
    wfull = lambda a: pl.BlockSpec((None,) + a.shape[1:], lambda i, lr: (lr[0],) + (0,) * (a.ndim - 1),
                                   pipeline_mode=pl.Buffered(1))
    weights = [W["w_qkv"], W["wuq"], W["wukv_n"], W["wukv_v"], W["cqn"], W["ckvn"], W["dqn"], W["dkn"]]
    in_specs = [row(D_MODEL),
                pl.BlockSpec((None, None, 1, 3 * D_MODEL), lambda i, lr: (lr[0], cond(i), 0, 0))]
    in_specs += [wfull(a) for a in weights]
    args = [x, mod4] + weights
    if rope:
        in_specs += [pl.BlockSpec((tm, LANES), lambda i, lr: (i % per_b, 0))] * 3
        widths = [(D_MODEL, BF), (512, BF), (512, BF), (1024, BF), (512, BF), (512, BF), (1024, BF),
                  (1024, BF), (1024, BF), (1024, BF), (512, BF), (LANES, BF), (2 * LANES, BF)]
        aliases = {}
    else:
        assert tm == SEQ
        in_specs += [pl.BlockSpec(memory_space=pl.ANY)] * len(CACHE_WIDTHS)
        widths = [(D_MODEL, BF), (512, BF), (512, BF), (1024, BF), (1024, BF), (512, BF), (512, BF)]
        aliases = {1 + len(args) + k: len(widths) + k for k in range(len(CACHE_WIDTHS))}
    args += list(extra)
    out_specs = [row(w) for w, _ in widths]
    out_shape = [jax.ShapeDtypeStruct((m, w), d) for w, d in widths]
    if not rope:
        out_specs += [pl.BlockSpec((None, None, SEQ, w), lambda i, lr: (i, lr[0], 0, 0)) for w in CACHE_WIDTHS]
        out_shape += [jax.ShapeDtypeStruct((BATCH, DEPTH, SEQ, w), F32) for w in CACHE_WIDTHS]
    return pl.pallas_call(
        functools.partial(_proj_body, rope=rope),
        grid_spec=pltpu.PrefetchScalarGridSpec(
            num_scalar_prefetch=1, grid=(m // tm,), in_specs=in_specs, out_specs=out_specs),
        out_shape=out_shape,
        input_output_aliases=aliases,
        compiler_params=_cparams(("arbitrary",)),
        name="proj_lat" if rope else "proj_ctx",
    )(l, *args)


def _zg_body(l_ref, h_ref, w_ref, o_ref):
    is_z = pl.program_id(0) < (N_BRANCH * BRANCH_W) // TN_ZG

    def run(silu):
        h = h_ref[...]
        for n in range(TN_ZG // MXU_N):
            sl = slice(n * MXU_N, (n + 1) * MXU_N)
            a = _dot_nt(h, w_ref[sl, :])
            s = _sigmoid(a)
            o_ref[:, sl] = ((a * s) if silu else s).astype(BF)

    @pl.when(is_z)
    def _():
        run(True)

    @pl.when(jnp.logical_not(is_z))
    def _():
        run(False)


def _zg(l, h, w_zg):
    m = h.shape[0]
    return pl.pallas_call(
        _zg_body,
        grid_spec=pltpu.PrefetchScalarGridSpec(
            num_scalar_prefetch=1, grid=(NZG // TN_ZG, m // TM_ZG),
            in_specs=[pl.BlockSpec((TM_ZG, D_MODEL), lambda j, i, lr: (i, 0)),
                      pl.BlockSpec((None, TN_ZG, D_MODEL), lambda j, i, lr: (lr[0], j, 0))],
            out_specs=pl.BlockSpec((TM_ZG, TN_ZG), lambda j, i, lr: (i, j))),
        out_shape=jax.ShapeDtypeStruct((m, NZG), BF),
        compiler_params=_cparams(("arbitrary", "arbitrary")),
        name="zg_proj",
    )(l, h, w_zg)


def _softmax_pv(scores, values):
    m = None
    for s in scores:
        sm = jnp.max(s, axis=-1, keepdims=True)
        m = sm if m is None else jnp.maximum(m, sm)
    acc = None
    for s, v in zip(scores, values):
        o = _dot(jnp.exp2((s - m).astype(BF)), v)
        acc = o if acc is None else acc + o
    return acc[:, :LANES] / acc[:, LANES:]


def _with_ones(v):
    return jnp.concatenate([v, jnp.ones_like(v)], axis=1)


def _diff_lambda(lam_ref, lam_init):
    la = lam_ref[...]
    s01 = jnp.sum(la[0:1] * la[1:2], axis=-1, keepdims=True)
    s23 = jnp.sum(la[2:3] * la[3:4], axis=-1, keepdims=True)
    return jnp.exp(s01) - jnp.exp(s23) + lam_init


def _diff_head(q, ks, vs, lam, subln, lam_init):
    lt = _lane_lt64(q.shape)
    zero = jnp.zeros_like(q)
    o = []
    for qm in (jnp.where(lt, q, zero), jnp.where(lt, zero, q)):
        o.append(_softmax_pv([_dot_nt(qm, k) for k in ks], vs))
    d = o[0] - lam * o[1]
    ms = jnp.mean(d * d, axis=-1, keepdims=True)
    return d * lax.rsqrt(ms + EPS) * subln * (1.0 - lam_init)


def _pair_heads(q, ks, vs, bias=None):
    lt = _lane_lt64(q.shape)
    zero = jnp.zeros_like(q)
    o = []
    for qm in (jnp.where(lt, q, zero), jnp.where(lt, zero, q)):
        sc = [_dot_nt(qm, k) for k in ks]
        if bias is not None:
            sc[0] = sc[0] + bias
        o.append(_softmax_pv(sc, vs))
    return jnp.where(_lane_lt64(o[0].shape), o[0], o[1])


def _ctx_attn_body(l_ref, li_ref, qa_ref, ka_ref, va_ref, qb_ref, kb_ref, vb_ref, qc_ref, kc_ref, vc_ref,
                   qd_ref, kd_ref, vd_ref, lam_ref, subln_ref, oa_ref, ob_ref, oc_ref, od_ref):
    lam_init = li_ref[l_ref[0]]
    lam = _diff_lambda(lam_ref, lam_init)
    subln = subln_ref[...]
    for h in range(H_A):
        sl = slice(h * LANES, (h + 1) * LANES)
        oa_ref[:, sl] = _diff_head(qa_ref[:, sl], [ka_ref[:, sl].astype(BF)],
                                   [_with_ones(va_ref[:, sl].astype(BF))], lam, subln, lam_init).astype(BF)
    for j in range(H_B // 2):
        sl = slice(j * LANES, (j + 1) * LANES)
        ob_ref[:, sl] = _pair_heads(qb_ref[:, sl], [kb_ref[:, sl].astype(BF)],
                                    [_with_ones(vb_ref[:, sl].astype(BF))]).astype(BF)
    for h in range(H_C):
        oc_ref[:, h * LANES:(h + 1) * LANES] = _softmax_pv(
            [_dot_nt(qc_ref[:, 2 * h * LANES:(2 * h + 2) * LANES], kc_ref[:, 2 * h * LANES:(2 * h + 2) * LANES])],
            [_with_ones(vc_ref[:, h * LANES:(h + 1) * LANES])]).astype(BF)
    kd = kd_ref[...].astype(BF)
    vd = _with_ones(vd_ref[...].astype(BF))
    for j in range(H_D // 2):
        sl = slice(j * LANES, (j + 1) * LANES)
        od_ref[:, sl] = _pair_heads(qd_ref[:, sl], [kd], [vd]).astype(BF)


def _ctx_attn(l, lam_tab, pc, lam_a, a_subln):
    row = lambda w: pl.BlockSpec((SEQ, w), lambda b, lr, li: (b, 0))
    ins = [pc["qa"], pc["ka"], pc["va"], pc["qb"], pc["kb"], pc["vb"], pc["qc"], pc["kc"], pc["vc"],
           pc["qd"], pc["kd"], pc["vd"]]
    layer_row = lambda w: pl.BlockSpec((None, None, SEQ, w), lambda b, lr, li: (b, lr[0], 0, 0))
    in_specs = [row(a.shape[1]) if a.ndim == 2 else layer_row(a.shape[3]) for a in ins]
    in_specs += [pl.BlockSpec((None, 4, D_A), lambda b, lr, li: (lr[0], 0, 0)),
                 pl.BlockSpec((None, 1, 2 * D_A), lambda b, lr, li: (lr[0], 0, 0))]
    return pl.pallas_call(
        _ctx_attn_body,
        grid_spec=pltpu.PrefetchScalarGridSpec(
            num_scalar_prefetch=2, grid=(BATCH,), in_specs=in_specs,
            out_specs=[row(512)] * 4),
        out_shape=[jax.ShapeDtypeStruct((N_CTX, 512), BF)] * 4,
        compiler_params=_cparams(("arbitrary",)),
        name="ctx_attn",
    )(l, lam_tab, *ins, lam_a, a_subln)


def _lat_a_body(l_ref, li_ref, q_ref, k_ref, v_ref, kc_ref, vc_ref, lam_ref, subln_ref, o_ref):
    lam_init = li_ref[l_ref[0]]
    lam = _diff_lambda(lam_ref, lam_init)
    subln = subln_ref[...]
    for h in range(H_A):
        sl = slice(h * LANES, (h + 1) * LANES)
        vsl = slice(2 * h * LANES, (2 * h + 2) * LANES)
        ks = [k_ref[:, sl], kc_ref[:, sl].astype(BF)]
        vs = [v_ref[:, vsl], _with_ones(vc_ref[:, sl].astype(BF))]
        for r in range(TQ_A // ROW_TILE):
            rows = slice(r * ROW_TILE, (r + 1) * ROW_TILE)
            o_ref[rows, sl] = _diff_head(q_ref[rows, sl], ks, vs, lam, subln, lam_init).astype(BF)


def _lat_specs(tq, q_w, k_w, v_w, ck_w, cv_w, cache_layer_first):
    def im(f):
        return lambda b, i, *pre: f(b, i, pre[0][0])
    cidx = (lambda b, i, l: (l, b, 0, 0)) if cache_layer_first else (lambda b, i, l: (b, l, 0, 0))
    qo = lambda w: pl.BlockSpec((None, tq, w), im(lambda b, i, l: (b, i, 0)))
    kv = lambda w: pl.BlockSpec((None, DEC_SEQ, w), im(lambda b, i, l: (b, 0, 0)), pipeline_mode=pl.Buffered(1))
    cache = lambda w: pl.BlockSpec((None, None, PAST_LEN, w), im(cidx))
    return qo, [qo(q_w), kv(k_w), kv(v_w), cache(ck_w), cache(cv_w)]


def _lat_a(l, lam_tab, q, k, v, cache_k, cache_v, lam_a, a_subln):
    qo, in_specs = _lat_specs(TQ_A, 512, 512, 1024, 512, 512, False)
    in_specs += [pl.BlockSpec((None, 4, D_A), lambda b, i, lr, li: (lr[0], 0, 0)),
                 pl.BlockSpec((None, 1, 2 * D_A), lambda b, i, lr, li: (lr[0], 0, 0))]
    return pl.pallas_call(
        _lat_a_body,
        grid_spec=pltpu.PrefetchScalarGridSpec(
            num_scalar_prefetch=2, grid=(DEC_BATCH, DEC_SEQ // TQ_A), in_specs=in_specs, out_specs=qo(512)),
        out_shape=jax.ShapeDtypeStruct((DEC_BATCH, DEC_SEQ, 512), BF),
        compiler_params=_cparams(("arbitrary",) * 2),
        name="lat_attn_a",
    )(l, lam_tab, q, k, v, cache_k, cache_v, lam_a, a_subln)


def _lat_c_body(l_ref, q_ref, k_ref, v_ref, kc_ref, vc_ref, o_ref):
    for h in range(H_C):
        sl = slice(h * LANES, (h + 1) * LANES)
        sl2 = slice(2 * h * LANES, (2 * h + 2) * LANES)
        vs = [v_ref[:, sl2], _with_ones(vc_ref[:, sl])]
        for r in range(TQ_C // ROW_TILE):
            rows = slice(r * ROW_TILE, (r + 1) * ROW_TILE)
            q = q_ref[rows, sl2]
            o_ref[rows, sl] = _softmax_pv([_dot_nt(q, k_ref[:, sl2]), _dot_nt(q, kc_ref[:, sl2])],
                                          vs).astype(BF)


def _lat_c(l, q, k, v, cache_k, cache_v):
    qo, in_specs = _lat_specs(TQ_C, 1024, 1024, 1024, 1024, 512, True)
    return pl.pallas_call(
        _lat_c_body,
        grid_spec=pltpu.PrefetchScalarGridSpec(
            num_scalar_prefetch=1, grid=(DEC_BATCH, DEC_SEQ // TQ_C), in_specs=in_specs, out_specs=qo(512)),
        out_shape=jax.ShapeDtypeStruct((DEC_BATCH, DEC_SEQ, 512), BF),
        compiler_params=_cparams(("arbitrary",) * 2),
        name="lat_attn_c",
    )(l, q, k, v, cache_k, cache_v)


def _lat_d_body(l_ref, q_ref, k_ref, v_ref, kc_ref, vc_ref, o_ref):
    ks = [k_ref[...], kc_ref[...].astype(BF)]
    vs = [v_ref[...], _with_ones(vc_ref[...].astype(BF))]
    for r in range(TQ_D // ROW_TILE):
        rows = slice(r * ROW_TILE, (r + 1) * ROW_TILE)
        for j in range(H_D // 2):
            sl = slice(j * LANES, (j + 1) * LANES)
            o_ref[rows, sl] = _pair_heads(q_ref[rows, sl], ks, vs).astype(BF)


def _lat_d(l, q, k, v, cache_k, cache_v):
    qo, in_specs = _lat_specs(TQ_D, 512, LANES, 2 * LANES, LANES, LANES, False)
    return pl.pallas_call(
        _lat_d_body,
        grid_spec=pltpu.PrefetchScalarGridSpec(
            num_scalar_prefetch=1, grid=(DEC_BATCH, DEC_SEQ // TQ_D), in_specs=in_specs, out_specs=qo(512)),
        out_shape=jax.ShapeDtypeStruct((DEC_BATCH, DEC_SEQ, 512), BF),
        compiler_params=_cparams(("arbitrary",) * 2),
        name="lat_attn_d",
    )(l, q, k, v, cache_k, cache_v)


def _lat_b_body(l_ref, q_ref, k_ref, v_ref, kc_ref, vc_ref, tab_ref, o_ref):
    i = pl.program_id(1)
    qr0 = i * NB_ROWS
    kr0 = jnp.clip(qr0 - NA_ROWS // 2, 0, ROWS - NB_KROWS)
    start = pl.multiple_of(kr0 * GRID_W, GRID_W)
    n_keys = NB_KROWS * GRID_W
    lt = _lane_lt64((GRID_W, LANES))
    for j in range(H_B // 2):
        sl = slice(j * LANES, (j + 1) * LANES)
        kwin = k_ref[pl.ds(start, n_keys), sl]
        vwin = v_ref[pl.ds(start, n_keys), 2 * j * LANES:(2 * j + 2) * LANES]
        ks = [kwin, kc_ref[:, sl].astype(BF)]
        vs = [vwin, _with_ones(vc_ref[:, sl].astype(BF))]
        q = q_ref[:, sl]
        ltq = _lane_lt64(q.shape)
        zero = jnp.zeros_like(q)
        outs = []
        for half, qm in ((0, jnp.where(ltq, q, zero)), (1, jnp.where(ltq, zero, q))):
            head = 2 * j + half
            rows = []
            for a in range(NB_ROWS):
                qr = qr0 + a
                r0 = jnp.clip(qr - NA_ROWS // 2, 0, ROWS - NA_ROWS)
                tiles = []
                for p in range(NB_KROWS // 2):
                    kr_l = kr0 + 2 * p
                    u = jnp.clip(kr_l - qr + NA_ROWS, 0, 2 * NA_ROWS - 1)
                    pen_l = jnp.where((kr_l >= r0) & (kr_l < r0 + NA_ROWS), 0.0, NEG)
                    pen_r = jnp.where((kr_l + 1 >= r0) & (kr_l + 1 < r0 + NA_ROWS), 0.0, NEG)
                    tiles.append(tab_ref[head, u] + jnp.where(lt, pen_l, pen_r))
                rows.append(jnp.concatenate(tiles, axis=1))
            bias = jnp.concatenate(rows, axis=0)
            sc = [_dot_nt(qm, ks[0]) + bias, _dot_nt(qm, ks[1])]
            outs.append(_softmax_pv(sc, vs))
        o_ref[:, sl] = jnp.where(ltq, outs[0], outs[1]).astype(BF)


def _lat_b(l, q, k, v, cache_k, cache_v, tab):
    nq = NB_ROWS * GRID_W
    kv = pl.BlockSpec((None, DEC_SEQ, 512), lambda b, i, lr: (b, 0, 0))
    vv = pl.BlockSpec((None, DEC_SEQ, 1024), lambda b, i, lr: (b, 0, 0))
    cache = pl.BlockSpec((None, None, PAST_LEN, 512), lambda b, i, lr: (b, lr[0], 0, 0))
    qo = pl.BlockSpec((None, nq, 512), lambda b, i, lr: (b, i, 0))
    return pl.pallas_call(
        _lat_b_body,
        grid_spec=pltpu.PrefetchScalarGridSpec(
            num_scalar_prefetch=1, grid=(DEC_BATCH, DEC_SEQ // nq),
            in_specs=[qo, kv, vv, cache, cache,
                      pl.BlockSpec((None, H_B, 2 * NA_ROWS, GRID_W, LANES), lambda b, i, lr: (lr[0], 0, 0, 0, 0))],
            out_specs=qo),
        out_shape=jax.ShapeDtypeStruct((DEC_BATCH, DEC_SEQ, 512), BF),
        compiler_params=_cparams(("arbitrary",) * 2),
        name="lat_attn_b",
    )(l, q, k, v, cache_k, cache_v, tab)


def _merge_body(l_ref, x_ref, mod_ref, oa_ref, ob_ref, oc_ref, od_ref, z_ref, g0_ref, g1_ref, g2_ref, g3_ref,
                wbr_ref, wout_ref, lng_ref, lnb_ref, o_ref):
    merged = None
    for i, (o_r, g_r) in enumerate(zip((oa_ref, ob_ref, oc_ref, od_ref), (g0_ref, g1_ref, g2_ref, g3_ref))):
        u = (o_r[...].astype(F32) * z_ref[:, i * BRANCH_W:(i + 1) * BRANCH_W].astype(F32)).astype(BF)
        term = g_r[...].astype(F32) * _dot(u, wbr_ref[i])
        merged = term if merged is None else merged + term
    y = _dot(merged.astype(BF), wout_ref[...])
    gate = mod_ref[:, 2 * D_MODEL:3 * D_MODEL]
    r = ALPHA * x_ref[...] + gate * y
    mu = jnp.mean(r, axis=-1, keepdims=True)
    d = r - mu
    var = jnp.mean(d * d, axis=-1, keepdims=True)
    o_ref[...] = d * lax.rsqrt(var + EPS) * lng_ref[...] + lnb_ref[...]


def _merge(l, x, mod4, outs, zg, W, *, latent):
    m = x.shape[0]
    tm = TM_MERGE
    per_b = DEC_SEQ // tm
    cond = (lambda i: 1 + i // per_b) if latent else (lambda i: 0)
    row = lambda w: pl.BlockSpec((tm, w), lambda i, lr: (i, 0))
    zgb = lambda j: pl.BlockSpec((tm, D_MODEL), lambda i, lr: (i, j))
    wfull = lambda a: pl.BlockSpec((None,) + a.shape[1:], lambda i, lr: (lr[0],) + (0,) * (a.ndim - 1),
                                   pipeline_mode=pl.Buffered(1))
    in_specs = [row(D_MODEL),
                pl.BlockSpec((None, None, 1, 3 * D_MODEL), lambda i, lr: (lr[0], cond(i), 0, 0)),
                row(512), row(512), row(512), row(512),
                zgb(0), zgb(1), zgb(2), zgb(3), zgb(4),
                wfull(W["w_br"]), wfull(W["w_out"]), wfull(W["ln_g"]), wfull(W["ln_b"])]
    return pl.pallas_call(
        _merge_body,
        grid_spec=pltpu.PrefetchScalarGridSpec(
            num_scalar_prefetch=1, grid=(m // tm,), in_specs=in_specs, out_specs=row(D_MODEL)),
        out_shape=jax.ShapeDtypeStruct((m, D_MODEL), F32),
        input_output_aliases={1: 0},
        compiler_params=_cparams(("arbitrary",)),
        name="merge_lat" if latent else "merge_ctx",
    )(l, x, mod4, *outs, zg, zg, zg, zg, zg, W["w_br"], W["w_out"], W["ln_g"], W["ln_b"])


def _rope_tables():
    t = jnp.arange(DEC_SEQ)
    row = (t // GRID_W).astype(F32)
    col = (t % GRID_W).astype(F32)
    quarter = D_A // 4
    inv_freq = ROPE_BASE ** (-jnp.arange(quarter, dtype=F32) / quarter)
    ar = row[:, None] * inv_freq
    ac = col[:, None] * inv_freq
    ang = jnp.concatenate([ar, ar, ac, ac], axis=-1)
    cos, sin = jnp.cos(ang), jnp.sin(ang)
    even = (jnp.arange(D_A) // quarter) % 2 == 0
    sa = jnp.where(even, -sin, 0.0)
    sb = jnp.where(even, 0.0, sin)
    tile2 = lambda a: jnp.concatenate([a, a], axis=-1)
    return tile2(cos), tile2(sa), tile2(sb)


def _perm_heads(a, axis):
    shp = a.shape
    a = a.reshape(shp[:axis] + (H_D, D_D) + shp[axis + 1:])
    a = jnp.take(a, jnp.array(GQA_PERM), axis=axis)
    return a.reshape(shp)


def _repack_plan():
    offs = [int(v) // 64 for v in np.concatenate([[0], np.cumsum(IN_SIZES)])]
    aq, ak, av, bq, bk, bv, cq, ckv, ckr, dq, dk, dv, z, g = offs[:14]
    gqa = lambda base: [base + h for j in range(H_D // 2) for h in (j, H_D // 2 + j)]
    qkv = list(range(aq, ckr)) + gqa(dq) + [dk, dk + 1, dv, dv + 1, ckr, None]
    z3 = z + 3 * BRANCH_W // 64
    zg = list(range(z, z3)) + gqa(z3) + list(range(g, offs[14]))

    def runs(chunks):
        out = []
        for d, s in enumerate(chunks):
            if out and s is not None and out[-1][1] is not None and out[-1][1] + out[-1][2] == s:
                out[-1][2] += 1
            else:
                out.append([d, s, 1])
        return out
    assert len(qkv) * 64 == NQ and len(zg) * 64 == NZG
    return runs(qkv), runs(zg)


N_IN = sum(IN_SIZES)


def _repack_body(w_ref, qkv_ref, zg_ref):
    qkv_plan, zg_plan = _repack_plan()
    for plan, o_ref in ((qkv_plan, qkv_ref), (zg_plan, zg_ref)):
        for d, s, n in plan:
            rows = slice(d * 64, (d + n) * 64)
            if s is None:
                o_ref[rows, :] = jnp.zeros((n * 64, TC_REPACK), BF)
            else:
                o_ref[rows, :] = w_ref[s * 64:(s + n) * 64, :].astype(BF)


def _repack_w_in(w_in_t):
    return pl.pallas_call(
        _repack_body,
        grid=(DEPTH, D_MODEL // TC_REPACK),
        in_specs=[pl.BlockSpec((None, N_IN, TC_REPACK), lambda l, i: (l, 0, i))],
        out_specs=[pl.BlockSpec((None, NQ, TC_REPACK), lambda l, i: (l, 0, i)),
                   pl.BlockSpec((None, NZG, TC_REPACK), lambda l, i: (l, 0, i))],
        out_shape=[jax.ShapeDtypeStruct((DEPTH, NQ, D_MODEL), BF),
                   jax.ShapeDtypeStruct((DEPTH, NZG, D_MODEL), BF)],
        compiler_params=_cparams(("arbitrary", "arbitrary")),
        name="repack_w_in",
    )(w_in_t)


def _prep_weights(w_in, c_q_norm, c_kv_norm, w_c_uq, w_c_ukv, d_q_norm, d_k_norm, w_br, w_out, ln_g, ln_b):
    w_qkv, w_zg = _repack_w_in(jnp.swapaxes(w_in, 1, 2))
    uq = w_c_uq.reshape(DEPTH, Q_RANK, H_C, NOPE_DIM + ROPE_DIM)
    wuq = jnp.concatenate([uq, jnp.zeros((DEPTH, Q_RANK, H_C, 2 * LANES - NOPE_DIM - ROPE_DIM), F32)],
                          axis=3).reshape(DEPTH, Q_RANK, H_C * 2 * LANES).astype(BF)
    ukv = w_c_ukv.reshape(DEPTH, KV_RANK, H_C, NOPE_DIM + V_DIM_C)
    wukv_n = ukv[..., :NOPE_DIM].reshape(DEPTH, KV_RANK, H_C * NOPE_DIM).astype(BF)
    wukv_v = ukv[..., NOPE_DIM:].reshape(DEPTH, KV_RANK, H_C * V_DIM_C).astype(BF)
    wbr = jnp.concatenate([w_br[:, :3], _perm_heads(w_br[:, 3:], 2)], axis=1).astype(BF)
    return {
        "w_qkv": w_qkv, "w_zg": w_zg, "wuq": wuq, "wukv_n": wukv_n, "wukv_v": wukv_v,
        "cqn": c_q_norm.reshape(DEPTH, 1, Q_RANK), "ckvn": c_kv_norm.reshape(DEPTH, 1, KV_RANK),
        "dqn": jnp.tile(d_q_norm, (1, H_D)).reshape(DEPTH, 1, H_D * D_D),
        "dkn": jnp.tile(d_k_norm, (1, G_D)).reshape(DEPTH, 1, G_D * D_D),
        "w_br": wbr, "w_out": w_out.astype(BF),
        "ln_g": ln_g.reshape(DEPTH, 1, D_MODEL), "ln_b": ln_b.reshape(DEPTH, 1, D_MODEL),
    }


def kernel(x_prompt, x_sample, cache_a_k, cache_a_v, cache_b_k, cache_b_v, cache_c_kv, cache_c_kr, cache_d_k,
           cache_d_v, c, c_ctx, w_ada, b_ada, w_in, lam_a, a_subln, b_rpb, c_q_norm, c_kv_norm, w_c_uq, w_c_ukv,
           d_q_norm, d_k_norm, w_br, w_out, ln_g, ln_b):
    W = _prep_weights(w_in, c_q_norm, c_kv_norm, w_c_uq, w_c_ukv, d_q_norm, d_k_norm, w_br, w_out, ln_g, ln_b)
    cond8 = jnp.concatenate([c_ctx[None], c, jnp.zeros((5, D_MODEL), F32)], axis=0)
    mod4 = _adaln(cond8, w_ada, b_ada).reshape(DEPTH, 8, 1, 3 * D_MODEL)
    nb_tab = _bias_table(b_rpb).reshape(DEPTH, H_B, 2 * NA_ROWS, GRID_W, LANES)
    kc_cache, vc_cache = _cache_mla(cache_c_kv, cache_c_kr, W["wukv_n"], W["wukv_v"])
    rope_tabs = _rope_tables()
    lam_tab = jnp.array([0.8 - 0.6 * math.exp(-0.3 * l) for l in range(DEPTH)], F32)
    subln = a_subln.reshape(DEPTH, 1, 2 * D_A)
    flat = lambda a: a.reshape(DEC_BATCH, DEPTH, PAST_LEN, -1)
    ca_k, ca_v, cb_k, cb_v, cd_k, cd_v = (flat(a) for a in (cache_a_k, cache_a_v, cache_b_k, cache_b_v,
                                                               cache_d_k, cache_d_v))
    lat3 = lambda a: a.reshape(DEC_BATCH, DEC_SEQ, a.shape[-1])

    def layer(carry, li):
        xp, xs, bufs = carry
        l = li.reshape(1)
        (h_c, qa, qb, qc, kc, vc, qd, ka, va, kb, vb, ckv, kr, kd, vd) = _proj(l, xp, mod4, W, bufs, rope=False)
        pc = dict(qa=qa, ka=ka, va=va, qb=qb, kb=kb, vb=vb, qc=qc, kc=kc, vc=vc, qd=qd, kd=kd, vd=vd)
        outs_c = _ctx_attn(l, lam_tab, pc, lam_a, subln)
        zg_c = _zg(l, h_c, W["w_zg"])
        xp_new = _merge(l, xp, mod4, outs_c, zg_c, W, latent=False)
        (h_l, lqa, lka, lva, lqb, lkb, lvb, lqc, lkc, lvc, lqd, lkd, lvd) = _proj(l, xs, mod4, W, rope_tabs,
                                                                                   rope=True)
        o_a = _lat_a(l, lam_tab, lat3(lqa), lat3(lka), lat3(lva), ca_k, ca_v, lam_a, subln)
        o_b = _lat_b(l, lat3(lqb), lat3(lkb), lat3(lvb), cb_k, cb_v, nb_tab)
        o_c = _lat_c(l, lat3(lqc), lat3(lkc), lat3(lvc), kc_cache, vc_cache)
        o_d = _lat_d(l, lat3(lqd), lat3(lkd), lat3(lvd), cd_k, cd_v)
        outs_l = [o.reshape(N_LAT, 512) for o in (o_a, o_b, o_c, o_d)]
        zg_l = _zg(l, h_l, W["w_zg"])
        xs_new = _merge(l, xs, mod4, outs_l, zg_l, W, latent=True)
        return (xp_new, xs_new, (ka, va, kb, vb, ckv, kr, kd, vd)), None

    bufs0 = tuple(jnp.zeros((BATCH, DEPTH, SEQ, w), F32) for w in CACHE_WIDTHS)
    (xp, xs, caches), _ = lax.scan(
        layer, (x_prompt.reshape(N_CTX, D_MODEL), x_sample.reshape(N_LAT, D_MODEL), bufs0),
        jnp.arange(DEPTH, dtype=jnp.int32))
    ka, va, kb, vb, ckv, kr, kd, vd = caches

    def out(a, tail):
        return a.reshape((BATCH, DEPTH, SEQ) + tail)

    return (xp.reshape(BATCH, SEQ, D_MODEL), xs.reshape(DEC_BATCH, DEC_SEQ, D_MODEL),
            out(ka, (H_A, 2 * D_A)), out(va, (H_A, 2 * D_A)), out(kb, (H_B, D_B)), out(vb, (H_B, D_B)),
            out(ckv, (KV_RANK,)), out(kr, (ROPE_DIM,)), out(kd, (G_D, D_D)), out(vd, (G_D, D_D)))
```

```python
import functools
import math

import jax
import jax.numpy as jnp
import numpy as np
from jax import lax
from jax.experimental import pallas as pl
from jax.experimental.pallas import tpu as pltpu

D_MODEL = 2048
BATCH = 16
SEQ = 256
DEPTH = 4
DEC_BATCH = 2
DEC_SEQ = 4096
PAST_LEN = 256
GRID_W = 64
ROWS = DEC_SEQ // GRID_W
N_BRANCH = 4
BRANCH_W = 512
H_A, D_A = 4, 64
H_B, D_B = 8, 64
NA_ROWS, NA_COLS = 8, 16
H_C, Q_RANK, KV_RANK, NOPE_DIM, ROPE_DIM, V_DIM_C = 4, 512, 256, 128, 64, 128
H_D, G_D, D_D = 8, 2, 64
ROPE_BASE = 10000.0
EPS = 1e-6
ALPHA = (2 * DEPTH) ** 0.25
IN_SIZES = (512, 512, 512, 512, 512, 512, Q_RANK, KV_RANK, ROPE_DIM, 512, 128, 128,
            N_BRANCH * BRANCH_W, N_BRANCH * D_MODEL)

BF = jnp.bfloat16
F32 = jnp.float32
LANES = 128
MXU_N = 256
LOG2E = 1.4426950408889634
VMEM_LIMIT = 56 * 1024 * 1024
NEG = -1e30

N_CTX = BATCH * SEQ
N_LAT = DEC_BATCH * DEC_SEQ
NQ = 4736
NZG = 10240
GQA_PERM = (0, 4, 1, 5, 2, 6, 3, 7)

O_AQ, O_AK, O_AV, O_BQ, O_BK, O_BV = 0, 512, 1024, 1536, 2048, 2560
O_CQ, O_CKV, O_DQ, O_DK, O_DV, O_KR = 3072, 3584, 3840, 4352, 4480, 4608

TM_PROJ = 256
TM_ZG = 1024
TN_ZG = 2048
TM_MERGE = 256
ROW_TILE = 256
TQ_A = 512
TQ_C = 1024
TQ_D = 512
TC_REPACK = 256
NB_ROWS = 4
NB_KROWS = 12


def _cparams(sem):
    return pltpu.CompilerParams(dimension_semantics=sem, vmem_limit_bytes=VMEM_LIMIT)


def _dot(a, b):
    return jnp.dot(a, b, preferred_element_type=F32)


def _dot_nt(a, b):
    return lax.dot_general(a, b, (((1,), (1,)), ((), ())), preferred_element_type=F32)


def _sigmoid(x):
    return 1.0 / (1.0 + jnp.exp(-x))


def _lane_lt64(shape):
    return lax.broadcasted_iota(jnp.int32, shape, len(shape) - 1) < 64


def _adaln_body(c_ref, w_ref, b_ref, o_ref):
    c = c_ref[...]
    s = (c * _sigmoid(c)).astype(BF)
    o_ref[...] = _dot(s, w_ref[...].astype(BF)) + b_ref[...]


def _adaln(cond8, w_ada, b_ada):
    tn = 1536
    return pl.pallas_call(
        _adaln_body,
        grid=(DEPTH, 3 * D_MODEL // tn),
        in_specs=[pl.BlockSpec((8, D_MODEL), lambda l, j: (0, 0)),
                  pl.BlockSpec((None, D_MODEL, tn), lambda l, j: (l, 0, j)),
                  pl.BlockSpec((None, 1, tn), lambda l, j: (l, 0, j))],
        out_specs=pl.BlockSpec((None, 8, tn), lambda l, j: (l, 0, j)),
        out_shape=jax.ShapeDtypeStruct((DEPTH, 8, 3 * D_MODEL), F32),
        compiler_params=_cparams(("arbitrary", "arbitrary")),
        name="adaln",
    )(cond8, w_ada, b_ada.reshape(DEPTH, 1, 3 * D_MODEL))


def _bias_table_body(rpb_ref, o_ref):
    lh = pl.program_id(0)
    qc = lax.broadcasted_iota(jnp.int32, (GRID_W, LANES), 0)
    lane = lax.broadcasted_iota(jnp.int32, (GRID_W, LANES), 1)
    kc = jnp.bitwise_and(lane, 63)
    c0 = jnp.clip(qc - NA_COLS // 2, 0, GRID_W - NA_COLS)
    col_ok = (kc >= c0) & (kc < c0 + NA_COLS)
    dc = jnp.where(col_ok, kc - qc + (NA_COLS - 1), -1)
    right = lane >= 64
    n_dr, n_dc = 2 * NA_ROWS - 1, 2 * NA_COLS - 1
    neg = jnp.full((GRID_W, LANES), NEG, F32)
    rows = []
    for dr in range(n_dr):
        val = neg
        for d in range(n_dc):
            val = jnp.where(dc == d, rpb_ref[(lh * n_dr + dr) * n_dc + d] * LOG2E, val)
        rows.append(val)
    for u in range(n_dr + 1):
        left = rows[u - 1] if u >= 1 else neg
        o_ref[u] = jnp.where(right, rows[u] if u < n_dr else neg, left)


def _bias_table(b_rpb):
    n = DEPTH * H_B
    return pl.pallas_call(
        _bias_table_body,
        grid_spec=pltpu.PrefetchScalarGridSpec(
            num_scalar_prefetch=1, grid=(n,),
            in_specs=[],
            out_specs=pl.BlockSpec((None, 2 * NA_ROWS, GRID_W, LANES), lambda i, r: (i, 0, 0, 0))),
        out_shape=jax.ShapeDtypeStruct((n, 2 * NA_ROWS, GRID_W, LANES), F32),
        compiler_params=_cparams(("arbitrary",)),
        name="nb_bias_table",
    )(b_rpb.reshape(-1))


def _cache_mla_body(ckv_ref, kr_ref, wn_ref, wv_ref, k_ref, v_ref):
    ckv = ckv_ref[...].astype(BF)
    kn = _dot(ckv, wn_ref[...])
    kr = kr_ref[...]
    k_ref[...] = jnp.concatenate(
        [t for h in range(H_C) for t in (kn[:, h * LANES:(h + 1) * LANES], kr)], axis=1).astype(BF)
    v_ref[...] = _dot(ckv, wv_ref[...]).astype(BF)


def _cache_mla(cache_c_kv, cache_c_kr, wukv_n, wukv_v):
    return pl.pallas_call(
        _cache_mla_body,
        grid=(DEPTH, DEC_BATCH),
        in_specs=[pl.BlockSpec((None, None, PAST_LEN, KV_RANK), lambda l, b: (b, l, 0, 0)),
                  pl.BlockSpec((None, None, PAST_LEN, LANES), lambda l, b: (b, l, 0, 0)),
                  pl.BlockSpec((None, KV_RANK, 512), lambda l, b: (l, 0, 0)),
                  pl.BlockSpec((None, KV_RANK, 512), lambda l, b: (l, 0, 0))],
        out_specs=[pl.BlockSpec((None, None, PAST_LEN, 1024), lambda l, b: (l, b, 0, 0)),
                   pl.BlockSpec((None, None, PAST_LEN, 512), lambda l, b: (l, b, 0, 0))],
        out_shape=[jax.ShapeDtypeStruct((DEPTH, DEC_BATCH, PAST_LEN, 1024), BF),
                   jax.ShapeDtypeStruct((DEPTH, DEC_BATCH, PAST_LEN, 512), BF)],
        compiler_params=_cparams(("arbitrary", "arbitrary")),
        name="cache_mla",
    )(cache_c_kv, jnp.pad(cache_c_kr, ((0, 0), (0, 0), (0, 0), (0, LANES - ROPE_DIM))), wukv_n, wukv_v)


def _rope_tiles(x, cos, sa, sb):
    outs = []
    for j in range(x.shape[1] // LANES):
        t = x[:, j * LANES:(j + 1) * LANES]
        outs.append(t * cos + pltpu.roll(t, LANES - 16, 1) * sa + pltpu.roll(t, 16, 1) * sb)
    return outs[0] if len(outs) == 1 else jnp.concatenate(outs, axis=1)


def _group64_rms(x, g):
    w = x.shape[1]
    r = lax.shift_right_logical(lax.broadcasted_iota(jnp.int32, (w, w), 0), 6)
    c = lax.shift_right_logical(lax.broadcasted_iota(jnp.int32, (w, w), 1), 6)
    bd = jnp.where(r == c, 1.0, 0.0).astype(BF)
    x2 = x * x
    hi = x2.astype(BF)
    lo = (x2 - hi.astype(F32)).astype(BF)
    ms = (_dot(hi, bd) + _dot(lo, bd)) * (1.0 / 64)
    return x * lax.rsqrt(ms + EPS) * g


def _interleave_ones(v):
    ones = jnp.ones((v.shape[0], LANES), BF)
    return jnp.concatenate([t for j in range(v.shape[1] // LANES)
                            for t in (v[:, j * LANES:(j + 1) * LANES].astype(BF), ones)], axis=1)


def _full_rms(x, g):
    ms = jnp.mean(x * x, axis=-1, keepdims=True)
    return x * lax.rsqrt(ms + EPS) * g


def _proj_body(l_ref, x_ref, mod_ref, w_ref, wuq_ref, wun_ref, wuv_ref, cqn_ref, ckvn_ref, dqn_ref, dkn_ref,
               *refs, rope):
    if rope:
        cos_ref, sa_ref, sb_ref = refs[:3]
        refs = refs[3:]
        cos, sa, sb = cos_ref[...], sa_ref[...], sb_ref[...]
        rp = lambda t: _rope_tiles(t, cos, sa, sb)
    else:
        rp = lambda t: t
    x = x_ref[...]
    shift = mod_ref[:, 0:D_MODEL]
    scale = mod_ref[:, D_MODEL:2 * D_MODEL]
    h = (x * (1.0 + scale) + shift).astype(BF)

    acc = _dot_nt(h, w_ref[...])

    def col(o, n):
        return acc[:, o:o + n]

    with_ones = _interleave_ones

    qa = rp(col(O_AQ, 512)) * (D_A ** -0.5 * LOG2E)
    ka = rp(col(O_AK, 512))
    va = col(O_AV, 512)
    qb = col(O_BQ, 512) * (D_B ** -0.5 * LOG2E)
    kb = col(O_BK, 512)
    vb = col(O_BV, 512)
    cq = _full_rms(col(O_CQ, Q_RANK), cqn_ref[...]).astype(BF)
    qc_raw = _dot(cq, wuq_ref[...])
    qc_scale = (NOPE_DIM + ROPE_DIM) ** -0.5 * LOG2E
    qc = jnp.concatenate(
        [t for hh in range(H_C) for t in (qc_raw[:, 2 * hh * LANES:(2 * hh + 1) * LANES],
                                          rp(qc_raw[:, (2 * hh + 1) * LANES:(2 * hh + 2) * LANES]))],
        axis=1) * qc_scale
    ckv = _full_rms(col(O_CKV, KV_RANK), ckvn_ref[...])
    ckv_b = ckv.astype(BF)
    kn = _dot(ckv_b, wun_ref[...])
    vc = _dot(ckv_b, wuv_ref[...])
    kr_raw = col(O_KR, LANES)
    kr = rp(kr_raw)
    kc = jnp.concatenate([t for hh in range(H_C) for t in (kn[:, hh * LANES:(hh + 1) * LANES], kr)], axis=1)
    qd = rp(_group64_rms(col(O_DQ, 512), dqn_ref[...])) * (D_D ** -0.5 * LOG2E)
    kd_n = _group64_rms(col(O_DK, LANES), dkn_ref[...])
    kd = rp(kd_n)
    vd = col(O_DV, LANES)

    if rope:
        (h_o, qa_o, ka_o, va_o, qb_o, kb_o, vb_o, qc_o, kc_o, vc_o, qd_o, kd_o, vd_o) = refs
        ka_o[...] = ka.astype(BF)
        va_o[...] = with_ones(va)
        kb_o[...] = kb.astype(BF)
        vb_o[...] = with_ones(vb)
        kd_o[...] = kd.astype(BF)
        vd_o[...] = with_ones(vd)
        vc_o[...] = with_ones(vc)
    else:
        (h_o, qa_o, qb_o, qc_o, kc_o, vc_o, qd_o,
         ka_o, va_o, kb_o, vb_o, ckv_o, kr_o, kd_o, vd_o) = refs[len(CACHE_WIDTHS):]
        ka_o[...] = ka
        va_o[...] = va
        kb_o[...] = kb
        vb_o[...] = vb
        ckv_o[...] = ckv
        kr_o[...] = kr_raw[:, 0:ROPE_DIM]
        kd_o[...] = kd_n
        vd_o[...] = vd
        vc_o[...] = vc.astype(BF)
    h_o[...] = h
    qa_o[...] = qa.astype(BF)
    qb_o[...] = qb.astype(BF)
    qc_o[...] = qc.astype(BF)
    kc_o[...] = kc.astype(BF)
    qd_o[...] = qd.astype(BF)


CACHE_WIDTHS = (512, 512, 512, 512, KV_RANK, ROPE_DIM, LANES, LANES)
KV_LEN = DEC_SEQ + PAST_LEN
LAT_KV_OUTS = (2, 3, 8, 9, 11, 12)


def _proj(l, x, mod4, W, extra, *, rope):
    m = x.shape[0]
    tm = TM_PROJ
    per_b = DEC_SEQ // tm
    cond = (lambda i: 1 + i // per_b) if rope else (lambda i: 0)
    row = lambda w: pl.BlockSpec((tm, w), lambda i, lr: (i, 0))
    wfull = lambda a: pl.BlockSpec((None,) + a.shape[1:], lambda i, lr: (lr[0],) + (0,) * (a.ndim - 1),
                                   pipeline_mode=pl.Buffered(1))
    weights = [W["w_qkv"], W["wuq"], W["wukv_n"], W["wukv_v"], W["cqn"], W["ckvn"], W["dqn"], W["dkn"]]
    in_specs = [row(D_MODEL),
                pl.BlockSpec((None, None, 1, 3 * D_MODEL), lambda i, lr: (lr[0], cond(i), 0, 0))]
    in_specs += [wfull(a) for a in weights]
    args = [x, mod4] + weights
    if rope:
        in_specs += [pl.BlockSpec((tm, LANES), lambda i, lr: (i % per_b, 0))] * 3
        widths = [(D_MODEL, BF), (512, BF), (512, BF), (1024, BF), (512, BF), (512, BF), (1024, BF),
                  (1024, BF), (1024, BF), (1024, BF), (512, BF), (LANES, BF), (2 * LANES, BF)]
        aliases = {}
    else:
        assert tm == SEQ
        in_specs += [pl.BlockSpec(memory_space=pl.ANY)] * len(CACHE_WIDTHS)
        widths = [(D_MODEL, BF), (512, BF), (512, BF), (1024, BF), (1024, BF), (512, BF), (512, BF)]
        aliases = {1 + len(args) + k: len(widths) + k for k in range(len(CACHE_WIDTHS))}
    args += list(extra)
    out_specs = [row(w) for w, _ in widths]
    out_shape = [jax.ShapeDtypeStruct((m, w), d) for w, d in widths]
    if rope:
        for k in LAT_KV_OUTS:
            w, d = widths[k]
            out_specs[k] = pl.BlockSpec((None, tm, w), lambda i, lr: (i // per_b, i % per_b, 0))
            out_shape[k] = jax.ShapeDtypeStruct((DEC_BATCH, KV_LEN, w), d)
    if not rope:
        out_specs += [pl.BlockSpec((None, None, SEQ, w), lambda i, lr: (i, lr[0], 0, 0)) for w in CACHE_WIDTHS]
        out_shape += [jax.ShapeDtypeStruct((BATCH, DEPTH, SEQ, w), F32) for w in CACHE_WIDTHS]
    return pl.pallas_call(
        functools.partial(_proj_body, rope=rope),
        grid_spec=pltpu.PrefetchScalarGridSpec(
            num_scalar_prefetch=1, grid=(m // tm,), in_specs=in_specs, out_specs=out_specs),
        out_shape=out_shape,
        input_output_aliases=aliases,
        compiler_params=_cparams(("arbitrary",)),
        name="proj_lat" if rope else "proj_ctx",
    )(l, *args)


def _cache_fill_body(l_ref, cak_ref, cav_ref, kcc_ref, vcc_ref, cdk_ref, cdv_ref, *refs):
    ka_o, va_o, kc_o, vc_o, kd_o, vd_o = refs[6:]
    ka_o[...] = cak_ref[...].astype(BF)
    va_o[...] = _interleave_ones(cav_ref[...])
    kc_o[...] = kcc_ref[...]
    vc_o[...] = _interleave_ones(vcc_ref[...])
    kd_o[...] = cdk_ref[...].astype(BF)
    vd_o[...] = _interleave_ones(cdv_ref[...])


def _cache_fill(l, caches, bufs):
    ca_k, ca_v, kc_cache, vc_cache, cd_k, cd_v = caches
    by_batch = lambda a: pl.BlockSpec((None, None, PAST_LEN, a.shape[3]), lambda b, lr: (b, lr[0], 0, 0))
    by_layer = lambda a: pl.BlockSpec((None, None, PAST_LEN, a.shape[3]), lambda b, lr: (lr[0], b, 0, 0))
    in_specs = [by_batch(ca_k), by_batch(ca_v), by_layer(kc_cache), by_layer(vc_cache), by_batch(cd_k),
                by_batch(cd_v)] + [pl.BlockSpec(memory_space=pl.ANY)] * len(bufs)
    tail = DEC_SEQ // PAST_LEN
    return pl.pallas_call(
        _cache_fill_body,
        grid_spec=pltpu.PrefetchScalarGridSpec(
            num_scalar_prefetch=1, grid=(DEC_BATCH,), in_specs=in_specs,
            out_specs=[pl.BlockSpec((None, PAST_LEN, a.shape[2]), lambda b, lr: (b, tail, 0)) for a in bufs]),
        out_shape=[jax.ShapeDtypeStruct(a.shape, a.dtype) for a in bufs],
        input_output_aliases={1 + len(caches) + k: k for k in range(len(bufs))},
        compiler_params=_cparams(("arbitrary",)),
        name="cache_fill",
    )(l, *caches, *bufs)


def _zg_body(l_ref, h_ref, w_ref, o_ref):
    is_z = pl.program_id(0) < (N_BRANCH * BRANCH_W) // TN_ZG

    def run(silu):
        h = h_ref[...]
        for n in range(TN_ZG // MXU_N):
            sl = slice(n * MXU_N, (n + 1) * MXU_N)
            a = _dot_nt(h, w_ref[sl, :])
            s = _sigmoid(a)
            o_ref[:, sl] = ((a * s) if silu else s).astype(BF)

    @pl.when(is_z)
    def _():
        run(True)

    @pl.when(jnp.logical_not(is_z))
    def _():
        run(False)


def _zg(l, h, w_zg):
    m = h.shape[0]
    return pl.pallas_call(
        _zg_body,
        grid_spec=pltpu.PrefetchScalarGridSpec(
            num_scalar_prefetch=1, grid=(NZG // TN_ZG, m // TM_ZG),
            in_specs=[pl.BlockSpec((TM_ZG, D_MODEL), lambda j, i, lr: (i, 0)),
                      pl.BlockSpec((None, TN_ZG, D_MODEL), lambda j, i, lr: (lr[0], j, 0))],
            out_specs=pl.BlockSpec((TM_ZG, TN_ZG), lambda j, i, lr: (i, j))),
        out_shape=jax.ShapeDtypeStruct((m, NZG), BF),
        compiler_params=_cparams(("arbitrary", "arbitrary")),
        name="zg_proj",
    )(l, h, w_zg)


def _softmax_pv(scores, values):
    m = None
    for s in scores:
        sm = jnp.max(s, axis=-1, keepdims=True)
        m = sm if m is None else jnp.maximum(m, sm)
    acc = None
    for s, v in zip(scores, values):
        o = _dot(jnp.exp2((s - m).astype(BF)), v)
        acc = o if acc is None else acc + o
    return acc[:, :LANES] / acc[:, LANES:]


def _with_ones(v):
    return jnp.concatenate([v, jnp.ones_like(v)], axis=1)


def _diff_lambda(lam_ref, lam_init):
    la = lam_ref[...]
    s01 = jnp.sum(la[0:1] * la[1:2], axis=-1, keepdims=True)
    s23 = jnp.sum(la[2:3] * la[3:4], axis=-1, keepdims=True)
    return jnp.exp(s01) - jnp.exp(s23) + lam_init


def _diff_head(q, ks, vs, lam, subln, lam_init):
    lt = _lane_lt64(q.shape)
    zero = jnp.zeros_like(q)
    o = []
    for qm in (jnp.where(lt, q, zero), jnp.where(lt, zero, q)):
        o.append(_softmax_pv([_dot_nt(qm, k) for k in ks], vs))
    d = o[0] - lam * o[1]
    ms = jnp.mean(d * d, axis=-1, keepdims=True)
    return d * lax.rsqrt(ms + EPS) * subln * (1.0 - lam_init)


def _pair_heads(q, ks, vs, bias=None):
    lt = _lane_lt64(q.shape)
    zero = jnp.zeros_like(q)
    o = []
    for qm in (jnp.where(lt, q, zero), jnp.where(lt, zero, q)):
        sc = [_dot_nt(qm, k) for k in ks]
        if bias is not None:
            sc[0] = sc[0] + bias
        o.append(_softmax_pv(sc, vs))
    return jnp.where(_lane_lt64(o[0].shape), o[0], o[1])


def _ctx_attn_body(l_ref, li_ref, qa_ref, ka_ref, va_ref, qb_ref, kb_ref, vb_ref, qc_ref, kc_ref, vc_ref,
                   qd_ref, kd_ref, vd_ref, lam_ref, subln_ref, oa_ref, ob_ref, oc_ref, od_ref):
    lam_init = li_ref[l_ref[0]]
    lam = _diff_lambda(lam_ref, lam_init)
    subln = subln_ref[...]
    for h in range(H_A):
        sl = slice(h * LANES, (h + 1) * LANES)
        oa_ref[:, sl] = _diff_head(qa_ref[:, sl], [ka_ref[:, sl].astype(BF)],
                                   [_with_ones(va_ref[:, sl].astype(BF))], lam, subln, lam_init).astype(BF)
    for j in range(H_B // 2):
        sl = slice(j * LANES, (j + 1) * LANES)
        ob_ref[:, sl] = _pair_heads(qb_ref[:, sl], [kb_ref[:, sl].astype(BF)],
                                    [_with_ones(vb_ref[:, sl].astype(BF))]).astype(BF)
    for h in range(H_C):
        oc_ref[:, h * LANES:(h + 1) * LANES] = _softmax_pv(
            [_dot_nt(qc_ref[:, 2 * h * LANES:(2 * h + 2) * LANES], kc_ref[:, 2 * h * LANES:(2 * h + 2) * LANES])],
            [_with_ones(vc_ref[:, h * LANES:(h + 1) * LANES])]).astype(BF)
    kd = kd_ref[...].astype(BF)
    vd = _with_ones(vd_ref[...].astype(BF))
    for j in range(H_D // 2):
        sl = slice(j * LANES, (j + 1) * LANES)
        od_ref[:, sl] = _pair_heads(qd_ref[:, sl], [kd], [vd]).astype(BF)


def _ctx_attn(l, lam_tab, pc, lam_a, a_subln):
    row = lambda w: pl.BlockSpec((SEQ, w), lambda b, lr, li: (b, 0))
    ins = [pc["qa"], pc["ka"], pc["va"], pc["qb"], pc["kb"], pc["vb"], pc["qc"], pc["kc"], pc["vc"],
           pc["qd"], pc["kd"], pc["vd"]]
    layer_row = lambda w: pl.BlockSpec((None, None, SEQ, w), lambda b, lr, li: (b, lr[0], 0, 0))
    in_specs = [row(a.shape[1]) if a.ndim == 2 else layer_row(a.shape[3]) for a in ins]
    in_specs += [pl.BlockSpec((None, 4, D_A), lambda b, lr, li: (lr[0], 0, 0)),
                 pl.BlockSpec((None, 1, 2 * D_A), lambda b, lr, li: (lr[0], 0, 0))]
    return pl.pallas_call(
        _ctx_attn_body,
        grid_spec=pltpu.PrefetchScalarGridSpec(
            num_scalar_prefetch=2, grid=(BATCH,), in_specs=in_specs,
            out_specs=[row(512)] * 4),
        out_shape=[jax.ShapeDtypeStruct((N_CTX, 512), BF)] * 4,
        compiler_params=_cparams(("arbitrary",)),
        name="ctx_attn",
    )(l, lam_tab, *ins, lam_a, a_subln)


def _lat_a_body(l_ref, li_ref, q_ref, k_ref, v_ref, lam_ref, subln_ref, o_ref):
    lam_init = li_ref[l_ref[0]]
    lam = _diff_lambda(lam_ref, lam_init)
    subln = subln_ref[...]
    for h in range(H_A):
        sl = slice(h * LANES, (h + 1) * LANES)
        ks = [k_ref[:, sl]]
        vs = [v_ref[:, 2 * h * LANES:(2 * h + 2) * LANES]]
        for r in range(TQ_A // ROW_TILE):
            rows = slice(r * ROW_TILE, (r + 1) * ROW_TILE)
            o_ref[rows, sl] = _diff_head(q_ref[rows, sl], ks, vs, lam, subln, lam_init).astype(BF)


def _lat_specs(tq, q_w, k_w, v_w):
    qo = lambda w: pl.BlockSpec((None, tq, w), lambda b, i, *pre: (b, i, 0))
    kv = lambda w: pl.BlockSpec((None, KV_LEN, w), lambda b, i, *pre: (b, 0, 0), pipeline_mode=pl.Buffered(1))
    return qo, [qo(q_w), kv(k_w), kv(v_w)]


def _lat_a(l, lam_tab, q, k, v, lam_a, a_subln):
    qo, in_specs = _lat_specs(TQ_A, 512, 512, 1024)
    in_specs += [pl.BlockSpec((None, 4, D_A), lambda b, i, lr, li: (lr[0], 0, 0)),
                 pl.BlockSpec((None, 1, 2 * D_A), lambda b, i, lr, li: (lr[0], 0, 0))]
    return pl.pallas_call(
        _lat_a_body,
        grid_spec=pltpu.PrefetchScalarGridSpec(
            num_scalar_prefetch=2, grid=(DEC_BATCH, DEC_SEQ // TQ_A), in_specs=in_specs, out_specs=qo(512)),
        out_shape=jax.ShapeDtypeStruct((DEC_BATCH, DEC_SEQ, 512), BF),
        compiler_params=_cparams(("arbitrary",) * 2),
        name="lat_attn_a",
    )(l, lam_tab, q, k, v, lam_a, a_subln)


def _lat_c_body(l_ref, q_ref, k_ref, v_ref, o_ref):
    for h in range(H_C):
        sl = slice(h * LANES, (h + 1) * LANES)
        sl2 = slice(2 * h * LANES, (2 * h + 2) * LANES)
        for r in range(TQ_C // ROW_TILE):
            rows = slice(r * ROW_TILE, (r + 1) * ROW_TILE)
            o_ref[rows, sl] = _softmax_pv([_dot_nt(q_ref[rows, sl2], k_ref[:, sl2])], [v_ref[:, sl2]]).astype(BF)


def _lat_c(l, q, k, v):
    qo, in_specs = _lat_specs(TQ_C, 1024, 1024, 1024)
    return pl.pallas_call(
        _lat_c_body,
        grid_spec=pltpu.PrefetchScalarGridSpec(
            num_scalar_prefetch=1, grid=(DEC_BATCH, DEC_SEQ // TQ_C), in_specs=in_specs, out_specs=qo(512)),
        out_shape=jax.ShapeDtypeStruct((DEC_BATCH, DEC_SEQ, 512), BF),
        compiler_params=_cparams(("arbitrary",) * 2),
        name="lat_attn_c",
    )(l, q, k, v)


def _lat_d_body(l_ref, q_ref, k_ref, v_ref, o_ref):
    ks = [k_ref[...]]
    vs = [v_ref[...]]
    for r in range(TQ_D // ROW_TILE):
        rows = slice(r * ROW_TILE, (r + 1) * ROW_TILE)
        for j in range(H_D // 2):
            sl = slice(j * LANES, (j + 1) * LANES)
            o_ref[rows, sl] = _pair_heads(q_ref[rows, sl], ks, vs).astype(BF)


def _lat_d(l, q, k, v):
    qo, in_specs = _lat_specs(TQ_D, 512, LANES, 2 * LANES)
    return pl.pallas_call(
        _lat_d_body,
        grid_spec=pltpu.PrefetchScalarGridSpec(
            num_scalar_prefetch=1, grid=(DEC_BATCH, DEC_SEQ // TQ_D), in_specs=in_specs, out_specs=qo(512)),
        out_shape=jax.ShapeDtypeStruct((DEC_BATCH, DEC_SEQ, 512), BF),
        compiler_params=_cparams(("arbitrary",) * 2),
        name="lat_attn_d",
    )(l, q, k, v)


def _lat_b_body(l_ref, q_ref, k_ref, v_ref, kc_ref, vc_ref, tab_ref, o_ref):
    i = pl.program_id(1)
    qr0 = i * NB_ROWS
    kr0 = jnp.clip(qr0 - NA_ROWS // 2, 0, ROWS - NB_KROWS)
    start = pl.multiple_of(kr0 * GRID_W, GRID_W)
    n_keys = NB_KROWS * GRID_W
    lt = _lane_lt64((GRID_W, LANES))
    for j in range(H_B // 2):
        sl = slice(j * LANES, (j + 1) * LANES)
        kwin = k_ref[pl.ds(start, n_keys), sl]
        vwin = v_ref[pl.ds(start, n_keys), 2 * j * LANES:(2 * j + 2) * LANES]
        ks = [kwin, kc_ref[:, sl].astype(BF)]
        vs = [vwin, _with_ones(vc_ref[:, sl].astype(BF))]
        q = q_ref[:, sl]
        ltq = _lane_lt64(q.shape)
        zero = jnp.zeros_like(q)
        outs = []
        for half, qm in ((0, jnp.where(ltq, q, zero)), (1, jnp.where(ltq, zero, q))):
            head = 2 * j + half
            rows = []
            for a in range(NB_ROWS):
                qr = qr0 + a
                r0 = jnp.clip(qr - NA_ROWS // 2, 0, ROWS - NA_ROWS)
                tiles = []
                for p in range(NB_KROWS // 2):
                    kr_l = kr0 + 2 * p
                    u = jnp.clip(kr_l - qr + NA_ROWS, 0, 2 * NA_ROWS - 1)
                    pen_l = jnp.where((kr_l >= r0) & (kr_l < r0 + NA_ROWS), 0.0, NEG)
                    pen_r = jnp.where((kr_l + 1 >= r0) & (kr_l + 1 < r0 + NA_ROWS), 0.0, NEG)
                    tiles.append(tab_ref[head, u] + jnp.where(lt, pen_l, pen_r))
                rows.append(jnp.concatenate(tiles, axis=1))
            bias = jnp.concatenate(rows, axis=0)
            sc = [_dot_nt(qm, ks[0]) + bias, _dot_nt(qm, ks[1])]
            outs.append(_softmax_pv(sc, vs))
        o_ref[:, sl] = jnp.where(ltq, outs[0], outs[1]).astype(BF)


def _lat_b(l, q, k, v, cache_k, cache_v, tab):
    nq = NB_ROWS * GRID_W
    kv = pl.BlockSpec((None, DEC_SEQ, 512), lambda b, i, lr: (b, 0, 0))
    vv = pl.BlockSpec((None, DEC_SEQ, 1024), lambda b, i, lr: (b, 0, 0))
    cache = pl.BlockSpec((None, None, PAST_LEN, 512), lambda b, i, lr: (b, lr[0], 0, 0))
    qo = pl.BlockSpec((None, nq, 512), lambda b, i, lr: (b, i, 0))
    return pl.pallas_call(
        _lat_b_body,
        grid_spec=pltpu.PrefetchScalarGridSpec(
            num_scalar_prefetch=1, grid=(DEC_BATCH, DEC_SEQ // nq),
            in_specs=[qo, kv, vv, cache, cache,
                      pl.BlockSpec((None, H_B, 2 * NA_ROWS, GRID_W, LANES), lambda b, i, lr: (lr[0], 0, 0, 0, 0))],
            out_specs=qo),
        out_shape=jax.ShapeDtypeStruct((DEC_BATCH, DEC_SEQ, 512), BF),
        compiler_params=_cparams(("arbitrary",) * 2),
        name="lat_attn_b",
    )(l, q, k, v, cache_k, cache_v, tab)


def _merge_body(l_ref, x_ref, mod_ref, oa_ref, ob_ref, oc_ref, od_ref, z_ref, g0_ref, g1_ref, g2_ref, g3_ref,
                wbr_ref, wout_ref, lng_ref, lnb_ref, o_ref):
    merged = None
    for i, (o_r, g_r) in enumerate(zip((oa_ref, ob_ref, oc_ref, od_ref), (g0_ref, g1_ref, g2_ref, g3_ref))):
        u = (o_r[...].astype(F32) * z_ref[:, i * BRANCH_W:(i + 1) * BRANCH_W].astype(F32)).astype(BF)
        term = g_r[...].astype(F32) * _dot(u, wbr_ref[i])
        merged = term if merged is None else merged + term
    y = _dot(merged.astype(BF), wout_ref[...])
    gate = mod_ref[:, 2 * D_MODEL:3 * D_MODEL]
    r = ALPHA * x_ref[...] + gate * y
    mu = jnp.mean(r, axis=-1, keepdims=True)
    d = r - mu
    var = jnp.mean(d * d, axis=-1, keepdims=True)
    o_ref[...] = d * lax.rsqrt(var + EPS) * lng_ref[...] + lnb_ref[...]


def _merge(l, x, mod4, outs, zg, W, *, latent):
    m = x.shape[0]
    tm = TM_MERGE
    per_b = DEC_SEQ // tm
    cond = (lambda i: 1 + i // per_b) if latent else (lambda i: 0)
    row = lambda w: pl.BlockSpec((tm, w), lambda i, lr: (i, 0))
    zgb = lambda j: pl.BlockSpec((tm, D_MODEL), lambda i, lr: (i, j))
    wfull = lambda a: pl.BlockSpec((None,) + a.shape[1:], lambda i, lr: (lr[0],) + (0,) * (a.ndim - 1),
                                   pipeline_mode=pl.Buffered(1))
    in_specs = [row(D_MODEL),
                pl.BlockSpec((None, None, 1, 3 * D_MODEL), lambda i, lr: (lr[0], cond(i), 0, 0)),
                row(512), row(512), row(512), row(512),
                zgb(0), zgb(1), zgb(2), zgb(3), zgb(4),
                wfull(W["w_br"]), wfull(W["w_out"]), wfull(W["ln_g"]), wfull(W["ln_b"])]
    return pl.pallas_call(
        _merge_body,
        grid_spec=pltpu.PrefetchScalarGridSpec(
            num_scalar_prefetch=1, grid=(m // tm,), in_specs=in_specs, out_specs=row(D_MODEL)),
        out_shape=jax.ShapeDtypeStruct((m, D_MODEL), F32),
        input_output_aliases={1: 0},
        compiler_params=_cparams(("arbitrary",)),
        name="merge_lat" if latent else "merge_ctx",
    )(l, x, mod4, *outs, zg, zg, zg, zg, zg, W["w_br"], W["w_out"], W["ln_g"], W["ln_b"])


def _rope_tables():
    t = jnp.arange(DEC_SEQ)
    row = (t // GRID_W).astype(F32)
    col = (t % GRID_W).astype(F32)
    quarter = D_A // 4
    inv_freq = ROPE_BASE ** (-jnp.arange(quarter, dtype=F32) / quarter)
    ar = row[:, None] * inv_freq
    ac = col[:, None] * inv_freq
    ang = jnp.concatenate([ar, ar, ac, ac], axis=-1)
    cos, sin = jnp.cos(ang), jnp.sin(ang)
    even = (jnp.arange(D_A) // quarter) % 2 == 0
    sa = jnp.where(even, -sin, 0.0)
    sb = jnp.where(even, 0.0, sin)
    tile2 = lambda a: jnp.concatenate([a, a], axis=-1)
    return tile2(cos), tile2(sa), tile2(sb)


def _perm_heads(a, axis):
    shp = a.shape
    a = a.reshape(shp[:axis] + (H_D, D_D) + shp[axis + 1:])
    a = jnp.take(a, jnp.array(GQA_PERM), axis=axis)
    return a.reshape(shp)


def _repack_plan():
    offs = [int(v) // 64 for v in np.concatenate([[0], np.cumsum(IN_SIZES)])]
    aq, ak, av, bq, bk, bv, cq, ckv, ckr, dq, dk, dv, z, g = offs[:14]
    gqa = lambda base: [base + h for j in range(H_D // 2) for h in (j, H_D // 2 + j)]
    qkv = list(range(aq, ckr)) + gqa(dq) + [dk, dk + 1, dv, dv + 1, ckr, None]
    z3 = z + 3 * BRANCH_W // 64
    zg = list(range(z, z3)) + gqa(z3) + list(range(g, offs[14]))

    def runs(chunks):
        out = []
        for d, s in enumerate(chunks):
            if out and s is not None and out[-1][1] is not None and out[-1][1] + out[-1][2] == s:
                out[-1][2] += 1
            else:
                out.append([d, s, 1])
        return out
    assert len(qkv) * 64 == NQ and len(zg) * 64 == NZG
    return runs(qkv), runs(zg)


N_IN = sum(IN_SIZES)


def _repack_body(w_ref, qkv_ref, zg_ref):
    qkv_plan, zg_plan = _repack_plan()
    for plan, o_ref in ((qkv_plan, qkv_ref), (zg_plan, zg_ref)):
        for d, s, n in plan:
            rows = slice(d * 64, (d + n) * 64)
            if s is None:
                o_ref[rows, :] = jnp.zeros((n * 64, TC_REPACK), BF)
            else:
                o_ref[rows, :] = w_ref[s * 64:(s + n) * 64, :].astype(BF)


def _repack_w_in(w_in_t):
    return pl.pallas_call(
        _repack_body,
        grid=(DEPTH, D_MODEL // TC_REPACK),
        in_specs=[pl.BlockSpec((None, N_IN, TC_REPACK), lambda l, i: (l, 0, i))],
        out_specs=[pl.BlockSpec((None, NQ, TC_REPACK), lambda l, i: (l, 0, i)),
                   pl.BlockSpec((None, NZG, TC_REPACK), lambda l, i: (l, 0, i))],
        out_shape=[jax.ShapeDtypeStruct((DEPTH, NQ, D_MODEL), BF),
                   jax.ShapeDtypeStruct((DEPTH, NZG, D_MODEL), BF)],
        compiler_params=_cparams(("arbitrary", "arbitrary")),
        name="repack_w_in",
    )(w_in_t)


def _prep_weights(w_in, c_q_norm, c_kv_norm, w_c_uq, w_c_ukv, d_q_norm, d_k_norm, w_br, w_out, ln_g, ln_b):
    w_qkv, w_zg = _repack_w_in(jnp.swapaxes(w_in, 1, 2))
    uq = w_c_uq.reshape(DEPTH, Q_RANK, H_C, NOPE_DIM + ROPE_DIM)
    wuq = jnp.concatenate([uq, jnp.zeros((DEPTH, Q_RANK, H_C, 2 * LANES - NOPE_DIM - ROPE_DIM), F32)],
                          axis=3).reshape(DEPTH, Q_RANK, H_C * 2 * LANES).astype(BF)
    ukv = w_c_ukv.reshape(DEPTH, KV_RANK, H_C, NOPE_DIM + V_DIM_C)
    wukv_n = ukv[..., :NOPE_DIM].reshape(DEPTH, KV_RANK, H_C * NOPE_DIM).astype(BF)
    wukv_v = ukv[..., NOPE_DIM:].reshape(DEPTH, KV_RANK, H_C * V_DIM_C).astype(BF)
    wbr = jnp.concatenate([w_br[:, :3], _perm_heads(w_br[:, 3:], 2)], axis=1).astype(BF)
    return {
        "w_qkv": w_qkv, "w_zg": w_zg, "wuq": wuq, "wukv_n": wukv_n, "wukv_v": wukv_v,
        "cqn": c_q_norm.reshape(DEPTH, 1, Q_RANK), "ckvn": c_kv_norm.reshape(DEPTH, 1, KV_RANK),
        "dqn": jnp.tile(d_q_norm, (1, H_D)).reshape(DEPTH, 1, H_D * D_D),
        "dkn": jnp.tile(d_k_norm, (1, G_D)).reshape(DEPTH, 1, G_D * D_D),
        "w_br": wbr, "w_out": w_out.astype(BF),
        "ln_g": ln_g.reshape(DEPTH, 1, D_MODEL), "ln_b": ln_b.reshape(DEPTH, 1, D_MODEL),
    }


def kernel(x_prompt, x_sample, cache_a_k, cache_a_v, cache_b_k, cache_b_v, cache_c_kv, cache_c_kr, cache_d_k,
           cache_d_v, c, c_ctx, w_ada, b_ada, w_in, lam_a, a_subln, b_rpb, c_q_norm, c_kv_norm, w_c_uq, w_c_ukv,
           d_q_norm, d_k_norm, w_br, w_out, ln_g, ln_b):
    W = _prep_weights(w_in, c_q_norm, c_kv_norm, w_c_uq, w_c_ukv, d_q_norm, d_k_norm, w_br, w_out, ln_g, ln_b)
    cond8 = jnp.concatenate([c_ctx[None], c, jnp.zeros((5, D_MODEL), F32)], axis=0)
    mod4 = _adaln(cond8, w_ada, b_ada).reshape(DEPTH, 8, 1, 3 * D_MODEL)
    nb_tab = _bias_table(b_rpb).reshape(DEPTH, H_B, 2 * NA_ROWS, GRID_W, LANES)
    kc_cache, vc_cache = _cache_mla(cache_c_kv, cache_c_kr, W["wukv_n"], W["wukv_v"])
    rope_tabs = _rope_tables()
    lam_tab = jnp.array([0.8 - 0.6 * math.exp(-0.3 * l) for l in range(DEPTH)], F32)
    subln = a_subln.reshape(DEPTH, 1, 2 * D_A)
    flat = lambda a: a.reshape(DEC_BATCH, DEPTH, PAST_LEN, -1)
    ca_k, ca_v, cb_k, cb_v, cd_k, cd_v = (flat(a) for a in (cache_a_k, cache_a_v, cache_b_k, cache_b_v,
                                                               cache_d_k, cache_d_v))
    lat3 = lambda a: a.reshape(DEC_BATCH, DEC_SEQ, a.shape[-1])

    def layer(carry, li):
        xp, xs, bufs = carry
        l = li.reshape(1)
        (h_c, qa, qb, qc, kc, vc, qd, ka, va, kb, vb, ckv, kr, kd, vd) = _proj(l, xp, mod4, W, bufs, rope=False)
        pc = dict(qa=qa, ka=ka, va=va, qb=qb, kb=kb, vb=vb, qc=qc, kc=kc, vc=vc, qd=qd, kd=kd, vd=vd)
        outs_c = _ctx_attn(l, lam_tab, pc, lam_a, subln)
        zg_c = _zg(l, h_c, W["w_zg"])
        xp_new = _merge(l, xp, mod4, outs_c, zg_c, W, latent=False)
        (h_l, lqa, lka, lva, lqb, lkb, lvb, lqc, lkc, lvc, lqd, lkd, lvd) = _proj(l, xs, mod4, W, rope_tabs,
                                                                                   rope=True)
        lka, lva, lkc, lvc, lkd, lvd = _cache_fill(l, (ca_k, ca_v, kc_cache, vc_cache, cd_k, cd_v),
                                                   (lka, lva, lkc, lvc, lkd, lvd))
        o_a = _lat_a(l, lam_tab, lat3(lqa), lka, lva, lam_a, subln)
        o_b = _lat_b(l, lat3(lqb), lat3(lkb), lat3(lvb), cb_k, cb_v, nb_tab)
        o_c = _lat_c(l, lat3(lqc), lkc, lvc)
        o_d = _lat_d(l, lat3(lqd), lkd, lvd)
        outs_l = [o.reshape(N_LAT, 512) for o in (o_a, o_b, o_c, o_d)]
        zg_l = _zg(l, h_l, W["w_zg"])
        xs_new = _merge(l, xs, mod4, outs_l, zg_l, W, latent=True)
        return (xp_new, xs_new, (ka, va, kb, vb, ckv, kr, kd, vd)), None

    bufs0 = tuple(jnp.zeros((BATCH, DEPTH, SEQ, w), F32) for w in CACHE_WIDTHS)
    (xp, xs, caches), _ = lax.scan(
        layer, (x_prompt.reshape(N_CTX, D_MODEL), x_sample.reshape(N_LAT, D_MODEL), bufs0),
        jnp.arange(DEPTH, dtype=jnp.int32))
    ka, va, kb, vb, ckv, kr, kd, vd = caches

    def out(a, tail):
        return a.reshape((BATCH, DEPTH, SEQ) + tail)

    return (xp.reshape(BATCH, SEQ, D_MODEL), xs.reshape(DEC_BATCH, DEC_SEQ, D_MODEL),
            out(ka, (H_A, 2 * D_A)), out(va, (H_A, 2 * D_A)), out(kb, (H_B, D_B)), out(vb, (H_B, D_B)),
            out(ckv, (KV_RANK,)), out(kr, (ROPE_DIM,)), out(kd, (G_D, D_D)), out(vd, (G_D, D_D)))
```

```python
import functools
import math

import jax
import jax.numpy as jnp
import numpy as np
from jax import lax
from jax.experimental import pallas as pl
from jax.experimental.pallas import tpu as pltpu

D_MODEL = 2048
BATCH = 16
SEQ = 256
DEPTH = 4
DEC_BATCH = 2
DEC_SEQ = 4096
PAST_LEN = 256
GRID_W = 64
ROWS = DEC_SEQ // GRID_W
N_BRANCH = 4
BRANCH_W = 512
H_A, D_A = 4, 64
H_B, D_B = 8, 64
NA_ROWS, NA_COLS = 8, 16
H_C, Q_RANK, KV_RANK, NOPE_DIM, ROPE_DIM, V_DIM_C = 4, 512, 256, 128, 64, 128
H_D, G_D, D_D = 8, 2, 64
ROPE_BASE = 10000.0
EPS = 1e-6
ALPHA = (2 * DEPTH) ** 0.25
IN_SIZES = (512, 512, 512, 512, 512, 512, Q_RANK, KV_RANK, ROPE_DIM, 512, 128, 128,
            N_BRANCH * BRANCH_W, N_BRANCH * D_MODEL)

BF = jnp.bfloat16
F32 = jnp.float32
LANES = 128
MXU_N = 256
LOG2E = 1.4426950408889634
VMEM_LIMIT = 56 * 1024 * 1024
NEG = -1e30

N_CTX = BATCH * SEQ
N_LAT = DEC_BATCH * DEC_SEQ
NQ = 4736
NZG = 10240
GQA_PERM = (0, 4, 1, 5, 2, 6, 3, 7)

O_AQ, O_AK, O_AV, O_BQ, O_BK, O_BV = 0, 512, 1024, 1536, 2048, 2560
O_CQ, O_CKV, O_DQ, O_DK, O_DV, O_KR = 3072, 3584, 3840, 4352, 4480, 4608

TM_PROJ = 256
TM_ZG = 1024
TN_ZG = 2048
TM_MERGE = 512
ROW_TILE = 256
TQ_A = 512
TQ_C = 1024
TQ_D = 512
TC_REPACK = 256
NB_ROWS = 4
NB_KROWS = 12


def _cparams(sem):
    return pltpu.CompilerParams(dimension_semantics=sem, vmem_limit_bytes=VMEM_LIMIT)


def _dot(a, b):
    return jnp.dot(a, b, preferred_element_type=F32)


def _dot_nt(a, b):
    return lax.dot_general(a, b, (((1,), (1,)), ((), ())), preferred_element_type=F32)


def _sigmoid(x):
    return 1.0 / (1.0 + jnp.exp(-x))


def _lane_lt64(shape):
    return lax.broadcasted_iota(jnp.int32, shape, len(shape) - 1) < 64


def _adaln_body(c_ref, w_ref, b_ref, o_ref):
    c = c_ref[...]
    s = (c * _sigmoid(c)).astype(BF)
    o_ref[...] = _dot(s, w_ref[...].astype(BF)) + b_ref[...]


def _adaln(cond8, w_ada, b_ada):
    tn = 1536
    return pl.pallas_call(
        _adaln_body,
        grid=(DEPTH, 3 * D_MODEL // tn),
        in_specs=[pl.BlockSpec((8, D_MODEL), lambda l, j: (0, 0)),
                  pl.BlockSpec((None, D_MODEL, tn), lambda l, j: (l, 0, j)),
                  pl.BlockSpec((None, 1, tn), lambda l, j: (l, 0, j))],
        out_specs=pl.BlockSpec((None, 8, tn), lambda l, j: (l, 0, j)),
        out_shape=jax.ShapeDtypeStruct((DEPTH, 8, 3 * D_MODEL), F32),
        compiler_params=_cparams(("arbitrary", "arbitrary")),
        name="adaln",
    )(cond8, w_ada, b_ada.reshape(DEPTH, 1, 3 * D_MODEL))


def _bias_table_body(rpb_ref, o_ref):
    lh = pl.program_id(0)
    qc = lax.broadcasted_iota(jnp.int32, (GRID_W, LANES), 0)
    lane = lax.broadcasted_iota(jnp.int32, (GRID_W, LANES), 1)
    kc = jnp.bitwise_and(lane, 63)
    c0 = jnp.clip(qc - NA_COLS // 2, 0, GRID_W - NA_COLS)
    col_ok = (kc >= c0) & (kc < c0 + NA_COLS)
    dc = jnp.where(col_ok, kc - qc + (NA_COLS - 1), -1)
    right = lane >= 64
    n_dr, n_dc = 2 * NA_ROWS - 1, 2 * NA_COLS - 1
    neg = jnp.full((GRID_W, LANES), NEG, F32)
    rows = []
    for dr in range(n_dr):
        val = neg
        for d in range(n_dc):
            val = jnp.where(dc == d, rpb_ref[(lh * n_dr + dr) * n_dc + d] * LOG2E, val)
        rows.append(val)
    for u in range(n_dr + 1):
        left = rows[u - 1] if u >= 1 else neg
        o_ref[u] = jnp.where(right, rows[u] if u < n_dr else neg, left)


def _bias_table(b_rpb):
    n = DEPTH * H_B
    return pl.pallas_call(
        _bias_table_body,
        grid_spec=pltpu.PrefetchScalarGridSpec(
            num_scalar_prefetch=1, grid=(n,),
            in_specs=[],
            out_specs=pl.BlockSpec((None, 2 * NA_ROWS, GRID_W, LANES), lambda i, r: (i, 0, 0, 0))),
        out_shape=jax.ShapeDtypeStruct((n, 2 * NA_ROWS, GRID_W, LANES), F32),
        compiler_params=_cparams(("arbitrary",)),
        name="nb_bias_table",
    )(b_rpb.reshape(-1))


def _cache_mla_body(ckv_ref, kr_ref, wn_ref, wv_ref, k_ref, v_ref):
    ckv = ckv_ref[...].astype(BF)
    kn = _dot(ckv, wn_ref[...])
    kr = kr_ref[...]
    k_ref[...] = jnp.concatenate(
        [t for h in range(H_C) for t in (kn[:, h * LANES:(h + 1) * LANES], kr)], axis=1).astype(BF)
    v_ref[...] = _dot(ckv, wv_ref[...]).astype(BF)


def _cache_mla(cache_c_kv, cache_c_kr, wukv_n, wukv_v):
    return pl.pallas_call(
        _cache_mla_body,
        grid=(DEPTH, DEC_BATCH),
        in_specs=[pl.BlockSpec((None, None, PAST_LEN, KV_RANK), lambda l, b: (b, l, 0, 0)),
                  pl.BlockSpec((None, None, PAST_LEN, LANES), lambda l, b: (b, l, 0, 0)),
                  pl.BlockSpec((None, KV_RANK, 512), lambda l, b: (l, 0, 0)),
                  pl.BlockSpec((None, KV_RANK, 512), lambda l, b: (l, 0, 0))],
        out_specs=[pl.BlockSpec((None, None, PAST_LEN, 1024), lambda l, b: (l, b, 0, 0)),
                   pl.BlockSpec((None, None, PAST_LEN, 512), lambda l, b: (l, b, 0, 0))],
        out_shape=[jax.ShapeDtypeStruct((DEPTH, DEC_BATCH, PAST_LEN, 1024), BF),
                   jax.ShapeDtypeStruct((DEPTH, DEC_BATCH, PAST_LEN, 512), BF)],
        compiler_params=_cparams(("arbitrary", "arbitrary")),
        name="cache_mla",
    )(cache_c_kv, jnp.pad(cache_c_kr, ((0, 0), (0, 0), (0, 0), (0, LANES - ROPE_DIM))), wukv_n, wukv_v)


def _rope_tiles(x, cos, sa, sb):
    outs = []
    for j in range(x.shape[1] // LANES):
        t = x[:, j * LANES:(j + 1) * LANES]
        outs.append(t * cos + pltpu.roll(t, LANES - 16, 1) * sa + pltpu.roll(t, 16, 1) * sb)
    return outs[0] if len(outs) == 1 else jnp.concatenate(outs, axis=1)


def _group64_rms(x, g):
    w = x.shape[1]
    r = lax.shift_right_logical(lax.broadcasted_iota(jnp.int32, (w, w), 0), 6)
    c = lax.shift_right_logical(lax.broadcasted_iota(jnp.int32, (w, w), 1), 6)
    bd = jnp.where(r == c, 1.0, 0.0).astype(BF)
    x2 = x * x
    hi = x2.astype(BF)
    lo = (x2 - hi.astype(F32)).astype(BF)
    ms = (_dot(hi, bd) + _dot(lo, bd)) * (1.0 / 64)
    return x * lax.rsqrt(ms + EPS) * g


def _interleave_ones(v):
    ones = jnp.ones((v.shape[0], LANES), BF)
    return jnp.concatenate([t for j in range(v.shape[1] // LANES)
                            for t in (v[:, j * LANES:(j + 1) * LANES].astype(BF), ones)], axis=1)


def _full_rms(x, g):
    ms = jnp.mean(x * x, axis=-1, keepdims=True)
    return x * lax.rsqrt(ms + EPS) * g


def _proj_body(l_ref, x_ref, mod_ref, w_ref, wuq_ref, wun_ref, wuv_ref, cqn_ref, ckvn_ref, dqn_ref, dkn_ref,
               *refs, rope):
    if rope:
        cos_ref, sa_ref, sb_ref = refs[:3]
        refs = refs[3:]
        cos, sa, sb = cos_ref[...], sa_ref[...], sb_ref[...]
        rp = lambda t: _rope_tiles(t, cos, sa, sb)
    else:
        rp = lambda t: t
    x = x_ref[...]
    shift = mod_ref[:, 0:D_MODEL]
    scale = mod_ref[:, D_MODEL:2 * D_MODEL]
    h = (x * (1.0 + scale) + shift).astype(BF)

    acc = _dot_nt(h, w_ref[...])

    def col(o, n):
        return acc[:, o:o + n]

    with_ones = _interleave_ones

    qa = rp(col(O_AQ, 512)) * (D_A ** -0.5 * LOG2E)
    ka = rp(col(O_AK, 512))
    va = col(O_AV, 512)
    qb = col(O_BQ, 512) * (D_B ** -0.5 * LOG2E)
    kb = col(O_BK, 512)
    vb = col(O_BV, 512)
    cq = _full_rms(col(O_CQ, Q_RANK), cqn_ref[...]).astype(BF)
    qc_raw = _dot(cq, wuq_ref[...])
    qc_scale = (NOPE_DIM + ROPE_DIM) ** -0.5 * LOG2E
    qc = jnp.concatenate(
        [t for hh in range(H_C) for t in (qc_raw[:, 2 * hh * LANES:(2 * hh + 1) * LANES],
                                          rp(qc_raw[:, (2 * hh + 1) * LANES:(2 * hh + 2) * LANES]))],
        axis=1) * qc_scale
    ckv = _full_rms(col(O_CKV, KV_RANK), ckvn_ref[...])
    ckv_b = ckv.astype(BF)
    kn = _dot(ckv_b, wun_ref[...])
    vc = _dot(ckv_b, wuv_ref[...])
    kr_raw = col(O_KR, LANES)
    kr = rp(kr_raw)
    kc = jnp.concatenate([t for hh in range(H_C) for t in (kn[:, hh * LANES:(hh + 1) * LANES], kr)], axis=1)
    qd = rp(_group64_rms(col(O_DQ, 512), dqn_ref[...])) * (D_D ** -0.5 * LOG2E)
    kd_n = _group64_rms(col(O_DK, LANES), dkn_ref[...])
    kd = rp(kd_n)
    vd = col(O_DV, LANES)

    if rope:
        (h_o, qa_o, ka_o, va_o, qb_o, kb_o, vb_o, qc_o, kc_o, vc_o, qd_o, kd_o, vd_o) = refs
        ka_o[...] = ka.astype(BF)
        va_o[...] = with_ones(va)
        kb_o[...] = kb.astype(BF)
        vb_o[...] = with_ones(vb)
        kd_o[...] = kd.astype(BF)
        vd_o[...] = with_ones(vd)
        vc_o[...] = with_ones(vc)
    else:
        (h_o, qa_o, qb_o, qc_o, kc_o, vc_o, qd_o,
         ka_o, va_o, kb_o, vb_o, ckv_o, kr_o, kd_o, vd_o) = refs[len(CACHE_WIDTHS):]
        ka_o[...] = ka
        va_o[...] = va
        kb_o[...] = kb
        vb_o[...] = vb
        ckv_o[...] = ckv
        kr_o[...] = kr_raw[:, 0:ROPE_DIM]
        kd_o[...] = kd_n
        vd_o[...] = vd
        vc_o[...] = vc.astype(BF)
    h_o[...] = h
    qa_o[...] = qa.astype(BF)
    qb_o[...] = qb.astype(BF)
    qc_o[...] = qc.astype(BF)
    kc_o[...] = kc.astype(BF)
    qd_o[...] = qd.astype(BF)


CACHE_WIDTHS = (512, 512, 512, 512, KV_RANK, ROPE_DIM, LANES, LANES)
KV_LEN = DEC_SEQ + PAST_LEN
LAT_KV_OUTS = (2, 3, 8, 9, 11, 12)


def _proj(l, x, mod4, W, extra, *, rope):
    m = x.shape[0]
    tm = TM_PROJ
    per_b = DEC_SEQ // tm
    cond = (lambda i: 1 + i // per_b) if rope else (lambda i: 0)
    row = lambda w: pl.BlockSpec((tm, w), lambda i, lr: (i, 0))
    wfull = lambda a: pl.BlockSpec((None,) + a.shape[1:], lambda i, lr: (lr[0],) + (0,) * (a.ndim - 1),
                                   pipeline_mode=pl.Buffered(1))
    weights = [W["w_qkv"], W["wuq"], W["wukv_n"], W["wukv_v"], W["cqn"], W["ckvn"], W["dqn"], W["dkn"]]
    in_specs = [row(D_MODEL),
                pl.BlockSpec((None, None, 1, 3 * D_MODEL), lambda i, lr: (lr[0], cond(i), 0, 0))]
    in_specs += [wfull(a) for a in weights]
    args = [x, mod4] + weights
    if rope:
        in_specs += [pl.BlockSpec((tm, LANES), lambda i, lr: (i % per_b, 0))] * 3
        widths = [(D_MODEL, BF), (512, BF), (512, BF), (1024, BF), (512, BF), (512, BF), (1024, BF),
                  (1024, BF), (1024, BF), (1024, BF), (512, BF), (LANES, BF), (2 * LANES, BF)]
        aliases = {}
    else:
        assert tm == SEQ
        in_specs += [pl.BlockSpec(memory_space=pl.ANY)] * len(CACHE_WIDTHS)
        widths = [(D_MODEL, BF), (512, BF), (512, BF), (1024, BF), (1024, BF), (512, BF), (512, BF)]
        aliases = {1 + len(args) + k: len(widths) + k for k in range(len(CACHE_WIDTHS))}
    args += list(extra)
    out_specs = [row(w) for w, _ in widths]
    out_shape = [jax.ShapeDtypeStruct((m, w), d) for w, d in widths]
    if rope:
        for k in LAT_KV_OUTS:
            w, d = widths[k]
            out_specs[k] = pl.BlockSpec((None, tm, w), lambda i, lr: (i // per_b, i % per_b, 0))
            out_shape[k] = jax.ShapeDtypeStruct((DEC_BATCH, KV_LEN, w), d)
    if not rope:
        out_specs += [pl.BlockSpec((None, None, SEQ, w), lambda i, lr: (i, lr[0], 0, 0)) for w in CACHE_WIDTHS]
        out_shape += [jax.ShapeDtypeStruct((BATCH, DEPTH, SEQ, w), F32) for w in CACHE_WIDTHS]
    return pl.pallas_call(
        functools.partial(_proj_body, rope=rope),
        grid_spec=pltpu.PrefetchScalarGridSpec(
            num_scalar_prefetch=1, grid=(m // tm,), in_specs=in_specs, out_specs=out_specs),
        out_shape=out_shape,
        input_output_aliases=aliases,
        compiler_params=_cparams(("arbitrary",)),
        name="proj_lat" if rope else "proj_ctx",
    )(l, *args)


def _cache_fill_body(l_ref, cak_ref, cav_ref, kcc_ref, vcc_ref, cdk_ref, cdv_ref, *refs):
    ka_o, va_o, kc_o, vc_o, kd_o, vd_o = refs[6:]
    ka_o[...] = cak_ref[...].astype(BF)
    va_o[...] = _interleave_ones(cav_ref[...])
    kc_o[...] = kcc_ref[...]
    vc_o[...] = _interleave_ones(vcc_ref[...])
    kd_o[...] = cdk_ref[...].astype(BF)
    vd_o[...] = _interleave_ones(cdv_ref[...])


def _cache_fill(l, caches, bufs):
    ca_k, ca_v, kc_cache, vc_cache, cd_k, cd_v = caches
    by_batch = lambda a: pl.BlockSpec((None, None, PAST_LEN, a.shape[3]), lambda b, lr: (b, lr[0], 0, 0))
    by_layer = lambda a: pl.BlockSpec((None, None, PAST_LEN, a.shape[3]), lambda b, lr: (lr[0], b, 0, 0))
    in_specs = [by_batch(ca_k), by_batch(ca_v), by_layer(kc_cache), by_layer(vc_cache), by_batch(cd_k),
                by_batch(cd_v)] + [pl.BlockSpec(memory_space=pl.ANY)] * len(bufs)
    tail = DEC_SEQ // PAST_LEN
    return pl.pallas_call(
        _cache_fill_body,
        grid_spec=pltpu.PrefetchScalarGridSpec(
            num_scalar_prefetch=1, grid=(DEC_BATCH,), in_specs=in_specs,
            out_specs=[pl.BlockSpec((None, PAST_LEN, a.shape[2]), lambda b, lr: (b, tail, 0)) for a in bufs]),
        out_shape=[jax.ShapeDtypeStruct(a.shape, a.dtype) for a in bufs],
        input_output_aliases={1 + len(caches) + k: k for k in range(len(bufs))},
        compiler_params=_cparams(("arbitrary",)),
        name="cache_fill",
    )(l, *caches, *bufs)


def _zg_body(l_ref, h_ref, w_ref, o_ref):
    is_z = pl.program_id(0) < (N_BRANCH * BRANCH_W) // TN_ZG

    def run(silu):
        h = h_ref[...]
        for n in range(TN_ZG // MXU_N):
            sl = slice(n * MXU_N, (n + 1) * MXU_N)
            a = _dot_nt(h, w_ref[sl, :])
            s = _sigmoid(a)
            o_ref[:, sl] = ((a * s) if silu else s).astype(BF)

    @pl.when(is_z)
    def _():
        run(True)

    @pl.when(jnp.logical_not(is_z))
    def _():
        run(False)


def _zg(l, h, w_zg):
    m = h.shape[0]
    return pl.pallas_call(
        _zg_body,
        grid_spec=pltpu.PrefetchScalarGridSpec(
            num_scalar_prefetch=1, grid=(NZG // TN_ZG, m // TM_ZG),
            in_specs=[pl.BlockSpec((TM_ZG, D_MODEL), lambda j, i, lr: (i, 0)),
                      pl.BlockSpec((None, TN_ZG, D_MODEL), lambda j, i, lr: (lr[0], j, 0))],
            out_specs=pl.BlockSpec((TM_ZG, TN_ZG), lambda j, i, lr: (i, j))),
        out_shape=jax.ShapeDtypeStruct((m, NZG), BF),
        compiler_params=_cparams(("arbitrary", "arbitrary")),
        name="zg_proj",
    )(l, h, w_zg)


def _softmax_pv(scores, values):
    m = None
    for s in scores:
        sm = jnp.max(s, axis=-1, keepdims=True)
        m = sm if m is None else jnp.maximum(m, sm)
    acc = None
    for s, v in zip(scores, values):
        o = _dot(jnp.exp2((s - m).astype(BF)), v)
        acc = o if acc is None else acc + o
    return acc[:, :LANES] / acc[:, LANES:]


def _with_ones(v):
    return jnp.concatenate([v, jnp.ones_like(v)], axis=1)


def _diff_lambda(lam_ref, lam_init):
    la = lam_ref[...]
    s01 = jnp.sum(la[0:1] * la[1:2], axis=-1, keepdims=True)
    s23 = jnp.sum(la[2:3] * la[3:4], axis=-1, keepdims=True)
    return jnp.exp(s01) - jnp.exp(s23) + lam_init


def _diff_head(q, ks, vs, lam, subln, lam_init):
    lt = _lane_lt64(q.shape)
    zero = jnp.zeros_like(q)
    o = []
    for qm in (jnp.where(lt, q, zero), jnp.where(lt, zero, q)):
        o.append(_softmax_pv([_dot_nt(qm, k) for k in ks], vs))
    d = o[0] - lam * o[1]
    ms = jnp.mean(d * d, axis=-1, keepdims=True)
    return d * lax.rsqrt(ms + EPS) * subln * (1.0 - lam_init)


def _pair_heads(q, ks, vs, bias=None):
    lt = _lane_lt64(q.shape)
    zero = jnp.zeros_like(q)
    o = []
    for qm in (jnp.where(lt, q, zero), jnp.where(lt, zero, q)):
        sc = [_dot_nt(qm, k) for k in ks]
        if bias is not None:
            sc[0] = sc[0] + bias
        o.append(_softmax_pv(sc, vs))
    return jnp.where(_lane_lt64(o[0].shape), o[0], o[1])


def _ctx_attn_body(l_ref, li_ref, qa_ref, ka_ref, va_ref, qb_ref, kb_ref, vb_ref, qc_ref, kc_ref, vc_ref,
                   qd_ref, kd_ref, vd_ref, lam_ref, subln_ref, oa_ref, ob_ref, oc_ref, od_ref):
    lam_init = li_ref[l_ref[0]]
    lam = _diff_lambda(lam_ref, lam_init)
    subln = subln_ref[...]
    for h in range(H_A):
        sl = slice(h * LANES, (h + 1) * LANES)
        oa_ref[:, sl] = _diff_head(qa_ref[:, sl], [ka_ref[:, sl].astype(BF)],
                                   [_with_ones(va_ref[:, sl].astype(BF))], lam, subln, lam_init).astype(BF)
    for j in range(H_B // 2):
        sl = slice(j * LANES, (j + 1) * LANES)
        ob_ref[:, sl] = _pair_heads(qb_ref[:, sl], [kb_ref[:, sl].astype(BF)],
                                    [_with_ones(vb_ref[:, sl].astype(BF))]).astype(BF)
    for h in range(H_C):
        oc_ref[:, h * LANES:(h + 1) * LANES] = _softmax_pv(
            [_dot_nt(qc_ref[:, 2 * h * LANES:(2 * h + 2) * LANES], kc_ref[:, 2 * h * LANES:(2 * h + 2) * LANES])],
            [_with_ones(vc_ref[:, h * LANES:(h + 1) * LANES])]).astype(BF)
    kd = kd_ref[...].astype(BF)
    vd = _with_ones(vd_ref[...].astype(BF))
    for j in range(H_D // 2):
        sl = slice(j * LANES, (j + 1) * LANES)
        od_ref[:, sl] = _pair_heads(qd_ref[:, sl], [kd], [vd]).astype(BF)


def _ctx_attn(l, lam_tab, pc, lam_a, a_subln):
    row = lambda w: pl.BlockSpec((SEQ, w), lambda b, lr, li: (b, 0))
    ins = [pc["qa"], pc["ka"], pc["va"], pc["qb"], pc["kb"], pc["vb"], pc["qc"], pc["kc"], pc["vc"],
           pc["qd"], pc["kd"], pc["vd"]]
    layer_row = lambda w: pl.BlockSpec((None, None, SEQ, w), lambda b, lr, li: (b, lr[0], 0, 0))
    in_specs = [row(a.shape[1]) if a.ndim == 2 else layer_row(a.shape[3]) for a in ins]
    in_specs += [pl.BlockSpec((None, 4, D_A), lambda b, lr, li: (lr[0], 0, 0)),
                 pl.BlockSpec((None, 1, 2 * D_A), lambda b, lr, li: (lr[0], 0, 0))]
    return pl.pallas_call(
        _ctx_attn_body,
        grid_spec=pltpu.PrefetchScalarGridSpec(
            num_scalar_prefetch=2, grid=(BATCH,), in_specs=in_specs,
            out_specs=[row(512)] * 4),
        out_shape=[jax.ShapeDtypeStruct((N_CTX, 512), BF)] * 4,
        compiler_params=_cparams(("arbitrary",)),
        name="ctx_attn",
    )(l, lam_tab, *ins, lam_a, a_subln)


def _lat_a_body(l_ref, li_ref, q_ref, k_ref, v_ref, lam_ref, subln_ref, o_ref):
    lam_init = li_ref[l_ref[0]]
    lam = _diff_lambda(lam_ref, lam_init)
    subln = subln_ref[...]
    for h in range(H_A):
        sl = slice(h * LANES, (h + 1) * LANES)
        ks = [k_ref[:, sl]]
        vs = [v_ref[:, 2 * h * LANES:(2 * h + 2) * LANES]]
        for r in range(TQ_A // ROW_TILE):
            rows = slice(r * ROW_TILE, (r + 1) * ROW_TILE)
            o_ref[rows, sl] = _diff_head(q_ref[rows, sl], ks, vs, lam, subln, lam_init).astype(BF)


def _lat_specs(tq, q_w, k_w, v_w):
    qo = lambda w: pl.BlockSpec((None, tq, w), lambda b, i, *pre: (b, i, 0))
    kv = lambda w: pl.BlockSpec((None, KV_LEN, w), lambda b, i, *pre: (b, 0, 0), pipeline_mode=pl.Buffered(1))
    return qo, [qo(q_w), kv(k_w), kv(v_w)]


def _lat_a(l, lam_tab, q, k, v, lam_a, a_subln):
    qo, in_specs = _lat_specs(TQ_A, 512, 512, 1024)
    in_specs += [pl.BlockSpec((None, 4, D_A), lambda b, i, lr, li: (lr[0], 0, 0)),
                 pl.BlockSpec((None, 1, 2 * D_A), lambda b, i, lr, li: (lr[0], 0, 0))]
    return pl.pallas_call(
        _lat_a_body,
        grid_spec=pltpu.PrefetchScalarGridSpec(
            num_scalar_prefetch=2, grid=(DEC_BATCH, DEC_SEQ // TQ_A), in_specs=in_specs, out_specs=qo(512)),
        out_shape=jax.ShapeDtypeStruct((DEC_BATCH, DEC_SEQ, 512), BF),
        compiler_params=_cparams(("arbitrary",) * 2),
        name="lat_attn_a",
    )(l, lam_tab, q, k, v, lam_a, a_subln)


def _lat_c_body(l_ref, q_ref, k_ref, v_ref, o_ref):
    for h in range(H_C):
        sl = slice(h * LANES, (h + 1) * LANES)
        sl2 = slice(2 * h * LANES, (2 * h + 2) * LANES)
        for r in range(TQ_C // ROW_TILE):
            rows = slice(r * ROW_TILE, (r + 1) * ROW_TILE)
            o_ref[rows, sl] = _softmax_pv([_dot_nt(q_ref[rows, sl2], k_ref[:, sl2])], [v_ref[:, sl2]]).astype(BF)


def _lat_c(l, q, k, v):
    qo, in_specs = _lat_specs(TQ_C, 1024, 1024, 1024)
    return pl.pallas_call(
        _lat_c_body,
        grid_spec=pltpu.PrefetchScalarGridSpec(
            num_scalar_prefetch=1, grid=(DEC_BATCH, DEC_SEQ // TQ_C), in_specs=in_specs, out_specs=qo(512)),
        out_shape=jax.ShapeDtypeStruct((DEC_BATCH, DEC_SEQ, 512), BF),
        compiler_params=_cparams(("arbitrary",) * 2),
        name="lat_attn_c",
    )(l, q, k, v)


def _lat_d_body(l_ref, q_ref, k_ref, v_ref, o_ref):
    ks = [k_ref[...]]
    vs = [v_ref[...]]
    for r in range(TQ_D // ROW_TILE):
        rows = slice(r * ROW_TILE, (r + 1) * ROW_TILE)
        for j in range(H_D // 2):
            sl = slice(j * LANES, (j + 1) * LANES)
            o_ref[rows, sl] = _pair_heads(q_ref[rows, sl], ks, vs).astype(BF)


def _lat_d(l, q, k, v):
    qo, in_specs = _lat_specs(TQ_D, 512, LANES, 2 * LANES)
    return pl.pallas_call(
        _lat_d_body,
        grid_spec=pltpu.PrefetchScalarGridSpec(
            num_scalar_prefetch=1, grid=(DEC_BATCH, DEC_SEQ // TQ_D), in_specs=in_specs, out_specs=qo(512)),
        out_shape=jax.ShapeDtypeStruct((DEC_BATCH, DEC_SEQ, 512), BF),
        compiler_params=_cparams(("arbitrary",) * 2),
        name="lat_attn_d",
    )(l, q, k, v)


def _lat_b_body(l_ref, q_ref, k_ref, v_ref, kc_ref, vc_ref, tab_ref, o_ref):
    i = pl.program_id(1)
    qr0 = i * NB_ROWS
    kr0 = jnp.clip(qr0 - NA_ROWS // 2, 0, ROWS - NB_KROWS)
    start = pl.multiple_of(kr0 * GRID_W, GRID_W)
    n_keys = NB_KROWS * GRID_W
    lt = _lane_lt64((GRID_W, LANES))
    for j in range(H_B // 2):
        sl = slice(j * LANES, (j + 1) * LANES)
        kwin = k_ref[pl.ds(start, n_keys), sl]
        vwin = v_ref[pl.ds(start, n_keys), 2 * j * LANES:(2 * j + 2) * LANES]
        ks = [kwin, kc_ref[:, sl].astype(BF)]
        vs = [vwin, _with_ones(vc_ref[:, sl].astype(BF))]
        q = q_ref[:, sl]
        ltq = _lane_lt64(q.shape)
        zero = jnp.zeros_like(q)
        outs = []
        for half, qm in ((0, jnp.where(ltq, q, zero)), (1, jnp.where(ltq, zero, q))):
            head = 2 * j + half
            rows = []
            for a in range(NB_ROWS):
                qr = qr0 + a
                r0 = jnp.clip(qr - NA_ROWS // 2, 0, ROWS - NA_ROWS)
                tiles = []
                for p in range(NB_KROWS // 2):
                    kr_l = kr0 + 2 * p
                    u = jnp.clip(kr_l - qr + NA_ROWS, 0, 2 * NA_ROWS - 1)
                    pen_l = jnp.where((kr_l >= r0) & (kr_l < r0 + NA_ROWS), 0.0, NEG)
                    pen_r = jnp.where((kr_l + 1 >= r0) & (kr_l + 1 < r0 + NA_ROWS), 0.0, NEG)
                    tiles.append(tab_ref[head, u] + jnp.where(lt, pen_l, pen_r))
                rows.append(jnp.concatenate(tiles, axis=1))
            bias = jnp.concatenate(rows, axis=0)
            sc = [_dot_nt(qm, ks[0]) + bias, _dot_nt(qm, ks[1])]
            outs.append(_softmax_pv(sc, vs))
        o_ref[:, sl] = jnp.where(ltq, outs[0], outs[1]).astype(BF)


def _lat_b(l, q, k, v, cache_k, cache_v, tab):
    nq = NB_ROWS * GRID_W
    kv = pl.BlockSpec((None, DEC_SEQ, 512), lambda b, i, lr: (b, 0, 0))
    vv = pl.BlockSpec((None, DEC_SEQ, 1024), lambda b, i, lr: (b, 0, 0))
    cache = pl.BlockSpec((None, None, PAST_LEN, 512), lambda b, i, lr: (b, lr[0], 0, 0))
    qo = pl.BlockSpec((None, nq, 512), lambda b, i, lr: (b, i, 0))
    return pl.pallas_call(
        _lat_b_body,
        grid_spec=pltpu.PrefetchScalarGridSpec(
            num_scalar_prefetch=1, grid=(DEC_BATCH, DEC_SEQ // nq),
            in_specs=[qo, kv, vv, cache, cache,
                      pl.BlockSpec((None, H_B, 2 * NA_ROWS, GRID_W, LANES), lambda b, i, lr: (lr[0], 0, 0, 0, 0))],
            out_specs=qo),
        out_shape=jax.ShapeDtypeStruct((DEC_BATCH, DEC_SEQ, 512), BF),
        compiler_params=_cparams(("arbitrary",) * 2),
        name="lat_attn_b",
    )(l, q, k, v, cache_k, cache_v, tab)


def _branch_merge_body(l_ref, oa_ref, ob_ref, oc_ref, od_ref, z_ref, g0_ref, g1_ref, g2_ref, g3_ref, wbr_ref,
                       o_ref):
    merged = None
    for i, (o_r, g_r) in enumerate(zip((oa_ref, ob_ref, oc_ref, od_ref), (g0_ref, g1_ref, g2_ref, g3_ref))):
        u = (o_r[...].astype(F32) * z_ref[:, i * BRANCH_W:(i + 1) * BRANCH_W].astype(F32)).astype(BF)
        term = g_r[...].astype(F32) * _dot(u, wbr_ref[i])
        merged = term if merged is None else merged + term
    o_ref[...] = merged.astype(BF)


def _out_norm_body(l_ref, x_ref, mod_ref, m_ref, wout_ref, lng_ref, lnb_ref, o_ref):
    y = _dot(m_ref[...], wout_ref[...])
    gate = mod_ref[:, 2 * D_MODEL:3 * D_MODEL]
    r = ALPHA * x_ref[...] + gate * y
    mu = jnp.mean(r, axis=-1, keepdims=True)
    d = r - mu
    var = jnp.mean(d * d, axis=-1, keepdims=True)
    o_ref[...] = d * lax.rsqrt(var + EPS) * lng_ref[...] + lnb_ref[...]


def _merge(l, x, mod4, outs, zg, W, *, latent):
    m = x.shape[0]
    tm = TM_MERGE
    per_b = DEC_SEQ // tm
    cond = (lambda i: 1 + i // per_b) if latent else (lambda i: 0)
    row = lambda w: pl.BlockSpec((tm, w), lambda i, lr: (i, 0))
    zgb = lambda j: pl.BlockSpec((tm, D_MODEL), lambda i, lr: (i, j))
    wfull = lambda a: pl.BlockSpec((None,) + a.shape[1:], lambda i, lr: (lr[0],) + (0,) * (a.ndim - 1),
                                   pipeline_mode=pl.Buffered(1))
    merged = pl.pallas_call(
        _branch_merge_body,
        grid_spec=pltpu.PrefetchScalarGridSpec(
            num_scalar_prefetch=1, grid=(m // tm,),
            in_specs=[row(512), row(512), row(512), row(512), zgb(0), zgb(1), zgb(2), zgb(3), zgb(4),
                      wfull(W["w_br"])],
            out_specs=row(D_MODEL)),
        out_shape=jax.ShapeDtypeStruct((m, D_MODEL), BF),
        compiler_params=_cparams(("arbitrary",)),
        name="branch_merge_lat" if latent else "branch_merge_ctx",
    )(l, *outs, zg, zg, zg, zg, zg, W["w_br"])
    return pl.pallas_call(
        _out_norm_body,
        grid_spec=pltpu.PrefetchScalarGridSpec(
            num_scalar_prefetch=1, grid=(m // tm,),
            in_specs=[row(D_MODEL),
                      pl.BlockSpec((None, None, 1, 3 * D_MODEL), lambda i, lr: (lr[0], cond(i), 0, 0)),
                      row(D_MODEL), wfull(W["w_out"]), wfull(W["ln_g"]), wfull(W["ln_b"])],
            out_specs=row(D_MODEL)),
        out_shape=jax.ShapeDtypeStruct((m, D_MODEL), F32),
        input_output_aliases={1: 0},
        compiler_params=_cparams(("arbitrary",)),
        name="out_norm_lat" if latent else "out_norm_ctx",
    )(l, x, mod4, merged, W["w_out"], W["ln_g"], W["ln_b"])


def _rope_tables():
    t = jnp.arange(DEC_SEQ)
    row = (t // GRID_W).astype(F32)
    col = (t % GRID_W).astype(F32)
    quarter = D_A // 4
    inv_freq = ROPE_BASE ** (-jnp.arange(quarter, dtype=F32) / quarter)
    ar = row[:, None] * inv_freq
    ac = col[:, None] * inv_freq
    ang = jnp.concatenate([ar, ar, ac, ac], axis=-1)
    cos, sin = jnp.cos(ang), jnp.sin(ang)
    even = (jnp.arange(D_A) // quarter) % 2 == 0
    sa = jnp.where(even, -sin, 0.0)
    sb = jnp.where(even, 0.0, sin)
    tile2 = lambda a: jnp.concatenate([a, a], axis=-1)
    return tile2(cos), tile2(sa), tile2(sb)


def _perm_heads(a, axis):
    shp = a.shape
    a = a.reshape(shp[:axis] + (H_D, D_D) + shp[axis + 1:])
    a = jnp.take(a, jnp.array(GQA_PERM), axis=axis)
    return a.reshape(shp)


def _repack_plan():
    offs = [int(v) // 64 for v in np.concatenate([[0], np.cumsum(IN_SIZES)])]
    aq, ak, av, bq, bk, bv, cq, ckv, ckr, dq, dk, dv, z, g = offs[:14]
    gqa = lambda base: [base + h for j in range(H_D // 2) for h in (j, H_D // 2 + j)]
    qkv = list(range(aq, ckr)) + gqa(dq) + [dk, dk + 1, dv, dv + 1, ckr, None]
    z3 = z + 3 * BRANCH_W // 64
    zg = list(range(z, z3)) + gqa(z3) + list(range(g, offs[14]))

    def runs(chunks):
        out = []
        for d, s in enumerate(chunks):
            if out and s is not None and out[-1][1] is not None and out[-1][1] + out[-1][2] == s:
                out[-1][2] += 1
            else:
                out.append([d, s, 1])
        return out
    assert len(qkv) * 64 == NQ and len(zg) * 64 == NZG
    return runs(qkv), runs(zg)


N_IN = sum(IN_SIZES)


def _repack_body(w_ref, qkv_ref, zg_ref):
    qkv_plan, zg_plan = _repack_plan()
    for plan, o_ref in ((qkv_plan, qkv_ref), (zg_plan, zg_ref)):
        for d, s, n in plan:
            rows = slice(d * 64, (d + n) * 64)
            if s is None:
                o_ref[rows, :] = jnp.zeros((n * 64, TC_REPACK), BF)
            else:
                o_ref[rows, :] = w_ref[s * 64:(s + n) * 64, :].astype(BF)


def _repack_w_in(w_in_t):
    return pl.pallas_call(
        _repack_body,
        grid=(DEPTH, D_MODEL // TC_REPACK),
        in_specs=[pl.BlockSpec((None, N_IN, TC_REPACK), lambda l, i: (l, 0, i))],
        out_specs=[pl.BlockSpec((None, NQ, TC_REPACK), lambda l, i: (l, 0, i)),
                   pl.BlockSpec((None, NZG, TC_REPACK), lambda l, i: (l, 0, i))],
        out_shape=[jax.ShapeDtypeStruct((DEPTH, NQ, D_MODEL), BF),
                   jax.ShapeDtypeStruct((DEPTH, NZG, D_MODEL), BF)],
        compiler_params=_cparams(("arbitrary", "arbitrary")),
        name="repack_w_in",
    )(w_in_t)


def _prep_weights(w_in, c_q_norm, c_kv_norm, w_c_uq, w_c_ukv, d_q_norm, d_k_norm, w_br, w_out, ln_g, ln_b):
    w_qkv, w_zg = _repack_w_in(jnp.swapaxes(w_in, 1, 2))
    uq = w_c_uq.reshape(DEPTH, Q_RANK, H_C, NOPE_DIM + ROPE_DIM)
    wuq = jnp.concatenate([uq, jnp.zeros((DEPTH, Q_RANK, H_C, 2 * LANES - NOPE_DIM - ROPE_DIM), F32)],
                          axis=3).reshape(DEPTH, Q_RANK, H_C * 2 * LANES).astype(BF)
    ukv = w_c_ukv.reshape(DEPTH, KV_RANK, H_C, NOPE_DIM + V_DIM_C)
    wukv_n = ukv[..., :NOPE_DIM].reshape(DEPTH, KV_RANK, H_C * NOPE_DIM).astype(BF)
    wukv_v = ukv[..., NOPE_DIM:].reshape(DEPTH, KV_RANK, H_C * V_DIM_C).astype(BF)
    wbr = jnp.concatenate([w_br[:, :3], _perm_heads(w_br[:, 3:], 2)], axis=1).astype(BF)
    return {
        "w_qkv": w_qkv, "w_zg": w_zg, "wuq": wuq, "wukv_n": wukv_n, "wukv_v": wukv_v,
        "cqn": c_q_norm.reshape(DEPTH, 1, Q_RANK), "ckvn": c_kv_norm.reshape(DEPTH, 1, KV_RANK),
        "dqn": jnp.tile(d_q_norm, (1, H_D)).reshape(DEPTH, 1, H_D * D_D),
        "dkn": jnp.tile(d_k_norm, (1, G_D)).reshape(DEPTH, 1, G_D * D_D),
        "w_br": wbr, "w_out": w_out.astype(BF),
        "ln_g": ln_g.reshape(DEPTH, 1, D_MODEL), "ln_b": ln_b.reshape(DEPTH, 1, D_MODEL),
    }


def kernel(x_prompt, x_sample, cache_a_k, cache_a_v, cache_b_k, cache_b_v, cache_c_kv, cache_c_kr, cache_d_k,
           cache_d_v, c, c_ctx, w_ada, b_ada, w_in, lam_a, a_subln, b_rpb, c_q_norm, c_kv_norm, w_c_uq, w_c_ukv,
           d_q_norm, d_k_norm, w_br, w_out, ln_g, ln_b):
    W = _prep_weights(w_in, c_q_norm, c_kv_norm, w_c_uq, w_c_ukv, d_q_norm, d_k_norm, w_br, w_out, ln_g, ln_b)
    cond8 = jnp.concatenate([c_ctx[None], c, jnp.zeros((5, D_MODEL), F32)], axis=0)
    mod4 = _adaln(cond8, w_ada, b_ada).reshape(DEPTH, 8, 1, 3 * D_MODEL)
    nb_tab = _bias_table(b_rpb).reshape(DEPTH, H_B, 2 * NA_ROWS, GRID_W, LANES)
    kc_cache, vc_cache = _cache_mla(cache_c_kv, cache_c_kr, W["wukv_n"], W["wukv_v"])
    rope_tabs = _rope_tables()
    lam_tab = jnp.array([0.8 - 0.6 * math.exp(-0.3 * l) for l in range(DEPTH)], F32)
    subln = a_subln.reshape(DEPTH, 1, 2 * D_A)
    flat = lambda a: a.reshape(DEC_BATCH, DEPTH, PAST_LEN, -1)
    ca_k, ca_v, cb_k, cb_v, cd_k, cd_v = (flat(a) for a in (cache_a_k, cache_a_v, cache_b_k, cache_b_v,
                                                               cache_d_k, cache_d_v))
    lat3 = lambda a: a.reshape(DEC_BATCH, DEC_SEQ, a.shape[-1])

    def layer(carry, li):
        xp, xs, bufs = carry
        l = li.reshape(1)
        (h_c, qa, qb, qc, kc, vc, qd, ka, va, kb, vb, ckv, kr, kd, vd) = _proj(l, xp, mod4, W, bufs, rope=False)
        pc = dict(qa=qa, ka=ka, va=va, qb=qb, kb=kb, vb=vb, qc=qc, kc=kc, vc=vc, qd=qd, kd=kd, vd=vd)
        outs_c = _ctx_attn(l, lam_tab, pc, lam_a, subln)
        zg_c = _zg(l, h_c, W["w_zg"])
        xp_new = _merge(l, xp, mod4, outs_c, zg_c, W, latent=False)
        (h_l, lqa, lka, lva, lqb, lkb, lvb, lqc, lkc, lvc, lqd, lkd, lvd) = _proj(l, xs, mod4, W, rope_tabs,
                                                                                   rope=True)
        lka, lva, lkc, lvc, lkd, lvd = _cache_fill(l, (ca_k, ca_v, kc_cache, vc_cache, cd_k, cd_v),
                                                   (lka, lva, lkc, lvc, lkd, lvd))
        o_a = _lat_a(l, lam_tab, lat3(lqa), lka, lva, lam_a, subln)
        o_b = _lat_b(l, lat3(lqb), lat3(lkb), lat3(lvb), cb_k, cb_v, nb_tab)
        o_c = _lat_c(l, lat3(lqc), lkc, lvc)
        o_d = _lat_d(l, lat3(lqd), lkd, lvd)
        outs_l = [o.reshape(N_LAT, 512) for o in (o_a, o_b, o_c, o_d)]
        zg_l = _zg(l, h_l, W["w_zg"])
        xs_new = _merge(l, xs, mod4, outs_l, zg_l, W, latent=True)
        return (xp_new, xs_new, (ka, va, kb, vb, ckv, kr, kd, vd)), None

    bufs0 = tuple(jnp.zeros((BATCH, DEPTH, SEQ, w), F32) for w in CACHE_WIDTHS)
    (xp, xs, caches), _ = lax.scan(
        layer, (x_prompt.reshape(N_CTX, D_MODEL), x_sample.reshape(N_LAT, D_MODEL), bufs0),
        jnp.arange(DEPTH, dtype=jnp.int32))
    ka, va, kb, vb, ckv, kr, kd, vd = caches

    def out(a, tail):
        return a.reshape((BATCH, DEPTH, SEQ) + tail)

    return (xp.reshape(BATCH, SEQ, D_MODEL), xs.reshape(DEC_BATCH, DEC_SEQ, D_MODEL),
            out(ka, (H_A, 2 * D_A)), out(va, (H_A, 2 * D_A)), out(kb, (H_B, D_B)), out(vb, (H_B, D_B)),
            out(ckv, (KV_RANK,)), out(kr, (ROPE_DIM,)), out(kd, (G_D, D_D)), out(vd, (G_D, D_D)))
```

```python
import functools
import math

import jax
import jax.numpy as jnp
import numpy as np
from jax import lax
from jax.experimental import pallas as pl
from jax.experimental.pallas import tpu as pltpu

D_MODEL = 2048
BATCH = 16
SEQ = 256
DEPTH = 4
DEC_BATCH = 2
DEC_SEQ = 4096
PAST_LEN = 256
GRID_W = 64
ROWS = DEC_SEQ // GRID_W
N_BRANCH = 4
BRANCH_W = 512
H_A, D_A = 4, 64
H_B, D_B = 8, 64
NA_ROWS, NA_COLS = 8, 16
H_C, Q_RANK, KV_RANK, NOPE_DIM, ROPE_DIM, V_DIM_C = 4, 512, 256, 128, 64, 128
H_D, G_D, D_D = 8, 2, 64
ROPE_BASE = 10000.0
EPS = 1e-6
ALPHA = (2 * DEPTH) ** 0.25
IN_SIZES = (512, 512, 512, 512, 512, 512, Q_RANK, KV_RANK, ROPE_DIM, 512, 128, 128,
            N_BRANCH * BRANCH_W, N_BRANCH * D_MODEL)

BF = jnp.bfloat16
F32 = jnp.float32
LANES = 128
MXU_N = 256
LOG2E = 1.4426950408889634
VMEM_LIMIT = 56 * 1024 * 1024
NEG = -1e30

N_CTX = BATCH * SEQ
N_LAT = DEC_BATCH * DEC_SEQ
NQ = 4736
NZG = 10240

O_AQ, O_AK, O_AV, O_BQ, O_BK, O_BV = 0, 512, 1024, 1536, 2048, 2560
O_CQ, O_CKV, O_DQ, O_DK, O_DV, O_KR = 3072, 3584, 3840, 4352, 4480, 4608

TM_PROJ = 256
TM_ZG = 1024
TN_ZG = 2048
TM_MERGE = 256
ROW_TILE = 256
TQ_A = 512
TQ_C = 1024
TQ_D = 512
TC_REPACK = 256
NB_ROWS = 4
NB_KROWS = 12


def _cparams(sem):
    return pltpu.CompilerParams(dimension_semantics=sem, vmem_limit_bytes=VMEM_LIMIT)


def _dot(a, b):
    return jnp.dot(a, b, preferred_element_type=F32)


def _dot_nt(a, b):
    return lax.dot_general(a, b, (((1,), (1,)), ((), ())), preferred_element_type=F32)


def _sigmoid(x):
    return 1.0 / (1.0 + jnp.exp(-x))


def _lane_lt64(shape):
    return lax.broadcasted_iota(jnp.int32, shape, len(shape) - 1) < 64


def _adaln_body(c_ref, w_ref, b_ref, o_ref):
    c = c_ref[...]
    s = (c * _sigmoid(c)).astype(BF)
    o_ref[...] = _dot(s, w_ref[...].astype(BF)) + b_ref[...]


def _adaln(cond8, w_ada, b_ada):
    tn = 1536
    return pl.pallas_call(
        _adaln_body,
        grid=(DEPTH, 3 * D_MODEL // tn),
        in_specs=[pl.BlockSpec((8, D_MODEL), lambda l, j: (0, 0)),
                  pl.BlockSpec((None, D_MODEL, tn), lambda l, j: (l, 0, j)),
                  pl.BlockSpec((None, 1, tn), lambda l, j: (l, 0, j))],
        out_specs=pl.BlockSpec((None, 8, tn), lambda l, j: (l, 0, j)),
        out_shape=jax.ShapeDtypeStruct((DEPTH, 8, 3 * D_MODEL), F32),
        compiler_params=_cparams(("arbitrary", "arbitrary")),
        name="adaln",
    )(cond8, w_ada, b_ada.reshape(DEPTH, 1, 3 * D_MODEL))


def _bias_table_body(rpb_ref, o_ref):
    lh = pl.program_id(0)
    qc = lax.broadcasted_iota(jnp.int32, (GRID_W, LANES), 0)
    lane = lax.broadcasted_iota(jnp.int32, (GRID_W, LANES), 1)
    kc = jnp.bitwise_and(lane, 63)
    c0 = jnp.clip(qc - NA_COLS // 2, 0, GRID_W - NA_COLS)
    col_ok = (kc >= c0) & (kc < c0 + NA_COLS)
    dc = jnp.where(col_ok, kc - qc + (NA_COLS - 1), -1)
    right = lane >= 64
    n_dr, n_dc = 2 * NA_ROWS - 1, 2 * NA_COLS - 1
    neg = jnp.full((GRID_W, LANES), NEG, F32)
    rows = []
    for dr in range(n_dr):
        val = neg
        for d in range(n_dc):
            val = jnp.where(dc == d, rpb_ref[(lh * n_dr + dr) * n_dc + d] * LOG2E, val)
        rows.append(val)
    for u in range(n_dr + 1):
        left = rows[u - 1] if u >= 1 else neg
        o_ref[u] = jnp.where(right, rows[u] if u < n_dr else neg, left)


def _bias_table(b_rpb):
    n = DEPTH * H_B
    return pl.pallas_call(
        _bias_table_body,
        grid_spec=pltpu.PrefetchScalarGridSpec(
            num_scalar_prefetch=1, grid=(n,),
            in_specs=[],
            out_specs=pl.BlockSpec((None, 2 * NA_ROWS, GRID_W, LANES), lambda i, r: (i, 0, 0, 0))),
        out_shape=jax.ShapeDtypeStruct((n, 2 * NA_ROWS, GRID_W, LANES), F32),
        compiler_params=_cparams(("arbitrary",)),
        name="nb_bias_table",
    )(b_rpb.reshape(-1))


def _cache_mla_body(ckv_ref, kr_ref, wn_ref, wv_ref, k_ref, v_ref):
    ckv = ckv_ref[...].astype(BF)
    kn = _dot(ckv, wn_ref[...])
    kr = kr_ref[...]
    k_ref[...] = jnp.concatenate(
        [t for h in range(H_C) for t in (kn[:, h * LANES:(h + 1) * LANES], kr)], axis=1).astype(BF)
    v_ref[...] = _dot(ckv, wv_ref[...]).astype(BF)


def _cache_mla(cache_c_kv, cache_c_kr, wukv_n, wukv_v):
    return pl.pallas_call(
        _cache_mla_body,
        grid=(DEPTH, DEC_BATCH),
        in_specs=[pl.BlockSpec((None, None, PAST_LEN, KV_RANK), lambda l, b: (b, l, 0, 0)),
                  pl.BlockSpec((None, None, PAST_LEN, LANES), lambda l, b: (b, l, 0, 0)),
                  pl.BlockSpec((None, KV_RANK, 512), lambda l, b: (l, 0, 0)),
                  pl.BlockSpec((None, KV_RANK, 512), lambda l, b: (l, 0, 0))],
        out_specs=[pl.BlockSpec((None, None, PAST_LEN, 1024), lambda l, b: (l, b, 0, 0)),
                   pl.BlockSpec((None, None, PAST_LEN, 512), lambda l, b: (l, b, 0, 0))],
        out_shape=[jax.ShapeDtypeStruct((DEPTH, DEC_BATCH, PAST_LEN, 1024), BF),
                   jax.ShapeDtypeStruct((DEPTH, DEC_BATCH, PAST_LEN, 512), BF)],
        compiler_params=_cparams(("arbitrary", "arbitrary")),
        name="cache_mla",
    )(cache_c_kv, jnp.pad(cache_c_kr, ((0, 0), (0, 0), (0, 0), (0, LANES - ROPE_DIM))), wukv_n, wukv_v)


def _rope_tiles(x, cos, sa, sb):
    outs = []
    for j in range(x.shape[1] // LANES):
        t = x[:, j * LANES:(j + 1) * LANES]
        outs.append(t * cos + pltpu.roll(t, LANES - 16, 1) * sa + pltpu.roll(t, 16, 1) * sb)
    return outs[0] if len(outs) == 1 else jnp.concatenate(outs, axis=1)


def _group64_rms(x, g):
    w = x.shape[1]
    r = lax.shift_right_logical(lax.broadcasted_iota(jnp.int32, (w, w), 0), 6)
    c = lax.shift_right_logical(lax.broadcasted_iota(jnp.int32, (w, w), 1), 6)
    bd = jnp.where(r == c, 1.0, 0.0).astype(BF)
    x2 = x * x
    hi = x2.astype(BF)
    lo = (x2 - hi.astype(F32)).astype(BF)
    ms = (_dot(hi, bd) + _dot(lo, bd)) * (1.0 / 64)
    return x * lax.rsqrt(ms + EPS) * g


def _interleave_ones(v):
    ones = jnp.ones((v.shape[0], LANES), BF)
    return jnp.concatenate([t for j in range(v.shape[1] // LANES)
                            for t in (v[:, j * LANES:(j + 1) * LANES].astype(BF), ones)], axis=1)


def _full_rms(x, g):
    ms = jnp.mean(x * x, axis=-1, keepdims=True)
    return x * lax.rsqrt(ms + EPS) * g


def _proj_body(l_ref, x_ref, mod_ref, w_ref, wuq_ref, wun_ref, wuv_ref, cqn_ref, ckvn_ref, dqn_ref, dkn_ref,
               *refs, rope):
    if rope:
        cos_ref, sa_ref, sb_ref = refs[:3]
        refs = refs[3:]
        cos, sa, sb = cos_ref[...], sa_ref[...], sb_ref[...]
        rp = lambda t: _rope_tiles(t, cos, sa, sb)
    else:
        rp = lambda t: t
    x = x_ref[...]
    shift = mod_ref[:, 0:D_MODEL]
    scale = mod_ref[:, D_MODEL:2 * D_MODEL]
    h = (x * (1.0 + scale) + shift).astype(BF)

    acc = _dot_nt(h, w_ref[...])

    def col(o, n):
        return acc[:, o:o + n]

    with_ones = _interleave_ones

    qa = rp(col(O_AQ, 512)) * (D_A ** -0.5 * LOG2E)
    ka = rp(col(O_AK, 512))
    va = col(O_AV, 512)
    qb = col(O_BQ, 512) * (D_B ** -0.5 * LOG2E)
    kb = col(O_BK, 512)
    vb = col(O_BV, 512)
    cq = _full_rms(col(O_CQ, Q_RANK), cqn_ref[...]).astype(BF)
    qc_raw = _dot(cq, wuq_ref[...])
    qc_scale = (NOPE_DIM + ROPE_DIM) ** -0.5 * LOG2E
    qc = jnp.concatenate(
        [t for hh in range(H_C) for t in (qc_raw[:, 2 * hh * LANES:(2 * hh + 1) * LANES],
                                          rp(qc_raw[:, (2 * hh + 1) * LANES:(2 * hh + 2) * LANES]))],
        axis=1) * qc_scale
    ckv = _full_rms(col(O_CKV, KV_RANK), ckvn_ref[...])
    ckv_b = ckv.astype(BF)
    kn = _dot(ckv_b, wun_ref[...])
    vc = _dot(ckv_b, wuv_ref[...])
    kr_raw = col(O_KR, LANES)
    kr = rp(kr_raw)
    kc = jnp.concatenate([t for hh in range(H_C) for t in (kn[:, hh * LANES:(hh + 1) * LANES], kr)], axis=1)
    qd = rp(_group64_rms(col(O_DQ, 512), dqn_ref[...])) * (D_D ** -0.5 * LOG2E)
    kd_n = _group64_rms(col(O_DK, LANES), dkn_ref[...])
    kd = rp(kd_n)
    vd = col(O_DV, LANES)

    if rope:
        (h_o, qa_o, ka_o, va_o, qb_o, kb_o, vb_o, qc_o, kc_o, vc_o, qd_o, kd_o, vd_o) = refs
        ka_o[...] = ka.astype(BF)
        va_o[...] = with_ones(va)
        kb_o[...] = kb.astype(BF)
        vb_o[...] = with_ones(vb)
        kd_o[...] = kd.astype(BF)
        vd_o[...] = with_ones(vd)
        vc_o[...] = with_ones(vc)
    else:
        (h_o, qa_o, qb_o, qc_o, kc_o, vc_o, qd_o,
         ka_o, va_o, kb_o, vb_o, ckv_o, kr_o, kd_o, vd_o) = refs[len(CACHE_WIDTHS):]
        ka_o[...] = ka
        va_o[...] = va
        kb_o[...] = kb
        vb_o[...] = vb
        ckv_o[...] = ckv
        kr_o[...] = kr_raw[:, 0:ROPE_DIM]
        kd_o[...] = kd_n
        vd_o[...] = vd
        vc_o[...] = vc.astype(BF)
    h_o[...] = h
    qa_o[...] = qa.astype(BF)
    qb_o[...] = qb.astype(BF)
    qc_o[...] = qc.astype(BF)
    kc_o[...] = kc.astype(BF)
    qd_o[...] = qd.astype(BF)


CACHE_WIDTHS = (512, 512, 512, 512, KV_RANK, ROPE_DIM, LANES, LANES)
KV_LEN = DEC_SEQ + PAST_LEN
LAT_KV_OUTS = (2, 3, 8, 9, 11, 12)


def _proj(l, x, mod4, W, extra, *, rope):
    m = x.shape[0]
    tm = TM_PROJ
    per_b = DEC_SEQ // tm
    cond = (lambda i: 1 + i // per_b) if rope else (lambda i: 0)
    row = lambda w: pl.BlockSpec((tm, w), lambda i, lr: (i, 0))
    wfull = lambda a: pl.BlockSpec((None,) + a.shape[1:], lambda i, lr: (lr[0],) + (0,) * (a.ndim - 1),
                                   pipeline_mode=pl.Buffered(1))
    weights = [W["w_qkv"], W["wuq"], W["wukv_n"], W["wukv_v"], W["cqn"], W["ckvn"], W["dqn"], W["dkn"]]
    in_specs = [row(D_MODEL),
                pl.BlockSpec((None, None, 1, 3 * D_MODEL), lambda i, lr: (lr[0], cond(i), 0, 0))]
    in_specs += [wfull(a) for a in weights]
    args = [x, mod4] + weights
    if rope:
        in_specs += [pl.BlockSpec((tm, LANES), lambda i, lr: (i % per_b, 0))] * 3
        widths = [(D_MODEL, BF), (512, BF), (512, BF), (1024, BF), (512, BF), (512, BF), (1024, BF),
                  (1024, BF), (1024, BF), (1024, BF), (512, BF), (LANES, BF), (2 * LANES, BF)]
        aliases = {}
    else:
        assert tm == SEQ
        in_specs += [pl.BlockSpec(memory_space=pl.ANY)] * len(CACHE_WIDTHS)
        widths = [(D_MODEL, BF), (512, BF), (512, BF), (1024, BF), (1024, BF), (512, BF), (512, BF)]
        aliases = {1 + len(args) + k: len(widths) + k for k in range(len(CACHE_WIDTHS))}
    args += list(extra)
    out_specs = [row(w) for w, _ in widths]
    out_shape = [jax.ShapeDtypeStruct((m, w), d) for w, d in widths]
    if rope:
        for k in LAT_KV_OUTS:
            w, d = widths[k]
            out_specs[k] = pl.BlockSpec((None, tm, w), lambda i, lr: (i // per_b, i % per_b, 0))
            out_shape[k] = jax.ShapeDtypeStruct((DEC_BATCH, KV_LEN, w), d)
    if not rope:
        out_specs += [pl.BlockSpec((None, None, SEQ, w), lambda i, lr: (i, lr[0], 0, 0)) for w in CACHE_WIDTHS]
        out_shape += [jax.ShapeDtypeStruct((BATCH, DEPTH, SEQ, w), F32) for w in CACHE_WIDTHS]
    return pl.pallas_call(
        functools.partial(_proj_body, rope=rope),
        grid_spec=pltpu.PrefetchScalarGridSpec(
            num_scalar_prefetch=1, grid=(m // tm,), in_specs=in_specs, out_specs=out_specs),
        out_shape=out_shape,
        input_output_aliases=aliases,
        compiler_params=_cparams(("arbitrary",)),
        name="proj_lat" if rope else "proj_ctx",
    )(l, *args)


def _cache_fill_body(l_ref, cak_ref, cav_ref, kcc_ref, vcc_ref, cdk_ref, cdv_ref, *refs):
    ka_o, va_o, kc_o, vc_o, kd_o, vd_o = refs[6:]
    ka_o[...] = cak_ref[...].astype(BF)
    va_o[...] = _interleave_ones(cav_ref[...])
    kc_o[...] = kcc_ref[...]
    vc_o[...] = _interleave_ones(vcc_ref[...])
    kd_o[...] = cdk_ref[...].astype(BF)
    vd_o[...] = _interleave_ones(cdv_ref[...])


def _cache_fill(l, caches, bufs):
    ca_k, ca_v, kc_cache, vc_cache, cd_k, cd_v = caches
    by_batch = lambda a: pl.BlockSpec((None, None, PAST_LEN, a.shape[3]), lambda b, lr: (b, lr[0], 0, 0))
    by_layer = lambda a: pl.BlockSpec((None, None, PAST_LEN, a.shape[3]), lambda b, lr: (lr[0], b, 0, 0))
    in_specs = [by_batch(ca_k), by_batch(ca_v), by_layer(kc_cache), by_layer(vc_cache), by_batch(cd_k),
                by_batch(cd_v)] + [pl.BlockSpec(memory_space=pl.ANY)] * len(bufs)
    tail = DEC_SEQ // PAST_LEN
    return pl.pallas_call(
        _cache_fill_body,
        grid_spec=pltpu.PrefetchScalarGridSpec(
            num_scalar_prefetch=1, grid=(DEC_BATCH,), in_specs=in_specs,
            out_specs=[pl.BlockSpec((None, PAST_LEN, a.shape[2]), lambda b, lr: (b, tail, 0)) for a in bufs]),
        out_shape=[jax.ShapeDtypeStruct(a.shape, a.dtype) for a in bufs],
        input_output_aliases={1 + len(caches) + k: k for k in range(len(bufs))},
        compiler_params=_cparams(("arbitrary",)),
        name="cache_fill",
    )(l, *caches, *bufs)


def _zg_body(l_ref, h_ref, w_ref, o_ref):
    is_z = pl.program_id(0) < (N_BRANCH * BRANCH_W) // TN_ZG

    def run(silu):
        h = h_ref[...]
        for n in range(TN_ZG // MXU_N):
            sl = slice(n * MXU_N, (n + 1) * MXU_N)
            a = _dot_nt(h, w_ref[sl, :])
            s = _sigmoid(a)
            o_ref[:, sl] = ((a * s) if silu else s).astype(BF)

    @pl.when(is_z)
    def _():
        run(True)

    @pl.when(jnp.logical_not(is_z))
    def _():
        run(False)


def _zg(l, h, w_zg):
    m = h.shape[0]
    return pl.pallas_call(
        _zg_body,
        grid_spec=pltpu.PrefetchScalarGridSpec(
            num_scalar_prefetch=1, grid=(NZG // TN_ZG, m // TM_ZG),
            in_specs=[pl.BlockSpec((TM_ZG, D_MODEL), lambda j, i, lr: (i, 0)),
                      pl.BlockSpec((None, TN_ZG, D_MODEL), lambda j, i, lr: (lr[0], j, 0))],
            out_specs=pl.BlockSpec((TM_ZG, TN_ZG), lambda j, i, lr: (i, j))),
        out_shape=jax.ShapeDtypeStruct((m, NZG), BF),
        compiler_params=_cparams(("arbitrary", "arbitrary")),
        name="zg_proj",
    )(l, h, w_zg)


def _softmax_pv(scores, values):
    m = None
    for s in scores:
        sm = jnp.max(s, axis=-1, keepdims=True)
        m = sm if m is None else jnp.maximum(m, sm)
    acc = None
    for s, v in zip(scores, values):
        o = _dot(jnp.exp2((s - m).astype(BF)), v)
        acc = o if acc is None else acc + o
    return acc[:, :LANES] / acc[:, LANES:]


def _with_ones(v):
    return jnp.concatenate([v, jnp.ones_like(v)], axis=1)


def _diff_lambda(lam_ref, lam_init):
    la = lam_ref[...]
    s01 = jnp.sum(la[0:1] * la[1:2], axis=-1, keepdims=True)
    s23 = jnp.sum(la[2:3] * la[3:4], axis=-1, keepdims=True)
    return jnp.exp(s01) - jnp.exp(s23) + lam_init


def _diff_head(q, ks, vs, lam, subln, lam_init):
    lt = _lane_lt64(q.shape)
    zero = jnp.zeros_like(q)
    o = []
    for qm in (jnp.where(lt, q, zero), jnp.where(lt, zero, q)):
        o.append(_softmax_pv([_dot_nt(qm, k) for k in ks], vs))
    d = o[0] - lam * o[1]
    ms = jnp.mean(d * d, axis=-1, keepdims=True)
    return d * lax.rsqrt(ms + EPS) * subln * (1.0 - lam_init)


def _pair_heads(q, ks, vs, bias=None):
    lt = _lane_lt64(q.shape)
    zero = jnp.zeros_like(q)
    o = []
    for qm in (jnp.where(lt, q, zero), jnp.where(lt, zero, q)):
        sc = [_dot_nt(qm, k) for k in ks]
        if bias is not None:
            sc[0] = sc[0] + bias
        o.append(_softmax_pv(sc, vs))
    return jnp.where(_lane_lt64(o[0].shape), o[0], o[1])


def _gqa_natural(tiles):
    t0, t1, t2, t3 = tiles
    lt = _lane_lt64(t0.shape)
    swap = lambda t: pltpu.roll(t, 64, 1)
    return jnp.concatenate([jnp.where(lt, t0, swap(t1)), jnp.where(lt, t2, swap(t3)),
                            jnp.where(lt, swap(t0), t1), jnp.where(lt, swap(t2), t3)], axis=1)


def _ctx_attn_body(l_ref, li_ref, qa_ref, ka_ref, va_ref, qb_ref, kb_ref, vb_ref, qc_ref, kc_ref, vc_ref,
                   qd_ref, kd_ref, vd_ref, lam_ref, subln_ref, oa_ref, ob_ref, oc_ref, od_ref):
    lam_init = li_ref[l_ref[0]]
    lam = _diff_lambda(lam_ref, lam_init)
    subln = subln_ref[...]
    for h in range(H_A):
        sl = slice(h * LANES, (h + 1) * LANES)
        oa_ref[:, sl] = _diff_head(qa_ref[:, sl], [ka_ref[:, sl].astype(BF)],
                                   [_with_ones(va_ref[:, sl].astype(BF))], lam, subln, lam_init).astype(BF)
    for j in range(H_B // 2):
        sl = slice(j * LANES, (j + 1) * LANES)
        ob_ref[:, sl] = _pair_heads(qb_ref[:, sl], [kb_ref[:, sl].astype(BF)],
                                    [_with_ones(vb_ref[:, sl].astype(BF))]).astype(BF)
    for h in range(H_C):
        oc_ref[:, h * LANES:(h + 1) * LANES] = _softmax_pv(
            [_dot_nt(qc_ref[:, 2 * h * LANES:(2 * h + 2) * LANES], kc_ref[:, 2 * h * LANES:(2 * h + 2) * LANES])],
            [_with_ones(vc_ref[:, h * LANES:(h + 1) * LANES])]).astype(BF)
    kd = kd_ref[...].astype(BF)
    vd = _with_ones(vd_ref[...].astype(BF))
    od_ref[...] = _gqa_natural(
        [_pair_heads(qd_ref[:, j * LANES:(j + 1) * LANES], [kd], [vd]) for j in range(H_D // 2)]).astype(BF)


def _ctx_attn(l, lam_tab, pc, lam_a, a_subln):
    row = lambda w: pl.BlockSpec((SEQ, w), lambda b, lr, li: (b, 0))
    ins = [pc["qa"], pc["ka"], pc["va"], pc["qb"], pc["kb"], pc["vb"], pc["qc"], pc["kc"], pc["vc"],
           pc["qd"], pc["kd"], pc["vd"]]
    layer_row = lambda w: pl.BlockSpec((None, None, SEQ, w), lambda b, lr, li: (b, lr[0], 0, 0))
    in_specs = [row(a.shape[1]) if a.ndim == 2 else layer_row(a.shape[3]) for a in ins]
    in_specs += [pl.BlockSpec((None, 4, D_A), lambda b, lr, li: (lr[0], 0, 0)),
                 pl.BlockSpec((None, 1, 2 * D_A), lambda b, lr, li: (lr[0], 0, 0))]
    return pl.pallas_call(
        _ctx_attn_body,
        grid_spec=pltpu.PrefetchScalarGridSpec(
            num_scalar_prefetch=2, grid=(BATCH,), in_specs=in_specs,
            out_specs=[row(512)] * 4),
        out_shape=[jax.ShapeDtypeStruct((N_CTX, 512), BF)] * 4,
        compiler_params=_cparams(("arbitrary",)),
        name="ctx_attn",
    )(l, lam_tab, *ins, lam_a, a_subln)


def _lat_a_body(l_ref, li_ref, q_ref, k_ref, v_ref, lam_ref, subln_ref, o_ref):
    lam_init = li_ref[l_ref[0]]
    lam = _diff_lambda(lam_ref, lam_init)
    subln = subln_ref[...]
    for h in range(H_A):
        sl = slice(h * LANES, (h + 1) * LANES)
        ks = [k_ref[:, sl]]
        vs = [v_ref[:, 2 * h * LANES:(2 * h + 2) * LANES]]
        for r in range(TQ_A // ROW_TILE):
            rows = slice(r * ROW_TILE, (r + 1) * ROW_TILE)
            o_ref[rows, sl] = _diff_head(q_ref[rows, sl], ks, vs, lam, subln, lam_init).astype(BF)


def _lat_specs(tq, q_w, k_w, v_w):
    qo = lambda w: pl.BlockSpec((None, tq, w), lambda b, i, *pre: (b, i, 0))
    kv = lambda w: pl.BlockSpec((None, KV_LEN, w), lambda b, i, *pre: (b, 0, 0), pipeline_mode=pl.Buffered(1))
    return qo, [qo(q_w), kv(k_w), kv(v_w)]


def _lat_a(l, lam_tab, q, k, v, lam_a, a_subln):
    qo, in_specs = _lat_specs(TQ_A, 512, 512, 1024)
    in_specs += [pl.BlockSpec((None, 4, D_A), lambda b, i, lr, li: (lr[0], 0, 0)),
                 pl.BlockSpec((None, 1, 2 * D_A), lambda b, i, lr, li: (lr[0], 0, 0))]
    return pl.pallas_call(
        _lat_a_body,
        grid_spec=pltpu.PrefetchScalarGridSpec(
            num_scalar_prefetch=2, grid=(DEC_BATCH, DEC_SEQ // TQ_A), in_specs=in_specs, out_specs=qo(512)),
        out_shape=jax.ShapeDtypeStruct((DEC_BATCH, DEC_SEQ, 512), BF),
        compiler_params=_cparams(("arbitrary",) * 2),
        name="lat_attn_a",
    )(l, lam_tab, q, k, v, lam_a, a_subln)


def _lat_c_body(l_ref, q_ref, k_ref, v_ref, o_ref):
    for h in range(H_C):
        sl = slice(h * LANES, (h + 1) * LANES)
        sl2 = slice(2 * h * LANES, (2 * h + 2) * LANES)
        for r in range(TQ_C // ROW_TILE):
            rows = slice(r * ROW_TILE, (r + 1) * ROW_TILE)
            o_ref[rows, sl] = _softmax_pv([_dot_nt(q_ref[rows, sl2], k_ref[:, sl2])], [v_ref[:, sl2]]).astype(BF)


def _lat_c(l, q, k, v):
    qo, in_specs = _lat_specs(TQ_C, 1024, 1024, 1024)
    return pl.pallas_call(
        _lat_c_body,
        grid_spec=pltpu.PrefetchScalarGridSpec(
            num_scalar_prefetch=1, grid=(DEC_BATCH, DEC_SEQ // TQ_C), in_specs=in_specs, out_specs=qo(512)),
        out_shape=jax.ShapeDtypeStruct((DEC_BATCH, DEC_SEQ, 512), BF),
        compiler_params=_cparams(("arbitrary",) * 2),
        name="lat_attn_c",
    )(l, q, k, v)


def _lat_d_body(l_ref, q_ref, k_ref, v_ref, o_ref):
    ks = [k_ref[...]]
    vs = [v_ref[...]]
    for r in range(TQ_D // ROW_TILE):
        rows = slice(r * ROW_TILE, (r + 1) * ROW_TILE)
        o_ref[rows, :] = _gqa_natural(
            [_pair_heads(q_ref[rows, j * LANES:(j + 1) * LANES], ks, vs) for j in range(H_D // 2)]).astype(BF)


def _lat_d(l, q, k, v):
    qo, in_specs = _lat_specs(TQ_D, 512, LANES, 2 * LANES)
    return pl.pallas_call(
        _lat_d_body,
        grid_spec=pltpu.PrefetchScalarGridSpec(
            num_scalar_prefetch=1, grid=(DEC_BATCH, DEC_SEQ // TQ_D), in_specs=in_specs, out_specs=qo(512)),
        out_shape=jax.ShapeDtypeStruct((DEC_BATCH, DEC_SEQ, 512), BF),
        compiler_params=_cparams(("arbitrary",) * 2),
        name="lat_attn_d",
    )(l, q, k, v)


def _lat_b_body(l_ref, q_ref, k_ref, v_ref, kc_ref, vc_ref, tab_ref, o_ref):
    i = pl.program_id(1)
    qr0 = i * NB_ROWS
    kr0 = jnp.clip(qr0 - NA_ROWS // 2, 0, ROWS - NB_KROWS)
    start = pl.multiple_of(kr0 * GRID_W, GRID_W)
    n_keys = NB_KROWS * GRID_W
    lt = _lane_lt64((GRID_W, LANES))
    for j in range(H_B // 2):
        sl = slice(j * LANES, (j + 1) * LANES)
        kwin = k_ref[pl.ds(start, n_keys), sl]
        vwin = v_ref[pl.ds(start, n_keys), 2 * j * LANES:(2 * j + 2) * LANES]
        ks = [kwin, kc_ref[:, sl].astype(BF)]
        vs = [vwin, _with_ones(vc_ref[:, sl].astype(BF))]
        q = q_ref[:, sl]
        ltq = _lane_lt64(q.shape)
        zero = jnp.zeros_like(q)
        outs = []
        for half, qm in ((0, jnp.where(ltq, q, zero)), (1, jnp.where(ltq, zero, q))):
            head = 2 * j + half
            rows = []
            for a in range(NB_ROWS):
                qr = qr0 + a
                r0 = jnp.clip(qr - NA_ROWS // 2, 0, ROWS - NA_ROWS)
                tiles = []
                for p in range(NB_KROWS // 2):
                    kr_l = kr0 + 2 * p
                    u = jnp.clip(kr_l - qr + NA_ROWS, 0, 2 * NA_ROWS - 1)
                    pen_l = jnp.where((kr_l >= r0) & (kr_l < r0 + NA_ROWS), 0.0, NEG)
                    pen_r = jnp.where((kr_l + 1 >= r0) & (kr_l + 1 < r0 + NA_ROWS), 0.0, NEG)
                    tiles.append(tab_ref[head, u] + jnp.where(lt, pen_l, pen_r))
                rows.append(jnp.concatenate(tiles, axis=1))
            bias = jnp.concatenate(rows, axis=0)
            sc = [_dot_nt(qm, ks[0]) + bias, _dot_nt(qm, ks[1])]
            outs.append(_softmax_pv(sc, vs))
        o_ref[:, sl] = jnp.where(ltq, outs[0], outs[1]).astype(BF)


def _lat_b(l, q, k, v, cache_k, cache_v, tab):
    nq = NB_ROWS * GRID_W
    kv = pl.BlockSpec((None, DEC_SEQ, 512), lambda b, i, lr: (b, 0, 0))
    vv = pl.BlockSpec((None, DEC_SEQ, 1024), lambda b, i, lr: (b, 0, 0))
    cache = pl.BlockSpec((None, None, PAST_LEN, 512), lambda b, i, lr: (b, lr[0], 0, 0))
    qo = pl.BlockSpec((None, nq, 512), lambda b, i, lr: (b, i, 0))
    return pl.pallas_call(
        _lat_b_body,
        grid_spec=pltpu.PrefetchScalarGridSpec(
            num_scalar_prefetch=1, grid=(DEC_BATCH, DEC_SEQ // nq),
            in_specs=[qo, kv, vv, cache, cache,
                      pl.BlockSpec((None, H_B, 2 * NA_ROWS, GRID_W, LANES), lambda b, i, lr: (lr[0], 0, 0, 0, 0))],
            out_specs=qo),
        out_shape=jax.ShapeDtypeStruct((DEC_BATCH, DEC_SEQ, 512), BF),
        compiler_params=_cparams(("arbitrary",) * 2),
        name="lat_attn_b",
    )(l, q, k, v, cache_k, cache_v, tab)


def _merge_body(l_ref, x_ref, mod_ref, oa_ref, ob_ref, oc_ref, od_ref, z_ref, g0_ref, g1_ref, g2_ref, g3_ref,
                wbr_ref, wout_ref, lng_ref, lnb_ref, o_ref):
    merged = None
    for i, (o_r, g_r) in enumerate(zip((oa_ref, ob_ref, oc_ref, od_ref), (g0_ref, g1_ref, g2_ref, g3_ref))):
        u = (o_r[...].astype(F32) * z_ref[:, i * BRANCH_W:(i + 1) * BRANCH_W].astype(F32)).astype(BF)
        term = g_r[...].astype(F32) * _dot(u, wbr_ref[i])
        merged = term if merged is None else merged + term
    y = _dot(merged.astype(BF), wout_ref[...])
    gate = mod_ref[:, 2 * D_MODEL:3 * D_MODEL]
    r = ALPHA * x_ref[...] + gate * y
    mu = jnp.mean(r, axis=-1, keepdims=True)
    d = r - mu
    var = jnp.mean(d * d, axis=-1, keepdims=True)
    o_ref[...] = d * lax.rsqrt(var + EPS) * lng_ref[...] + lnb_ref[...]


def _merge(l, x, mod4, outs, zg, W, *, latent):
    m = x.shape[0]
    tm = TM_MERGE
    per_b = DEC_SEQ // tm
    cond = (lambda i: 1 + i // per_b) if latent else (lambda i: 0)
    row = lambda w: pl.BlockSpec((tm, w), lambda i, lr: (i, 0))
    zgb = lambda j: pl.BlockSpec((tm, D_MODEL), lambda i, lr: (i, j))
    wfull = lambda a: pl.BlockSpec((None,) + a.shape[1:], lambda i, lr: (lr[0],) + (0,) * (a.ndim - 1),
                                   pipeline_mode=pl.Buffered(1))
    in_specs = [row(D_MODEL),
                pl.BlockSpec((None, None, 1, 3 * D_MODEL), lambda i, lr: (lr[0], cond(i), 0, 0)),
                row(512), row(512), row(512), row(512),
                zgb(0), zgb(1), zgb(2), zgb(3), zgb(4),
                wfull(W["w_br"]), wfull(W["w_out"]), wfull(W["ln_g"]), wfull(W["ln_b"])]
    return pl.pallas_call(
        _merge_body,
        grid_spec=pltpu.PrefetchScalarGridSpec(
            num_scalar_prefetch=1, grid=(m // tm,), in_specs=in_specs, out_specs=row(D_MODEL)),
        out_shape=jax.ShapeDtypeStruct((m, D_MODEL), F32),
        input_output_aliases={1: 0},
        compiler_params=_cparams(("arbitrary",)),
        name="merge_lat" if latent else "merge_ctx",
    )(l, x, mod4, *outs, zg, zg, zg, zg, zg, W["w_br"], W["w_out"], W["ln_g"], W["ln_b"])


def _rope_tables():
    t = jnp.arange(DEC_SEQ)
    row = (t // GRID_W).astype(F32)
    col = (t % GRID_W).astype(F32)
    quarter = D_A // 4
    inv_freq = ROPE_BASE ** (-jnp.arange(quarter, dtype=F32) / quarter)
    ar = row[:, None] * inv_freq
    ac = col[:, None] * inv_freq
    ang = jnp.concatenate([ar, ar, ac, ac], axis=-1)
    cos, sin = jnp.cos(ang), jnp.sin(ang)
    even = (jnp.arange(D_A) // quarter) % 2 == 0
    sa = jnp.where(even, -sin, 0.0)
    sb = jnp.where(even, 0.0, sin)
    tile2 = lambda a: jnp.concatenate([a, a], axis=-1)
    return tile2(cos), tile2(sa), tile2(sb)


def _repack_plan():
    offs = [int(v) // 64 for v in np.concatenate([[0], np.cumsum(IN_SIZES)])]
    aq, ak, av, bq, bk, bv, cq, ckv, ckr, dq, dk, dv, z, g = offs[:14]
    gqa = lambda base: [base + h for j in range(H_D // 2) for h in (j, H_D // 2 + j)]
    qkv = list(range(aq, ckr)) + gqa(dq) + [dk, dk + 1, dv, dv + 1, ckr, None]
    zg = list(range(z, offs[14]))

    def runs(chunks):
        out = []
        for d, s in enumerate(chunks):
            if out and s is not None and out[-1][1] is not None and out[-1][1] + out[-1][2] == s:
                out[-1][2] += 1
            else:
                out.append([d, s, 1])
        return out
    assert len(qkv) * 64 == NQ and len(zg) * 64 == NZG
    return runs(qkv), runs(zg)


N_IN = sum(IN_SIZES)


def _repack_body(w_ref, qkv_ref, zg_ref):
    qkv_plan, zg_plan = _repack_plan()
    for plan, o_ref in ((qkv_plan, qkv_ref), (zg_plan, zg_ref)):
        for d, s, n in plan:
            rows = slice(d * 64, (d + n) * 64)
            if s is None:
                o_ref[rows, :] = jnp.zeros((n * 64, TC_REPACK), BF)
            else:
                o_ref[rows, :] = w_ref[s * 64:(s + n) * 64, :].astype(BF)


def _repack_w_in(w_in_t):
    return pl.pallas_call(
        _repack_body,
        grid=(DEPTH, D_MODEL // TC_REPACK),
        in_specs=[pl.BlockSpec((None, N_IN, TC_REPACK), lambda l, i: (l, 0, i))],
        out_specs=[pl.BlockSpec((None, NQ, TC_REPACK), lambda l, i: (l, 0, i)),
                   pl.BlockSpec((None, NZG, TC_REPACK), lambda l, i: (l, 0, i))],
        out_shape=[jax.ShapeDtypeStruct((DEPTH, NQ, D_MODEL), BF),
                   jax.ShapeDtypeStruct((DEPTH, NZG, D_MODEL), BF)],
        compiler_params=_cparams(("arbitrary", "arbitrary")),
        name="repack_w_in",
    )(w_in_t)


def _prep_weights(w_in, c_q_norm, c_kv_norm, w_c_uq, w_c_ukv, d_q_norm, d_k_norm, w_br, w_out, ln_g, ln_b):
    w_qkv, w_zg = _repack_w_in(jnp.swapaxes(w_in, 1, 2))
    uq = w_c_uq.reshape(DEPTH, Q_RANK, H_C, NOPE_DIM + ROPE_DIM)
    wuq = jnp.concatenate([uq, jnp.zeros((DEPTH, Q_RANK, H_C, 2 * LANES - NOPE_DIM - ROPE_DIM), F32)],
                          axis=3).reshape(DEPTH, Q_RANK, H_C * 2 * LANES).astype(BF)
    ukv = w_c_ukv.reshape(DEPTH, KV_RANK, H_C, NOPE_DIM + V_DIM_C)
    wukv_n = ukv[..., :NOPE_DIM].reshape(DEPTH, KV_RANK, H_C * NOPE_DIM).astype(BF)
    wukv_v = ukv[..., NOPE_DIM:].reshape(DEPTH, KV_RANK, H_C * V_DIM_C).astype(BF)
    wbr = w_br.astype(BF)
    return {
        "w_qkv": w_qkv, "w_zg": w_zg, "wuq": wuq, "wukv_n": wukv_n, "wukv_v": wukv_v,
        "cqn": c_q_norm.reshape(DEPTH, 1, Q_RANK), "ckvn": c_kv_norm.reshape(DEPTH, 1, KV_RANK),
        "dqn": jnp.tile(d_q_norm, (1, H_D)).reshape(DEPTH, 1, H_D * D_D),
        "dkn": jnp.tile(d_k_norm, (1, G_D)).reshape(DEPTH, 1, G_D * D_D),
        "w_br": wbr, "w_out": w_out.astype(BF),
        "ln_g": ln_g.reshape(DEPTH, 1, D_MODEL), "ln_b": ln_b.reshape(DEPTH, 1, D_MODEL),
    }


def kernel(x_prompt, x_sample, cache_a_k, cache_a_v, cache_b_k, cache_b_v, cache_c_kv, cache_c_kr, cache_d_k,
           cache_d_v, c, c_ctx, w_ada, b_ada, w_in, lam_a, a_subln, b_rpb, c_q_norm, c_kv_norm, w_c_uq, w_c_ukv,
           d_q_norm, d_k_norm, w_br, w_out, ln_g, ln_b):
    W = _prep_weights(w_in, c_q_norm, c_kv_norm, w_c_uq, w_c_ukv, d_q_norm, d_k_norm, w_br, w_out, ln_g, ln_b)
    cond8 = jnp.concatenate([c_ctx[None], c, jnp.zeros((5, D_MODEL), F32)], axis=0)
    mod4 = _adaln(cond8, w_ada, b_ada).reshape(DEPTH, 8, 1, 3 * D_MODEL)
    nb_tab = _bias_table(b_rpb).reshape(DEPTH, H_B, 2 * NA_ROWS, GRID_W, LANES)
    kc_cache, vc_cache = _cache_mla(cache_c_kv, cache_c_kr, W["wukv_n"], W["wukv_v"])
    rope_tabs = _rope_tables()
    lam_tab = jnp.array([0.8 - 0.6 * math.exp(-0.3 * l) for l in range(DEPTH)], F32)
    subln = a_subln.reshape(DEPTH, 1, 2 * D_A)
    flat = lambda a: a.reshape(DEC_BATCH, DEPTH, PAST_LEN, -1)
    ca_k, ca_v, cb_k, cb_v, cd_k, cd_v = (flat(a) for a in (cache_a_k, cache_a_v, cache_b_k, cache_b_v,
                                                               cache_d_k, cache_d_v))
    lat3 = lambda a: a.reshape(DEC_BATCH, DEC_SEQ, a.shape[-1])

    def layer(carry, li):
        xp, xs, bufs = carry
        l = li.reshape(1)
        (h_c, qa, qb, qc, kc, vc, qd, ka, va, kb, vb, ckv, kr, kd, vd) = _proj(l, xp, mod4, W, bufs, rope=False)
        pc = dict(qa=qa, ka=ka, va=va, qb=qb, kb=kb, vb=vb, qc=qc, kc=kc, vc=vc, qd=qd, kd=kd, vd=vd)
        outs_c = _ctx_attn(l, lam_tab, pc, lam_a, subln)
        zg_c = _zg(l, h_c, W["w_zg"])
        xp_new = _merge(l, xp, mod4, outs_c, zg_c, W, latent=False)
        (h_l, lqa, lka, lva, lqb, lkb, lvb, lqc, lkc, lvc, lqd, lkd, lvd) = _proj(l, xs, mod4, W, rope_tabs,
                                                                                   rope=True)
        lka, lva, lkc, lvc, lkd, lvd = _cache_fill(l, (ca_k, ca_v, kc_cache, vc_cache, cd_k, cd_v),
                                                   (lka, lva, lkc, lvc, lkd, lvd))
        o_a = _lat_a(l, lam_tab, lat3(lqa), lka, lva, lam_a, subln)
        o_b = _lat_b(l, lat3(lqb), lat3(lkb), lat3(lvb), cb_k, cb_v, nb_tab)
        o_c = _lat_c(l, lat3(lqc), lkc, lvc)
        o_d = _lat_d(l, lat3(lqd), lkd, lvd)
        outs_l = [o.reshape(N_LAT, 512) for o in (o_a, o_b, o_c, o_d)]
        zg_l = _zg(l, h_l, W["w_zg"])
        xs_new = _merge(l, xs, mod4, outs_l, zg_l, W, latent=True)
        return (xp_new, xs_new, (ka, va, kb, vb, ckv, kr, kd, vd)), None

    bufs0 = tuple(jnp.zeros((BATCH, DEPTH, SEQ, w), F32) for w in CACHE_WIDTHS)
    (xp, xs, caches), _ = lax.scan(
        layer, (x_prompt.reshape(N_CTX, D_MODEL), x_sample.reshape(N_LAT, D_MODEL), bufs0),
        jnp.arange(DEPTH, dtype=jnp.int32))
    ka, va, kb, vb, ckv, kr, kd, vd = caches

    def out(a, tail):
        return a.reshape((BATCH, DEPTH, SEQ) + tail)

    return (xp.reshape(BATCH, SEQ, D_MODEL), xs.reshape(DEC_BATCH, DEC_SEQ, D_MODEL),
            out(ka, (H_A, 2 * D_A)), out(va, (H_A, 2 * D_A)), out(kb, (H_B, D_B)), out(vb, (H_B, D_B)),
            out(ckv, (KV_RANK,)), out(kr, (ROPE_DIM,)), out(kd, (G_D, D_D)), out(vd, (G_D, D_D)))
```

```python
import functools
import math

import jax
import jax.numpy as jnp
import numpy as np
from jax import lax
from jax.experimental import pallas as pl
from jax.experimental.pallas import tpu as pltpu

D_MODEL = 2048
BATCH = 16
SEQ = 256
DEPTH = 4
DEC_BATCH = 2
DEC_SEQ = 4096
PAST_LEN = 256
GRID_W = 64
ROWS = DEC_SEQ // GRID_W
N_BRANCH = 4
BRANCH_W = 512
H_A, D_A = 4, 64
H_B, D_B = 8, 64
NA_ROWS, NA_COLS = 8, 16
H_C, Q_RANK, KV_RANK, NOPE_DIM, ROPE_DIM, V_DIM_C = 4, 512, 256, 128, 64, 128
H_D, G_D, D_D = 8, 2, 64
ROPE_BASE = 10000.0
EPS = 1e-6
ALPHA = (2 * DEPTH) ** 0.25
IN_SIZES = (512, 512, 512, 512, 512, 512, Q_RANK, KV_RANK, ROPE_DIM, 512, 128, 128,
            N_BRANCH * BRANCH_W, N_BRANCH * D_MODEL)

BF = jnp.bfloat16
F32 = jnp.float32
LANES = 128
MXU_N = 256
LOG2E = 1.4426950408889634
VMEM_LIMIT = 56 * 1024 * 1024
NEG = -1e30

N_CTX = BATCH * SEQ
N_LAT = DEC_BATCH * DEC_SEQ
NQ = 4736
NZG = 10240

O_AQ, O_AK, O_AV, O_BQ, O_BK, O_BV = 0, 512, 1024, 1536, 2048, 2560
O_CQ, O_CKV, O_DQ, O_DK, O_DV, O_KR = 3072, 3584, 3840, 4352, 4480, 4608

TM_PROJ = 256
TM_ZG = 1024
TN_ZG = 2048
TM_MERGE = 256
ROW_TILE = 256
TQ_A = 512
TQ_C = 1024
TQ_D = 512
TC_REPACK = 256
NB_ROWS = 4
NB_KROWS = 12


def _cparams(sem):
    return pltpu.CompilerParams(dimension_semantics=sem, vmem_limit_bytes=VMEM_LIMIT)


def _dot(a, b):
    return jnp.dot(a, b, preferred_element_type=F32)


def _dot_nt(a, b):
    return lax.dot_general(a, b, (((1,), (1,)), ((), ())), preferred_element_type=F32)


def _sigmoid(x):
    return 1.0 / (1.0 + jnp.exp(-x))


def _lane_lt64(shape):
    return lax.broadcasted_iota(jnp.int32, shape, len(shape) - 1) < 64


def _adaln_body(c_ref, w_ref, b_ref, o_ref):
    c = c_ref[...]
    s = (c * _sigmoid(c)).astype(BF)
    o_ref[...] = _dot(s, w_ref[...].astype(BF)) + b_ref[...]


def _adaln(cond8, w_ada, b_ada):
    tn = 1536
    return pl.pallas_call(
        _adaln_body,
        grid=(DEPTH, 3 * D_MODEL // tn),
        in_specs=[pl.BlockSpec((8, D_MODEL), lambda l, j: (0, 0)),
                  pl.BlockSpec((None, D_MODEL, tn), lambda l, j: (l, 0, j)),
                  pl.BlockSpec((None, 1, tn), lambda l, j: (l, 0, j))],
        out_specs=pl.BlockSpec((None, 8, tn), lambda l, j: (l, 0, j)),
        out_shape=jax.ShapeDtypeStruct((DEPTH, 8, 3 * D_MODEL), F32),
        compiler_params=_cparams(("arbitrary", "arbitrary")),
        name="adaln",
    )(cond8, w_ada, b_ada.reshape(DEPTH, 1, 3 * D_MODEL))


def _bias_table_body(rpb_ref, o_ref):
    lh = pl.program_id(0)
    qc = lax.broadcasted_iota(jnp.int32, (GRID_W, LANES), 0)
    lane = lax.broadcasted_iota(jnp.int32, (GRID_W, LANES), 1)
    kc = jnp.bitwise_and(lane, 63)
    c0 = jnp.clip(qc - NA_COLS // 2, 0, GRID_W - NA_COLS)
    col_ok = (kc >= c0) & (kc < c0 + NA_COLS)
    dc = jnp.where(col_ok, kc - qc + (NA_COLS - 1), -1)
    right = lane >= 64
    n_dr, n_dc = 2 * NA_ROWS - 1, 2 * NA_COLS - 1
    neg = jnp.full((GRID_W, LANES), NEG, F32)
    rows = []
    for dr in range(n_dr):
        val = neg
        for d in range(n_dc):
            val = jnp.where(dc == d, rpb_ref[(lh * n_dr + dr) * n_dc + d] * LOG2E, val)
        rows.append(val)
    for u in range(n_dr + 1):
        left = rows[u - 1] if u >= 1 else neg
        o_ref[u] = jnp.where(right, rows[u] if u < n_dr else neg, left)


def _bias_table(b_rpb):
    n = DEPTH * H_B
    return pl.pallas_call(
        _bias_table_body,
        grid_spec=pltpu.PrefetchScalarGridSpec(
            num_scalar_prefetch=1, grid=(n,),
            in_specs=[],
            out_specs=pl.BlockSpec((None, 2 * NA_ROWS, GRID_W, LANES), lambda i, r: (i, 0, 0, 0))),
        out_shape=jax.ShapeDtypeStruct((n, 2 * NA_ROWS, GRID_W, LANES), F32),
        compiler_params=_cparams(("arbitrary",)),
        name="nb_bias_table",
    )(b_rpb.reshape(-1))


def _cache_mla_body(ckv_ref, kr_ref, wn_ref, wv_ref, k_ref, v_ref):
    ckv = ckv_ref[...].astype(BF)
    kn = _dot(ckv, wn_ref[...])
    kr = kr_ref[...]
    k_ref[...] = jnp.concatenate(
        [t for h in range(H_C) for t in (kn[:, h * LANES:(h + 1) * LANES], kr)], axis=1).astype(BF)
    v_ref[...] = _dot(ckv, wv_ref[...]).astype(BF)


def _cache_mla(cache_c_kv, cache_c_kr, wukv_n, wukv_v):
    return pl.pallas_call(
        _cache_mla_body,
        grid=(DEPTH, DEC_BATCH),
        in_specs=[pl.BlockSpec((None, None, PAST_LEN, KV_RANK), lambda l, b: (b, l, 0, 0)),
                  pl.BlockSpec((None, None, PAST_LEN, LANES), lambda l, b: (b, l, 0, 0)),
                  pl.BlockSpec((None, KV_RANK, 512), lambda l, b: (l, 0, 0)),
                  pl.BlockSpec((None, KV_RANK, 512), lambda l, b: (l, 0, 0))],
        out_specs=[pl.BlockSpec((None, None, PAST_LEN, 1024), lambda l, b: (l, b, 0, 0)),
                   pl.BlockSpec((None, None, PAST_LEN, 512), lambda l, b: (l, b, 0, 0))],
        out_shape=[jax.ShapeDtypeStruct((DEPTH, DEC_BATCH, PAST_LEN, 1024), BF),
                   jax.ShapeDtypeStruct((DEPTH, DEC_BATCH, PAST_LEN, 512), BF)],
        compiler_params=_cparams(("arbitrary", "arbitrary")),
        name="cache_mla",
    )(cache_c_kv, jnp.pad(cache_c_kr, ((0, 0), (0, 0), (0, 0), (0, LANES - ROPE_DIM))), wukv_n, wukv_v)


def _rope_tiles(x, cos, sa, sb):
    outs = []
    for j in range(x.shape[1] // LANES):
        t = x[:, j * LANES:(j + 1) * LANES]
        outs.append(t * cos + pltpu.roll(t, LANES - 16, 1) * sa + pltpu.roll(t, 16, 1) * sb)
    return outs[0] if len(outs) == 1 else jnp.concatenate(outs, axis=1)


def _group64_rms(x, g):
    w = x.shape[1]
    r = lax.shift_right_logical(lax.broadcasted_iota(jnp.int32, (w, w), 0), 6)
    c = lax.shift_right_logical(lax.broadcasted_iota(jnp.int32, (w, w), 1), 6)
    bd = jnp.where(r == c, 1.0, 0.0).astype(BF)
    x2 = x * x
    hi = x2.astype(BF)
    lo = (x2 - hi.astype(F32)).astype(BF)
    ms = (_dot(hi, bd) + _dot(lo, bd)) * (1.0 / 64)
    return x * lax.rsqrt(ms + EPS) * g


def _interleave_ones(v):
    ones = jnp.ones((v.shape[0], LANES), BF)
    return jnp.concatenate([t for j in range(v.shape[1] // LANES)
                            for t in (v[:, j * LANES:(j + 1) * LANES].astype(BF), ones)], axis=1)


def _full_rms(x, g):
    ms = jnp.mean(x * x, axis=-1, keepdims=True)
    return x * lax.rsqrt(ms + EPS) * g


def _proj_body(l_ref, x_ref, mod_ref, w_ref, wuq_ref, wun_ref, wuv_ref, cqn_ref, ckvn_ref, dqn_ref, dkn_ref,
               *refs, rope):
    if rope:
        cos_ref, sa_ref, sb_ref = refs[:3]
        refs = refs[3:]
        cos, sa, sb = cos_ref[...], sa_ref[...], sb_ref[...]
        rp = lambda t: _rope_tiles(t, cos, sa, sb)
    else:
        rp = lambda t: t
    x = x_ref[...]
    shift = mod_ref[:, 0:D_MODEL]
    scale = mod_ref[:, D_MODEL:2 * D_MODEL]
    h = (x * (1.0 + scale) + shift).astype(BF)

    acc = _dot_nt(h, w_ref[...])

    def col(o, n):
        return acc[:, o:o + n]

    with_ones = _interleave_ones

    qa = rp(col(O_AQ, 512)) * (D_A ** -0.5 * LOG2E)
    ka = rp(col(O_AK, 512))
    va = col(O_AV, 512)
    qb = col(O_BQ, 512) * (D_B ** -0.5 * LOG2E)
    kb = col(O_BK, 512)
    vb = col(O_BV, 512)
    cq = _full_rms(col(O_CQ, Q_RANK), cqn_ref[...]).astype(BF)
    qc_raw = _dot(cq, wuq_ref[...])
    qc_scale = (NOPE_DIM + ROPE_DIM) ** -0.5 * LOG2E
    qc = jnp.concatenate(
        [t for hh in range(H_C) for t in (qc_raw[:, 2 * hh * LANES:(2 * hh + 1) * LANES],
                                          rp(qc_raw[:, (2 * hh + 1) * LANES:(2 * hh + 2) * LANES]))],
        axis=1) * qc_scale
    ckv = _full_rms(col(O_CKV, KV_RANK), ckvn_ref[...])
    ckv_b = ckv.astype(BF)
    kn = _dot(ckv_b, wun_ref[...])
    vc = _dot(ckv_b, wuv_ref[...])
    kr_raw = col(O_KR, LANES)
    kr = rp(kr_raw)
    kc = jnp.concatenate([t for hh in range(H_C) for t in (kn[:, hh * LANES:(hh + 1) * LANES], kr)], axis=1)
    qd = rp(_group64_rms(col(O_DQ, 512), dqn_ref[...])) * (D_D ** -0.5 * LOG2E)
    kd_n = _group64_rms(col(O_DK, LANES), dkn_ref[...])
    kd = rp(kd_n)
    vd = col(O_DV, LANES)

    if rope:
        (h_o, qa_o, ka_o, va_o, qb_o, kb_o, vb_o, qc_o, kc_o, vc_o, qd_o, kd_o, vd_o) = refs
        ka_o[...] = ka.astype(BF)
        va_o[...] = with_ones(va)
        kb_o[...] = kb.astype(BF)
        vb_o[...] = with_ones(vb)
        kd_o[...] = kd.astype(BF)
        vd_o[...] = with_ones(vd)
        vc_o[...] = with_ones(vc)
    else:
        (h_o, qa_o, qb_o, qc_o, kc_o, vc_o, qd_o,
         ka_o, va_o, kb_o, vb_o, ckv_o, kr_o, kd_o, vd_o) = refs[len(CACHE_WIDTHS):]
        for hh in range(H_A):
            ka_o[:, hh, :] = ka[:, hh * LANES:(hh + 1) * LANES]
            va_o[:, hh, :] = va[:, hh * LANES:(hh + 1) * LANES]
        kb_o[...] = kb
        vb_o[...] = vb
        ckv_o[...] = ckv
        kr_o[...] = kr_raw[:, 0:ROPE_DIM]
        kd_o[...] = kd_n
        vd_o[...] = vd
        vc_o[...] = vc.astype(BF)
    h_o[...] = h
    qa_o[...] = qa.astype(BF)
    qb_o[...] = qb.astype(BF)
    qc_o[...] = qc.astype(BF)
    kc_o[...] = kc.astype(BF)
    qd_o[...] = qd.astype(BF)


CACHE_WIDTHS = (512, 512, 512, 512, KV_RANK, ROPE_DIM, LANES, LANES)
KV_LEN = DEC_SEQ + PAST_LEN
LAT_KV_OUTS = (2, 3, 8, 9, 11, 12)


def _proj(l, x, mod4, W, extra, *, rope):
    m = x.shape[0]
    tm = TM_PROJ
    per_b = DEC_SEQ // tm
    cond = (lambda i: 1 + i // per_b) if rope else (lambda i: 0)
    row = lambda w: pl.BlockSpec((tm, w), lambda i, lr: (i, 0))
    wfull = lambda a: pl.BlockSpec((None,) + a.shape[1:], lambda i, lr: (lr[0],) + (0,) * (a.ndim - 1),
                                   pipeline_mode=pl.Buffered(1))
    weights = [W["w_qkv"], W["wuq"], W["wukv_n"], W["wukv_v"], W["cqn"], W["ckvn"], W["dqn"], W["dkn"]]
    in_specs = [row(D_MODEL),
                pl.BlockSpec((None, None, 1, 3 * D_MODEL), lambda i, lr: (lr[0], cond(i), 0, 0))]
    in_specs += [wfull(a) for a in weights]
    args = [x, mod4] + weights
    if rope:
        in_specs += [pl.BlockSpec((tm, LANES), lambda i, lr: (i % per_b, 0))] * 3
        widths = [(D_MODEL, BF), (512, BF), (512, BF), (1024, BF), (512, BF), (512, BF), (1024, BF),
                  (1024, BF), (1024, BF), (1024, BF), (512, BF), (LANES, BF), (2 * LANES, BF)]
        aliases = {}
    else:
        assert tm == SEQ
        in_specs += [pl.BlockSpec(memory_space=pl.ANY)] * len(CACHE_WIDTHS)
        widths = [(D_MODEL, BF), (512, BF), (512, BF), (1024, BF), (1024, BF), (512, BF), (512, BF)]
        aliases = {1 + len(args) + k: len(widths) + k for k in range(len(CACHE_WIDTHS))}
    args += list(extra)
    out_specs = [row(w) for w, _ in widths]
    out_shape = [jax.ShapeDtypeStruct((m, w), d) for w, d in widths]
    if rope:
        for k in LAT_KV_OUTS:
            w, d = widths[k]
            out_specs[k] = pl.BlockSpec((None, tm, w), lambda i, lr: (i // per_b, i % per_b, 0))
            out_shape[k] = jax.ShapeDtypeStruct((DEC_BATCH, KV_LEN, w), d)
    if not rope:
        for buf in extra:
            tail = buf.shape[3:]
            out_specs.append(pl.BlockSpec((None, None, SEQ) + tail,
                                          lambda i, lr, n=len(tail): (i, lr[0], 0) + (0,) * n))
            out_shape.append(jax.ShapeDtypeStruct(buf.shape, F32))
    return pl.pallas_call(
        functools.partial(_proj_body, rope=rope),
        grid_spec=pltpu.PrefetchScalarGridSpec(
            num_scalar_prefetch=1, grid=(m // tm,), in_specs=in_specs, out_specs=out_specs),
        out_shape=out_shape,
        input_output_aliases=aliases,
        compiler_params=_cparams(("arbitrary",)),
        name="proj_lat" if rope else "proj_ctx",
    )(l, *args)


def _cache_fill_body(l_ref, cak_ref, cav_ref, kcc_ref, vcc_ref, cdk_ref, cdv_ref, *refs):
    ka_o, va_o, kc_o, vc_o, kd_o, vd_o = refs[6:]
    ka_o[...] = cak_ref[...].astype(BF)
    va_o[...] = _interleave_ones(cav_ref[...])
    kc_o[...] = kcc_ref[...]
    vc_o[...] = _interleave_ones(vcc_ref[...])
    kd_o[...] = cdk_ref[...].astype(BF)
    vd_o[...] = _interleave_ones(cdv_ref[...])


def _cache_fill(l, caches, bufs):
    ca_k, ca_v, kc_cache, vc_cache, cd_k, cd_v = caches
    by_batch = lambda a: pl.BlockSpec((None, None, PAST_LEN, a.shape[3]), lambda b, lr: (b, lr[0], 0, 0))
    by_layer = lambda a: pl.BlockSpec((None, None, PAST_LEN, a.shape[3]), lambda b, lr: (lr[0], b, 0, 0))
    in_specs = [by_batch(ca_k), by_batch(ca_v), by_layer(kc_cache), by_layer(vc_cache), by_batch(cd_k),
                by_batch(cd_v)] + [pl.BlockSpec(memory_space=pl.ANY)] * len(bufs)
    tail = DEC_SEQ // PAST_LEN
    return pl.pallas_call(
        _cache_fill_body,
        grid_spec=pltpu.PrefetchScalarGridSpec(
            num_scalar_prefetch=1, grid=(DEC_BATCH,), in_specs=in_specs,
            out_specs=[pl.BlockSpec((None, PAST_LEN, a.shape[2]), lambda b, lr: (b, tail, 0)) for a in bufs]),
        out_shape=[jax.ShapeDtypeStruct(a.shape, a.dtype) for a in bufs],
        input_output_aliases={1 + len(caches) + k: k for k in range(len(bufs))},
        compiler_params=_cparams(("arbitrary",)),
        name="cache_fill",
    )(l, *caches, *bufs)


def _zg_body(l_ref, h_ref, w_ref, o_ref):
    is_z = pl.program_id(0) < (N_BRANCH * BRANCH_W) // TN_ZG

    def run(silu):
        h = h_ref[...]
        for n in range(TN_ZG // MXU_N):
            sl = slice(n * MXU_N, (n + 1) * MXU_N)
            a = _dot_nt(h, w_ref[sl, :])
            s = _sigmoid(a)
            o_ref[:, sl] = ((a * s) if silu else s).astype(BF)

    @pl.when(is_z)
    def _():
        run(True)

    @pl.when(jnp.logical_not(is_z))
    def _():
        run(False)


def _zg(l, h, w_zg):
    m = h.shape[0]
    return pl.pallas_call(
        _zg_body,
        grid_spec=pltpu.PrefetchScalarGridSpec(
            num_scalar_prefetch=1, grid=(NZG // TN_ZG, m // TM_ZG),
            in_specs=[pl.BlockSpec((TM_ZG, D_MODEL), lambda j, i, lr: (i, 0)),
                      pl.BlockSpec((None, TN_ZG, D_MODEL), lambda j, i, lr: (lr[0], j, 0))],
            out_specs=pl.BlockSpec((TM_ZG, TN_ZG), lambda j, i, lr: (i, j))),
        out_shape=jax.ShapeDtypeStruct((m, NZG), BF),
        compiler_params=_cparams(("arbitrary", "arbitrary")),
        name="zg_proj",
    )(l, h, w_zg)


def _softmax_pv(scores, values):
    m = None
    for s in scores:
        sm = jnp.max(s, axis=-1, keepdims=True)
        m = sm if m is None else jnp.maximum(m, sm)
    acc = None
    for s, v in zip(scores, values):
        o = _dot(jnp.exp2((s - m).astype(BF)), v)
        acc = o if acc is None else acc + o
    return acc[:, :LANES] / acc[:, LANES:]


def _with_ones(v):
    return jnp.concatenate([v, jnp.ones_like(v)], axis=1)


def _diff_lambda(lam_ref, lam_init):
    la = lam_ref[...]
    s01 = jnp.sum(la[0:1] * la[1:2], axis=-1, keepdims=True)
    s23 = jnp.sum(la[2:3] * la[3:4], axis=-1, keepdims=True)
    return jnp.exp(s01) - jnp.exp(s23) + lam_init


def _diff_head(q, ks, vs, lam, subln, lam_init):
    lt = _lane_lt64(q.shape)
    zero = jnp.zeros_like(q)
    o = []
    for qm in (jnp.where(lt, q, zero), jnp.where(lt, zero, q)):
        o.append(_softmax_pv([_dot_nt(qm, k) for k in ks], vs))
    d = o[0] - lam * o[1]
    ms = jnp.mean(d * d, axis=-1, keepdims=True)
    return d * lax.rsqrt(ms + EPS) * subln * (1.0 - lam_init)


def _pair_heads(q, ks, vs, bias=None):
    lt = _lane_lt64(q.shape)
    zero = jnp.zeros_like(q)
    o = []
    for qm in (jnp.where(lt, q, zero), jnp.where(lt, zero, q)):
        sc = [_dot_nt(qm, k) for k in ks]
        if bias is not None:
            sc[0] = sc[0] + bias
        o.append(_softmax_pv(sc, vs))
    return jnp.where(_lane_lt64(o[0].shape), o[0], o[1])


def _gqa_natural(tiles):
    t0, t1, t2, t3 = tiles
    lt = _lane_lt64(t0.shape)
    swap = lambda t: pltpu.roll(t, 64, 1)
    return jnp.concatenate([jnp.where(lt, t0, swap(t1)), jnp.where(lt, t2, swap(t3)),
                            jnp.where(lt, swap(t0), t1), jnp.where(lt, swap(t2), t3)], axis=1)


def _ctx_attn_body(l_ref, li_ref, qa_ref, ka_ref, va_ref, qb_ref, kb_ref, vb_ref, qc_ref, kc_ref, vc_ref,
                   qd_ref, kd_ref, vd_ref, lam_ref, subln_ref, oa_ref, ob_ref, oc_ref, od_ref):
    lam_init = li_ref[l_ref[0]]
    lam = _diff_lambda(lam_ref, lam_init)
    subln = subln_ref[...]
    for h in range(H_A):
        sl = slice(h * LANES, (h + 1) * LANES)
        oa_ref[:, sl] = _diff_head(qa_ref[:, sl], [ka_ref[:, h, :].astype(BF)],
                                   [_with_ones(va_ref[:, h, :].astype(BF))], lam, subln, lam_init).astype(BF)
    for j in range(H_B // 2):
        sl = slice(j * LANES, (j + 1) * LANES)
        ob_ref[:, sl] = _pair_heads(qb_ref[:, sl], [kb_ref[:, sl].astype(BF)],
                                    [_with_ones(vb_ref[:, sl].astype(BF))]).astype(BF)
    for h in range(H_C):
        oc_ref[:, h * LANES:(h + 1) * LANES] = _softmax_pv(
            [_dot_nt(qc_ref[:, 2 * h * LANES:(2 * h + 2) * LANES], kc_ref[:, 2 * h * LANES:(2 * h + 2) * LANES])],
            [_with_ones(vc_ref[:, h * LANES:(h + 1) * LANES])]).astype(BF)
    kd = kd_ref[...].astype(BF)
    vd = _with_ones(vd_ref[...].astype(BF))
    od_ref[...] = _gqa_natural(
        [_pair_heads(qd_ref[:, j * LANES:(j + 1) * LANES], [kd], [vd]) for j in range(H_D // 2)]).astype(BF)


def _ctx_attn(l, lam_tab, pc, lam_a, a_subln):
    row = lambda w: pl.BlockSpec((SEQ, w), lambda b, lr, li: (b, 0))
    ins = [pc["qa"], pc["ka"], pc["va"], pc["qb"], pc["kb"], pc["vb"], pc["qc"], pc["kc"], pc["vc"],
           pc["qd"], pc["kd"], pc["vd"]]
    layer_row = lambda a: pl.BlockSpec((None, None, SEQ) + a.shape[3:],
                                       lambda b, lr, li: (b, lr[0], 0) + (0,) * (a.ndim - 3))
    in_specs = [row(a.shape[1]) if a.ndim == 2 else layer_row(a) for a in ins]
    in_specs += [pl.BlockSpec((None, 4, D_A), lambda b, lr, li: (lr[0], 0, 0)),
                 pl.BlockSpec((None, 1, 2 * D_A), lambda b, lr, li: (lr[0], 0, 0))]
    return pl.pallas_call(
        _ctx_attn_body,
        grid_spec=pltpu.PrefetchScalarGridSpec(
            num_scalar_prefetch=2, grid=(BATCH,), in_specs=in_specs,
            out_specs=[row(512)] * 4),
        out_shape=[jax.ShapeDtypeStruct((N_CTX, 512), BF)] * 4,
        compiler_params=_cparams(("arbitrary",)),
        name="ctx_attn",
    )(l, lam_tab, *ins, lam_a, a_subln)


def _lat_a_body(l_ref, li_ref, q_ref, k_ref, v_ref, lam_ref, subln_ref, o_ref):
    lam_init = li_ref[l_ref[0]]
    lam = _diff_lambda(lam_ref, lam_init)
    subln = subln_ref[...]
    for h in range(H_A):
        sl = slice(h * LANES, (h + 1) * LANES)
        ks = [k_ref[:, sl]]
        vs = [v_ref[:, 2 * h * LANES:(2 * h + 2) * LANES]]
        for r in range(TQ_A // ROW_TILE):
            rows = slice(r * ROW_TILE, (r + 1) * ROW_TILE)
            o_ref[rows, sl] = _diff_head(q_ref[rows, sl], ks, vs, lam, subln, lam_init).astype(BF)


def _lat_specs(tq, q_w, k_w, v_w):
    qo = lambda w: pl.BlockSpec((None, tq, w), lambda b, i, *pre: (b, i, 0))
    kv = lambda w: pl.BlockSpec((None, KV_LEN, w), lambda b, i, *pre: (b, 0, 0), pipeline_mode=pl.Buffered(1))
    return qo, [qo(q_w), kv(k_w), kv(v_w)]


def _lat_a(l, lam_tab, q, k, v, lam_a, a_subln):
    qo, in_specs = _lat_specs(TQ_A, 512, 512, 1024)
    in_specs += [pl.BlockSpec((None, 4, D_A), lambda b, i, lr, li: (lr[0], 0, 0)),
                 pl.BlockSpec((None, 1, 2 * D_A), lambda b, i, lr, li: (lr[0], 0, 0))]
    return pl.pallas_call(
        _lat_a_body,
        grid_spec=pltpu.PrefetchScalarGridSpec(
            num_scalar_prefetch=2, grid=(DEC_BATCH, DEC_SEQ // TQ_A), in_specs=in_specs, out_specs=qo(512)),
        out_shape=jax.ShapeDtypeStruct((DEC_BATCH, DEC_SEQ, 512), BF),
        compiler_params=_cparams(("arbitrary",) * 2),
        name="lat_attn_a",
    )(l, lam_tab, q, k, v, lam_a, a_subln)


def _lat_c_body(l_ref, q_ref, k_ref, v_ref, o_ref):
    for h in range(H_C):
        sl = slice(h * LANES, (h + 1) * LANES)
        sl2 = slice(2 * h * LANES, (2 * h + 2) * LANES)
        for r in range(TQ_C // ROW_TILE):
            rows = slice(r * ROW_TILE, (r + 1) * ROW_TILE)
            o_ref[rows, sl] = _softmax_pv([_dot_nt(q_ref[rows, sl2], k_ref[:, sl2])], [v_ref[:, sl2]]).astype(BF)


def _lat_c(l, q, k, v):
    qo, in_specs = _lat_specs(TQ_C, 1024, 1024, 1024)
    return pl.pallas_call(
        _lat_c_body,
        grid_spec=pltpu.PrefetchScalarGridSpec(
            num_scalar_prefetch=1, grid=(DEC_BATCH, DEC_SEQ // TQ_C), in_specs=in_specs, out_specs=qo(512)),
        out_shape=jax.ShapeDtypeStruct((DEC_BATCH, DEC_SEQ, 512), BF),
        compiler_params=_cparams(("arbitrary",) * 2),
        name="lat_attn_c",
    )(l, q, k, v)


def _lat_d_body(l_ref, q_ref, k_ref, v_ref, o_ref):
    ks = [k_ref[...]]
    vs = [v_ref[...]]
    for r in range(TQ_D // ROW_TILE):
        rows = slice(r * ROW_TILE, (r + 1) * ROW_TILE)
        o_ref[rows, :] = _gqa_natural(
            [_pair_heads(q_ref[rows, j * LANES:(j + 1) * LANES], ks, vs) for j in range(H_D // 2)]).astype(BF)


def _lat_d(l, q, k, v):
    qo, in_specs = _lat_specs(TQ_D, 512, LANES, 2 * LANES)
    return pl.pallas_call(
        _lat_d_body,
        grid_spec=pltpu.PrefetchScalarGridSpec(
            num_scalar_prefetch=1, grid=(DEC_BATCH, DEC_SEQ // TQ_D), in_specs=in_specs, out_specs=qo(512)),
        out_shape=jax.ShapeDtypeStruct((DEC_BATCH, DEC_SEQ, 512), BF),
        compiler_params=_cparams(("arbitrary",) * 2),
        name="lat_attn_d",
    )(l, q, k, v)


def _lat_b_body(l_ref, q_ref, k_ref, v_ref, kc_ref, vc_ref, tab_ref, o_ref):
    i = pl.program_id(1)
    qr0 = i * NB_ROWS
    kr0 = jnp.clip(qr0 - NA_ROWS // 2, 0, ROWS - NB_KROWS)
    start = pl.multiple_of(kr0 * GRID_W, GRID_W)
    n_keys = NB_KROWS * GRID_W
    lt = _lane_lt64((GRID_W, LANES))
    for j in range(H_B // 2):
        sl = slice(j * LANES, (j + 1) * LANES)
        kwin = k_ref[pl.ds(start, n_keys), sl]
        vwin = v_ref[pl.ds(start, n_keys), 2 * j * LANES:(2 * j + 2) * LANES]
        ks = [kwin, kc_ref[:, sl].astype(BF)]
        vs = [vwin, _with_ones(vc_ref[:, sl].astype(BF))]
        q = q_ref[:, sl]
        ltq = _lane_lt64(q.shape)
        zero = jnp.zeros_like(q)
        outs = []
        for half, qm in ((0, jnp.where(ltq, q, zero)), (1, jnp.where(ltq, zero, q))):
            head = 2 * j + half
            rows = []
            for a in range(NB_ROWS):
                qr = qr0 + a
                r0 = jnp.clip(qr - NA_ROWS // 2, 0, ROWS - NA_ROWS)
                tiles = []
                for p in range(NB_KROWS // 2):
                    kr_l = kr0 + 2 * p
                    u = jnp.clip(kr_l - qr + NA_ROWS, 0, 2 * NA_ROWS - 1)
                    pen_l = jnp.where((kr_l >= r0) & (kr_l < r0 + NA_ROWS), 0.0, NEG)
                    pen_r = jnp.where((kr_l + 1 >= r0) & (kr_l + 1 < r0 + NA_ROWS), 0.0, NEG)
                    tiles.append(tab_ref[head, u] + jnp.where(lt, pen_l, pen_r))
                rows.append(jnp.concatenate(tiles, axis=1))
            bias = jnp.concatenate(rows, axis=0)
            sc = [_dot_nt(qm, ks[0]) + bias, _dot_nt(qm, ks[1])]
            outs.append(_softmax_pv(sc, vs))
        o_ref[:, sl] = jnp.where(ltq, outs[0], outs[1]).astype(BF)


def _lat_b(l, q, k, v, cache_k, cache_v, tab):
    nq = NB_ROWS * GRID_W
    kv = pl.BlockSpec((None, DEC_SEQ, 512), lambda b, i, lr: (b, 0, 0))
    vv = pl.BlockSpec((None, DEC_SEQ, 1024), lambda b, i, lr: (b, 0, 0))
    cache = pl.BlockSpec((None, None, PAST_LEN, 512), lambda b, i, lr: (b, lr[0], 0, 0))
    qo = pl.BlockSpec((None, nq, 512), lambda b, i, lr: (b, i, 0))
    return pl.pallas_call(
        _lat_b_body,
        grid_spec=pltpu.PrefetchScalarGridSpec(
            num_scalar_prefetch=1, grid=(DEC_BATCH, DEC_SEQ // nq),
            in_specs=[qo, kv, vv, cache, cache,
                      pl.BlockSpec((None, H_B, 2 * NA_ROWS, GRID_W, LANES), lambda b, i, lr: (lr[0], 0, 0, 0, 0))],
            out_specs=qo),
        out_shape=jax.ShapeDtypeStruct((DEC_BATCH, DEC_SEQ, 512), BF),
        compiler_params=_cparams(("arbitrary",) * 2),
        name="lat_attn_b",
    )(l, q, k, v, cache_k, cache_v, tab)


def _merge_body(l_ref, x_ref, mod_ref, oa_ref, ob_ref, oc_ref, od_ref, z_ref, g0_ref, g1_ref, g2_ref, g3_ref,
                wbr_ref, wout_ref, lng_ref, lnb_ref, o_ref):
    merged = None
    for i, (o_r, g_r) in enumerate(zip((oa_ref, ob_ref, oc_ref, od_ref), (g0_ref, g1_ref, g2_ref, g3_ref))):
        u = (o_r[...].astype(F32) * z_ref[:, i * BRANCH_W:(i + 1) * BRANCH_W].astype(F32)).astype(BF)
        term = g_r[...].astype(F32) * _dot(u, wbr_ref[i])
        merged = term if merged is None else merged + term
    y = _dot(merged.astype(BF), wout_ref[...])
    gate = mod_ref[:, 2 * D_MODEL:3 * D_MODEL]
    r = ALPHA * x_ref[...] + gate * y
    mu = jnp.mean(r, axis=-1, keepdims=True)
    d = r - mu
    var = jnp.mean(d * d, axis=-1, keepdims=True)
    o_ref[...] = d * lax.rsqrt(var + EPS) * lng_ref[...] + lnb_ref[...]


def _merge(l, x, mod4, outs, zg, W, *, latent):
    m = x.shape[0]
    tm = TM_MERGE
    per_b = DEC_SEQ // tm
    cond = (lambda i: 1 + i // per_b) if latent else (lambda i: 0)
    row = lambda w: pl.BlockSpec((tm, w), lambda i, lr: (i, 0))
    zgb = lambda j: pl.BlockSpec((tm, D_MODEL), lambda i, lr: (i, j))
    wfull = lambda a: pl.BlockSpec((None,) + a.shape[1:], lambda i, lr: (lr[0],) + (0,) * (a.ndim - 1),
                                   pipeline_mode=pl.Buffered(1))
    in_specs = [row(D_MODEL),
                pl.BlockSpec((None, None, 1, 3 * D_MODEL), lambda i, lr: (lr[0], cond(i), 0, 0)),
                row(512), row(512), row(512), row(512),
                zgb(0), zgb(1), zgb(2), zgb(3), zgb(4),
                wfull(W["w_br"]), wfull(W["w_out"]), wfull(W["ln_g"]), wfull(W["ln_b"])]
    return pl.pallas_call(
        _merge_body,
        grid_spec=pltpu.PrefetchScalarGridSpec(
            num_scalar_prefetch=1, grid=(m // tm,), in_specs=in_specs, out_specs=row(D_MODEL)),
        out_shape=jax.ShapeDtypeStruct((m, D_MODEL), F32),
        input_output_aliases={1: 0},
        compiler_params=_cparams(("arbitrary",)),
        name="merge_lat" if latent else "merge_ctx",
    )(l, x, mod4, *outs, zg, zg, zg, zg, zg, W["w_br"], W["w_out"], W["ln_g"], W["ln_b"])


def _rope_tables():
    t = jnp.arange(DEC_SEQ)
    row = (t // GRID_W).astype(F32)
    col = (t % GRID_W).astype(F32)
    quarter = D_A // 4
    inv_freq = ROPE_BASE ** (-jnp.arange(quarter, dtype=F32) / quarter)
    ar = row[:, None] * inv_freq
    ac = col[:, None] * inv_freq
    ang = jnp.concatenate([ar, ar, ac, ac], axis=-1)
    cos, sin = jnp.cos(ang), jnp.sin(ang)
    even = (jnp.arange(D_A) // quarter) % 2 == 0
    sa = jnp.where(even, -sin, 0.0)
    sb = jnp.where(even, 0.0, sin)
    tile2 = lambda a: jnp.concatenate([a, a], axis=-1)
    return tile2(cos), tile2(sa), tile2(sb)


def _repack_plan():
    offs = [int(v) // 64 for v in np.concatenate([[0], np.cumsum(IN_SIZES)])]
    aq, ak, av, bq, bk, bv, cq, ckv, ckr, dq, dk, dv, z, g = offs[:14]
    gqa = lambda base: [base + h for j in range(H_D // 2) for h in (j, H_D // 2 + j)]
    qkv = list(range(aq, ckr)) + gqa(dq) + [dk, dk + 1, dv, dv + 1, ckr, None]
    zg = list(range(z, offs[14]))

    def runs(chunks):
        out = []
        for d, s in enumerate(chunks):
            if out and s is not None and out[-1][1] is not None and out[-1][1] + out[-1][2] == s:
                out[-1][2] += 1
            else:
                out.append([d, s, 1])
        return out
    assert len(qkv) * 64 == NQ and len(zg) * 64 == NZG
    return runs(qkv), runs(zg)


N_IN = sum(IN_SIZES)


def _repack_body(w_ref, qkv_ref, zg_ref):
    qkv_plan, zg_plan = _repack_plan()
    for plan, o_ref in ((qkv_plan, qkv_ref), (zg_plan, zg_ref)):
        for d, s, n in plan:
            rows = slice(d * 64, (d + n) * 64)
            if s is None:
                o_ref[rows, :] = jnp.zeros((n * 64, TC_REPACK), BF)
            else:
                o_ref[rows, :] = w_ref[s * 64:(s + n) * 64, :].astype(BF)


def _repack_w_in(w_in_t):
    return pl.pallas_call(
        _repack_body,
        grid=(DEPTH, D_MODEL // TC_REPACK),
        in_specs=[pl.BlockSpec((None, N_IN, TC_REPACK), lambda l, i: (l, 0, i))],
        out_specs=[pl.BlockSpec((None, NQ, TC_REPACK), lambda l, i: (l, 0, i)),
                   pl.BlockSpec((None, NZG, TC_REPACK), lambda l, i: (l, 0, i))],
        out_shape=[jax.ShapeDtypeStruct((DEPTH, NQ, D_MODEL), BF),
                   jax.ShapeDtypeStruct((DEPTH, NZG, D_MODEL), BF)],
        compiler_params=_cparams(("arbitrary", "arbitrary")),
        name="repack_w_in",
    )(w_in_t)


def _prep_weights(w_in, c_q_norm, c_kv_norm, w_c_uq, w_c_ukv, d_q_norm, d_k_norm, w_br, w_out, ln_g, ln_b):
    w_qkv, w_zg = _repack_w_in(jnp.swapaxes(w_in, 1, 2))
    uq = w_c_uq.reshape(DEPTH, Q_RANK, H_C, NOPE_DIM + ROPE_DIM)
    wuq = jnp.concatenate([uq, jnp.zeros((DEPTH, Q_RANK, H_C, 2 * LANES - NOPE_DIM - ROPE_DIM), F32)],
                          axis=3).reshape(DEPTH, Q_RANK, H_C * 2 * LANES).astype(BF)
    ukv = w_c_ukv.reshape(DEPTH, KV_RANK, H_C, NOPE_DIM + V_DIM_C)
    wukv_n = ukv[..., :NOPE_DIM].reshape(DEPTH, KV_RANK, H_C * NOPE_DIM).astype(BF)
    wukv_v = ukv[..., NOPE_DIM:].reshape(DEPTH, KV_RANK, H_C * V_DIM_C).astype(BF)
    wbr = w_br.astype(BF)
    return {
        "w_qkv": w_qkv, "w_zg": w_zg, "wuq": wuq, "wukv_n": wukv_n, "wukv_v": wukv_v,
        "cqn": c_q_norm.reshape(DEPTH, 1, Q_RANK), "ckvn": c_kv_norm.reshape(DEPTH, 1, KV_RANK),
        "dqn": jnp.tile(d_q_norm, (1, H_D)).reshape(DEPTH, 1, H_D * D_D),
        "dkn": jnp.tile(d_k_norm, (1, G_D)).reshape(DEPTH, 1, G_D * D_D),
        "w_br": wbr, "w_out": w_out.astype(BF),
        "ln_g": ln_g.reshape(DEPTH, 1, D_MODEL), "ln_b": ln_b.reshape(DEPTH, 1, D_MODEL),
    }


def kernel(x_prompt, x_sample, cache_a_k, cache_a_v, cache_b_k, cache_b_v, cache_c_kv, cache_c_kr, cache_d_k,
           cache_d_v, c, c_ctx, w_ada, b_ada, w_in, lam_a, a_subln, b_rpb, c_q_norm, c_kv_norm, w_c_uq, w_c_ukv,
           d_q_norm, d_k_norm, w_br, w_out, ln_g, ln_b):
    W = _prep_weights(w_in, c_q_norm, c_kv_norm, w_c_uq, w_c_ukv, d_q_norm, d_k_norm, w_br, w_out, ln_g, ln_b)
    cond8 = jnp.concatenate([c_ctx[None], c, jnp.zeros((5, D_MODEL), F32)], axis=0)
    mod4 = _adaln(cond8, w_ada, b_ada).reshape(DEPTH, 8, 1, 3 * D_MODEL)
    nb_tab = _bias_table(b_rpb).reshape(DEPTH, H_B, 2 * NA_ROWS, GRID_W, LANES)
    kc_cache, vc_cache = _cache_mla(cache_c_kv, cache_c_kr, W["wukv_n"], W["wukv_v"])
    rope_tabs = _rope_tables()
    lam_tab = jnp.array([0.8 - 0.6 * math.exp(-0.3 * l) for l in range(DEPTH)], F32)
    subln = a_subln.reshape(DEPTH, 1, 2 * D_A)
    flat = lambda a: a.reshape(DEC_BATCH, DEPTH, PAST_LEN, -1)
    ca_k, ca_v, cb_k, cb_v, cd_k, cd_v = (flat(a) for a in (cache_a_k, cache_a_v, cache_b_k, cache_b_v,
                                                               cache_d_k, cache_d_v))
    lat3 = lambda a: a.reshape(DEC_BATCH, DEC_SEQ, a.shape[-1])

    def layer(carry, li):
        xp, xs, bufs = carry
        l = li.reshape(1)
        (h_c, qa, qb, qc, kc, vc, qd, ka, va, kb, vb, ckv, kr, kd, vd) = _proj(l, xp, mod4, W, bufs, rope=False)
        pc = dict(qa=qa, ka=ka, va=va, qb=qb, kb=kb, vb=vb, qc=qc, kc=kc, vc=vc, qd=qd, kd=kd, vd=vd)
        outs_c = _ctx_attn(l, lam_tab, pc, lam_a, subln)
        zg_c = _zg(l, h_c, W["w_zg"])
        xp_new = _merge(l, xp, mod4, outs_c, zg_c, W, latent=False)
        (h_l, lqa, lka, lva, lqb, lkb, lvb, lqc, lkc, lvc, lqd, lkd, lvd) = _proj(l, xs, mod4, W, rope_tabs,
                                                                                   rope=True)
        lka, lva, lkc, lvc, lkd, lvd = _cache_fill(l, (ca_k, ca_v, kc_cache, vc_cache, cd_k, cd_v),
                                                   (lka, lva, lkc, lvc, lkd, lvd))
        o_a = _lat_a(l, lam_tab, lat3(lqa), lka, lva, lam_a, subln)
        o_b = _lat_b(l, lat3(lqb), lat3(lkb), lat3(lvb), cb_k, cb_v, nb_tab)
        o_c = _lat_c(l, lat3(lqc), lkc, lvc)
        o_d = _lat_d(l, lat3(lqd), lkd, lvd)
        outs_l = [o.reshape(N_LAT, 512) for o in (o_a, o_b, o_c, o_d)]
        zg_l = _zg(l, h_l, W["w_zg"])
        xs_new = _merge(l, xs, mod4, outs_l, zg_l, W, latent=True)
        return (xp_new, xs_new, (ka, va, kb, vb, ckv, kr, kd, vd)), None

    bufs0 = tuple(jnp.zeros((BATCH, DEPTH, SEQ) + ((H_A, 2 * D_A) if k < 2 else (w,)), F32)
                  for k, w in enumerate(CACHE_WIDTHS))
    (xp, xs, caches), _ = lax.scan(
        layer, (x_prompt.reshape(N_CTX, D_MODEL), x_sample.reshape(N_LAT, D_MODEL), bufs0),
        jnp.arange(DEPTH, dtype=jnp.int32))
    ka, va, kb, vb, ckv, kr, kd, vd = caches

    def out(a, tail):
        return a.reshape((BATCH, DEPTH, SEQ) + tail)

    return (xp.reshape(BATCH, SEQ, D_MODEL), xs.reshape(DEC_BATCH, DEC_SEQ, D_MODEL),
            out(ka, (H_A, 2 * D_A)), out(va, (H_A, 2 * D_A)), out(kb, (H_B, D_B)), out(vb, (H_B, D_B)),
            out(ckv, (KV_RANK,)), out(kr, (ROPE_DIM,)), out(kd, (G_D, D_D)), out(vd, (G_D, D_D)))
```

```python
import functools
import math

import jax
import jax.numpy as jnp
import numpy as np
from jax import lax
from jax.experimental import pallas as pl
from jax.experimental.pallas import tpu as pltpu

D_MODEL = 2048
BATCH = 16
SEQ = 256
DEPTH = 4
DEC_BATCH = 2
DEC_SEQ = 4096
PAST_LEN = 256
GRID_W = 64
ROWS = DEC_SEQ // GRID_W
N_BRANCH = 4
BRANCH_W = 512
H_A, D_A = 4, 64
H_B, D_B = 8, 64
NA_ROWS, NA_COLS = 8, 16
H_C, Q_RANK, KV_RANK, NOPE_DIM, ROPE_DIM, V_DIM_C = 4, 512, 256, 128, 64, 128
H_D, G_D, D_D = 8, 2, 64
ROPE_BASE = 10000.0
EPS = 1e-6
ALPHA = (2 * DEPTH) ** 0.25
IN_SIZES = (512, 512, 512, 512, 512, 512, Q_RANK, KV_RANK, ROPE_DIM, 512, 128, 128,
            N_BRANCH * BRANCH_W, N_BRANCH * D_MODEL)

BF = jnp.bfloat16
F32 = jnp.float32
LANES = 128
ZG_CHUNK = 512
LOG2E = 1.4426950408889634
VMEM_LIMIT = 56 * 1024 * 1024
NEG = -1e30

N_CTX = BATCH * SEQ
N_LAT = DEC_BATCH * DEC_SEQ
NQ = 4736
NZG = 10240

O_AQ, O_AK, O_AV, O_BQ, O_BK, O_BV = 0, 512, 1024, 1536, 2048, 2560
O_CQ, O_CKV, O_DQ, O_DK, O_DV, O_KR = 3072, 3584, 3840, 4352, 4480, 4608

TM_PROJ = 256
TM_ZG = 1024
TN_ZG = 2048
TM_MERGE = 256
ROW_TILE = 256
TQ_A = 1024
TQ_C = 1024
TQ_D = 1024
TC_REPACK = 256
NB_ROWS = 4
NB_KROWS = 12


def _cparams(sem):
    return pltpu.CompilerParams(dimension_semantics=sem, vmem_limit_bytes=VMEM_LIMIT)


def _dot(a, b):
    return jnp.dot(a, b, preferred_element_type=F32)


def _dot_nt(a, b):
    return lax.dot_general(a, b, (((1,), (1,)), ((), ())), preferred_element_type=F32)


def _sigmoid(x):
    return 1.0 / (1.0 + jnp.exp(-x))


def _lane_lt64(shape):
    return lax.broadcasted_iota(jnp.int32, shape, len(shape) - 1) < 64


def _adaln_body(c_ref, w_ref, b_ref, o_ref):
    c = c_ref[...]
    s = (c * _sigmoid(c)).astype(BF)
    o_ref[...] = _dot(s, w_ref[...].astype(BF)) + b_ref[...]


def _adaln(cond8, w_ada, b_ada):
    tn = 1536
    return pl.pallas_call(
        _adaln_body,
        grid=(DEPTH, 3 * D_MODEL // tn),
        in_specs=[pl.BlockSpec((8, D_MODEL), lambda l, j: (0, 0)),
                  pl.BlockSpec((None, D_MODEL, tn), lambda l, j: (l, 0, j)),
                  pl.BlockSpec((None, 1, tn), lambda l, j: (l, 0, j))],
        out_specs=pl.BlockSpec((None, 8, tn), lambda l, j: (l, 0, j)),
        out_shape=jax.ShapeDtypeStruct((DEPTH, 8, 3 * D_MODEL), F32),
        compiler_params=_cparams(("arbitrary", "arbitrary")),
        name="adaln",
    )(cond8, w_ada, b_ada.reshape(DEPTH, 1, 3 * D_MODEL))


def _bias_table_body(rpb_ref, o_ref):
    lh = pl.program_id(0)
    qc = lax.broadcasted_iota(jnp.int32, (GRID_W, LANES), 0)
    lane = lax.broadcasted_iota(jnp.int32, (GRID_W, LANES), 1)
    kc = jnp.bitwise_and(lane, 63)
    c0 = jnp.clip(qc - NA_COLS // 2, 0, GRID_W - NA_COLS)
    col_ok = (kc >= c0) & (kc < c0 + NA_COLS)
    dc = jnp.where(col_ok, kc - qc + (NA_COLS - 1), -1)
    right = lane >= 64
    n_dr, n_dc = 2 * NA_ROWS - 1, 2 * NA_COLS - 1
    neg = jnp.full((GRID_W, LANES), NEG, F32)
    rows = []
    for dr in range(n_dr):
        val = neg
        for d in range(n_dc):
            val = jnp.where(dc == d, rpb_ref[(lh * n_dr + dr) * n_dc + d] * LOG2E, val)
        rows.append(val)
    for u in range(n_dr + 1):
        left = rows[u - 1] if u >= 1 else neg
        o_ref[u] = jnp.where(right, rows[u] if u < n_dr else neg, left)


def _bias_table(b_rpb):
    n = DEPTH * H_B
    return pl.pallas_call(
        _bias_table_body,
        grid_spec=pltpu.PrefetchScalarGridSpec(
            num_scalar_prefetch=1, grid=(n,),
            in_specs=[],
            out_specs=pl.BlockSpec((None, 2 * NA_ROWS, GRID_W, LANES), lambda i, r: (i, 0, 0, 0))),
        out_shape=jax.ShapeDtypeStruct((n, 2 * NA_ROWS, GRID_W, LANES), F32),
        compiler_params=_cparams(("arbitrary",)),
        name="nb_bias_table",
    )(b_rpb.reshape(-1))


def _cache_mla_body(ckv_ref, kr_ref, wn_ref, wv_ref, k_ref, v_ref):
    ckv = ckv_ref[...].astype(BF)
    kn = _dot(ckv, wn_ref[...])
    kr = kr_ref[...]
    k_ref[...] = jnp.concatenate(
        [t for h in range(H_C) for t in (kn[:, h * LANES:(h + 1) * LANES], kr)], axis=1).astype(BF)
    v_ref[...] = _dot(ckv, wv_ref[...]).astype(BF)


def _cache_mla(cache_c_kv, cache_c_kr, wukv_n, wukv_v):
    return pl.pallas_call(
        _cache_mla_body,
        grid=(DEPTH, DEC_BATCH),
        in_specs=[pl.BlockSpec((None, None, PAST_LEN, KV_RANK), lambda l, b: (b, l, 0, 0)),
                  pl.BlockSpec((None, None, PAST_LEN, LANES), lambda l, b: (b, l, 0, 0)),
                  pl.BlockSpec((None, KV_RANK, 512), lambda l, b: (l, 0, 0)),
                  pl.BlockSpec((None, KV_RANK, 512), lambda l, b: (l, 0, 0))],
        out_specs=[pl.BlockSpec((None, None, PAST_LEN, 1024), lambda l, b: (l, b, 0, 0)),
                   pl.BlockSpec((None, None, PAST_LEN, 512), lambda l, b: (l, b, 0, 0))],
        out_shape=[jax.ShapeDtypeStruct((DEPTH, DEC_BATCH, PAST_LEN, 1024), BF),
                   jax.ShapeDtypeStruct((DEPTH, DEC_BATCH, PAST_LEN, 512), BF)],
        compiler_params=_cparams(("arbitrary", "arbitrary")),
        name="cache_mla",
    )(cache_c_kv, jnp.pad(cache_c_kr, ((0, 0), (0, 0), (0, 0), (0, LANES - ROPE_DIM))), wukv_n, wukv_v)


def _rope_tiles(x, cos, sa, sb):
    outs = []
    for j in range(x.shape[1] // LANES):
        t = x[:, j * LANES:(j + 1) * LANES]
        outs.append(t * cos + pltpu.roll(t, LANES - 16, 1) * sa + pltpu.roll(t, 16, 1) * sb)
    return outs[0] if len(outs) == 1 else jnp.concatenate(outs, axis=1)


def _group64_rms(x, g):
    w = x.shape[1]
    r = lax.shift_right_logical(lax.broadcasted_iota(jnp.int32, (w, w), 0), 6)
    c = lax.shift_right_logical(lax.broadcasted_iota(jnp.int32, (w, w), 1), 6)
    bd = jnp.where(r == c, 1.0, 0.0).astype(BF)
    x2 = x * x
    hi = x2.astype(BF)
    lo = (x2 - hi.astype(F32)).astype(BF)
    ms = (_dot(hi, bd) + _dot(lo, bd)) * (1.0 / 64)
    return x * lax.rsqrt(ms + EPS) * g


def _interleave_ones(v):
    ones = jnp.ones((v.shape[0], LANES), BF)
    return jnp.concatenate([t for j in range(v.shape[1] // LANES)
                            for t in (v[:, j * LANES:(j + 1) * LANES].astype(BF), ones)], axis=1)


def _full_rms(x, g):
    ms = jnp.mean(x * x, axis=-1, keepdims=True)
    return x * lax.rsqrt(ms + EPS) * g


def _proj_body(l_ref, x_ref, mod_ref, w_ref, wuq_ref, wun_ref, wuv_ref, cqn_ref, ckvn_ref, dqn_ref, dkn_ref,
               *refs, rope):
    if rope:
        cos_ref, sa_ref, sb_ref = refs[:3]
        refs = refs[3:]
        cos, sa, sb = cos_ref[...], sa_ref[...], sb_ref[...]
        rp = lambda t: _rope_tiles(t, cos, sa, sb)
    else:
        rp = lambda t: t
    x = x_ref[...]
    shift = mod_ref[:, 0:D_MODEL]
    scale = mod_ref[:, D_MODEL:2 * D_MODEL]
    h = (x * (1.0 + scale) + shift).astype(BF)

    acc = _dot_nt(h, w_ref[...])

    def col(o, n):
        return acc[:, o:o + n]

    with_ones = _interleave_ones

    qa = rp(col(O_AQ, 512)) * (D_A ** -0.5 * LOG2E)
    ka = rp(col(O_AK, 512))
    va = col(O_AV, 512)
    qb = col(O_BQ, 512) * (D_B ** -0.5 * LOG2E)
    kb = col(O_BK, 512)
    vb = col(O_BV, 512)
    cq = _full_rms(col(O_CQ, Q_RANK), cqn_ref[...]).astype(BF)
    qc_raw = _dot(cq, wuq_ref[...])
    qc_scale = (NOPE_DIM + ROPE_DIM) ** -0.5 * LOG2E
    qc = jnp.concatenate(
        [t for hh in range(H_C) for t in (qc_raw[:, 2 * hh * LANES:(2 * hh + 1) * LANES],
                                          rp(qc_raw[:, (2 * hh + 1) * LANES:(2 * hh + 2) * LANES]))],
        axis=1) * qc_scale
    ckv = _full_rms(col(O_CKV, KV_RANK), ckvn_ref[...])
    ckv_b = ckv.astype(BF)
    kn = _dot(ckv_b, wun_ref[...])
    vc = _dot(ckv_b, wuv_ref[...])
    kr_raw = col(O_KR, LANES)
    kr = rp(kr_raw)
    kc = jnp.concatenate([t for hh in range(H_C) for t in (kn[:, hh * LANES:(hh + 1) * LANES], kr)], axis=1)
    qd = rp(_group64_rms(col(O_DQ, 512), dqn_ref[...])) * (D_D ** -0.5 * LOG2E)
    kd_n = _group64_rms(col(O_DK, LANES), dkn_ref[...])
    kd = rp(kd_n)
    vd = col(O_DV, LANES)

    if rope:
        (h_o, qa_o, ka_o, va_o, qb_o, kb_o, vb_o, qc_o, kc_o, vc_o, qd_o, kd_o, vd_o) = refs
        ka_o[...] = ka.astype(BF)
        va_o[...] = with_ones(va)
        kb_o[...] = kb.astype(BF)
        vb_o[...] = with_ones(vb)
        kd_o[...] = kd.astype(BF)
        vd_o[...] = with_ones(vd)
        vc_o[...] = with_ones(vc)
    else:
        (h_o, qa_o, qb_o, qc_o, kc_o, vc_o, qd_o,
         ka_o, va_o, kb_o, vb_o, ckv_o, kr_o, kd_o, vd_o) = refs[len(CACHE_WIDTHS):]
        ka_o[...] = ka
        va_o[...] = va
        kb_o[...] = kb
        vb_o[...] = vb
        ckv_o[...] = ckv
        kr_o[...] = kr_raw[:, 0:ROPE_DIM]
        kd_o[...] = kd_n
        vd_o[...] = vd
        vc_o[...] = vc.astype(BF)
    h_o[...] = h
    qa_o[...] = qa.astype(BF)
    qb_o[...] = qb.astype(BF)
    qc_o[...] = qc.astype(BF)
    kc_o[...] = kc.astype(BF)
    qd_o[...] = qd.astype(BF)


CACHE_WIDTHS = (512, 512, 512, 512, KV_RANK, ROPE_DIM, LANES, LANES)
KV_LEN = DEC_SEQ + PAST_LEN
LAT_KV_OUTS = (2, 3, 8, 9, 11, 12)


def _proj(l, x, mod4, W, extra, *, rope):
    m = x.shape[0]
    tm = TM_PROJ
    per_b = DEC_SEQ // tm
    cond = (lambda i: 1 + i // per_b) if rope else (lambda i: 0)
    row = lambda w: pl.BlockSpec((tm, w), lambda i, lr: (i, 0))
    wfull = lambda a: pl.BlockSpec((None,) + a.shape[1:], lambda i, lr: (lr[0],) + (0,) * (a.ndim - 1),
                                   pipeline_mode=pl.Buffered(1))
    weights = [W["w_qkv"], W["wuq"], W["wukv_n"], W["wukv_v"], W["cqn"], W["ckvn"], W["dqn"], W["dkn"]]
    in_specs = [row(D_MODEL),
                pl.BlockSpec((None, None, 1, 3 * D_MODEL), lambda i, lr: (lr[0], cond(i), 0, 0))]
    in_specs += [wfull(a) for a in weights]
    args = [x, mod4] + weights
    if rope:
        in_specs += [pl.BlockSpec((tm, LANES), lambda i, lr: (i % per_b, 0))] * 3
        widths = [(D_MODEL, BF), (512, BF), (512, BF), (1024, BF), (512, BF), (512, BF), (1024, BF),
                  (1024, BF), (1024, BF), (1024, BF), (512, BF), (LANES, BF), (2 * LANES, BF)]
        aliases = {}
    else:
        assert tm == SEQ
        in_specs += [pl.BlockSpec(memory_space=pl.ANY)] * len(CACHE_WIDTHS)
        widths = [(D_MODEL, BF), (512, BF), (512, BF), (1024, BF), (1024, BF), (512, BF), (512, BF)]
        aliases = {1 + len(args) + k: len(widths) + k for k in range(len(CACHE_WIDTHS))}
    args += list(extra)
    out_specs = [row(w) for w, _ in widths]
    out_shape = [jax.ShapeDtypeStruct((m, w), d) for w, d in widths]
    if rope:
        for k in LAT_KV_OUTS:
            w, d = widths[k]
            out_specs[k] = pl.BlockSpec((None, tm, w), lambda i, lr: (i // per_b, i % per_b, 0))
            out_shape[k] = jax.ShapeDtypeStruct((DEC_BATCH, KV_LEN, w), d)
    if not rope:
        out_specs += [pl.BlockSpec((None, None, SEQ, w), lambda i, lr: (i, lr[0], 0, 0)) for w in CACHE_WIDTHS]
        out_shape += [jax.ShapeDtypeStruct((BATCH, DEPTH, SEQ, w), F32) for w in CACHE_WIDTHS]
    return pl.pallas_call(
        functools.partial(_proj_body, rope=rope),
        grid_spec=pltpu.PrefetchScalarGridSpec(
            num_scalar_prefetch=1, grid=(m // tm,), in_specs=in_specs, out_specs=out_specs),
        out_shape=out_shape,
        input_output_aliases=aliases,
        compiler_params=_cparams(("arbitrary",)),
        name="proj_lat" if rope else "proj_ctx",
    )(l, *args)


def _cache_fill_body(l_ref, cak_ref, cav_ref, kcc_ref, vcc_ref, cdk_ref, cdv_ref, *refs):
    ka_o, va_o, kc_o, vc_o, kd_o, vd_o = refs[6:]
    ka_o[...] = cak_ref[...].astype(BF)
    va_o[...] = _interleave_ones(cav_ref[...])
    kc_o[...] = kcc_ref[...]
    vc_o[...] = _interleave_ones(vcc_ref[...])
    kd_o[...] = cdk_ref[...].astype(BF)
    vd_o[...] = _interleave_ones(cdv_ref[...])


def _cache_fill(l, caches, bufs):
    ca_k, ca_v, kc_cache, vc_cache, cd_k, cd_v = caches
    by_batch = lambda a: pl.BlockSpec((None, None, PAST_LEN, a.shape[3]), lambda b, lr: (b, lr[0], 0, 0))
    by_layer = lambda a: pl.BlockSpec((None, None, PAST_LEN, a.shape[3]), lambda b, lr: (lr[0], b, 0, 0))
    in_specs = [by_batch(ca_k), by_batch(ca_v), by_layer(kc_cache), by_layer(vc_cache), by_batch(cd_k),
                by_batch(cd_v)] + [pl.BlockSpec(memory_space=pl.ANY)] * len(bufs)
    tail = DEC_SEQ // PAST_LEN
    return pl.pallas_call(
        _cache_fill_body,
        grid_spec=pltpu.PrefetchScalarGridSpec(
            num_scalar_prefetch=1, grid=(DEC_BATCH,), in_specs=in_specs,
            out_specs=[pl.BlockSpec((None, PAST_LEN, a.shape[2]), lambda b, lr: (b, tail, 0)) for a in bufs]),
        out_shape=[jax.ShapeDtypeStruct(a.shape, a.dtype) for a in bufs],
        input_output_aliases={1 + len(caches) + k: k for k in range(len(bufs))},
        compiler_params=_cparams(("arbitrary",)),
        name="cache_fill",
    )(l, *caches, *bufs)


def _zg_body(l_ref, h_ref, w_ref, o_ref):
    is_z = pl.program_id(0) < (N_BRANCH * BRANCH_W) // TN_ZG

    def run(silu):
        h = h_ref[...]
        for n in range(TN_ZG // ZG_CHUNK):
            sl = slice(n * ZG_CHUNK, (n + 1) * ZG_CHUNK)
            a = _dot_nt(h, w_ref[sl, :])
            s = _sigmoid(a)
            o_ref[:, sl] = ((a * s) if silu else s).astype(BF)

    @pl.when(is_z)
    def _():
        run(True)

    @pl.when(jnp.logical_not(is_z))
    def _():
        run(False)


def _zg(l, h, w_zg):
    m = h.shape[0]
    return pl.pallas_call(
        _zg_body,
        grid_spec=pltpu.PrefetchScalarGridSpec(
            num_scalar_prefetch=1, grid=(NZG // TN_ZG, m // TM_ZG),
            in_specs=[pl.BlockSpec((TM_ZG, D_MODEL), lambda j, i, lr: (i, 0)),
                      pl.BlockSpec((None, TN_ZG, D_MODEL), lambda j, i, lr: (lr[0], j, 0))],
            out_specs=pl.BlockSpec((TM_ZG, TN_ZG), lambda j, i, lr: (i, j))),
        out_shape=jax.ShapeDtypeStruct((m, NZG), BF),
        compiler_params=_cparams(("arbitrary", "arbitrary")),
        name="zg_proj",
    )(l, h, w_zg)


def _softmax_pv(scores, values):
    m = None
    for s in scores:
        sm = jnp.max(s, axis=-1, keepdims=True)
        m = sm if m is None else jnp.maximum(m, sm)
    acc = None
    for s, v in zip(scores, values):
        o = _dot(jnp.exp2((s - m).astype(BF)), v)
        acc = o if acc is None else acc + o
    return acc[:, :LANES] / acc[:, LANES:]


def _with_ones(v):
    return jnp.concatenate([v, jnp.ones_like(v)], axis=1)


def _diff_lambda(lam_ref, lam_init):
    la = lam_ref[...]
    s01 = jnp.sum(la[0:1] * la[1:2], axis=-1, keepdims=True)
    s23 = jnp.sum(la[2:3] * la[3:4], axis=-1, keepdims=True)
    return jnp.exp(s01) - jnp.exp(s23) + lam_init


def _diff_head(q, ks, vs, lam, subln, lam_init):
    lt = _lane_lt64(q.shape)
    zero = jnp.zeros_like(q)
    o = []
    for qm in (jnp.where(lt, q, zero), jnp.where(lt, zero, q)):
        o.append(_softmax_pv([_dot_nt(qm, k) for k in ks], vs))
    d = o[0] - lam * o[1]
    ms = jnp.mean(d * d, axis=-1, keepdims=True)
    return d * lax.rsqrt(ms + EPS) * subln * (1.0 - lam_init)


def _pair_heads(q, ks, vs, bias=None):
    lt = _lane_lt64(q.shape)
    zero = jnp.zeros_like(q)
    o = []
    for qm in (jnp.where(lt, q, zero), jnp.where(lt, zero, q)):
        sc = [_dot_nt(qm, k) for k in ks]
        if bias is not None:
            sc[0] = sc[0] + bias
        o.append(_softmax_pv(sc, vs))
    return jnp.where(_lane_lt64(o[0].shape), o[0], o[1])


def _gqa_natural(tiles):
    t0, t1, t2, t3 = tiles
    lt = _lane_lt64(t0.shape)
    swap = lambda t: pltpu.roll(t, 64, 1)
    return jnp.concatenate([jnp.where(lt, t0, swap(t1)), jnp.where(lt, t2, swap(t3)),
                            jnp.where(lt, swap(t0), t1), jnp.where(lt, swap(t2), t3)], axis=1)


def _ctx_attn_body(l_ref, li_ref, qa_ref, ka_ref, va_ref, qb_ref, kb_ref, vb_ref, qc_ref, kc_ref, vc_ref,
                   qd_ref, kd_ref, vd_ref, lam_ref, subln_ref, oa_ref, ob_ref, oc_ref, od_ref):
    lam_init = li_ref[l_ref[0]]
    lam = _diff_lambda(lam_ref, lam_init)
    subln = subln_ref[...]
    for h in range(H_A):
        sl = slice(h * LANES, (h + 1) * LANES)
        oa_ref[:, sl] = _diff_head(qa_ref[:, sl], [ka_ref[:, sl].astype(BF)],
                                   [_with_ones(va_ref[:, sl].astype(BF))], lam, subln, lam_init).astype(BF)
    for j in range(H_B // 2):
        sl = slice(j * LANES, (j + 1) * LANES)
        ob_ref[:, sl] = _pair_heads(qb_ref[:, sl], [kb_ref[:, sl].astype(BF)],
                                    [_with_ones(vb_ref[:, sl].astype(BF))]).astype(BF)
    for h in range(H_C):
        oc_ref[:, h * LANES:(h + 1) * LANES] = _softmax_pv(
            [_dot_nt(qc_ref[:, 2 * h * LANES:(2 * h + 2) * LANES], kc_ref[:, 2 * h * LANES:(2 * h + 2) * LANES])],
            [_with_ones(vc_ref[:, h * LANES:(h + 1) * LANES])]).astype(BF)
    kd = kd_ref[...].astype(BF)
    vd = _with_ones(vd_ref[...].astype(BF))
    od_ref[...] = _gqa_natural(
        [_pair_heads(qd_ref[:, j * LANES:(j + 1) * LANES], [kd], [vd]) for j in range(H_D // 2)]).astype(BF)


def _ctx_attn(l, lam_tab, pc, lam_a, a_subln):
    row = lambda w: pl.BlockSpec((SEQ, w), lambda b, lr, li: (b, 0))
    ins = [pc["qa"], pc["ka"], pc["va"], pc["qb"], pc["kb"], pc["vb"], pc["qc"], pc["kc"], pc["vc"],
           pc["qd"], pc["kd"], pc["vd"]]
    layer_row = lambda w: pl.BlockSpec((None, None, SEQ, w), lambda b, lr, li: (b, lr[0], 0, 0))
    in_specs = [row(a.shape[1]) if a.ndim == 2 else layer_row(a.shape[3]) for a in ins]
    in_specs += [pl.BlockSpec((None, 4, D_A), lambda b, lr, li: (lr[0], 0, 0)),
                 pl.BlockSpec((None, 1, 2 * D_A), lambda b, lr, li: (lr[0], 0, 0))]
    return pl.pallas_call(
        _ctx_attn_body,
        grid_spec=pltpu.PrefetchScalarGridSpec(
            num_scalar_prefetch=2, grid=(BATCH,), in_specs=in_specs,
            out_specs=[row(512)] * 4),
        out_shape=[jax.ShapeDtypeStruct((N_CTX, 512), BF)] * 4,
        compiler_params=_cparams(("arbitrary",)),
        name="ctx_attn",
    )(l, lam_tab, *ins, lam_a, a_subln)


def _lat_a_body(l_ref, li_ref, q_ref, k_ref, v_ref, lam_ref, subln_ref, o_ref):
    lam_init = li_ref[l_ref[0]]
    lam = _diff_lambda(lam_ref, lam_init)
    subln = subln_ref[...]
    for h in range(H_A):
        sl = slice(h * LANES, (h + 1) * LANES)
        ks = [k_ref[:, sl]]
        vs = [v_ref[:, 2 * h * LANES:(2 * h + 2) * LANES]]
        for r in range(TQ_A // ROW_TILE):
            rows = slice(r * ROW_TILE, (r + 1) * ROW_TILE)
            o_ref[rows, sl] = _diff_head(q_ref[rows, sl], ks, vs, lam, subln, lam_init).astype(BF)


def _lat_specs(tq, q_w, k_w, v_w):
    qo = lambda w: pl.BlockSpec((None, tq, w), lambda b, i, *pre: (b, i, 0))
    kv = lambda w: pl.BlockSpec((None, KV_LEN, w), lambda b, i, *pre: (b, 0, 0), pipeline_mode=pl.Buffered(1))
    return qo, [qo(q_w), kv(k_w), kv(v_w)]


def _lat_a(l, lam_tab, q, k, v, lam_a, a_subln):
    qo, in_specs = _lat_specs(TQ_A, 512, 512, 1024)
    in_specs += [pl.BlockSpec((None, 4, D_A), lambda b, i, lr, li: (lr[0], 0, 0)),
                 pl.BlockSpec((None, 1, 2 * D_A), lambda b, i, lr, li: (lr[0], 0, 0))]
    return pl.pallas_call(
        _lat_a_body,
        grid_spec=pltpu.PrefetchScalarGridSpec(
            num_scalar_prefetch=2, grid=(DEC_BATCH, DEC_SEQ // TQ_A), in_specs=in_specs, out_specs=qo(512)),
        out_shape=jax.ShapeDtypeStruct((DEC_BATCH, DEC_SEQ, 512), BF),
        compiler_params=_cparams(("arbitrary",) * 2),
        name="lat_attn_a",
    )(l, lam_tab, q, k, v, lam_a, a_subln)


def _lat_c_body(l_ref, q_ref, k_ref, v_ref, o_ref):
    for h in range(H_C):
        sl = slice(h * LANES, (h + 1) * LANES)
        sl2 = slice(2 * h * LANES, (2 * h + 2) * LANES)
        for r in range(TQ_C // ROW_TILE):
            rows = slice(r * ROW_TILE, (r + 1) * ROW_TILE)
            o_ref[rows, sl] = _softmax_pv([_dot_nt(q_ref[rows, sl2], k_ref[:, sl2])], [v_ref[:, sl2]]).astype(BF)


def _lat_c(l, q, k, v):
    qo, in_specs = _lat_specs(TQ_C, 1024, 1024, 1024)
    return pl.pallas_call(
        _lat_c_body,
        grid_spec=pltpu.PrefetchScalarGridSpec(
            num_scalar_prefetch=1, grid=(DEC_BATCH, DEC_SEQ // TQ_C), in_specs=in_specs, out_specs=qo(512)),
        out_shape=jax.ShapeDtypeStruct((DEC_BATCH, DEC_SEQ, 512), BF),
        compiler_params=_cparams(("arbitrary",) * 2),
        name="lat_attn_c",
    )(l, q, k, v)


def _lat_d_body(l_ref, q_ref, k_ref, v_ref, o_ref):
    ks = [k_ref[...]]
    vs = [v_ref[...]]
    for r in range(TQ_D // ROW_TILE):
        rows = slice(r * ROW_TILE, (r + 1) * ROW_TILE)
        o_ref[rows, :] = _gqa_natural(
            [_pair_heads(q_ref[rows, j * LANES:(j + 1) * LANES], ks, vs) for j in range(H_D // 2)]).astype(BF)


def _lat_d(l, q, k, v):
    qo, in_specs = _lat_specs(TQ_D, 512, LANES, 2 * LANES)
    return pl.pallas_call(
        _lat_d_body,
        grid_spec=pltpu.PrefetchScalarGridSpec(
            num_scalar_prefetch=1, grid=(DEC_BATCH, DEC_SEQ // TQ_D), in_specs=in_specs, out_specs=qo(512)),
        out_shape=jax.ShapeDtypeStruct((DEC_BATCH, DEC_SEQ, 512), BF),
        compiler_params=_cparams(("arbitrary",) * 2),
        name="lat_attn_d",
    )(l, q, k, v)


def _lat_b_body(l_ref, q_ref, k_ref, v_ref, kc_ref, vc_ref, tab_ref, o_ref):
    i = pl.program_id(1)
    qr0 = i * NB_ROWS
    kr0 = jnp.clip(qr0 - NA_ROWS // 2, 0, ROWS - NB_KROWS)
    start = pl.multiple_of(kr0 * GRID_W, GRID_W)
    n_keys = NB_KROWS * GRID_W
    lt = _lane_lt64((GRID_W, LANES))
    for j in range(H_B // 2):
        sl = slice(j * LANES, (j + 1) * LANES)
        kwin = k_ref[pl.ds(start, n_keys), sl]
        vwin = v_ref[pl.ds(start, n_keys), 2 * j * LANES:(2 * j + 2) * LANES]
        ks = [kwin, kc_ref[:, sl].astype(BF)]
        vs = [vwin, _with_ones(vc_ref[:, sl].astype(BF))]
        q = q_ref[:, sl]
        ltq = _lane_lt64(q.shape)
        zero = jnp.zeros_like(q)
        outs = []
        for half, qm in ((0, jnp.where(ltq, q, zero)), (1, jnp.where(ltq, zero, q))):
            head = 2 * j + half
            rows = []
            for a in range(NB_ROWS):
                qr = qr0 + a
                r0 = jnp.clip(qr - NA_ROWS // 2, 0, ROWS - NA_ROWS)
                tiles = []
                for p in range(NB_KROWS // 2):
                    kr_l = kr0 + 2 * p
                    u = jnp.clip(kr_l - qr + NA_ROWS, 0, 2 * NA_ROWS - 1)
                    pen_l = jnp.where((kr_l >= r0) & (kr_l < r0 + NA_ROWS), 0.0, NEG)
                    pen_r = jnp.where((kr_l + 1 >= r0) & (kr_l + 1 < r0 + NA_ROWS), 0.0, NEG)
                    tiles.append(tab_ref[head, u] + jnp.where(lt, pen_l, pen_r))
                rows.append(jnp.concatenate(tiles, axis=1))
            bias = jnp.concatenate(rows, axis=0)
            sc = [_dot_nt(qm, ks[0]) + bias, _dot_nt(qm, ks[1])]
            outs.append(_softmax_pv(sc, vs))
        o_ref[:, sl] = jnp.where(ltq, outs[0], outs[1]).astype(BF)


def _lat_b(l, q, k, v, cache_k, cache_v, tab):
    nq = NB_ROWS * GRID_W
    kv = pl.BlockSpec((None, DEC_SEQ, 512), lambda b, i, lr: (b, 0, 0))
    vv = pl.BlockSpec((None, DEC_SEQ, 1024), lambda b, i, lr: (b, 0, 0))
    cache = pl.BlockSpec((None, None, PAST_LEN, 512), lambda b, i, lr: (b, lr[0], 0, 0))
    qo = pl.BlockSpec((None, nq, 512), lambda b, i, lr: (b, i, 0))
    return pl.pallas_call(
        _lat_b_body,
        grid_spec=pltpu.PrefetchScalarGridSpec(
            num_scalar_prefetch=1, grid=(DEC_BATCH, DEC_SEQ // nq),
            in_specs=[qo, kv, vv, cache, cache,
                      pl.BlockSpec((None, H_B, 2 * NA_ROWS, GRID_W, LANES), lambda b, i, lr: (lr[0], 0, 0, 0, 0))],
            out_specs=qo),
        out_shape=jax.ShapeDtypeStruct((DEC_BATCH, DEC_SEQ, 512), BF),
        compiler_params=_cparams(("arbitrary",) * 2),
        name="lat_attn_b",
    )(l, q, k, v, cache_k, cache_v, tab)


def _merge_body(l_ref, x_ref, mod_ref, oa_ref, ob_ref, oc_ref, od_ref, z_ref, g0_ref, g1_ref, g2_ref, g3_ref,
                wbr_ref, wout_ref, lng_ref, lnb_ref, o_ref):
    merged = None
    for i, (o_r, g_r) in enumerate(zip((oa_ref, ob_ref, oc_ref, od_ref), (g0_ref, g1_ref, g2_ref, g3_ref))):
        u = (o_r[...].astype(F32) * z_ref[:, i * BRANCH_W:(i + 1) * BRANCH_W].astype(F32)).astype(BF)
        term = g_r[...].astype(F32) * _dot(u, wbr_ref[i])
        merged = term if merged is None else merged + term
    y = _dot(merged.astype(BF), wout_ref[...])
    gate = mod_ref[:, 2 * D_MODEL:3 * D_MODEL]
    r = ALPHA * x_ref[...] + gate * y
    mu = jnp.mean(r, axis=-1, keepdims=True)
    d = r - mu
    var = jnp.mean(d * d, axis=-1, keepdims=True)
    o_ref[...] = d * lax.rsqrt(var + EPS) * lng_ref[...] + lnb_ref[...]


def _merge(l, x, mod4, outs, zg, W, *, latent):
    m = x.shape[0]
    tm = TM_MERGE
    per_b = DEC_SEQ // tm
    cond = (lambda i: 1 + i // per_b) if latent else (lambda i: 0)
    row = lambda w: pl.BlockSpec((tm, w), lambda i, lr: (i, 0))
    zgb = lambda j: pl.BlockSpec((tm, D_MODEL), lambda i, lr: (i, j))
    wfull = lambda a: pl.BlockSpec((None,) + a.shape[1:], lambda i, lr: (lr[0],) + (0,) * (a.ndim - 1),
                                   pipeline_mode=pl.Buffered(1))
    in_specs = [row(D_MODEL),
                pl.BlockSpec((None, None, 1, 3 * D_MODEL), lambda i, lr: (lr[0], cond(i), 0, 0)),
                row(512), row(512), row(512), row(512),
                zgb(0), zgb(1), zgb(2), zgb(3), zgb(4),
                wfull(W["w_br"]), wfull(W["w_out"]), wfull(W["ln_g"]), wfull(W["ln_b"])]
    return pl.pallas_call(
        _merge_body,
        grid_spec=pltpu.PrefetchScalarGridSpec(
            num_scalar_prefetch=1, grid=(m // tm,), in_specs=in_specs, out_specs=row(D_MODEL)),
        out_shape=jax.ShapeDtypeStruct((m, D_MODEL), F32),
        input_output_aliases={1: 0},
        compiler_params=_cparams(("arbitrary",)),
        name="merge_lat" if latent else "merge_ctx",
    )(l, x, mod4, *outs, zg, zg, zg, zg, zg, W["w_br"], W["w_out"], W["ln_g"], W["ln_b"])


def _rope_tables():
    t = jnp.arange(DEC_SEQ)
    row = (t // GRID_W).astype(F32)
    col = (t % GRID_W).astype(F32)
    quarter = D_A // 4
    inv_freq = ROPE_BASE ** (-jnp.arange(quarter, dtype=F32) / quarter)
    ar = row[:, None] * inv_freq
    ac = col[:, None] * inv_freq
    ang = jnp.concatenate([ar, ar, ac, ac], axis=-1)
    cos, sin = jnp.cos(ang), jnp.sin(ang)
    even = (jnp.arange(D_A) // quarter) % 2 == 0
    sa = jnp.where(even, -sin, 0.0)
    sb = jnp.where(even, 0.0, sin)
    tile2 = lambda a: jnp.concatenate([a, a], axis=-1)
    return tile2(cos), tile2(sa), tile2(sb)


def _repack_plan():
    offs = [int(v) // 64 for v in np.concatenate([[0], np.cumsum(IN_SIZES)])]
    aq, ak, av, bq, bk, bv, cq, ckv, ckr, dq, dk, dv, z, g = offs[:14]
    gqa = lambda base: [base + h for j in range(H_D // 2) for h in (j, H_D // 2 + j)]
    qkv = list(range(aq, ckr)) + gqa(dq) + [dk, dk + 1, dv, dv + 1, ckr, None]
    zg = list(range(z, offs[14]))

    def runs(chunks):
        out = []
        for d, s in enumerate(chunks):
            if out and s is not None and out[-1][1] is not None and out[-1][1] + out[-1][2] == s:
                out[-1][2] += 1
            else:
                out.append([d, s, 1])
        return out
    assert len(qkv) * 64 == NQ and len(zg) * 64 == NZG
    return runs(qkv), runs(zg)


N_IN = sum(IN_SIZES)


def _repack_body(w_ref, qkv_ref, zg_ref):
    qkv_plan, zg_plan = _repack_plan()
    for plan, o_ref in ((qkv_plan, qkv_ref), (zg_plan, zg_ref)):
        for d, s, n in plan:
            rows = slice(d * 64, (d + n) * 64)
            if s is None:
                o_ref[rows, :] = jnp.zeros((n * 64, TC_REPACK), BF)
            else:
                o_ref[rows, :] = w_ref[s * 64:(s + n) * 64, :].astype(BF)


def _repack_w_in(w_in_t):
    return pl.pallas_call(
        _repack_body,
        grid=(DEPTH, D_MODEL // TC_REPACK),
        in_specs=[pl.BlockSpec((None, N_IN, TC_REPACK), lambda l, i: (l, 0, i))],
        out_specs=[pl.BlockSpec((None, NQ, TC_REPACK), lambda l, i: (l, 0, i)),
                   pl.BlockSpec((None, NZG, TC_REPACK), lambda l, i: (l, 0, i))],
        out_shape=[jax.ShapeDtypeStruct((DEPTH, NQ, D_MODEL), BF),
                   jax.ShapeDtypeStruct((DEPTH, NZG, D_MODEL), BF)],
        compiler_params=_cparams(("arbitrary", "arbitrary")),
        name="repack_w_in",
    )(w_in_t)


def _prep_weights(w_in, c_q_norm, c_kv_norm, w_c_uq, w_c_ukv, d_q_norm, d_k_norm, w_br, w_out, ln_g, ln_b):
    w_qkv, w_zg = _repack_w_in(jnp.swapaxes(w_in, 1, 2))
    uq = w_c_uq.reshape(DEPTH, Q_RANK, H_C, NOPE_DIM + ROPE_DIM)
    wuq = jnp.concatenate([uq, jnp.zeros((DEPTH, Q_RANK, H_C, 2 * LANES - NOPE_DIM - ROPE_DIM), F32)],
                          axis=3).reshape(DEPTH, Q_RANK, H_C * 2 * LANES).astype(BF)
    ukv = w_c_ukv.reshape(DEPTH, KV_RANK, H_C, NOPE_DIM + V_DIM_C)
    wukv_n = ukv[..., :NOPE_DIM].reshape(DEPTH, KV_RANK, H_C * NOPE_DIM).astype(BF)
    wukv_v = ukv[..., NOPE_DIM:].reshape(DEPTH, KV_RANK, H_C * V_DIM_C).astype(BF)
    wbr = w_br.astype(BF)
    return {
        "w_qkv": w_qkv, "w_zg": w_zg, "wuq": wuq, "wukv_n": wukv_n, "wukv_v": wukv_v,
        "cqn": c_q_norm.reshape(DEPTH, 1, Q_RANK), "ckvn": c_kv_norm.reshape(DEPTH, 1, KV_RANK),
        "dqn": jnp.tile(d_q_norm, (1, H_D)).reshape(DEPTH, 1, H_D * D_D),
        "dkn": jnp.tile(d_k_norm, (1, G_D)).reshape(DEPTH, 1, G_D * D_D),
        "w_br": wbr, "w_out": w_out.astype(BF),
        "ln_g": ln_g.reshape(DEPTH, 1, D_MODEL), "ln_b": ln_b.reshape(DEPTH, 1, D_MODEL),
    }


def kernel(x_prompt, x_sample, cache_a_k, cache_a_v, cache_b_k, cache_b_v, cache_c_kv, cache_c_kr, cache_d_k,
           cache_d_v, c, c_ctx, w_ada, b_ada, w_in, lam_a, a_subln, b_rpb, c_q_norm, c_kv_norm, w_c_uq, w_c_ukv,
           d_q_norm, d_k_norm, w_br, w_out, ln_g, ln_b):
    W = _prep_weights(w_in, c_q_norm, c_kv_norm, w_c_uq, w_c_ukv, d_q_norm, d_k_norm, w_br, w_out, ln_g, ln_b)
    cond8 = jnp.concatenate([c_ctx[None], c, jnp.zeros((5, D_MODEL), F32)], axis=0)
    mod4 = _adaln(cond8, w_ada, b_ada).reshape(DEPTH, 8, 1, 3 * D_MODEL)
    nb_tab = _bias_table(b_rpb).reshape(DEPTH, H_B, 2 * NA_ROWS, GRID_W, LANES)
    kc_cache, vc_cache = _cache_mla(cache_c_kv, cache_c_kr, W["wukv_n"], W["wukv_v"])
    rope_tabs = _rope_tables()
    lam_tab = jnp.array([0.8 - 0.6 * math.exp(-0.3 * l) for l in range(DEPTH)], F32)
    subln = a_subln.reshape(DEPTH, 1, 2 * D_A)
    flat = lambda a: a.reshape(DEC_BATCH, DEPTH, PAST_LEN, -1)
    ca_k, ca_v, cb_k, cb_v, cd_k, cd_v = (flat(a) for a in (cache_a_k, cache_a_v, cache_b_k, cache_b_v,
                                                               cache_d_k, cache_d_v))
    lat3 = lambda a: a.reshape(DEC_BATCH, DEC_SEQ, a.shape[-1])

    def layer(carry, li):
        xp, xs, bufs = carry
        l = li.reshape(1)
        (h_c, qa, qb, qc, kc, vc, qd, ka, va, kb, vb, ckv, kr, kd, vd) = _proj(l, xp, mod4, W, bufs, rope=False)
        pc = dict(qa=qa, ka=ka, va=va, qb=qb, kb=kb, vb=vb, qc=qc, kc=kc, vc=vc, qd=qd, kd=kd, vd=vd)
        outs_c = _ctx_attn(l, lam_tab, pc, lam_a, subln)
        zg_c = _zg(l, h_c, W["w_zg"])
        xp_new = _merge(l, xp, mod4, outs_c, zg_c, W, latent=False)
        (h_l, lqa, lka, lva, lqb, lkb, lvb, lqc, lkc, lvc, lqd, lkd, lvd) = _proj(l, xs, mod4, W, rope_tabs,
                                                                                   rope=True)
        lka, lva, lkc, lvc, lkd, lvd = _cache_fill(l, (ca_k, ca_v, kc_cache, vc_cache, cd_k, cd_v),
                                                   (lka, lva, lkc, lvc, lkd, lvd))
        o_a = _lat_a(l, lam_tab, lat3(lqa), lka, lva, lam_a, subln)
        o_b = _lat_b(l, lat3(lqb), lat3(lkb), lat3(lvb), cb_k, cb_v, nb_tab)
        o_c = _lat_c(l, lat3(lqc), lkc, lvc)
        o_d = _lat_d(l, lat3(lqd), lkd, lvd)
        outs_l = [o.reshape(N_LAT, 512) for o in (o_a, o_b, o_c, o_d)]
        zg_l = _zg(l, h_l, W["w_zg"])
        xs_new = _merge(l, xs, mod4, outs_l, zg_l, W, latent=True)
        return (xp_new, xs_new, (ka, va, kb, vb, ckv, kr, kd, vd)), None

    bufs0 = tuple(jnp.zeros((BATCH, DEPTH, SEQ, w), F32) for w in CACHE_WIDTHS)
    (xp, xs, caches), _ = lax.scan(
        layer, (x_prompt.reshape(N_CTX, D_MODEL), x_sample.reshape(N_LAT, D_MODEL), bufs0),
        jnp.arange(DEPTH, dtype=jnp.int32))
    ka, va, kb, vb, ckv, kr, kd, vd = caches

    def out(a, tail):
        return a.reshape((BATCH, DEPTH, SEQ) + tail)

    return (xp.reshape(BATCH, SEQ, D_MODEL), xs.reshape(DEC_BATCH, DEC_SEQ, D_MODEL),
            out(ka, (H_A, 2 * D_A)), out(va, (H_A, 2 * D_A)), out(kb, (H_B, D_B)), out(vb, (H_B, D_B)),
            out(ckv, (KV_RANK,)), out(kr, (ROPE_DIM,)), out(kd, (G_D, D_D)), out(vd, (G_D, D_D)))
```

```python
import functools
import math

import jax
import jax.numpy as jnp
import numpy as np
from jax import lax
from jax.experimental import pallas as pl
from jax.experimental.pallas import tpu as pltpu

D_MODEL = 2048
BATCH = 16
SEQ = 256
DEPTH = 4
DEC_BATCH = 2
DEC_SEQ = 4096
PAST_LEN = 256
GRID_W = 64
ROWS = DEC_SEQ // GRID_W
N_BRANCH = 4
BRANCH_W = 512
H_A, D_A = 4, 64
H_B, D_B = 8, 64
NA_ROWS, NA_COLS = 8, 16
H_C, Q_RANK, KV_RANK, NOPE_DIM, ROPE_DIM, V_DIM_C = 4, 512, 256, 128, 64, 128
H_D, G_D, D_D = 8, 2, 64
ROPE_BASE = 10000.0
EPS = 1e-6
ALPHA = (2 * DEPTH) ** 0.25
IN_SIZES = (512, 512, 512, 512, 512, 512, Q_RANK, KV_RANK, ROPE_DIM, 512, 128, 128,
            N_BRANCH * BRANCH_W, N_BRANCH * D_MODEL)

BF = jnp.bfloat16
F32 = jnp.float32
LANES = 128
MXU_N = 256
LOG2E = 1.4426950408889634
VMEM_LIMIT = 56 * 1024 * 1024
NEG = -1e30

N_CTX = BATCH * SEQ
N_LAT = DEC_BATCH * DEC_SEQ
NQ = 4736
NZG = 10240

O_AQ, O_AK, O_AV, O_BQ, O_BK, O_BV = 0, 512, 1024, 1536, 2048, 2560
O_CQ, O_CKV, O_DQ, O_DK, O_DV, O_KR = 3072, 3584, 3840, 4352, 4480, 4608

TM_PROJ = 256
TM_ZG = 1024
TN_ZG = 2048
TM_MERGE = 256
ROW_TILE = 256
TQ_A = 512
TQ_C = 1024
TQ_D = 512
TC_REPACK = 256
NB_ROWS = 4
NB_KROWS = 12


def _cparams(sem):
    return pltpu.CompilerParams(dimension_semantics=sem, vmem_limit_bytes=VMEM_LIMIT)


def _dot(a, b):
    return jnp.dot(a, b, preferred_element_type=F32)


def _dot_nt(a, b):
    return lax.dot_general(a, b, (((1,), (1,)), ((), ())), preferred_element_type=F32)


def _sigmoid(x):
    return 1.0 / (1.0 + jnp.exp(-x))


def _lane_lt64(shape):
    return lax.broadcasted_iota(jnp.int32, shape, len(shape) - 1) < 64


def _adaln_body(c_ref, w_ref, b_ref, o_ref):
    c = c_ref[...]
    s = (c * _sigmoid(c)).astype(BF)
    o_ref[...] = _dot(s, w_ref[...].astype(BF)) + b_ref[...]


def _adaln(cond8, w_ada, b_ada):
    tn = 1536
    return pl.pallas_call(
        _adaln_body,
        grid=(DEPTH, 3 * D_MODEL // tn),
        in_specs=[pl.BlockSpec((8, D_MODEL), lambda l, j: (0, 0)),
                  pl.BlockSpec((None, D_MODEL, tn), lambda l, j: (l, 0, j)),
                  pl.BlockSpec((None, 1, tn), lambda l, j: (l, 0, j))],
        out_specs=pl.BlockSpec((None, 8, tn), lambda l, j: (l, 0, j)),
        out_shape=jax.ShapeDtypeStruct((DEPTH, 8, 3 * D_MODEL), F32),
        compiler_params=_cparams(("arbitrary", "arbitrary")),
        name="adaln",
    )(cond8, w_ada, b_ada.reshape(DEPTH, 1, 3 * D_MODEL))


def _bias_table_body(rpb_ref, o_ref):
    lh = pl.program_id(0)
    qc = lax.broadcasted_iota(jnp.int32, (GRID_W, LANES), 0)
    lane = lax.broadcasted_iota(jnp.int32, (GRID_W, LANES), 1)
    kc = jnp.bitwise_and(lane, 63)
    c0 = jnp.clip(qc - NA_COLS // 2, 0, GRID_W - NA_COLS)
    col_ok = (kc >= c0) & (kc < c0 + NA_COLS)
    dc = jnp.where(col_ok, kc - qc + (NA_COLS - 1), -1)
    right = lane >= 64
    n_dr, n_dc = 2 * NA_ROWS - 1, 2 * NA_COLS - 1
    neg = jnp.full((GRID_W, LANES), NEG, F32)
    rows = []
    for dr in range(n_dr):
        val = neg
        for d in range(n_dc):
            val = jnp.where(dc == d, rpb_ref[(lh * n_dr + dr) * n_dc + d] * LOG2E, val)
        rows.append(val)
    for u in range(n_dr + 1):
        left = rows[u - 1] if u >= 1 else neg
        o_ref[u] = jnp.where(right, rows[u] if u < n_dr else neg, left)


def _bias_table(b_rpb):
    n = DEPTH * H_B
    return pl.pallas_call(
        _bias_table_body,
        grid_spec=pltpu.PrefetchScalarGridSpec(
            num_scalar_prefetch=1, grid=(n,),
            in_specs=[],
            out_specs=pl.BlockSpec((None, 2 * NA_ROWS, GRID_W, LANES), lambda i, r: (i, 0, 0, 0))),
        out_shape=jax.ShapeDtypeStruct((n, 2 * NA_ROWS, GRID_W, LANES), F32),
        compiler_params=_cparams(("arbitrary",)),
        name="nb_bias_table",
    )(b_rpb.reshape(-1))


def _cache_mla_body(ckv_ref, kr_ref, wn_ref, wv_ref, k_ref, v_ref):
    ckv = ckv_ref[...].astype(BF)
    kn = _dot(ckv, wn_ref[...])
    kr = kr_ref[...]
    k_ref[...] = jnp.concatenate(
        [t for h in range(H_C) for t in (kn[:, h * LANES:(h + 1) * LANES], kr)], axis=1).astype(BF)
    v_ref[...] = _dot(ckv, wv_ref[...]).astype(BF)


def _cache_mla(cache_c_kv, cache_c_kr, wukv_n, wukv_v):
    return pl.pallas_call(
        _cache_mla_body,
        grid=(DEPTH, DEC_BATCH),
        in_specs=[pl.BlockSpec((None, None, PAST_LEN, KV_RANK), lambda l, b: (b, l, 0, 0)),
                  pl.BlockSpec((None, None, PAST_LEN, LANES), lambda l, b: (b, l, 0, 0)),
                  pl.BlockSpec((None, KV_RANK, 512), lambda l, b: (l, 0, 0)),
                  pl.BlockSpec((None, KV_RANK, 512), lambda l, b: (l, 0, 0))],
        out_specs=[pl.BlockSpec((None, None, PAST_LEN, 1024), lambda l, b: (l, b, 0, 0)),
                   pl.BlockSpec((None, None, PAST_LEN, 512), lambda l, b: (l, b, 0, 0))],
        out_shape=[jax.ShapeDtypeStruct((DEPTH, DEC_BATCH, PAST_LEN, 1024), BF),
                   jax.ShapeDtypeStruct((DEPTH, DEC_BATCH, PAST_LEN, 512), BF)],
        compiler_params=_cparams(("arbitrary", "arbitrary")),
        name="cache_mla",
    )(cache_c_kv, jnp.pad(cache_c_kr, ((0, 0), (0, 0), (0, 0), (0, LANES - ROPE_DIM))), wukv_n, wukv_v)


def _rope_tiles(x, cos, sa, sb):
    outs = []
    for j in range(x.shape[1] // LANES):
        t = x[:, j * LANES:(j + 1) * LANES]
        outs.append(t * cos + pltpu.roll(t, LANES - 16, 1) * sa + pltpu.roll(t, 16, 1) * sb)
    return outs[0] if len(outs) == 1 else jnp.concatenate(outs, axis=1)


def _group64_rms(x, g):
    w = x.shape[1]
    r = lax.shift_right_logical(lax.broadcasted_iota(jnp.int32, (w, w), 0), 6)
    c = lax.shift_right_logical(lax.broadcasted_iota(jnp.int32, (w, w), 1), 6)
    bd = jnp.where(r == c, 1.0, 0.0).astype(BF)
    x2 = x * x
    hi = x2.astype(BF)
    lo = (x2 - hi.astype(F32)).astype(BF)
    ms = (_dot(hi, bd) + _dot(lo, bd)) * (1.0 / 64)
    return x * lax.rsqrt(ms + EPS) * g


def _interleave_ones(v):
    ones = jnp.ones((v.shape[0], LANES), BF)
    return jnp.concatenate([t for j in range(v.shape[1] // LANES)
                            for t in (v[:, j * LANES:(j + 1) * LANES].astype(BF), ones)], axis=1)


def _full_rms(x, g):
    ms = jnp.mean(x * x, axis=-1, keepdims=True)
    return x * lax.rsqrt(ms + EPS) * g


def _proj_body(l_ref, x_ref, mod_ref, w_ref, wuq_ref, wun_ref, wuv_ref, cqn_ref, ckvn_ref, dqn_ref, dkn_ref,
               *refs, rope):
    if rope:
        cos_ref, sa_ref, sb_ref = refs[:3]
        refs = refs[3:]
        cos, sa, sb = cos_ref[...], sa_ref[...], sb_ref[...]
        rp = lambda t: _rope_tiles(t, cos, sa, sb)
    else:
        rp = lambda t: t
    x = x_ref[...]
    shift = mod_ref[:, 0:D_MODEL]
    scale = mod_ref[:, D_MODEL:2 * D_MODEL]
    h = (x * (1.0 + scale) + shift).astype(BF)

    acc = _dot_nt(h, w_ref[...])

    def col(o, n):
        return acc[:, o:o + n]

    with_ones = _interleave_ones

    qa = rp(col(O_AQ, 512)) * (D_A ** -0.5 * LOG2E)
    ka = rp(col(O_AK, 512))
    va = col(O_AV, 512)
    qb = col(O_BQ, 512) * (D_B ** -0.5 * LOG2E)
    kb = col(O_BK, 512)
    vb = col(O_BV, 512)
    cq = _full_rms(col(O_CQ, Q_RANK), cqn_ref[...]).astype(BF)
    qc_raw = _dot(cq, wuq_ref[...])
    qc_scale = (NOPE_DIM + ROPE_DIM) ** -0.5 * LOG2E
    qc = jnp.concatenate(
        [t for hh in range(H_C) for t in (qc_raw[:, 2 * hh * LANES:(2 * hh + 1) * LANES],
                                          rp(qc_raw[:, (2 * hh + 1) * LANES:(2 * hh + 2) * LANES]))],
        axis=1) * qc_scale
    ckv = _full_rms(col(O_CKV, KV_RANK), ckvn_ref[...])
    ckv_b = ckv.astype(BF)
    kn = _dot(ckv_b, wun_ref[...])
    vc = _dot(ckv_b, wuv_ref[...])
    kr_raw = col(O_KR, LANES)
    kr = rp(kr_raw)
    kc = jnp.concatenate([t for hh in range(H_C) for t in (kn[:, hh * LANES:(hh + 1) * LANES], kr)], axis=1)
    qd = rp(_group64_rms(col(O_DQ, 512), dqn_ref[...])) * (D_D ** -0.5 * LOG2E)
    kd_n = _group64_rms(col(O_DK, LANES), dkn_ref[...])
    kd = rp(kd_n)
    vd = col(O_DV, LANES)

    if rope:
        (h_o, qa_o, ka_o, va_o, qb_o, kb_o, vb_o, qc_o, kc_o, vc_o, qd_o, kd_o, vd_o) = refs
        ka_o[...] = ka.astype(BF)
        va_o[...] = with_ones(va)
        kb_o[...] = kb.astype(BF)
        vb_o[...] = with_ones(vb)
        kd_o[...] = kd.astype(BF)
        vd_o[...] = with_ones(vd)
        vc_o[...] = with_ones(vc)
    else:
        (h_o, qa_o, qb_o, qc_o, kc_o, vc_o, qd_o,
         ka_o, va_o, kb_o, vb_o, ckv_o, kr_o, kd_o, vd_o) = refs[len(CACHE_WIDTHS):]
        ka_o[...] = ka
        va_o[...] = va
        kb_o[...] = kb
        vb_o[...] = vb
        ckv_o[...] = ckv
        kr_o[...] = kr_raw[:, 0:ROPE_DIM]
        kd_o[...] = kd_n
        vd_o[...] = vd
        vc_o[...] = vc.astype(BF)
    h_o[...] = h
    qa_o[...] = qa.astype(BF)
    qb_o[...] = qb.astype(BF)
    qc_o[...] = qc.astype(BF)
    kc_o[...] = kc.astype(BF)
    qd_o[...] = qd.astype(BF)


CACHE_WIDTHS = (512, 512, 512, 512, KV_RANK, ROPE_DIM, LANES, LANES)
KV_LEN = DEC_SEQ + PAST_LEN
LAT_KV_OUTS = (2, 3, 8, 9, 11, 12)


def _proj(l, x, mod4, W, extra, *, rope):
    m = x.shape[0]
    tm = TM_PROJ
    per_b = DEC_SEQ // tm
    cond = (lambda i: 1 + i // per_b) if rope else (lambda i: 0)
    row = lambda w: pl.BlockSpec((tm, w), lambda i, lr: (i, 0))
    wfull = lambda a: pl.BlockSpec((None,) + a.shape[1:], lambda i, lr: (lr[0],) + (0,) * (a.ndim - 1),
                                   pipeline_mode=pl.Buffered(1))
    weights = [W["w_qkv"], W["wuq"], W["wukv_n"], W["wukv_v"], W["cqn"], W["ckvn"], W["dqn"], W["dkn"]]
    in_specs = [row(D_MODEL),
                pl.BlockSpec((None, None, 1, 3 * D_MODEL), lambda i, lr: (lr[0], cond(i), 0, 0))]
    in_specs += [wfull(a) for a in weights]
    args = [x, mod4] + weights
    if rope:
        in_specs += [pl.BlockSpec((tm, LANES), lambda i, lr: (i % per_b, 0))] * 3
        widths = [(D_MODEL, BF), (512, BF), (512, BF), (1024, BF), (512, BF), (512, BF), (1024, BF),
                  (1024, BF), (1024, BF), (1024, BF), (512, BF), (LANES, BF), (2 * LANES, BF)]
        aliases = {}
    else:
        assert tm == SEQ
        in_specs += [pl.BlockSpec(memory_space=pl.ANY)] * len(CACHE_WIDTHS)
        widths = [(D_MODEL, BF), (512, BF), (512, BF), (1024, BF), (1024, BF), (512, BF), (512, BF)]
        aliases = {1 + len(args) + k: len(widths) + k for k in range(len(CACHE_WIDTHS))}
    args += list(extra)
    out_specs = [row(w) for w, _ in widths]
    out_shape = [jax.ShapeDtypeStruct((m, w), d) for w, d in widths]
    if rope:
        for k in LAT_KV_OUTS:
            w, d = widths[k]
            out_specs[k] = pl.BlockSpec((None, tm, w), lambda i, lr: (i // per_b, i % per_b, 0))
            out_shape[k] = jax.ShapeDtypeStruct((DEC_BATCH, KV_LEN, w), d)
    if not rope:
        out_specs += [pl.BlockSpec((None, None, SEQ, w), lambda i, lr: (i, lr[0], 0, 0)) for w in CACHE_WIDTHS]
        out_shape += [jax.ShapeDtypeStruct((BATCH, DEPTH, SEQ, w), F32) for w in CACHE_WIDTHS]
    return pl.pallas_call(
        functools.partial(_proj_body, rope=rope),
        grid_spec=pltpu.PrefetchScalarGridSpec(
            num_scalar_prefetch=1, grid=(m // tm,), in_specs=in_specs, out_specs=out_specs),
        out_shape=out_shape,
        input_output_aliases=aliases,
        compiler_params=_cparams(("arbitrary",)),
        name="proj_lat" if rope else "proj_ctx",
    )(l, *args)


def _cache_fill_body(l_ref, cak_ref, cav_ref, kcc_ref, vcc_ref, cdk_ref, cdv_ref, *refs):
    ka_o, va_o, kc_o, vc_o, kd_o, vd_o = refs[6:]
    ka_o[...] = cak_ref[...].astype(BF)
    va_o[...] = _interleave_ones(cav_ref[...])
    kc_o[...] = kcc_ref[...]
    vc_o[...] = _interleave_ones(vcc_ref[...])
    kd_o[...] = cdk_ref[...].astype(BF)
    vd_o[...] = _interleave_ones(cdv_ref[...])


def _cache_fill(l, caches, bufs):
    ca_k, ca_v, kc_cache, vc_cache, cd_k, cd_v = caches
    by_batch = lambda a: pl.BlockSpec((None, None, PAST_LEN, a.shape[3]), lambda b, lr: (b, lr[0], 0, 0))
    by_layer = lambda a: pl.BlockSpec((None, None, PAST_LEN, a.shape[3]), lambda b, lr: (lr[0], b, 0, 0))
    in_specs = [by_batch(ca_k), by_batch(ca_v), by_layer(kc_cache), by_layer(vc_cache), by_batch(cd_k),
                by_batch(cd_v)] + [pl.BlockSpec(memory_space=pl.ANY)] * len(bufs)
    tail = DEC_SEQ // PAST_LEN
    return pl.pallas_call(
        _cache_fill_body,
        grid_spec=pltpu.PrefetchScalarGridSpec(
            num_scalar_prefetch=1, grid=(DEC_BATCH,), in_specs=in_specs,
            out_specs=[pl.BlockSpec((None, PAST_LEN, a.shape[2]), lambda b, lr: (b, tail, 0)) for a in bufs]),
        out_shape=[jax.ShapeDtypeStruct(a.shape, a.dtype) for a in bufs],
        input_output_aliases={1 + len(caches) + k: k for k in range(len(bufs))},
        compiler_params=_cparams(("arbitrary",)),
        name="cache_fill",
    )(l, *caches, *bufs)


def _zg_body(l_ref, h_ref, w_ref, o_ref):
    is_z = pl.program_id(0) < (N_BRANCH * BRANCH_W) // TN_ZG

    def run(silu):
        h = h_ref[...]
        for n in range(TN_ZG // MXU_N):
            sl = slice(n * MXU_N, (n + 1) * MXU_N)
            a = _dot_nt(h, w_ref[sl, :])
            s = _sigmoid(a)
            o_ref[:, sl] = ((a * s) if silu else s).astype(BF)

    @pl.when(is_z)
    def _():
        run(True)

    @pl.when(jnp.logical_not(is_z))
    def _():
        run(False)


def _zg(l, h, w_zg):
    m = h.shape[0]
    return pl.pallas_call(
        _zg_body,
        grid_spec=pltpu.PrefetchScalarGridSpec(
            num_scalar_prefetch=1, grid=(NZG // TN_ZG, m // TM_ZG),
            in_specs=[pl.BlockSpec((TM_ZG, D_MODEL), lambda j, i, lr: (i, 0)),
                      pl.BlockSpec((None, TN_ZG, D_MODEL), lambda j, i, lr: (lr[0], j, 0))],
            out_specs=pl.BlockSpec((TM_ZG, TN_ZG), lambda j, i, lr: (i, j))),
        out_shape=jax.ShapeDtypeStruct((m, NZG), BF),
        compiler_params=_cparams(("arbitrary", "arbitrary")),
        name="zg_proj",
    )(l, h, w_zg)


def _softmax_pv(scores, values):
    m = None
    for s in scores:
        sm = jnp.max(s, axis=-1, keepdims=True)
        m = sm if m is None else jnp.maximum(m, sm)
    acc = None
    for s, v in zip(scores, values):
        o = _dot(jnp.exp2((s - m).astype(BF)), v)
        acc = o if acc is None else acc + o
    return acc[:, :LANES] / acc[:, LANES:]


def _with_ones(v):
    return jnp.concatenate([v, jnp.ones_like(v)], axis=1)


def _diff_lambda(lam_ref, lam_init):
    la = lam_ref[...]
    s01 = jnp.sum(la[0:1] * la[1:2], axis=-1, keepdims=True)
    s23 = jnp.sum(la[2:3] * la[3:4], axis=-1, keepdims=True)
    return jnp.exp(s01) - jnp.exp(s23) + lam_init


def _diff_head(q, ks, vs, lam, subln, lam_init):
    lt = _lane_lt64(q.shape)
    zero = jnp.zeros_like(q)
    o = []
    for qm in (jnp.where(lt, q, zero), jnp.where(lt, zero, q)):
        o.append(_softmax_pv([_dot_nt(qm, k) for k in ks], vs))
    d = o[0] - lam * o[1]
    ms = jnp.mean(d * d, axis=-1, keepdims=True)
    return d * lax.rsqrt(ms + EPS) * subln * (1.0 - lam_init)


def _pair_heads(q, ks, vs):
    lt = _lane_lt64(q.shape)
    zero = jnp.zeros_like(q)
    o = [_softmax_pv([_dot_nt(qm, k) for k in ks], vs)
         for qm in (jnp.where(lt, q, zero), jnp.where(lt, zero, q))]
    return jnp.where(lt, o[0], o[1])


def _gqa_natural(tiles):
    t0, t1, t2, t3 = tiles
    lt = _lane_lt64(t0.shape)
    swap = lambda t: pltpu.roll(t, 64, 1)
    return jnp.concatenate([jnp.where(lt, t0, swap(t1)), jnp.where(lt, t2, swap(t3)),
                            jnp.where(lt, swap(t0), t1), jnp.where(lt, swap(t2), t3)], axis=1)


def _ctx_attn_body(l_ref, li_ref, qa_ref, ka_ref, va_ref, qb_ref, kb_ref, vb_ref, qc_ref, kc_ref, vc_ref,
                   qd_ref, kd_ref, vd_ref, lam_ref, subln_ref, oa_ref, ob_ref, oc_ref, od_ref):
    lam_init = li_ref[l_ref[0]]
    lam = _diff_lambda(lam_ref, lam_init)
    subln = subln_ref[...]
    for h in range(H_A):
        sl = slice(h * LANES, (h + 1) * LANES)
        oa_ref[:, sl] = _diff_head(qa_ref[:, sl], [ka_ref[:, sl].astype(BF)],
                                   [_with_ones(va_ref[:, sl].astype(BF))], lam, subln, lam_init).astype(BF)
    for j in range(H_B // 2):
        sl = slice(j * LANES, (j + 1) * LANES)
        ob_ref[:, sl] = _pair_heads(qb_ref[:, sl], [kb_ref[:, sl].astype(BF)],
                                    [_with_ones(vb_ref[:, sl].astype(BF))]).astype(BF)
    for h in range(H_C):
        oc_ref[:, h * LANES:(h + 1) * LANES] = _softmax_pv(
            [_dot_nt(qc_ref[:, 2 * h * LANES:(2 * h + 2) * LANES], kc_ref[:, 2 * h * LANES:(2 * h + 2) * LANES])],
            [_with_ones(vc_ref[:, h * LANES:(h + 1) * LANES])]).astype(BF)
    kd = kd_ref[...].astype(BF)
    vd = _with_ones(vd_ref[...].astype(BF))
    od_ref[...] = _gqa_natural(
        [_pair_heads(qd_ref[:, j * LANES:(j + 1) * LANES], [kd], [vd]) for j in range(H_D // 2)]).astype(BF)


def _ctx_attn(l, lam_tab, pc, lam_a, a_subln):
    row = lambda w: pl.BlockSpec((SEQ, w), lambda b, lr, li: (b, 0))
    ins = [pc["qa"], pc["ka"], pc["va"], pc["qb"], pc["kb"], pc["vb"], pc["qc"], pc["kc"], pc["vc"],
           pc["qd"], pc["kd"], pc["vd"]]
    layer_row = lambda w: pl.BlockSpec((None, None, SEQ, w), lambda b, lr, li: (b, lr[0], 0, 0))
    in_specs = [row(a.shape[1]) if a.ndim == 2 else layer_row(a.shape[3]) for a in ins]
    in_specs += [pl.BlockSpec((None, 4, D_A), lambda b, lr, li: (lr[0], 0, 0)),
                 pl.BlockSpec((None, 1, 2 * D_A), lambda b, lr, li: (lr[0], 0, 0))]
    return pl.pallas_call(
        _ctx_attn_body,
        grid_spec=pltpu.PrefetchScalarGridSpec(
            num_scalar_prefetch=2, grid=(BATCH,), in_specs=in_specs,
            out_specs=[row(BRANCH_W)] * N_BRANCH),
        out_shape=[jax.ShapeDtypeStruct((N_CTX, BRANCH_W), BF)] * N_BRANCH,
        compiler_params=_cparams(("arbitrary",)),
        name="ctx_attn",
    )(l, lam_tab, *ins, lam_a, a_subln)


def _lat_a_body(l_ref, li_ref, q_ref, k_ref, v_ref, lam_ref, subln_ref, o_ref):
    lam_init = li_ref[l_ref[0]]
    lam = _diff_lambda(lam_ref, lam_init)
    subln = subln_ref[...]
    for h in range(H_A):
        sl = slice(h * LANES, (h + 1) * LANES)
        ks = [k_ref[:, sl]]
        vs = [v_ref[:, 2 * h * LANES:(2 * h + 2) * LANES]]
        for r in range(TQ_A // ROW_TILE):
            rows = slice(r * ROW_TILE, (r + 1) * ROW_TILE)
            o_ref[rows, sl] = _diff_head(q_ref[rows, sl], ks, vs, lam, subln, lam_init).astype(BF)


def _lat_specs(tq, q_w, k_w, v_w):
    qo = lambda w: pl.BlockSpec((None, tq, w), lambda b, i, *pre: (b, i, 0))
    kv = lambda w: pl.BlockSpec((None, KV_LEN, w), lambda b, i, *pre: (b, 0, 0), pipeline_mode=pl.Buffered(1))
    return qo, [qo(q_w), kv(k_w), kv(v_w)]


def _lat_a(l, lam_tab, q, k, v, lam_a, a_subln):
    qo, in_specs = _lat_specs(TQ_A, 512, 512, 1024)
    in_specs += [pl.BlockSpec((None, 4, D_A), lambda b, i, lr, li: (lr[0], 0, 0)),
                 pl.BlockSpec((None, 1, 2 * D_A), lambda b, i, lr, li: (lr[0], 0, 0))]
    return pl.pallas_call(
        _lat_a_body,
        grid_spec=pltpu.PrefetchScalarGridSpec(
            num_scalar_prefetch=2, grid=(DEC_BATCH, DEC_SEQ // TQ_A), in_specs=in_specs, out_specs=qo(BRANCH_W)),
        out_shape=jax.ShapeDtypeStruct((DEC_BATCH, DEC_SEQ, BRANCH_W), BF),
        compiler_params=_cparams(("arbitrary",) * 2),
        name="lat_attn_a",
    )(l, lam_tab, q, k, v, lam_a, a_subln)


def _lat_c_body(l_ref, q_ref, k_ref, v_ref, o_ref):
    for h in range(H_C):
        sl = slice(h * LANES, (h + 1) * LANES)
        sl2 = slice(2 * h * LANES, (2 * h + 2) * LANES)
        for r in range(TQ_C // ROW_TILE):
            rows = slice(r * ROW_TILE, (r + 1) * ROW_TILE)
            o_ref[rows, sl] = _softmax_pv([_dot_nt(q_ref[rows, sl2], k_ref[:, sl2])], [v_ref[:, sl2]]).astype(BF)


def _lat_c(l, q, k, v):
    qo, in_specs = _lat_specs(TQ_C, 1024, 1024, 1024)
    return pl.pallas_call(
        _lat_c_body,
        grid_spec=pltpu.PrefetchScalarGridSpec(
            num_scalar_prefetch=1, grid=(DEC_BATCH, DEC_SEQ // TQ_C), in_specs=in_specs, out_specs=qo(BRANCH_W)),
        out_shape=jax.ShapeDtypeStruct((DEC_BATCH, DEC_SEQ, BRANCH_W), BF),
        compiler_params=_cparams(("arbitrary",) * 2),
        name="lat_attn_c",
    )(l, q, k, v)


def _lat_d_body(l_ref, q_ref, k_ref, v_ref, o_ref):
    ks = [k_ref[...]]
    vs = [v_ref[...]]
    for r in range(TQ_D // ROW_TILE):
        rows = slice(r * ROW_TILE, (r + 1) * ROW_TILE)
        o_ref[rows, :] = _gqa_natural(
            [_pair_heads(q_ref[rows, j * LANES:(j + 1) * LANES], ks, vs) for j in range(H_D // 2)]).astype(BF)


def _lat_d(l, q, k, v):
    qo, in_specs = _lat_specs(TQ_D, 512, LANES, 2 * LANES)
    return pl.pallas_call(
        _lat_d_body,
        grid_spec=pltpu.PrefetchScalarGridSpec(
            num_scalar_prefetch=1, grid=(DEC_BATCH, DEC_SEQ // TQ_D), in_specs=in_specs, out_specs=qo(BRANCH_W)),
        out_shape=jax.ShapeDtypeStruct((DEC_BATCH, DEC_SEQ, BRANCH_W), BF),
        compiler_params=_cparams(("arbitrary",) * 2),
        name="lat_attn_d",
    )(l, q, k, v)


def _lat_b_body(l_ref, q_ref, k_ref, v_ref, kc_ref, vc_ref, tab_ref, o_ref):
    i = pl.program_id(1)
    qr0 = i * NB_ROWS
    kr0 = jnp.clip(qr0 - NA_ROWS // 2, 0, ROWS - NB_KROWS)
    start = pl.multiple_of(kr0 * GRID_W, GRID_W)
    n_keys = NB_KROWS * GRID_W
    lt = _lane_lt64((GRID_W, LANES))
    for j in range(H_B // 2):
        sl = slice(j * LANES, (j + 1) * LANES)
        kwin = k_ref[pl.ds(start, n_keys), sl]
        vwin = v_ref[pl.ds(start, n_keys), 2 * j * LANES:(2 * j + 2) * LANES]
        ks = [kwin, kc_ref[:, sl].astype(BF)]
        vs = [vwin, _with_ones(vc_ref[:, sl].astype(BF))]
        q = q_ref[:, sl]
        ltq = _lane_lt64(q.shape)
        zero = jnp.zeros_like(q)
        outs = []
        for half, qm in ((0, jnp.where(ltq, q, zero)), (1, jnp.where(ltq, zero, q))):
            head = 2 * j + half
            rows = []
            for a in range(NB_ROWS):
                qr = qr0 + a
                r0 = jnp.clip(qr - NA_ROWS // 2, 0, ROWS - NA_ROWS)
                tiles = []
                for p in range(NB_KROWS // 2):
                    kr_l = kr0 + 2 * p
                    u = jnp.clip(kr_l - qr + NA_ROWS, 0, 2 * NA_ROWS - 1)
                    pen_l = jnp.where((kr_l >= r0) & (kr_l < r0 + NA_ROWS), 0.0, NEG)
                    pen_r = jnp.where((kr_l + 1 >= r0) & (kr_l + 1 < r0 + NA_ROWS), 0.0, NEG)
                    tiles.append(tab_ref[head, u] + jnp.where(lt, pen_l, pen_r))
                rows.append(jnp.concatenate(tiles, axis=1))
            bias = jnp.concatenate(rows, axis=0)
            sc = [_dot_nt(qm, ks[0]) + bias, _dot_nt(qm, ks[1])]
            outs.append(_softmax_pv(sc, vs))
        o_ref[:, sl] = jnp.where(ltq, outs[0], outs[1]).astype(BF)


def _lat_b(l, q, k, v, cache_k, cache_v, tab):
    nq = NB_ROWS * GRID_W
    kv = pl.BlockSpec((None, DEC_SEQ, 512), lambda b, i, lr: (b, 0, 0))
    vv = pl.BlockSpec((None, DEC_SEQ, 1024), lambda b, i, lr: (b, 0, 0))
    cache = pl.BlockSpec((None, None, PAST_LEN, 512), lambda b, i, lr: (b, lr[0], 0, 0))
    qo = pl.BlockSpec((None, nq, 512), lambda b, i, lr: (b, i, 0))
    return pl.pallas_call(
        _lat_b_body,
        grid_spec=pltpu.PrefetchScalarGridSpec(
            num_scalar_prefetch=1, grid=(DEC_BATCH, DEC_SEQ // nq),
            in_specs=[qo, kv, vv, cache, cache,
                      pl.BlockSpec((None, H_B, 2 * NA_ROWS, GRID_W, LANES), lambda b, i, lr: (lr[0], 0, 0, 0, 0))],
            out_specs=qo),
        out_shape=jax.ShapeDtypeStruct((DEC_BATCH, DEC_SEQ, BRANCH_W), BF),
        compiler_params=_cparams(("arbitrary",) * 2),
        name="lat_attn_b",
    )(l, q, k, v, cache_k, cache_v, tab)


def _merge_body(l_ref, x_ref, mod_ref, oa_ref, ob_ref, oc_ref, od_ref, z_ref, g0_ref, g1_ref, g2_ref, g3_ref,
                wbr_ref, wout_ref, lng_ref, lnb_ref, o_ref):
    merged = None
    for i, (o_r, g_r) in enumerate(zip((oa_ref, ob_ref, oc_ref, od_ref), (g0_ref, g1_ref, g2_ref, g3_ref))):
        u = (o_r[...].astype(F32) * z_ref[:, i * BRANCH_W:(i + 1) * BRANCH_W].astype(F32)).astype(BF)
        term = g_r[...].astype(F32) * _dot(u, wbr_ref[i])
        merged = term if merged is None else merged + term
    y = _dot(merged.astype(BF), wout_ref[...])
    gate = mod_ref[:, 2 * D_MODEL:3 * D_MODEL]
    r = ALPHA * x_ref[...] + gate * y
    mu = jnp.mean(r, axis=-1, keepdims=True)
    d = r - mu
    var = jnp.mean(d * d, axis=-1, keepdims=True)
    o_ref[...] = d * lax.rsqrt(var + EPS) * lng_ref[...] + lnb_ref[...]


def _merge(l, x, mod4, outs, zg, W, *, latent):
    m = x.shape[0]
    tm = TM_MERGE
    per_b = DEC_SEQ // tm
    cond = (lambda i: 1 + i // per_b) if latent else (lambda i: 0)
    row = lambda w: pl.BlockSpec((tm, w), lambda i, lr: (i, 0))
    zgb = lambda j: pl.BlockSpec((tm, D_MODEL), lambda i, lr: (i, j))
    wfull = lambda a: pl.BlockSpec((None,) + a.shape[1:], lambda i, lr: (lr[0],) + (0,) * (a.ndim - 1),
                                   pipeline_mode=pl.Buffered(1))
    in_specs = [row(D_MODEL),
                pl.BlockSpec((None, None, 1, 3 * D_MODEL), lambda i, lr: (lr[0], cond(i), 0, 0)),
                row(BRANCH_W), row(BRANCH_W), row(BRANCH_W), row(BRANCH_W),
                zgb(0), zgb(1), zgb(2), zgb(3), zgb(4),
                wfull(W["w_br"]), wfull(W["w_out"]), wfull(W["ln_g"]), wfull(W["ln_b"])]
    return pl.pallas_call(
        _merge_body,
        grid_spec=pltpu.PrefetchScalarGridSpec(
            num_scalar_prefetch=1, grid=(m // tm,), in_specs=in_specs, out_specs=row(D_MODEL)),
        out_shape=jax.ShapeDtypeStruct((m, D_MODEL), F32),
        input_output_aliases={1: 0},
        compiler_params=_cparams(("arbitrary",)),
        name="merge_lat" if latent else "merge_ctx",
    )(l, x, mod4, *outs, zg, zg, zg, zg, zg, W["w_br"], W["w_out"], W["ln_g"], W["ln_b"])


def _rope_tables():
    t = jnp.arange(DEC_SEQ)
    row = (t // GRID_W).astype(F32)
    col = (t % GRID_W).astype(F32)
    quarter = D_A // 4
    inv_freq = ROPE_BASE ** (-jnp.arange(quarter, dtype=F32) / quarter)
    ar = row[:, None] * inv_freq
    ac = col[:, None] * inv_freq
    ang = jnp.concatenate([ar, ar, ac, ac], axis=-1)
    cos, sin = jnp.cos(ang), jnp.sin(ang)
    even = (jnp.arange(D_A) // quarter) % 2 == 0
    sa = jnp.where(even, -sin, 0.0)
    sb = jnp.where(even, 0.0, sin)
    tile2 = lambda a: jnp.concatenate([a, a], axis=-1)
    return tile2(cos), tile2(sa), tile2(sb)


def _repack_plan():
    offs = [int(v) // 64 for v in np.concatenate([[0], np.cumsum(IN_SIZES)])]
    aq, ak, av, bq, bk, bv, cq, ckv, ckr, dq, dk, dv, z, g = offs[:14]
    gqa = lambda base: [base + h for j in range(H_D // 2) for h in (j, H_D // 2 + j)]
    qkv = list(range(aq, ckr)) + gqa(dq) + [dk, dk + 1, dv, dv + 1, ckr, None]
    zg = list(range(z, offs[14]))

    def runs(chunks):
        out = []
        for d, s in enumerate(chunks):
            if out and s is not None and out[-1][1] is not None and out[-1][1] + out[-1][2] == s:
                out[-1][2] += 1
            else:
                out.append([d, s, 1])
        return out
    assert len(qkv) * 64 == NQ and len(zg) * 64 == NZG
    return runs(qkv), runs(zg)


N_IN = sum(IN_SIZES)


def _repack_body(w_ref, qkv_ref, zg_ref):
    qkv_plan, zg_plan = _repack_plan()
    for plan, o_ref in ((qkv_plan, qkv_ref), (zg_plan, zg_ref)):
        for d, s, n in plan:
            rows = slice(d * 64, (d + n) * 64)
            if s is None:
                o_ref[rows, :] = jnp.zeros((n * 64, TC_REPACK), BF)
            else:
                o_ref[rows, :] = w_ref[s * 64:(s + n) * 64, :].astype(BF)


def _repack_w_in(w_in_t):
    return pl.pallas_call(
        _repack_body,
        grid=(DEPTH, D_MODEL // TC_REPACK),
        in_specs=[pl.BlockSpec((None, N_IN, TC_REPACK), lambda l, i: (l, 0, i))],
        out_specs=[pl.BlockSpec((None, NQ, TC_REPACK), lambda l, i: (l, 0, i)),
                   pl.BlockSpec((None, NZG, TC_REPACK), lambda l, i: (l, 0, i))],
        out_shape=[jax.ShapeDtypeStruct((DEPTH, NQ, D_MODEL), BF),
                   jax.ShapeDtypeStruct((DEPTH, NZG, D_MODEL), BF)],
        compiler_params=_cparams(("arbitrary", "arbitrary")),
        name="repack_w_in",
    )(w_in_t)


def _prep_weights(w_in, c_q_norm, c_kv_norm, w_c_uq, w_c_ukv, d_q_norm, d_k_norm, w_br, w_out, ln_g, ln_b):
    w_qkv, w_zg = _repack_w_in(jnp.swapaxes(w_in, 1, 2))
    uq = w_c_uq.reshape(DEPTH, Q_RANK, H_C, NOPE_DIM + ROPE_DIM)
    wuq = jnp.concatenate([uq, jnp.zeros((DEPTH, Q_RANK, H_C, 2 * LANES - NOPE_DIM - ROPE_DIM), F32)],
                          axis=3).reshape(DEPTH, Q_RANK, H_C * 2 * LANES).astype(BF)
    ukv = w_c_ukv.reshape(DEPTH, KV_RANK, H_C, NOPE_DIM + V_DIM_C)
    wukv_n = ukv[..., :NOPE_DIM].reshape(DEPTH, KV_RANK, H_C * NOPE_DIM).astype(BF)
    wukv_v = ukv[..., NOPE_DIM:].reshape(DEPTH, KV_RANK, H_C * V_DIM_C).astype(BF)
    wbr = w_br.astype(BF)
    return {
        "w_qkv": w_qkv, "w_zg": w_zg, "wuq": wuq, "wukv_n": wukv_n, "wukv_v": wukv_v,
        "cqn": c_q_norm.reshape(DEPTH, 1, Q_RANK), "ckvn": c_kv_norm.reshape(DEPTH, 1, KV_RANK),
        "dqn": jnp.tile(d_q_norm, (1, H_D)).reshape(DEPTH, 1, H_D * D_D),
        "dkn": jnp.tile(d_k_norm, (1, G_D)).reshape(DEPTH, 1, G_D * D_D),
        "w_br": wbr, "w_out": w_out.astype(BF),
        "ln_g": ln_g.reshape(DEPTH, 1, D_MODEL), "ln_b": ln_b.reshape(DEPTH, 1, D_MODEL),
    }


def kernel(x_prompt, x_sample, cache_a_k, cache_a_v, cache_b_k, cache_b_v, cache_c_kv, cache_c_kr, cache_d_k,
           cache_d_v, c, c_ctx, w_ada, b_ada, w_in, lam_a, a_subln, b_rpb, c_q_norm, c_kv_norm, w_c_uq, w_c_ukv,
           d_q_norm, d_k_norm, w_br, w_out, ln_g, ln_b):
    W = _prep_weights(w_in, c_q_norm, c_kv_norm, w_c_uq, w_c_ukv, d_q_norm, d_k_norm, w_br, w_out, ln_g, ln_b)
    cond8 = jnp.concatenate([c_ctx[None], c, jnp.zeros((5, D_MODEL), F32)], axis=0)
    mod4 = _adaln(cond8, w_ada, b_ada).reshape(DEPTH, 8, 1, 3 * D_MODEL)
    nb_tab = _bias_table(b_rpb).reshape(DEPTH, H_B, 2 * NA_ROWS, GRID_W, LANES)
    kc_cache, vc_cache = _cache_mla(cache_c_kv, cache_c_kr, W["wukv_n"], W["wukv_v"])
    rope_tabs = _rope_tables()
    lam_tab = jnp.array([0.8 - 0.6 * math.exp(-0.3 * l) for l in range(DEPTH)], F32)
    subln = a_subln.reshape(DEPTH, 1, 2 * D_A)
    flat = lambda a: a.reshape(DEC_BATCH, DEPTH, PAST_LEN, -1)
    ca_k, ca_v, cb_k, cb_v, cd_k, cd_v = (flat(a) for a in (cache_a_k, cache_a_v, cache_b_k, cache_b_v,
                                                               cache_d_k, cache_d_v))
    lat3 = lambda a: a.reshape(DEC_BATCH, DEC_SEQ, a.shape[-1])

    def layer(carry, li):
        xp, xs, bufs = carry
        l = li.reshape(1)
        (h_c, qa, qb, qc, kc, vc, qd, ka, va, kb, vb, ckv, kr, kd, vd) = _proj(l, xp, mod4, W, bufs, rope=False)
        pc = dict(qa=qa, ka=ka, va=va, qb=qb, kb=kb, vb=vb, qc=qc, kc=kc, vc=vc, qd=qd, kd=kd, vd=vd)
        outs_c = _ctx_attn(l, lam_tab, pc, lam_a, subln)
        zg_c = _zg(l, h_c, W["w_zg"])
        xp_new = _merge(l, xp, mod4, outs_c, zg_c, W, latent=False)
        (h_l, lqa, lka, lva, lqb, lkb, lvb, lqc, lkc, lvc, lqd, lkd, lvd) = _proj(l, xs, mod4, W, rope_tabs,
                                                                                   rope=True)
        lka, lva, lkc, lvc, lkd, lvd = _cache_fill(l, (ca_k, ca_v, kc_cache, vc_cache, cd_k, cd_v),
                                                   (lka, lva, lkc, lvc, lkd, lvd))
        o_a = _lat_a(l, lam_tab, lat3(lqa), lka, lva, lam_a, subln)
        o_b = _lat_b(l, lat3(lqb), lat3(lkb), lat3(lvb), cb_k, cb_v, nb_tab)
        o_c = _lat_c(l, lat3(lqc), lkc, lvc)
        o_d = _lat_d(l, lat3(lqd), lkd, lvd)
        outs_l = [o.reshape(N_LAT, BRANCH_W) for o in (o_a, o_b, o_c, o_d)]
        zg_l = _zg(l, h_l, W["w_zg"])
        xs_new = _merge(l, xs, mod4, outs_l, zg_l, W, latent=True)
        return (xp_new, xs_new, (ka, va, kb, vb, ckv, kr, kd, vd)), None

    bufs0 = tuple(jnp.zeros((BATCH, DEPTH, SEQ, w), F32) for w in CACHE_WIDTHS)
    (xp, xs, caches), _ = lax.scan(
        layer, (x_prompt.reshape(N_CTX, D_MODEL), x_sample.reshape(N_LAT, D_MODEL), bufs0),
        jnp.arange(DEPTH, dtype=jnp.int32))
    ka, va, kb, vb, ckv, kr, kd, vd = caches

    def out(a, tail):
        return a.reshape((BATCH, DEPTH, SEQ) + tail)

    return (xp.reshape(BATCH, SEQ, D_MODEL), xs.reshape(DEC_BATCH, DEC_SEQ, D_MODEL),
            out(ka, (H_A, 2 * D_A)), out(va, (H_A, 2 * D_A)), out(kb, (H_B, D_B)), out(vb, (H_B, D_B)),
            out(ckv, (KV_RANK,)), out(kr, (ROPE_DIM,)), out(kd, (G_D, D_D)), out(vd, (G_D, D_D)))
```

```python
import functools
import math

import jax
import jax.numpy as jnp
import numpy as np
from jax import lax
from jax.experimental import pallas as pl
from jax.experimental.pallas import tpu as pltpu

D_MODEL = 2048
BATCH = 16
SEQ = 256
DEPTH = 4
DEC_BATCH = 2
DEC_SEQ = 4096
PAST_LEN = 256
GRID_W = 64
ROWS = DEC_SEQ // GRID_W
N_BRANCH = 4
BRANCH_W = 512
H_A, D_A = 4, 64
H_B, D_B = 8, 64
NA_ROWS, NA_COLS = 8, 16
H_C, Q_RANK, KV_RANK, NOPE_DIM, ROPE_DIM, V_DIM_C = 4, 512, 256, 128, 64, 128
H_D, G_D, D_D = 8, 2, 64
ROPE_BASE = 10000.0
EPS = 1e-6
ALPHA = (2 * DEPTH) ** 0.25
IN_SIZES = (512, 512, 512, 512, 512, 512, Q_RANK, KV_RANK, ROPE_DIM, 512, 128, 128,
            N_BRANCH * BRANCH_W, N_BRANCH * D_MODEL)

BF = jnp.bfloat16
F32 = jnp.float32
LANES = 128
MXU_N = 256
LOG2E = 1.4426950408889634
VMEM_LIMIT = 56 * 1024 * 1024
NEG = -1e30

N_CTX = BATCH * SEQ
N_LAT = DEC_BATCH * DEC_SEQ
NQ = 4736
NZG = 10240

O_AQ, O_AK, O_AV, O_BQ, O_BK, O_BV = 0, 512, 1024, 1536, 2048, 2560
O_CQ, O_CKV, O_DQ, O_DK, O_DV, O_KR = 3072, 3584, 3840, 4352, 4480, 4608

TM_PROJ = 256
TM_ZG = 1024
TN_ZG = 2048
TM_MERGE = 256
ROW_TILE = 256
TQ_A = 512
TQ_C = 1024
TQ_D = 512
TC_REPACK = 256
NB_ROWS = 4
NB_KROWS = 12


def _cparams(sem):
    return pltpu.CompilerParams(dimension_semantics=sem, vmem_limit_bytes=VMEM_LIMIT)


def _dot(a, b):
    return jnp.dot(a, b, preferred_element_type=F32)


def _dot_nt(a, b):
    return lax.dot_general(a, b, (((1,), (1,)), ((), ())), preferred_element_type=F32)


def _sigmoid(x):
    return 1.0 / (1.0 + jnp.exp(-x))


def _lane_lt64(shape):
    return lax.broadcasted_iota(jnp.int32, shape, len(shape) - 1) < 64


def _adaln_body(c_ref, w_ref, b_ref, o_ref):
    c = c_ref[...]
    s = (c * _sigmoid(c)).astype(BF)
    o_ref[...] = _dot(s, w_ref[...].astype(BF)) + b_ref[...]


def _adaln(cond8, w_ada, b_ada):
    tn = 1536
    return pl.pallas_call(
        _adaln_body,
        grid=(DEPTH, 3 * D_MODEL // tn),
        in_specs=[pl.BlockSpec((8, D_MODEL), lambda l, j: (0, 0)),
                  pl.BlockSpec((None, D_MODEL, tn), lambda l, j: (l, 0, j)),
                  pl.BlockSpec((None, 1, tn), lambda l, j: (l, 0, j))],
        out_specs=pl.BlockSpec((None, 8, tn), lambda l, j: (l, 0, j)),
        out_shape=jax.ShapeDtypeStruct((DEPTH, 8, 3 * D_MODEL), F32),
        compiler_params=_cparams(("arbitrary", "arbitrary")),
        name="adaln",
    )(cond8, w_ada, b_ada.reshape(DEPTH, 1, 3 * D_MODEL))


def _bias_table_body(rpb_ref, o_ref):
    lh = pl.program_id(0)
    qc = lax.broadcasted_iota(jnp.int32, (GRID_W, LANES), 0)
    lane = lax.broadcasted_iota(jnp.int32, (GRID_W, LANES), 1)
    kc = jnp.bitwise_and(lane, 63)
    c0 = jnp.clip(qc - NA_COLS // 2, 0, GRID_W - NA_COLS)
    col_ok = (kc >= c0) & (kc < c0 + NA_COLS)
    dc = jnp.where(col_ok, kc - qc + (NA_COLS - 1), -1)
    right = lane >= 64
    n_dr, n_dc = 2 * NA_ROWS - 1, 2 * NA_COLS - 1
    neg = jnp.full((GRID_W, LANES), NEG, F32)
    rows = []
    for dr in range(n_dr):
        val = neg
        for d in range(n_dc):
            val = jnp.where(dc == d, rpb_ref[(lh * n_dr + dr) * n_dc + d] * LOG2E, val)
        rows.append(val)
    for u in range(n_dr + 1):
        left = rows[u - 1] if u >= 1 else neg
        o_ref[u] = jnp.where(right, rows[u] if u < n_dr else neg, left)


def _bias_table(b_rpb):
    n = DEPTH * H_B
    return pl.pallas_call(
        _bias_table_body,
        grid_spec=pltpu.PrefetchScalarGridSpec(
            num_scalar_prefetch=1, grid=(n,),
            in_specs=[],
            out_specs=pl.BlockSpec((None, 2 * NA_ROWS, GRID_W, LANES), lambda i, r: (i, 0, 0, 0))),
        out_shape=jax.ShapeDtypeStruct((n, 2 * NA_ROWS, GRID_W, LANES), F32),
        compiler_params=_cparams(("arbitrary",)),
        name="nb_bias_table",
    )(b_rpb.reshape(-1))


def _cache_mla_body(ckv_ref, kr_ref, wn_ref, wv_ref, k_ref, v_ref):
    ckv = ckv_ref[...].astype(BF)
    kn = _dot(ckv, wn_ref[...])
    kr = kr_ref[...]
    k_ref[...] = jnp.concatenate(
        [t for h in range(H_C) for t in (kn[:, h * LANES:(h + 1) * LANES], kr)], axis=1).astype(BF)
    v_ref[...] = _dot(ckv, wv_ref[...]).astype(BF)


def _cache_mla(cache_c_kv, cache_c_kr, wukv_n, wukv_v):
    return pl.pallas_call(
        _cache_mla_body,
        grid=(DEPTH, DEC_BATCH),
        in_specs=[pl.BlockSpec((None, None, PAST_LEN, KV_RANK), lambda l, b: (b, l, 0, 0)),
                  pl.BlockSpec((None, None, PAST_LEN, LANES), lambda l, b: (b, l, 0, 0)),
                  pl.BlockSpec((None, KV_RANK, 512), lambda l, b: (l, 0, 0)),
                  pl.BlockSpec((None, KV_RANK, 512), lambda l, b: (l, 0, 0))],
        out_specs=[pl.BlockSpec((None, None, PAST_LEN, 1024), lambda l, b: (l, b, 0, 0)),
                   pl.BlockSpec((None, None, PAST_LEN, 512), lambda l, b: (l, b, 0, 0))],
        out_shape=[jax.ShapeDtypeStruct((DEPTH, DEC_BATCH, PAST_LEN, 1024), BF),
                   jax.ShapeDtypeStruct((DEPTH, DEC_BATCH, PAST_LEN, 512), BF)],
        compiler_params=_cparams(("arbitrary", "arbitrary")),
        name="cache_mla",
    )(cache_c_kv, jnp.pad(cache_c_kr, ((0, 0), (0, 0), (0, 0), (0, LANES - ROPE_DIM))), wukv_n, wukv_v)


def _rope_tiles(x, cos, sa, sb):
    outs = []
    for j in range(x.shape[1] // LANES):
        t = x[:, j * LANES:(j + 1) * LANES]
        outs.append(t * cos + pltpu.roll(t, LANES - 16, 1) * sa + pltpu.roll(t, 16, 1) * sb)
    return outs[0] if len(outs) == 1 else jnp.concatenate(outs, axis=1)


def _group64_rms(x, g):
    w = x.shape[1]
    r = lax.shift_right_logical(lax.broadcasted_iota(jnp.int32, (w, w), 0), 6)
    c = lax.shift_right_logical(lax.broadcasted_iota(jnp.int32, (w, w), 1), 6)
    bd = jnp.where(r == c, 1.0, 0.0).astype(BF)
    x2 = x * x
    hi = x2.astype(BF)
    lo = (x2 - hi.astype(F32)).astype(BF)
    ms = (_dot(hi, bd) + _dot(lo, bd)) * (1.0 / 64)
    return x * lax.rsqrt(ms + EPS) * g


def _interleave_ones(v):
    ones = jnp.ones((v.shape[0], LANES), BF)
    return jnp.concatenate([t for j in range(v.shape[1] // LANES)
                            for t in (v[:, j * LANES:(j + 1) * LANES].astype(BF), ones)], axis=1)


def _full_rms(x, g):
    ms = jnp.mean(x * x, axis=-1, keepdims=True)
    return x * lax.rsqrt(ms + EPS) * g


def _proj_body(l_ref, x_ref, mod_ref, w_ref, wuq_ref, wun_ref, wuv_ref, cqn_ref, ckvn_ref, dqn_ref, dkn_ref,
               *refs, rope):
    if rope:
        cos_ref, sa_ref, sb_ref = refs[:3]
        refs = refs[3:]
        cos, sa, sb = cos_ref[...], sa_ref[...], sb_ref[...]
        rp = lambda t: _rope_tiles(t, cos, sa, sb)
    else:
        rp = lambda t: t
    x = x_ref[...]
    shift = mod_ref[:, 0:D_MODEL]
    scale = mod_ref[:, D_MODEL:2 * D_MODEL]
    h = (x * (1.0 + scale) + shift).astype(BF)

    acc = _dot_nt(h, w_ref[...])

    def col(o, n):
        return acc[:, o:o + n]

    with_ones = _interleave_ones

    qa = rp(col(O_AQ, 512)) * (D_A ** -0.5 * LOG2E)
    ka = rp(col(O_AK, 512))
    va = col(O_AV, 512)
    qb = col(O_BQ, 512) * (D_B ** -0.5 * LOG2E)
    kb = col(O_BK, 512)
    vb = col(O_BV, 512)
    cq = _full_rms(col(O_CQ, Q_RANK), cqn_ref[...]).astype(BF)
    qc_raw = _dot(cq, wuq_ref[...])
    qc_scale = (NOPE_DIM + ROPE_DIM) ** -0.5 * LOG2E
    qc = jnp.concatenate(
        [t for hh in range(H_C) for t in (qc_raw[:, 2 * hh * LANES:(2 * hh + 1) * LANES],
                                          rp(qc_raw[:, (2 * hh + 1) * LANES:(2 * hh + 2) * LANES]))],
        axis=1) * qc_scale
    ckv = _full_rms(col(O_CKV, KV_RANK), ckvn_ref[...])
    ckv_b = ckv.astype(BF)
    kn = _dot(ckv_b, wun_ref[...])
    vc = _dot(ckv_b, wuv_ref[...])
    kr_raw = col(O_KR, LANES)
    kr = rp(kr_raw)
    kc = jnp.concatenate([t for hh in range(H_C) for t in (kn[:, hh * LANES:(hh + 1) * LANES], kr)], axis=1)
    qd = rp(_group64_rms(col(O_DQ, 512), dqn_ref[...])) * (D_D ** -0.5 * LOG2E)
    kd_n = _group64_rms(col(O_DK, LANES), dkn_ref[...])
    kd = rp(kd_n)
    vd = col(O_DV, LANES)

    if rope:
        (h_o, qa_o, ka_o, va_o, qb_o, kb_o, vb_o, qc_o, kc_o, vc_o, qd_o, kd_o, vd_o) = refs
        ka_o[...] = ka.astype(BF)
        va_o[...] = with_ones(va)
        kb_o[...] = kb.astype(BF)
        vb_o[...] = with_ones(vb)
        kd_o[...] = kd.astype(BF)
        vd_o[...] = with_ones(vd)
        vc_o[...] = with_ones(vc)
    else:
        (h_o, qa_o, qb_o, qc_o, kc_o, vc_o, qd_o,
         ka_o, va_o, kb_o, vb_o, ckv_o, kr_o, kd_o, vd_o) = refs[len(CACHE_WIDTHS):]
        ka_o[...] = ka
        va_o[...] = va
        kb_o[...] = kb
        vb_o[...] = vb
        ckv_o[...] = ckv
        kr_o[...] = kr_raw[:, 0:ROPE_DIM]
        kd_o[...] = kd_n
        vd_o[...] = vd
        vc_o[...] = vc.astype(BF)
    h_o[...] = h
    qa_o[...] = qa.astype(BF)
    qb_o[...] = qb.astype(BF)
    qc_o[...] = qc.astype(BF)
    kc_o[...] = kc.astype(BF)
    qd_o[...] = qd.astype(BF)


CACHE_WIDTHS = (512, 512, 512, 512, KV_RANK, ROPE_DIM, LANES, LANES)
KV_LEN = DEC_SEQ + PAST_LEN
LAT_KV_OUTS = (2, 3, 8, 9, 11, 12)


def _proj(l, x, mod4, W, extra, *, rope):
    m = x.shape[0]
    tm = TM_PROJ
    per_b = DEC_SEQ // tm
    cond = (lambda i: 1 + i // per_b) if rope else (lambda i: 0)
    row = lambda w: pl.BlockSpec((tm, w), lambda i, lr: (i, 0))
    wfull = lambda a: pl.BlockSpec((None,) + a.shape[1:], lambda i, lr: (lr[0],) + (0,) * (a.ndim - 1),
                                   pipeline_mode=pl.Buffered(1))
    weights = [W["w_qkv"], W["wuq"], W["wukv_n"], W["wukv_v"], W["cqn"], W["ckvn"], W["dqn"], W["dkn"]]
    in_specs = [row(D_MODEL),
                pl.BlockSpec((None, None, 1, 3 * D_MODEL), lambda i, lr: (lr[0], cond(i), 0, 0))]
    in_specs += [wfull(a) for a in weights]
    args = [x, mod4] + weights
    if rope:
        in_specs += [pl.BlockSpec((tm, LANES), lambda i, lr: (i % per_b, 0))] * 3
        widths = [(D_MODEL, BF), (512, BF), (512, BF), (1024, BF), (512, BF), (512, BF), (1024, BF),
                  (1024, BF), (1024, BF), (1024, BF), (512, BF), (LANES, BF), (2 * LANES, BF)]
        aliases = {}
    else:
        assert tm == SEQ
        in_specs += [pl.BlockSpec(memory_space=pl.ANY)] * len(CACHE_WIDTHS)
        widths = [(D_MODEL, BF), (512, BF), (512, BF), (1024, BF), (1024, BF), (512, BF), (512, BF)]
        aliases = {1 + len(args) + k: len(widths) + k for k in range(len(CACHE_WIDTHS))}
    args += list(extra)
    out_specs = [row(w) for w, _ in widths]
    out_shape = [jax.ShapeDtypeStruct((m, w), d) for w, d in widths]
    if rope:
        for k in LAT_KV_OUTS:
            w, d = widths[k]
            out_specs[k] = pl.BlockSpec((None, tm, w), lambda i, lr: (i // per_b, i % per_b, 0))
            out_shape[k] = jax.ShapeDtypeStruct((DEC_BATCH, KV_LEN, w), d)
    if not rope:
        out_specs += [pl.BlockSpec((None, None, SEQ, w), lambda i, lr: (i, lr[0], 0, 0)) for w in CACHE_WIDTHS]
        out_shape += [jax.ShapeDtypeStruct((BATCH, DEPTH, SEQ, w), F32) for w in CACHE_WIDTHS]
    return pl.pallas_call(
        functools.partial(_proj_body, rope=rope),
        grid_spec=pltpu.PrefetchScalarGridSpec(
            num_scalar_prefetch=1, grid=(m // tm,), in_specs=in_specs, out_specs=out_specs),
        out_shape=out_shape,
        input_output_aliases=aliases,
        compiler_params=_cparams(("arbitrary",)),
        name="proj_lat" if rope else "proj_ctx",
    )(l, *args)


def _cache_fill_body(l_ref, cak_ref, cav_ref, kcc_ref, vcc_ref, cdk_ref, cdv_ref, *refs):
    ka_o, va_o, kc_o, vc_o, kd_o, vd_o = refs[6:]
    ka_o[...] = cak_ref[...].astype(BF)
    va_o[...] = _interleave_ones(cav_ref[...])
    kc_o[...] = kcc_ref[...]
    vc_o[...] = _interleave_ones(vcc_ref[...])
    kd_o[...] = cdk_ref[...].astype(BF)
    vd_o[...] = _interleave_ones(cdv_ref[...])


def _cache_fill(l, caches, bufs):
    ca_k, ca_v, kc_cache, vc_cache, cd_k, cd_v = caches
    by_batch = lambda a: pl.BlockSpec((None, None, PAST_LEN, a.shape[3]), lambda b, lr: (b, lr[0], 0, 0))
    by_layer = lambda a: pl.BlockSpec((None, None, PAST_LEN, a.shape[3]), lambda b, lr: (lr[0], b, 0, 0))
    in_specs = [by_batch(ca_k), by_batch(ca_v), by_layer(kc_cache), by_layer(vc_cache), by_batch(cd_k),
                by_batch(cd_v)] + [pl.BlockSpec(memory_space=pl.ANY)] * len(bufs)
    tail = DEC_SEQ // PAST_LEN
    return pl.pallas_call(
        _cache_fill_body,
        grid_spec=pltpu.PrefetchScalarGridSpec(
            num_scalar_prefetch=1, grid=(DEC_BATCH,), in_specs=in_specs,
            out_specs=[pl.BlockSpec((None, PAST_LEN, a.shape[2]), lambda b, lr: (b, tail, 0)) for a in bufs]),
        out_shape=[jax.ShapeDtypeStruct(a.shape, a.dtype) for a in bufs],
        input_output_aliases={1 + len(caches) + k: k for k in range(len(bufs))},
        compiler_params=_cparams(("arbitrary",)),
        name="cache_fill",
    )(l, *caches, *bufs)


def _zg_body(l_ref, h_ref, w_ref, o_ref):
    is_z = pl.program_id(0) < (N_BRANCH * BRANCH_W) // TN_ZG

    def run(silu):
        h = h_ref[...]
        for n in range(TN_ZG // MXU_N):
            sl = slice(n * MXU_N, (n + 1) * MXU_N)
            a = _dot_nt(h, w_ref[sl, :])
            s = _sigmoid(a)
            o_ref[:, sl] = ((a * s) if silu else s).astype(BF)

    @pl.when(is_z)
    def _():
        run(True)

    @pl.when(jnp.logical_not(is_z))
    def _():
        run(False)


def _zg(l, h, w_zg):
    m = h.shape[0]
    return pl.pallas_call(
        _zg_body,
        grid_spec=pltpu.PrefetchScalarGridSpec(
            num_scalar_prefetch=1, grid=(NZG // TN_ZG, m // TM_ZG),
            in_specs=[pl.BlockSpec((TM_ZG, D_MODEL), lambda j, i, lr: (i, 0)),
                      pl.BlockSpec((None, TN_ZG, D_MODEL), lambda j, i, lr: (lr[0], j, 0))],
            out_specs=pl.BlockSpec((TM_ZG, TN_ZG), lambda j, i, lr: (i, j))),
        out_shape=jax.ShapeDtypeStruct((m, NZG), BF),
        compiler_params=_cparams(("arbitrary", "arbitrary")),
        name="zg_proj",
    )(l, h, w_zg)


def _softmax_pv(scores, values):
    m = None
    for s in scores:
        sm = jnp.max(s, axis=-1, keepdims=True)
        m = sm if m is None else jnp.maximum(m, sm)
    acc = None
    for s, v in zip(scores, values):
        o = _dot(jnp.exp2((s - m).astype(BF)), v)
        acc = o if acc is None else acc + o
    return acc[:, :LANES] / acc[:, LANES:]


def _with_ones(v):
    return jnp.concatenate([v, jnp.ones_like(v)], axis=1)


def _diff_lambda(lam_ref, lam_init):
    la = lam_ref[...]
    s01 = jnp.sum(la[0:1] * la[1:2], axis=-1, keepdims=True)
    s23 = jnp.sum(la[2:3] * la[3:4], axis=-1, keepdims=True)
    return jnp.exp(s01) - jnp.exp(s23) + lam_init


def _diff_head(q, ks, vs, lam, subln, lam_init):
    lt = _lane_lt64(q.shape)
    zero = jnp.zeros_like(q)
    o = []
    for qm in (jnp.where(lt, q, zero), jnp.where(lt, zero, q)):
        o.append(_softmax_pv([_dot_nt(qm, k) for k in ks], vs))
    d = o[0] - lam * o[1]
    ms = jnp.mean(d * d, axis=-1, keepdims=True)
    return d * lax.rsqrt(ms + EPS) * subln * (1.0 - lam_init)


def _pair_heads(q, ks, vs):
    lt = _lane_lt64(q.shape)
    zero = jnp.zeros_like(q)
    o = [_softmax_pv([_dot_nt(qm, k) for k in ks], vs)
         for qm in (jnp.where(lt, q, zero), jnp.where(lt, zero, q))]
    return jnp.where(lt, o[0], o[1])


def _gqa_natural(tiles):
    t0, t1, t2, t3 = tiles
    lt = _lane_lt64(t0.shape)
    swap = lambda t: pltpu.roll(t, 64, 1)
    return jnp.concatenate([jnp.where(lt, t0, swap(t1)), jnp.where(lt, t2, swap(t3)),
                            jnp.where(lt, swap(t0), t1), jnp.where(lt, swap(t2), t3)], axis=1)


def _ctx_attn_body(l_ref, li_ref, qa_ref, ka_ref, va_ref, qb_ref, kb_ref, vb_ref, qc_ref, kc_ref, vc_ref,
                   qd_ref, kd_ref, vd_ref, lam_ref, subln_ref, oa_ref, ob_ref, oc_ref, od_ref):
    lam_init = li_ref[l_ref[0]]
    lam = _diff_lambda(lam_ref, lam_init)
    subln = subln_ref[...]
    for h in range(H_A):
        sl = slice(h * LANES, (h + 1) * LANES)
        oa_ref[:, sl] = _diff_head(qa_ref[:, sl], [ka_ref[:, sl].astype(BF)],
                                   [_with_ones(va_ref[:, sl].astype(BF))], lam, subln, lam_init).astype(BF)
    for j in range(H_B // 2):
        sl = slice(j * LANES, (j + 1) * LANES)
        ob_ref[:, sl] = _pair_heads(qb_ref[:, sl], [kb_ref[:, sl].astype(BF)],
                                    [_with_ones(vb_ref[:, sl].astype(BF))]).astype(BF)
    for h in range(H_C):
        oc_ref[:, h * LANES:(h + 1) * LANES] = _softmax_pv(
            [_dot_nt(qc_ref[:, 2 * h * LANES:(2 * h + 2) * LANES], kc_ref[:, 2 * h * LANES:(2 * h + 2) * LANES])],
            [_with_ones(vc_ref[:, h * LANES:(h + 1) * LANES])]).astype(BF)
    kd = kd_ref[...].astype(BF)
    vd = _with_ones(vd_ref[...].astype(BF))
    od_ref[...] = _gqa_natural(
        [_pair_heads(qd_ref[:, j * LANES:(j + 1) * LANES], [kd], [vd]) for j in range(H_D // 2)]).astype(BF)


def _ctx_attn(l, lam_tab, pc, lam_a, a_subln):
    row = lambda w: pl.BlockSpec((SEQ, w), lambda b, lr, li: (b, 0))
    ins = [pc["qa"], pc["ka"], pc["va"], pc["qb"], pc["kb"], pc["vb"], pc["qc"], pc["kc"], pc["vc"],
           pc["qd"], pc["kd"], pc["vd"]]
    layer_row = lambda w: pl.BlockSpec((None, None, SEQ, w), lambda b, lr, li: (b, lr[0], 0, 0))
    in_specs = [row(a.shape[1]) if a.ndim == 2 else layer_row(a.shape[3]) for a in ins]
    in_specs += [pl.BlockSpec((None, 4, D_A), lambda b, lr, li: (lr[0], 0, 0)),
                 pl.BlockSpec((None, 1, 2 * D_A), lambda b, lr, li: (lr[0], 0, 0))]
    return pl.pallas_call(
        _ctx_attn_body,
        grid_spec=pltpu.PrefetchScalarGridSpec(
            num_scalar_prefetch=2, grid=(BATCH,), in_specs=in_specs,
            out_specs=[row(BRANCH_W)] * N_BRANCH),
        out_shape=[jax.ShapeDtypeStruct((N_CTX, BRANCH_W), BF)] * N_BRANCH,
        compiler_params=_cparams(("arbitrary",)),
        name="ctx_attn",
    )(l, lam_tab, *ins, lam_a, a_subln)


def _lat_a_body(l_ref, li_ref, q_ref, k_ref, v_ref, lam_ref, subln_ref, o_ref):
    lam_init = li_ref[l_ref[0]]
    lam = _diff_lambda(lam_ref, lam_init)
    subln = subln_ref[...]
    for h in range(H_A):
        sl = slice(h * LANES, (h + 1) * LANES)
        ks = [k_ref[:, sl]]
        vs = [v_ref[:, 2 * h * LANES:(2 * h + 2) * LANES]]
        for r in range(TQ_A // ROW_TILE):
            rows = slice(r * ROW_TILE, (r + 1) * ROW_TILE)
            o_ref[rows, sl] = _diff_head(q_ref[rows, sl], ks, vs, lam, subln, lam_init).astype(BF)


def _lat_specs(tq, q_w, k_w, v_w):
    qo = lambda w: pl.BlockSpec((None, tq, w), lambda b, i, *pre: (b, i, 0))
    kv = lambda w: pl.BlockSpec((None, KV_LEN, w), lambda b, i, *pre: (b, 0, 0), pipeline_mode=pl.Buffered(1))
    return qo, [qo(q_w), kv(k_w), kv(v_w)]


def _lat_a(l, lam_tab, q, k, v, lam_a, a_subln):
    qo, in_specs = _lat_specs(TQ_A, 512, 512, 1024)
    in_specs += [pl.BlockSpec((None, 4, D_A), lambda b, i, lr, li: (lr[0], 0, 0)),
                 pl.BlockSpec((None, 1, 2 * D_A), lambda b, i, lr, li: (lr[0], 0, 0))]
    return pl.pallas_call(
        _lat_a_body,
        grid_spec=pltpu.PrefetchScalarGridSpec(
            num_scalar_prefetch=2, grid=(DEC_BATCH, DEC_SEQ // TQ_A), in_specs=in_specs, out_specs=qo(BRANCH_W)),
        out_shape=jax.ShapeDtypeStruct((DEC_BATCH, DEC_SEQ, BRANCH_W), BF),
        compiler_params=_cparams(("arbitrary",) * 2),
        name="lat_attn_a",
    )(l, lam_tab, q, k, v, lam_a, a_subln)


def _lat_c_body(l_ref, q_ref, k_ref, v_ref, o_ref):
    for h in range(H_C):
        sl = slice(h * LANES, (h + 1) * LANES)
        sl2 = slice(2 * h * LANES, (2 * h + 2) * LANES)
        for r in range(TQ_C // ROW_TILE):
            rows = slice(r * ROW_TILE, (r + 1) * ROW_TILE)
            o_ref[rows, sl] = _softmax_pv([_dot_nt(q_ref[rows, sl2], k_ref[:, sl2])], [v_ref[:, sl2]]).astype(BF)


def _lat_c(l, q, k, v):
    qo, in_specs = _lat_specs(TQ_C, 1024, 1024, 1024)
    return pl.pallas_call(
        _lat_c_body,
        grid_spec=pltpu.PrefetchScalarGridSpec(
            num_scalar_prefetch=1, grid=(DEC_BATCH, DEC_SEQ // TQ_C), in_specs=in_specs, out_specs=qo(BRANCH_W)),
        out_shape=jax.ShapeDtypeStruct((DEC_BATCH, DEC_SEQ, BRANCH_W), BF),
        compiler_params=_cparams(("arbitrary",) * 2),
        name="lat_attn_c",
    )(l, q, k, v)


def _lat_d_body(l_ref, q_ref, k_ref, v_ref, o_ref):
    ks = [k_ref[...]]
    vs = [v_ref[...]]
    for r in range(TQ_D // ROW_TILE):
        rows = slice(r * ROW_TILE, (r + 1) * ROW_TILE)
        o_ref[rows, :] = _gqa_natural(
            [_pair_heads(q_ref[rows, j * LANES:(j + 1) * LANES], ks, vs) for j in range(H_D // 2)]).astype(BF)


def _lat_d(l, q, k, v):
    qo, in_specs = _lat_specs(TQ_D, 512, LANES, 2 * LANES)
    return pl.pallas_call(
        _lat_d_body,
        grid_spec=pltpu.PrefetchScalarGridSpec(
            num_scalar_prefetch=1, grid=(DEC_BATCH, DEC_SEQ // TQ_D), in_specs=in_specs, out_specs=qo(BRANCH_W)),
        out_shape=jax.ShapeDtypeStruct((DEC_BATCH, DEC_SEQ, BRANCH_W), BF),
        compiler_params=_cparams(("arbitrary",) * 2),
        name="lat_attn_d",
    )(l, q, k, v)


def _lat_b_body(l_ref, q_ref, k_ref, v_ref, kc_ref, vc_ref, tab_ref, o_ref):
    i = pl.program_id(1)
    qr0 = i * NB_ROWS
    kr0 = jnp.clip(qr0 - NA_ROWS // 2, 0, ROWS - NB_KROWS)
    start = pl.multiple_of(kr0 * GRID_W, GRID_W)
    n_keys = NB_KROWS * GRID_W
    lt = _lane_lt64((GRID_W, LANES))
    for j in range(H_B // 2):
        sl = slice(j * LANES, (j + 1) * LANES)
        kwin = k_ref[pl.ds(start, n_keys), sl]
        vwin = v_ref[pl.ds(start, n_keys), 2 * j * LANES:(2 * j + 2) * LANES]
        kcat = jnp.concatenate([kwin, kc_ref[:, sl].astype(BF)], axis=0)
        vcat = jnp.concatenate([vwin, _with_ones(vc_ref[:, sl].astype(BF))], axis=0)
        q = q_ref[:, sl]
        ltq = _lane_lt64(q.shape)
        zero = jnp.zeros_like(q)
        outs = []
        for half, qm in ((0, jnp.where(ltq, q, zero)), (1, jnp.where(ltq, zero, q))):
            head = 2 * j + half
            rows = []
            for a in range(NB_ROWS):
                qr = qr0 + a
                r0 = jnp.clip(qr - NA_ROWS // 2, 0, ROWS - NA_ROWS)
                tiles = []
                for p in range(NB_KROWS // 2):
                    kr_l = kr0 + 2 * p
                    u = jnp.clip(kr_l - qr + NA_ROWS, 0, 2 * NA_ROWS - 1)
                    pen_l = jnp.where((kr_l >= r0) & (kr_l < r0 + NA_ROWS), 0.0, NEG)
                    pen_r = jnp.where((kr_l + 1 >= r0) & (kr_l + 1 < r0 + NA_ROWS), 0.0, NEG)
                    tiles.append(tab_ref[head, u] + jnp.where(lt, pen_l, pen_r))
                rows.append(jnp.concatenate(tiles, axis=1))
            bias = jnp.concatenate(rows, axis=0)
            s = _dot_nt(qm, kcat)
            s = jnp.concatenate([s[:, :n_keys] + bias, s[:, n_keys:]], axis=1)
            outs.append(_softmax_pv([s], [vcat]))
        o_ref[:, sl] = jnp.where(ltq, outs[0], outs[1]).astype(BF)


def _lat_b(l, q, k, v, cache_k, cache_v, tab):
    nq = NB_ROWS * GRID_W
    kv = pl.BlockSpec((None, DEC_SEQ, 512), lambda b, i, lr: (b, 0, 0))
    vv = pl.BlockSpec((None, DEC_SEQ, 1024), lambda b, i, lr: (b, 0, 0))
    cache = pl.BlockSpec((None, None, PAST_LEN, 512), lambda b, i, lr: (b, lr[0], 0, 0))
    qo = pl.BlockSpec((None, nq, 512), lambda b, i, lr: (b, i, 0))
    return pl.pallas_call(
        _lat_b_body,
        grid_spec=pltpu.PrefetchScalarGridSpec(
            num_scalar_prefetch=1, grid=(DEC_BATCH, DEC_SEQ // nq),
            in_specs=[qo, kv, vv, cache, cache,
                      pl.BlockSpec((None, H_B, 2 * NA_ROWS, GRID_W, LANES), lambda b, i, lr: (lr[0], 0, 0, 0, 0))],
            out_specs=qo),
        out_shape=jax.ShapeDtypeStruct((DEC_BATCH, DEC_SEQ, BRANCH_W), BF),
        compiler_params=_cparams(("arbitrary",) * 2),
        name="lat_attn_b",
    )(l, q, k, v, cache_k, cache_v, tab)


def _merge_body(l_ref, x_ref, mod_ref, oa_ref, ob_ref, oc_ref, od_ref, z_ref, g0_ref, g1_ref, g2_ref, g3_ref,
                wbr_ref, wout_ref, lng_ref, lnb_ref, o_ref):
    merged = None
    for i, (o_r, g_r) in enumerate(zip((oa_ref, ob_ref, oc_ref, od_ref), (g0_ref, g1_ref, g2_ref, g3_ref))):
        u = (o_r[...].astype(F32) * z_ref[:, i * BRANCH_W:(i + 1) * BRANCH_W].astype(F32)).astype(BF)
        term = g_r[...].astype(F32) * _dot(u, wbr_ref[i])
        merged = term if merged is None else merged + term
    y = _dot(merged.astype(BF), wout_ref[...])
    gate = mod_ref[:, 2 * D_MODEL:3 * D_MODEL]
    r = ALPHA * x_ref[...] + gate * y
    mu = jnp.mean(r, axis=-1, keepdims=True)
    d = r - mu
    var = jnp.mean(d * d, axis=-1, keepdims=True)
    o_ref[...] = d * lax.rsqrt(var + EPS) * lng_ref[...] + lnb_ref[...]


def _merge(l, x, mod4, outs, zg, W, *, latent):
    m = x.shape[0]
    tm = TM_MERGE
    per_b = DEC_SEQ // tm
    cond = (lambda i: 1 + i // per_b) if latent else (lambda i: 0)
    row = lambda w: pl.BlockSpec((tm, w), lambda i, lr: (i, 0))
    zgb = lambda j: pl.BlockSpec((tm, D_MODEL), lambda i, lr: (i, j))
    wfull = lambda a: pl.BlockSpec((None,) + a.shape[1:], lambda i, lr: (lr[0],) + (0,) * (a.ndim - 1),
                                   pipeline_mode=pl.Buffered(1))
    in_specs = [row(D_MODEL),
                pl.BlockSpec((None, None, 1, 3 * D_MODEL), lambda i, lr: (lr[0], cond(i), 0, 0)),
                row(BRANCH_W), row(BRANCH_W), row(BRANCH_W), row(BRANCH_W),
                zgb(0), zgb(1), zgb(2), zgb(3), zgb(4),
                wfull(W["w_br"]), wfull(W["w_out"]), wfull(W["ln_g"]), wfull(W["ln_b"])]
    return pl.pallas_call(
        _merge_body,
        grid_spec=pltpu.PrefetchScalarGridSpec(
            num_scalar_prefetch=1, grid=(m // tm,), in_specs=in_specs, out_specs=row(D_MODEL)),
        out_shape=jax.ShapeDtypeStruct((m, D_MODEL), F32),
        input_output_aliases={1: 0},
        compiler_params=_cparams(("arbitrary",)),
        name="merge_lat" if latent else "merge_ctx",
    )(l, x, mod4, *outs, zg, zg, zg, zg, zg, W["w_br"], W["w_out"], W["ln_g"], W["ln_b"])


def _rope_tables():
    t = jnp.arange(DEC_SEQ)
    row = (t // GRID_W).astype(F32)
    col = (t % GRID_W).astype(F32)
    quarter = D_A // 4
    inv_freq = ROPE_BASE ** (-jnp.arange(quarter, dtype=F32) / quarter)
    ar = row[:, None] * inv_freq
    ac = col[:, None] * inv_freq
    ang = jnp.concatenate([ar, ar, ac, ac], axis=-1)
    cos, sin = jnp.cos(ang), jnp.sin(ang)
    even = (jnp.arange(D_A) // quarter) % 2 == 0
    sa = jnp.where(even, -sin, 0.0)
    sb = jnp.where(even, 0.0, sin)
    tile2 = lambda a: jnp.concatenate([a, a], axis=-1)
    return tile2(cos), tile2(sa), tile2(sb)


def _repack_plan():
    offs = [int(v) // 64 for v in np.concatenate([[0], np.cumsum(IN_SIZES)])]
    aq, ak, av, bq, bk, bv, cq, ckv, ckr, dq, dk, dv, z, g = offs[:14]
    gqa = lambda base: [base + h for j in range(H_D // 2) for h in (j, H_D // 2 + j)]
    qkv = list(range(aq, ckr)) + gqa(dq) + [dk, dk + 1, dv, dv + 1, ckr, None]
    zg = list(range(z, offs[14]))

    def runs(chunks):
        out = []
        for d, s in enumerate(chunks):
            if out and s is not None and out[-1][1] is not None and out[-1][1] + out[-1][2] == s:
                out[-1][2] += 1
            else:
                out.append([d, s, 1])
        return out
    assert len(qkv) * 64 == NQ and len(zg) * 64 == NZG
    return runs(qkv), runs(zg)


N_IN = sum(IN_SIZES)


def _repack_body(w_ref, qkv_ref, zg_ref):
    qkv_plan, zg_plan = _repack_plan()
    for plan, o_ref in ((qkv_plan, qkv_ref), (zg_plan, zg_ref)):
        for d, s, n in plan:
            rows = slice(d * 64, (d + n) * 64)
            if s is None:
                o_ref[rows, :] = jnp.zeros((n * 64, TC_REPACK), BF)
            else:
                o_ref[rows, :] = w_ref[s * 64:(s + n) * 64, :].astype(BF)


def _repack_w_in(w_in_t):
    return pl.pallas_call(
        _repack_body,
        grid=(DEPTH, D_MODEL // TC_REPACK),
        in_specs=[pl.BlockSpec((None, N_IN, TC_REPACK), lambda l, i: (l, 0, i))],
        out_specs=[pl.BlockSpec((None, NQ, TC_REPACK), lambda l, i: (l, 0, i)),
                   pl.BlockSpec((None, NZG, TC_REPACK), lambda l, i: (l, 0, i))],
        out_shape=[jax.ShapeDtypeStruct((DEPTH, NQ, D_MODEL), BF),
                   jax.ShapeDtypeStruct((DEPTH, NZG, D_MODEL), BF)],
        compiler_params=_cparams(("arbitrary", "arbitrary")),
        name="repack_w_in",
    )(w_in_t)


def _prep_weights(w_in, c_q_norm, c_kv_norm, w_c_uq, w_c_ukv, d_q_norm, d_k_norm, w_br, w_out, ln_g, ln_b):
    w_qkv, w_zg = _repack_w_in(jnp.swapaxes(w_in, 1, 2))
    uq = w_c_uq.reshape(DEPTH, Q_RANK, H_C, NOPE_DIM + ROPE_DIM)
    wuq = jnp.concatenate([uq, jnp.zeros((DEPTH, Q_RANK, H_C, 2 * LANES - NOPE_DIM - ROPE_DIM), F32)],
                          axis=3).reshape(DEPTH, Q_RANK, H_C * 2 * LANES).astype(BF)
    ukv = w_c_ukv.reshape(DEPTH, KV_RANK, H_C, NOPE_DIM + V_DIM_C)
    wukv_n = ukv[..., :NOPE_DIM].reshape(DEPTH, KV_RANK, H_C * NOPE_DIM).astype(BF)
    wukv_v = ukv[..., NOPE_DIM:].reshape(DEPTH, KV_RANK, H_C * V_DIM_C).astype(BF)
    wbr = w_br.astype(BF)
    return {
        "w_qkv": w_qkv, "w_zg": w_zg, "wuq": wuq, "wukv_n": wukv_n, "wukv_v": wukv_v,
        "cqn": c_q_norm.reshape(DEPTH, 1, Q_RANK), "ckvn": c_kv_norm.reshape(DEPTH, 1, KV_RANK),
        "dqn": jnp.tile(d_q_norm, (1, H_D)).reshape(DEPTH, 1, H_D * D_D),
        "dkn": jnp.tile(d_k_norm, (1, G_D)).reshape(DEPTH, 1, G_D * D_D),
        "w_br": wbr, "w_out": w_out.astype(BF),
        "ln_g": ln_g.reshape(DEPTH, 1, D_MODEL), "ln_b": ln_b.reshape(DEPTH, 1, D_MODEL),
    }


def kernel(x_prompt, x_sample, cache_a_k, cache_a_v, cache_b_k, cache_b_v, cache_c_kv, cache_c_kr, cache_d_k,
           cache_d_v, c, c_ctx, w_ada, b_ada, w_in, lam_a, a_subln, b_rpb, c_q_norm, c_kv_norm, w_c_uq, w_c_ukv,
           d_q_norm, d_k_norm, w_br, w_out, ln_g, ln_b):
    W = _prep_weights(w_in, c_q_norm, c_kv_norm, w_c_uq, w_c_ukv, d_q_norm, d_k_norm, w_br, w_out, ln_g, ln_b)
    cond8 = jnp.concatenate([c_ctx[None], c, jnp.zeros((5, D_MODEL), F32)], axis=0)
    mod4 = _adaln(cond8, w_ada, b_ada).reshape(DEPTH, 8, 1, 3 * D_MODEL)
    nb_tab = _bias_table(b_rpb).reshape(DEPTH, H_B, 2 * NA_ROWS, GRID_W, LANES)
    kc_cache, vc_cache = _cache_mla(cache_c_kv, cache_c_kr, W["wukv_n"], W["wukv_v"])
    rope_tabs = _rope_tables()
    lam_tab = jnp.array([0.8 - 0.6 * math.exp(-0.3 * l) for l in range(DEPTH)], F32)
    subln = a_subln.reshape(DEPTH, 1, 2 * D_A)
    flat = lambda a: a.reshape(DEC_BATCH, DEPTH, PAST_LEN, -1)
    ca_k, ca_v, cb_k, cb_v, cd_k, cd_v = (flat(a) for a in (cache_a_k, cache_a_v, cache_b_k, cache_b_v,
                                                               cache_d_k, cache_d_v))
    lat3 = lambda a: a.reshape(DEC_BATCH, DEC_SEQ, a.shape[-1])

    def layer(carry, li):
        xp, xs, bufs = carry
        l = li.reshape(1)
        (h_c, qa, qb, qc, kc, vc, qd, ka, va, kb, vb, ckv, kr, kd, vd) = _proj(l, xp, mod4, W, bufs, rope=False)
        pc = dict(qa=qa, ka=ka, va=va, qb=qb, kb=kb, vb=vb, qc=qc, kc=kc, vc=vc, qd=qd, kd=kd, vd=vd)
        outs_c = _ctx_attn(l, lam_tab, pc, lam_a, subln)
        zg_c = _zg(l, h_c, W["w_zg"])
        xp_new = _merge(l, xp, mod4, outs_c, zg_c, W, latent=False)
        (h_l, lqa, lka, lva, lqb, lkb, lvb, lqc, lkc, lvc, lqd, lkd, lvd) = _proj(l, xs, mod4, W, rope_tabs,
                                                                                   rope=True)
        lka, lva, lkc, lvc, lkd, lvd = _cache_fill(l, (ca_k, ca_v, kc_cache, vc_cache, cd_k, cd_v),
                                                   (lka, lva, lkc, lvc, lkd, lvd))
        o_a = _lat_a(l, lam_tab, lat3(lqa), lka, lva, lam_a, subln)
        o_b = _lat_b(l, lat3(lqb), lat3(lkb), lat3(lvb), cb_k, cb_v, nb_tab)
        o_c = _lat_c(l, lat3(lqc), lkc, lvc)
        o_d = _lat_d(l, lat3(lqd), lkd, lvd)
        outs_l = [o.reshape(N_LAT, BRANCH_W) for o in (o_a, o_b, o_c, o_d)]
        zg_l = _zg(l, h_l, W["w_zg"])
        xs_new = _merge(l, xs, mod4, outs_l, zg_l, W, latent=True)
        return (xp_new, xs_new, (ka, va, kb, vb, ckv, kr, kd, vd)), None

    bufs0 = tuple(jnp.zeros((BATCH, DEPTH, SEQ, w), F32) for w in CACHE_WIDTHS)
    (xp, xs, caches), _ = lax.scan(
        layer, (x_prompt.reshape(N_CTX, D_MODEL), x_sample.reshape(N_LAT, D_MODEL), bufs0),
        jnp.arange(DEPTH, dtype=jnp.int32))
    ka, va, kb, vb, ckv, kr, kd, vd = caches

    def out(a, tail):
        return a.reshape((BATCH, DEPTH, SEQ) + tail)

    return (xp.reshape(BATCH, SEQ, D_MODEL), xs.reshape(DEC_BATCH, DEC_SEQ, D_MODEL),
            out(ka, (H_A, 2 * D_A)), out(va, (H_A, 2 * D_A)), out(kb, (H_B, D_B)), out(vb, (H_B, D_B)),
            out(ckv, (KV_RANK,)), out(kr, (ROPE_DIM,)), out(kd, (G_D, D_D)), out(vd, (G_D, D_D)))
```

```python
import functools
import math

import jax
import jax.numpy as jnp
import numpy as np
from jax import lax
from jax.experimental import pallas as pl
from jax.experimental.pallas import tpu as pltpu

D_MODEL = 2048
BATCH = 16
SEQ = 256
DEPTH = 4
DEC_BATCH = 2
DEC_SEQ = 4096
PAST_LEN = 256
GRID_W = 64
ROWS = DEC_SEQ // GRID_W
N_BRANCH = 4
BRANCH_W = 512
H_A, D_A = 4, 64
H_B, D_B = 8, 64
NA_ROWS, NA_COLS = 8, 16
H_C, Q_RANK, KV_RANK, NOPE_DIM, ROPE_DIM, V_DIM_C = 4, 512, 256, 128, 64, 128
H_D, G_D, D_D = 8, 2, 64
ROPE_BASE = 10000.0
EPS = 1e-6
ALPHA = (2 * DEPTH) ** 0.25
IN_SIZES = (512, 512, 512, 512, 512, 512, Q_RANK, KV_RANK, ROPE_DIM, 512, 128, 128,
            N_BRANCH * BRANCH_W, N_BRANCH * D_MODEL)

BF = jnp.bfloat16
F32 = jnp.float32
LANES = 128
MXU_N = 256
LOG2E = 1.4426950408889634
VMEM_LIMIT = 56 * 1024 * 1024
NEG = -1e30

N_CTX = BATCH * SEQ
N_LAT = DEC_BATCH * DEC_SEQ
NQ = 4736
NZG = 10240

O_AQ, O_AK, O_AV, O_BQ, O_BK, O_BV = 0, 512, 1024, 1536, 2048, 2560
O_CQ, O_CKV, O_DQ, O_DK, O_DV, O_KR = 3072, 3584, 3840, 4352, 4480, 4608

TM_PROJ = 256
TM_ZG = 2048
TN_ZG = 1024
TM_MERGE = 256
ROW_TILE = 256
TQ_A = 512
TQ_C = 1024
TQ_D = 512
TC_REPACK = 256
NB_ROWS = 4
NB_KROWS = 12


def _cparams(sem):
    return pltpu.CompilerParams(dimension_semantics=sem, vmem_limit_bytes=VMEM_LIMIT)


def _dot(a, b):
    return jnp.dot(a, b, preferred_element_type=F32)


def _dot_nt(a, b):
    return lax.dot_general(a, b, (((1,), (1,)), ((), ())), preferred_element_type=F32)


def _sigmoid(x):
    return 1.0 / (1.0 + jnp.exp(-x))


def _lane_lt64(shape):
    return lax.broadcasted_iota(jnp.int32, shape, len(shape) - 1) < 64


def _adaln_body(c_ref, w_ref, b_ref, o_ref):
    c = c_ref[...]
    s = (c * _sigmoid(c)).astype(BF)
    o_ref[...] = _dot(s, w_ref[...].astype(BF)) + b_ref[...]


def _adaln(cond8, w_ada, b_ada):
    tn = 1536
    return pl.pallas_call(
        _adaln_body,
        grid=(DEPTH, 3 * D_MODEL // tn),
        in_specs=[pl.BlockSpec((8, D_MODEL), lambda l, j: (0, 0)),
                  pl.BlockSpec((None, D_MODEL, tn), lambda l, j: (l, 0, j)),
                  pl.BlockSpec((None, 1, tn), lambda l, j: (l, 0, j))],
        out_specs=pl.BlockSpec((None, 8, tn), lambda l, j: (l, 0, j)),
        out_shape=jax.ShapeDtypeStruct((DEPTH, 8, 3 * D_MODEL), F32),
        compiler_params=_cparams(("arbitrary", "arbitrary")),
        name="adaln",
    )(cond8, w_ada, b_ada.reshape(DEPTH, 1, 3 * D_MODEL))


def _bias_table_body(rpb_ref, o_ref):
    lh = pl.program_id(0)
    qc = lax.broadcasted_iota(jnp.int32, (GRID_W, LANES), 0)
    lane = lax.broadcasted_iota(jnp.int32, (GRID_W, LANES), 1)
    kc = jnp.bitwise_and(lane, 63)
    c0 = jnp.clip(qc - NA_COLS // 2, 0, GRID_W - NA_COLS)
    col_ok = (kc >= c0) & (kc < c0 + NA_COLS)
    dc = jnp.where(col_ok, kc - qc + (NA_COLS - 1), -1)
    right = lane >= 64
    n_dr, n_dc = 2 * NA_ROWS - 1, 2 * NA_COLS - 1
    neg = jnp.full((GRID_W, LANES), NEG, F32)
    rows = []
    for dr in range(n_dr):
        val = neg
        for d in range(n_dc):
            val = jnp.where(dc == d, rpb_ref[(lh * n_dr + dr) * n_dc + d] * LOG2E, val)
        rows.append(val)
    for u in range(n_dr + 1):
        left = rows[u - 1] if u >= 1 else neg
        o_ref[u] = jnp.where(right, rows[u] if u < n_dr else neg, left)


def _bias_table(b_rpb):
    n = DEPTH * H_B
    return pl.pallas_call(
        _bias_table_body,
        grid_spec=pltpu.PrefetchScalarGridSpec(
            num_scalar_prefetch=1, grid=(n,),
            in_specs=[],
            out_specs=pl.BlockSpec((None, 2 * NA_ROWS, GRID_W, LANES), lambda i, r: (i, 0, 0, 0))),
        out_shape=jax.ShapeDtypeStruct((n, 2 * NA_ROWS, GRID_W, LANES), F32),
        compiler_params=_cparams(("arbitrary",)),
        name="nb_bias_table",
    )(b_rpb.reshape(-1))


def _cache_mla_body(ckv_ref, kr_ref, wn_ref, wv_ref, k_ref, v_ref):
    ckv = ckv_ref[...].astype(BF)
    kn = _dot(ckv, wn_ref[...])
    kr = kr_ref[...]
    k_ref[...] = jnp.concatenate(
        [t for h in range(H_C) for t in (kn[:, h * LANES:(h + 1) * LANES], kr)], axis=1).astype(BF)
    v_ref[...] = _dot(ckv, wv_ref[...]).astype(BF)


def _cache_mla(cache_c_kv, cache_c_kr, wukv_n, wukv_v):
    return pl.pallas_call(
        _cache_mla_body,
        grid=(DEPTH, DEC_BATCH),
        in_specs=[pl.BlockSpec((None, None, PAST_LEN, KV_RANK), lambda l, b: (b, l, 0, 0)),
                  pl.BlockSpec((None, None, PAST_LEN, LANES), lambda l, b: (b, l, 0, 0)),
                  pl.BlockSpec((None, KV_RANK, 512), lambda l, b: (l, 0, 0)),
                  pl.BlockSpec((None, KV_RANK, 512), lambda l, b: (l, 0, 0))],
        out_specs=[pl.BlockSpec((None, None, PAST_LEN, 1024), lambda l, b: (l, b, 0, 0)),
                   pl.BlockSpec((None, None, PAST_LEN, 512), lambda l, b: (l, b, 0, 0))],
        out_shape=[jax.ShapeDtypeStruct((DEPTH, DEC_BATCH, PAST_LEN, 1024), BF),
                   jax.ShapeDtypeStruct((DEPTH, DEC_BATCH, PAST_LEN, 512), BF)],
        compiler_params=_cparams(("arbitrary", "arbitrary")),
        name="cache_mla",
    )(cache_c_kv, jnp.pad(cache_c_kr, ((0, 0), (0, 0), (0, 0), (0, LANES - ROPE_DIM))), wukv_n, wukv_v)


def _rope_tiles(x, cos, sa, sb):
    outs = []
    for j in range(x.shape[1] // LANES):
        t = x[:, j * LANES:(j + 1) * LANES]
        outs.append(t * cos + pltpu.roll(t, LANES - 16, 1) * sa + pltpu.roll(t, 16, 1) * sb)
    return outs[0] if len(outs) == 1 else jnp.concatenate(outs, axis=1)


def _group64_rms(x, g):
    w = x.shape[1]
    r = lax.shift_right_logical(lax.broadcasted_iota(jnp.int32, (w, w), 0), 6)
    c = lax.shift_right_logical(lax.broadcasted_iota(jnp.int32, (w, w), 1), 6)
    bd = jnp.where(r == c, 1.0, 0.0).astype(BF)
    x2 = x * x
    hi = x2.astype(BF)
    lo = (x2 - hi.astype(F32)).astype(BF)
    ms = (_dot(hi, bd) + _dot(lo, bd)) * (1.0 / 64)
    return x * lax.rsqrt(ms + EPS) * g


def _interleave_ones(v):
    ones = jnp.ones((v.shape[0], LANES), BF)
    return jnp.concatenate([t for j in range(v.shape[1] // LANES)
                            for t in (v[:, j * LANES:(j + 1) * LANES].astype(BF), ones)], axis=1)


def _full_rms(x, g):
    ms = jnp.mean(x * x, axis=-1, keepdims=True)
    return x * lax.rsqrt(ms + EPS) * g


def _proj_body(l_ref, x_ref, mod_ref, w_ref, wuq_ref, wun_ref, wuv_ref, cqn_ref, ckvn_ref, dqn_ref, dkn_ref,
               *refs, rope):
    if rope:
        cos_ref, sa_ref, sb_ref = refs[:3]
        refs = refs[3:]
        cos, sa, sb = cos_ref[...], sa_ref[...], sb_ref[...]
        rp = lambda t: _rope_tiles(t, cos, sa, sb)
    else:
        rp = lambda t: t
    x = x_ref[...]
    shift = mod_ref[:, 0:D_MODEL]
    scale = mod_ref[:, D_MODEL:2 * D_MODEL]
    h = (x * (1.0 + scale) + shift).astype(BF)

    acc = _dot_nt(h, w_ref[...])

    def col(o, n):
        return acc[:, o:o + n]

    with_ones = _interleave_ones

    qa = rp(col(O_AQ, 512)) * (D_A ** -0.5 * LOG2E)
    ka = rp(col(O_AK, 512))
    va = col(O_AV, 512)
    qb = col(O_BQ, 512) * (D_B ** -0.5 * LOG2E)
    kb = col(O_BK, 512)
    vb = col(O_BV, 512)
    cq = _full_rms(col(O_CQ, Q_RANK), cqn_ref[...]).astype(BF)
    qc_raw = _dot(cq, wuq_ref[...])
    qc_scale = (NOPE_DIM + ROPE_DIM) ** -0.5 * LOG2E
    qc = jnp.concatenate(
        [t for hh in range(H_C) for t in (qc_raw[:, 2 * hh * LANES:(2 * hh + 1) * LANES],
                                          rp(qc_raw[:, (2 * hh + 1) * LANES:(2 * hh + 2) * LANES]))],
        axis=1) * qc_scale
    ckv = _full_rms(col(O_CKV, KV_RANK), ckvn_ref[...])
    ckv_b = ckv.astype(BF)
    kn = _dot(ckv_b, wun_ref[...])
    vc = _dot(ckv_b, wuv_ref[...])
    kr_raw = col(O_KR, LANES)
    kr = rp(kr_raw)
    kc = jnp.concatenate([t for hh in range(H_C) for t in (kn[:, hh * LANES:(hh + 1) * LANES], kr)], axis=1)
    qd = rp(_group64_rms(col(O_DQ, 512), dqn_ref[...])) * (D_D ** -0.5 * LOG2E)
    kd_n = _group64_rms(col(O_DK, LANES), dkn_ref[...])
    kd = rp(kd_n)
    vd = col(O_DV, LANES)

    if rope:
        (h_o, qa_o, ka_o, va_o, qb_o, kb_o, vb_o, qc_o, kc_o, vc_o, qd_o, kd_o, vd_o) = refs
        ka_o[...] = ka.astype(BF)
        va_o[...] = with_ones(va)
        kb_o[...] = kb.astype(BF)
        vb_o[...] = with_ones(vb)
        kd_o[...] = kd.astype(BF)
        vd_o[...] = with_ones(vd)
        vc_o[...] = with_ones(vc)
    else:
        (h_o, qa_o, qb_o, qc_o, kc_o, vc_o, qd_o,
         ka_o, va_o, kb_o, vb_o, ckv_o, kr_o, kd_o, vd_o) = refs[len(CACHE_WIDTHS):]
        ka_o[...] = ka
        va_o[...] = va
        kb_o[...] = kb
        vb_o[...] = vb
        ckv_o[...] = ckv
        kr_o[...] = kr_raw[:, 0:ROPE_DIM]
        kd_o[...] = kd_n
        vd_o[...] = vd
        vc_o[...] = vc.astype(BF)
    h_o[...] = h
    qa_o[...] = qa.astype(BF)
    qb_o[...] = qb.astype(BF)
    qc_o[...] = qc.astype(BF)
    kc_o[...] = kc.astype(BF)
    qd_o[...] = qd.astype(BF)


CACHE_WIDTHS = (512, 512, 512, 512, KV_RANK, ROPE_DIM, LANES, LANES)
KV_LEN = DEC_SEQ + PAST_LEN
LAT_KV_OUTS = (2, 3, 8, 9, 11, 12)


def _proj(l, x, mod4, W, extra, *, rope):
    m = x.shape[0]
    tm = TM_PROJ
    per_b = DEC_SEQ // tm
    cond = (lambda i: 1 + i // per_b) if rope else (lambda i: 0)
    row = lambda w: pl.BlockSpec((tm, w), lambda i, lr: (i, 0))
    wfull = lambda a: pl.BlockSpec((None,) + a.shape[1:], lambda i, lr: (lr[0],) + (0,) * (a.ndim - 1),
                                   pipeline_mode=pl.Buffered(1))
    weights = [W["w_qkv"], W["wuq"], W["wukv_n"], W["wukv_v"], W["cqn"], W["ckvn"], W["dqn"], W["dkn"]]
    in_specs = [row(D_MODEL),
                pl.BlockSpec((None, None, 1, 3 * D_MODEL), lambda i, lr: (lr[0], cond(i), 0, 0))]
    in_specs += [wfull(a) for a in weights]
    args = [x, mod4] + weights
    if rope:
        in_specs += [pl.BlockSpec((tm, LANES), lambda i, lr: (i % per_b, 0))] * 3
        widths = [(D_MODEL, BF), (512, BF), (512, BF), (1024, BF), (512, BF), (512, BF), (1024, BF),
                  (1024, BF), (1024, BF), (1024, BF), (512, BF), (LANES, BF), (2 * LANES, BF)]
        aliases = {}
    else:
        assert tm == SEQ
        in_specs += [pl.BlockSpec(memory_space=pl.ANY)] * len(CACHE_WIDTHS)
        widths = [(D_MODEL, BF), (512, BF), (512, BF), (1024, BF), (1024, BF), (512, BF), (512, BF)]
        aliases = {1 + len(args) + k: len(widths) + k for k in range(len(CACHE_WIDTHS))}
    args += list(extra)
    out_specs = [row(w) for w, _ in widths]
    out_shape = [jax.ShapeDtypeStruct((m, w), d) for w, d in widths]
    if rope:
        for k in LAT_KV_OUTS:
            w, d = widths[k]
            out_specs[k] = pl.BlockSpec((None, tm, w), lambda i, lr: (i // per_b, i % per_b, 0))
            out_shape[k] = jax.ShapeDtypeStruct((DEC_BATCH, KV_LEN, w), d)
    if not rope:
        out_specs += [pl.BlockSpec((None, None, SEQ, w), lambda i, lr: (i, lr[0], 0, 0)) for w in CACHE_WIDTHS]
        out_shape += [jax.ShapeDtypeStruct((BATCH, DEPTH, SEQ, w), F32) for w in CACHE_WIDTHS]
    return pl.pallas_call(
        functools.partial(_proj_body, rope=rope),
        grid_spec=pltpu.PrefetchScalarGridSpec(
            num_scalar_prefetch=1, grid=(m // tm,), in_specs=in_specs, out_specs=out_specs),
        out_shape=out_shape,
        input_output_aliases=aliases,
        compiler_params=_cparams(("arbitrary",)),
        name="proj_lat" if rope else "proj_ctx",
    )(l, *args)


def _cache_fill_body(l_ref, cak_ref, cav_ref, kcc_ref, vcc_ref, cdk_ref, cdv_ref, *refs):
    ka_o, va_o, kc_o, vc_o, kd_o, vd_o = refs[6:]
    ka_o[...] = cak_ref[...].astype(BF)
    va_o[...] = _interleave_ones(cav_ref[...])
    kc_o[...] = kcc_ref[...]
    vc_o[...] = _interleave_ones(vcc_ref[...])
    kd_o[...] = cdk_ref[...].astype(BF)
    vd_o[...] = _interleave_ones(cdv_ref[...])


def _cache_fill(l, caches, bufs):
    ca_k, ca_v, kc_cache, vc_cache, cd_k, cd_v = caches
    by_batch = lambda a: pl.BlockSpec((None, None, PAST_LEN, a.shape[3]), lambda b, lr: (b, lr[0], 0, 0))
    by_layer = lambda a: pl.BlockSpec((None, None, PAST_LEN, a.shape[3]), lambda b, lr: (lr[0], b, 0, 0))
    in_specs = [by_batch(ca_k), by_batch(ca_v), by_layer(kc_cache), by_layer(vc_cache), by_batch(cd_k),
                by_batch(cd_v)] + [pl.BlockSpec(memory_space=pl.ANY)] * len(bufs)
    tail = DEC_SEQ // PAST_LEN
    return pl.pallas_call(
        _cache_fill_body,
        grid_spec=pltpu.PrefetchScalarGridSpec(
            num_scalar_prefetch=1, grid=(DEC_BATCH,), in_specs=in_specs,
            out_specs=[pl.BlockSpec((None, PAST_LEN, a.shape[2]), lambda b, lr: (b, tail, 0)) for a in bufs]),
        out_shape=[jax.ShapeDtypeStruct(a.shape, a.dtype) for a in bufs],
        input_output_aliases={1 + len(caches) + k: k for k in range(len(bufs))},
        compiler_params=_cparams(("arbitrary",)),
        name="cache_fill",
    )(l, *caches, *bufs)


def _zg_body(l_ref, h_ref, w_ref, o_ref):
    is_z = pl.program_id(0) < (N_BRANCH * BRANCH_W) // TN_ZG

    def run(silu):
        h = h_ref[...]
        for n in range(TN_ZG // MXU_N):
            sl = slice(n * MXU_N, (n + 1) * MXU_N)
            a = _dot_nt(h, w_ref[sl, :])
            s = _sigmoid(a)
            o_ref[:, sl] = ((a * s) if silu else s).astype(BF)

    @pl.when(is_z)
    def _():
        run(True)

    @pl.when(jnp.logical_not(is_z))
    def _():
        run(False)


def _zg(l, h, w_zg):
    m = h.shape[0]
    return pl.pallas_call(
        _zg_body,
        grid_spec=pltpu.PrefetchScalarGridSpec(
            num_scalar_prefetch=1, grid=(NZG // TN_ZG, m // TM_ZG),
            in_specs=[pl.BlockSpec((TM_ZG, D_MODEL), lambda j, i, lr: (i, 0)),
                      pl.BlockSpec((None, TN_ZG, D_MODEL), lambda j, i, lr: (lr[0], j, 0))],
            out_specs=pl.BlockSpec((TM_ZG, TN_ZG), lambda j, i, lr: (i, j))),
        out_shape=jax.ShapeDtypeStruct((m, NZG), BF),
        compiler_params=_cparams(("arbitrary", "arbitrary")),
        name="zg_proj",
    )(l, h, w_zg)


def _softmax_pv(scores, values):
    m = None
    for s in scores:
        sm = jnp.max(s, axis=-1, keepdims=True)
        m = sm if m is None else jnp.maximum(m, sm)
    acc = None
    for s, v in zip(scores, values):
        o = _dot(jnp.exp2((s - m).astype(BF)), v)
        acc = o if acc is None else acc + o
    return acc[:, :LANES] / acc[:, LANES:]


def _with_ones(v):
    return jnp.concatenate([v, jnp.ones_like(v)], axis=1)


def _diff_lambda(lam_ref, lam_init):
    la = lam_ref[...]
    s01 = jnp.sum(la[0:1] * la[1:2], axis=-1, keepdims=True)
    s23 = jnp.sum(la[2:3] * la[3:4], axis=-1, keepdims=True)
    return jnp.exp(s01) - jnp.exp(s23) + lam_init


def _diff_head(q, ks, vs, lam, subln, lam_init):
    lt = _lane_lt64(q.shape)
    zero = jnp.zeros_like(q)
    o = []
    for qm in (jnp.where(lt, q, zero), jnp.where(lt, zero, q)):
        o.append(_softmax_pv([_dot_nt(qm, k) for k in ks], vs))
    d = o[0] - lam * o[1]
    ms = jnp.mean(d * d, axis=-1, keepdims=True)
    return d * lax.rsqrt(ms + EPS) * subln * (1.0 - lam_init)


def _pair_heads(q, ks, vs):
    lt = _lane_lt64(q.shape)
    zero = jnp.zeros_like(q)
    o = [_softmax_pv([_dot_nt(qm, k) for k in ks], vs)
         for qm in (jnp.where(lt, q, zero), jnp.where(lt, zero, q))]
    return jnp.where(lt, o[0], o[1])


def _gqa_natural(tiles):
    t0, t1, t2, t3 = tiles
    lt = _lane_lt64(t0.shape)
    swap = lambda t: pltpu.roll(t, 64, 1)
    return jnp.concatenate([jnp.where(lt, t0, swap(t1)), jnp.where(lt, t2, swap(t3)),
                            jnp.where(lt, swap(t0), t1), jnp.where(lt, swap(t2), t3)], axis=1)


def _ctx_attn_body(l_ref, li_ref, qa_ref, ka_ref, va_ref, qb_ref, kb_ref, vb_ref, qc_ref, kc_ref, vc_ref,
                   qd_ref, kd_ref, vd_ref, lam_ref, subln_ref, oa_ref, ob_ref, oc_ref, od_ref):
    lam_init = li_ref[l_ref[0]]
    lam = _diff_lambda(lam_ref, lam_init)
    subln = subln_ref[...]
    for h in range(H_A):
        sl = slice(h * LANES, (h + 1) * LANES)
        oa_ref[:, sl] = _diff_head(qa_ref[:, sl], [ka_ref[:, sl].astype(BF)],
                                   [_with_ones(va_ref[:, sl].astype(BF))], lam, subln, lam_init).astype(BF)
    for j in range(H_B // 2):
        sl = slice(j * LANES, (j + 1) * LANES)
        ob_ref[:, sl] = _pair_heads(qb_ref[:, sl], [kb_ref[:, sl].astype(BF)],
                                    [_with_ones(vb_ref[:, sl].astype(BF))]).astype(BF)
    for h in range(H_C):
        oc_ref[:, h * LANES:(h + 1) * LANES] = _softmax_pv(
            [_dot_nt(qc_ref[:, 2 * h * LANES:(2 * h + 2) * LANES], kc_ref[:, 2 * h * LANES:(2 * h + 2) * LANES])],
            [_with_ones(vc_ref[:, h * LANES:(h + 1) * LANES])]).astype(BF)
    kd = kd_ref[...].astype(BF)
    vd = _with_ones(vd_ref[...].astype(BF))
    od_ref[...] = _gqa_natural(
        [_pair_heads(qd_ref[:, j * LANES:(j + 1) * LANES], [kd], [vd]) for j in range(H_D // 2)]).astype(BF)


def _ctx_attn(l, lam_tab, pc, lam_a, a_subln):
    row = lambda w: pl.BlockSpec((SEQ, w), lambda b, lr, li: (b, 0))
    ins = [pc["qa"], pc["ka"], pc["va"], pc["qb"], pc["kb"], pc["vb"], pc["qc"], pc["kc"], pc["vc"],
           pc["qd"], pc["kd"], pc["vd"]]
    layer_row = lambda w: pl.BlockSpec((None, None, SEQ, w), lambda b, lr, li: (b, lr[0], 0, 0))
    in_specs = [row(a.shape[1]) if a.ndim == 2 else layer_row(a.shape[3]) for a in ins]
    in_specs += [pl.BlockSpec((None, 4, D_A), lambda b, lr, li: (lr[0], 0, 0)),
                 pl.BlockSpec((None, 1, 2 * D_A), lambda b, lr, li: (lr[0], 0, 0))]
    return pl.pallas_call(
        _ctx_attn_body,
        grid_spec=pltpu.PrefetchScalarGridSpec(
            num_scalar_prefetch=2, grid=(BATCH,), in_specs=in_specs,
            out_specs=[row(BRANCH_W)] * N_BRANCH),
        out_shape=[jax.ShapeDtypeStruct((N_CTX, BRANCH_W), BF)] * N_BRANCH,
        compiler_params=_cparams(("arbitrary",)),
        name="ctx_attn",
    )(l, lam_tab, *ins, lam_a, a_subln)


def _lat_a_body(l_ref, li_ref, q_ref, k_ref, v_ref, lam_ref, subln_ref, o_ref):
    lam_init = li_ref[l_ref[0]]
    lam = _diff_lambda(lam_ref, lam_init)
    subln = subln_ref[...]
    for h in range(H_A):
        sl = slice(h * LANES, (h + 1) * LANES)
        ks = [k_ref[:, sl]]
        vs = [v_ref[:, 2 * h * LANES:(2 * h + 2) * LANES]]
        for r in range(TQ_A // ROW_TILE):
            rows = slice(r * ROW_TILE, (r + 1) * ROW_TILE)
            o_ref[rows, sl] = _diff_head(q_ref[rows, sl], ks, vs, lam, subln, lam_init).astype(BF)


def _lat_specs(tq, q_w, k_w, v_w):
    qo = lambda w: pl.BlockSpec((None, tq, w), lambda b, i, *pre: (b, i, 0))
    kv = lambda w: pl.BlockSpec((None, KV_LEN, w), lambda b, i, *pre: (b, 0, 0), pipeline_mode=pl.Buffered(1))
    return qo, [qo(q_w), kv(k_w), kv(v_w)]


def _lat_a(l, lam_tab, q, k, v, lam_a, a_subln):
    qo, in_specs = _lat_specs(TQ_A, 512, 512, 1024)
    in_specs += [pl.BlockSpec((None, 4, D_A), lambda b, i, lr, li: (lr[0], 0, 0)),
                 pl.BlockSpec((None, 1, 2 * D_A), lambda b, i, lr, li: (lr[0], 0, 0))]
    return pl.pallas_call(
        _lat_a_body,
        grid_spec=pltpu.PrefetchScalarGridSpec(
            num_scalar_prefetch=2, grid=(DEC_BATCH, DEC_SEQ // TQ_A), in_specs=in_specs, out_specs=qo(BRANCH_W)),
        out_shape=jax.ShapeDtypeStruct((DEC_BATCH, DEC_SEQ, BRANCH_W), BF),
        compiler_params=_cparams(("arbitrary",) * 2),
        name="lat_attn_a",
    )(l, lam_tab, q, k, v, lam_a, a_subln)


def _lat_c_body(l_ref, q_ref, k_ref, v_ref, o_ref):
    for h in range(H_C):
        sl = slice(h * LANES, (h + 1) * LANES)
        sl2 = slice(2 * h * LANES, (2 * h + 2) * LANES)
        for r in range(TQ_C // ROW_TILE):
            rows = slice(r * ROW_TILE, (r + 1) * ROW_TILE)
            o_ref[rows, sl] = _softmax_pv([_dot_nt(q_ref[rows, sl2], k_ref[:, sl2])], [v_ref[:, sl2]]).astype(BF)


def _lat_c(l, q, k, v):
    qo, in_specs = _lat_specs(TQ_C, 1024, 1024, 1024)
    return pl.pallas_call(
        _lat_c_body,
        grid_spec=pltpu.PrefetchScalarGridSpec(
            num_scalar_prefetch=1, grid=(DEC_BATCH, DEC_SEQ // TQ_C), in_specs=in_specs, out_specs=qo(BRANCH_W)),
        out_shape=jax.ShapeDtypeStruct((DEC_BATCH, DEC_SEQ, BRANCH_W), BF),
        compiler_params=_cparams(("arbitrary",) * 2),
        name="lat_attn_c",
    )(l, q, k, v)


def _lat_d_body(l_ref, q_ref, k_ref, v_ref, o_ref):
    ks = [k_ref[...]]
    vs = [v_ref[...]]
    for r in range(TQ_D // ROW_TILE):
        rows = slice(r * ROW_TILE, (r + 1) * ROW_TILE)
        o_ref[rows, :] = _gqa_natural(
            [_pair_heads(q_ref[rows, j * LANES:(j + 1) * LANES], ks, vs) for j in range(H_D // 2)]).astype(BF)


def _lat_d(l, q, k, v):
    qo, in_specs = _lat_specs(TQ_D, 512, LANES, 2 * LANES)
    return pl.pallas_call(
        _lat_d_body,
        grid_spec=pltpu.PrefetchScalarGridSpec(
            num_scalar_prefetch=1, grid=(DEC_BATCH, DEC_SEQ // TQ_D), in_specs=in_specs, out_specs=qo(BRANCH_W)),
        out_shape=jax.ShapeDtypeStruct((DEC_BATCH, DEC_SEQ, BRANCH_W), BF),
        compiler_params=_cparams(("arbitrary",) * 2),
        name="lat_attn_d",
    )(l, q, k, v)


def _lat_b_body(l_ref, q_ref, k_ref, v_ref, kc_ref, vc_ref, tab_ref, o_ref):
    i = pl.program_id(1)
    qr0 = i * NB_ROWS
    kr0 = jnp.clip(qr0 - NA_ROWS // 2, 0, ROWS - NB_KROWS)
    start = pl.multiple_of(kr0 * GRID_W, GRID_W)
    n_keys = NB_KROWS * GRID_W
    lt = _lane_lt64((GRID_W, LANES))
    for j in range(H_B // 2):
        sl = slice(j * LANES, (j + 1) * LANES)
        kwin = k_ref[pl.ds(start, n_keys), sl]
        vwin = v_ref[pl.ds(start, n_keys), 2 * j * LANES:(2 * j + 2) * LANES]
        kcat = jnp.concatenate([kwin, kc_ref[:, sl].astype(BF)], axis=0)
        vcat = jnp.concatenate([vwin, _with_ones(vc_ref[:, sl].astype(BF))], axis=0)
        q = q_ref[:, sl]
        ltq = _lane_lt64(q.shape)
        zero = jnp.zeros_like(q)
        outs = []
        for half, qm in ((0, jnp.where(ltq, q, zero)), (1, jnp.where(ltq, zero, q))):
            head = 2 * j + half
            rows = []
            for a in range(NB_ROWS):
                qr = qr0 + a
                r0 = jnp.clip(qr - NA_ROWS // 2, 0, ROWS - NA_ROWS)
                tiles = []
                for p in range(NB_KROWS // 2):
                    kr_l = kr0 + 2 * p
                    u = jnp.clip(kr_l - qr + NA_ROWS, 0, 2 * NA_ROWS - 1)
                    pen_l = jnp.where((kr_l >= r0) & (kr_l < r0 + NA_ROWS), 0.0, NEG)
                    pen_r = jnp.where((kr_l + 1 >= r0) & (kr_l + 1 < r0 + NA_ROWS), 0.0, NEG)
                    tiles.append(tab_ref[head, u] + jnp.where(lt, pen_l, pen_r))
                rows.append(jnp.concatenate(tiles, axis=1))
            bias = jnp.concatenate(rows, axis=0)
            s = _dot_nt(qm, kcat)
            s = jnp.concatenate([s[:, :n_keys] + bias, s[:, n_keys:]], axis=1)
            outs.append(_softmax_pv([s], [vcat]))
        o_ref[:, sl] = jnp.where(ltq, outs[0], outs[1]).astype(BF)


def _lat_b(l, q, k, v, cache_k, cache_v, tab):
    nq = NB_ROWS * GRID_W
    kv = pl.BlockSpec((None, DEC_SEQ, 512), lambda b, i, lr: (b, 0, 0))
    vv = pl.BlockSpec((None, DEC_SEQ, 1024), lambda b, i, lr: (b, 0, 0))
    cache = pl.BlockSpec((None, None, PAST_LEN, 512), lambda b, i, lr: (b, lr[0], 0, 0))
    qo = pl.BlockSpec((None, nq, 512), lambda b, i, lr: (b, i, 0))
    return pl.pallas_call(
        _lat_b_body,
        grid_spec=pltpu.PrefetchScalarGridSpec(
            num_scalar_prefetch=1, grid=(DEC_BATCH, DEC_SEQ // nq),
            in_specs=[qo, kv, vv, cache, cache,
                      pl.BlockSpec((None, H_B, 2 * NA_ROWS, GRID_W, LANES), lambda b, i, lr: (lr[0], 0, 0, 0, 0))],
            out_specs=qo),
        out_shape=jax.ShapeDtypeStruct((DEC_BATCH, DEC_SEQ, BRANCH_W), BF),
        compiler_params=_cparams(("arbitrary",) * 2),
        name="lat_attn_b",
    )(l, q, k, v, cache_k, cache_v, tab)


def _merge_body(l_ref, x_ref, mod_ref, oa_ref, ob_ref, oc_ref, od_ref, z_ref, g0_ref, g1_ref, g2_ref, g3_ref,
                wbr_ref, wout_ref, lng_ref, lnb_ref, o_ref):
    merged = None
    for i, (o_r, g_r) in enumerate(zip((oa_ref, ob_ref, oc_ref, od_ref), (g0_ref, g1_ref, g2_ref, g3_ref))):
        u = (o_r[...].astype(F32) * z_ref[:, i * BRANCH_W:(i + 1) * BRANCH_W].astype(F32)).astype(BF)
        term = g_r[...].astype(F32) * _dot(u, wbr_ref[i])
        merged = term if merged is None else merged + term
    y = _dot(merged.astype(BF), wout_ref[...])
    gate = mod_ref[:, 2 * D_MODEL:3 * D_MODEL]
    r = ALPHA * x_ref[...] + gate * y
    mu = jnp.mean(r, axis=-1, keepdims=True)
    d = r - mu
    var = jnp.mean(d * d, axis=-1, keepdims=True)
    o_ref[...] = d * lax.rsqrt(var + EPS) * lng_ref[...] + lnb_ref[...]


def _merge(l, x, mod4, outs, zg, W, *, latent):
    m = x.shape[0]
    tm = TM_MERGE
    per_b = DEC_SEQ // tm
    cond = (lambda i: 1 + i // per_b) if latent else (lambda i: 0)
    row = lambda w: pl.BlockSpec((tm, w), lambda i, lr: (i, 0))
    zgb = lambda j: pl.BlockSpec((tm, D_MODEL), lambda i, lr: (i, j))
    wfull = lambda a: pl.BlockSpec((None,) + a.shape[1:], lambda i, lr: (lr[0],) + (0,) * (a.ndim - 1),
                                   pipeline_mode=pl.Buffered(1))
    in_specs = [row(D_MODEL),
                pl.BlockSpec((None, None, 1, 3 * D_MODEL), lambda i, lr: (lr[0], cond(i), 0, 0)),
                row(BRANCH_W), row(BRANCH_W), row(BRANCH_W), row(BRANCH_W),
                zgb(0), zgb(1), zgb(2), zgb(3), zgb(4),
                wfull(W["w_br"]), wfull(W["w_out"]), wfull(W["ln_g"]), wfull(W["ln_b"])]
    return pl.pallas_call(
        _merge_body,
        grid_spec=pltpu.PrefetchScalarGridSpec(
            num_scalar_prefetch=1, grid=(m // tm,), in_specs=in_specs, out_specs=row(D_MODEL)),
        out_shape=jax.ShapeDtypeStruct((m, D_MODEL), F32),
        input_output_aliases={1: 0},
        compiler_params=_cparams(("arbitrary",)),
        name="merge_lat" if latent else "merge_ctx",
    )(l, x, mod4, *outs, zg, zg, zg, zg, zg, W["w_br"], W["w_out"], W["ln_g"], W["ln_b"])


def _rope_tables():
    t = jnp.arange(DEC_SEQ)
    row = (t // GRID_W).astype(F32)
    col = (t % GRID_W).astype(F32)
    quarter = D_A // 4
    inv_freq = ROPE_BASE ** (-jnp.arange(quarter, dtype=F32) / quarter)
    ar = row[:, None] * inv_freq
    ac = col[:, None] * inv_freq
    ang = jnp.concatenate([ar, ar, ac, ac], axis=-1)
    cos, sin = jnp.cos(ang), jnp.sin(ang)
    even = (jnp.arange(D_A) // quarter) % 2 == 0
    sa = jnp.where(even, -sin, 0.0)
    sb = jnp.where(even, 0.0, sin)
    tile2 = lambda a: jnp.concatenate([a, a], axis=-1)
    return tile2(cos), tile2(sa), tile2(sb)


def _repack_plan():
    offs = [int(v) // 64 for v in np.concatenate([[0], np.cumsum(IN_SIZES)])]
    aq, ak, av, bq, bk, bv, cq, ckv, ckr, dq, dk, dv, z, g = offs[:14]
    gqa = lambda base: [base + h for j in range(H_D // 2) for h in (j, H_D // 2 + j)]
    qkv = list(range(aq, ckr)) + gqa(dq) + [dk, dk + 1, dv, dv + 1, ckr, None]
    zg = list(range(z, offs[14]))

    def runs(chunks):
        out = []
        for d, s in enumerate(chunks):
            if out and s is not None and out[-1][1] is not None and out[-1][1] + out[-1][2] == s:
                out[-1][2] += 1
            else:
                out.append([d, s, 1])
        return out
    assert len(qkv) * 64 == NQ and len(zg) * 64 == NZG
    return runs(qkv), runs(zg)


N_IN = sum(IN_SIZES)


def _repack_body(w_ref, qkv_ref, zg_ref):
    qkv_plan, zg_plan = _repack_plan()
    for plan, o_ref in ((qkv_plan, qkv_ref), (zg_plan, zg_ref)):
        for d, s, n in plan:
            rows = slice(d * 64, (d + n) * 64)
            if s is None:
                o_ref[rows, :] = jnp.zeros((n * 64, TC_REPACK), BF)
            else:
                o_ref[rows, :] = w_ref[s * 64:(s + n) * 64, :].astype(BF)


def _repack_w_in(w_in_t):
    return pl.pallas_call(
        _repack_body,
        grid=(DEPTH, D_MODEL // TC_REPACK),
        in_specs=[pl.BlockSpec((None, N_IN, TC_REPACK), lambda l, i: (l, 0, i))],
        out_specs=[pl.BlockSpec((None, NQ, TC_REPACK), lambda l, i: (l, 0, i)),
                   pl.BlockSpec((None, NZG, TC_REPACK), lambda l, i: (l, 0, i))],
        out_shape=[jax.ShapeDtypeStruct((DEPTH, NQ, D_MODEL), BF),
                   jax.ShapeDtypeStruct((DEPTH, NZG, D_MODEL), BF)],
        compiler_params=_cparams(("arbitrary", "arbitrary")),
        name="repack_w_in",
    )(w_in_t)


def _prep_weights(w_in, c_q_norm, c_kv_norm, w_c_uq, w_c_ukv, d_q_norm, d_k_norm, w_br, w_out, ln_g, ln_b):
    w_qkv, w_zg = _repack_w_in(jnp.swapaxes(w_in, 1, 2))
    uq = w_c_uq.reshape(DEPTH, Q_RANK, H_C, NOPE_DIM + ROPE_DIM)
    wuq = jnp.concatenate([uq, jnp.zeros((DEPTH, Q_RANK, H_C, 2 * LANES - NOPE_DIM - ROPE_DIM), F32)],
                          axis=3).reshape(DEPTH, Q_RANK, H_C * 2 * LANES).astype(BF)
    ukv = w_c_ukv.reshape(DEPTH, KV_RANK, H_C, NOPE_DIM + V_DIM_C)
    wukv_n = ukv[..., :NOPE_DIM].reshape(DEPTH, KV_RANK, H_C * NOPE_DIM).astype(BF)
    wukv_v = ukv[..., NOPE_DIM:].reshape(DEPTH, KV_RANK, H_C * V_DIM_C).astype(BF)
    wbr = w_br.astype(BF)
    return {
        "w_qkv": w_qkv, "w_zg": w_zg, "wuq": wuq, "wukv_n": wukv_n, "wukv_v": wukv_v,
        "cqn": c_q_norm.reshape(DEPTH, 1, Q_RANK), "ckvn": c_kv_norm.reshape(DEPTH, 1, KV_RANK),
        "dqn": jnp.tile(d_q_norm, (1, H_D)).reshape(DEPTH, 1, H_D * D_D),
        "dkn": jnp.tile(d_k_norm, (1, G_D)).reshape(DEPTH, 1, G_D * D_D),
        "w_br": wbr, "w_out": w_out.astype(BF),
        "ln_g": ln_g.reshape(DEPTH, 1, D_MODEL), "ln_b": ln_b.reshape(DEPTH, 1, D_MODEL),
    }


def kernel(x_prompt, x_sample, cache_a_k, cache_a_v, cache_b_k, cache_b_v, cache_c_kv, cache_c_kr, cache_d_k,
           cache_d_v, c, c_ctx, w_ada, b_ada, w_in, lam_a, a_subln, b_rpb, c_q_norm, c_kv_norm, w_c_uq, w_c_ukv,
           d_q_norm, d_k_norm, w_br, w_out, ln_g, ln_b):
    W = _prep_weights(w_in, c_q_norm, c_kv_norm, w_c_uq, w_c_ukv, d_q_norm, d_k_norm, w_br, w_out, ln_g, ln_b)
    cond8 = jnp.concatenate([c_ctx[None], c, jnp.zeros((5, D_MODEL), F32)], axis=0)
    mod4 = _adaln(cond8, w_ada, b_ada).reshape(DEPTH, 8, 1, 3 * D_MODEL)
    nb_tab = _bias_table(b_rpb).reshape(DEPTH, H_B, 2 * NA_ROWS, GRID_W, LANES)
    kc_cache, vc_cache = _cache_mla(cache_c_kv, cache_c_kr, W["wukv_n"], W["wukv_v"])
    rope_tabs = _rope_tables()
    lam_tab = jnp.array([0.8 - 0.6 * math.exp(-0.3 * l) for l in range(DEPTH)], F32)
    subln = a_subln.reshape(DEPTH, 1, 2 * D_A)
    flat = lambda a: a.reshape(DEC_BATCH, DEPTH, PAST_LEN, -1)
    ca_k, ca_v, cb_k, cb_v, cd_k, cd_v = (flat(a) for a in (cache_a_k, cache_a_v, cache_b_k, cache_b_v,
                                                               cache_d_k, cache_d_v))
    lat3 = lambda a: a.reshape(DEC_BATCH, DEC_SEQ, a.shape[-1])

    def layer(carry, li):
        xp, xs, bufs = carry
        l = li.reshape(1)
        (h_c, qa, qb, qc, kc, vc, qd, ka, va, kb, vb, ckv, kr, kd, vd) = _proj(l, xp, mod4, W, bufs, rope=False)
        pc = dict(qa=qa, ka=ka, va=va, qb=qb, kb=kb, vb=vb, qc=qc, kc=kc, vc=vc, qd=qd, kd=kd, vd=vd)
        outs_c = _ctx_attn(l, lam_tab, pc, lam_a, subln)
        zg_c = _zg(l, h_c, W["w_zg"])
        xp_new = _merge(l, xp, mod4, outs_c, zg_c, W, latent=False)
        (h_l, lqa, lka, lva, lqb, lkb, lvb, lqc, lkc, lvc, lqd, lkd, lvd) = _proj(l, xs, mod4, W, rope_tabs,
                                                                                   rope=True)
        lka, lva, lkc, lvc, lkd, lvd = _cache_fill(l, (ca_k, ca_v, kc_cache, vc_cache, cd_k, cd_v),
                                                   (lka, lva, lkc, lvc, lkd, lvd))
        o_a = _lat_a(l, lam_tab, lat3(lqa), lka, lva, lam_a, subln)
        o_b = _lat_b(l, lat3(lqb), lat3(lkb), lat3(lvb), cb_k, cb_v, nb_tab)
        o_c = _lat_c(l, lat3(lqc), lkc, lvc)
        o_d = _lat_d(l, lat3(lqd), lkd, lvd)
        outs_l = [o.reshape(N_LAT, BRANCH_W) for o in (o_a, o_b, o_c, o_d)]
        zg_l = _zg(l, h_l, W["w_zg"])
        xs_new = _merge(l, xs, mod4, outs_l, zg_l, W, latent=True)
        return (xp_new, xs_new, (ka, va, kb, vb, ckv, kr, kd, vd)), None

    bufs0 = tuple(jnp.zeros((BATCH, DEPTH, SEQ, w), F32) for w in CACHE_WIDTHS)
    (xp, xs, caches), _ = lax.scan(
        layer, (x_prompt.reshape(N_CTX, D_MODEL), x_sample.reshape(N_LAT, D_MODEL), bufs0),
        jnp.arange(DEPTH, dtype=jnp.int32))
    ka, va, kb, vb, ckv, kr, kd, vd = caches

    def out(a, tail):
        return a.reshape((BATCH, DEPTH, SEQ) + tail)

    return (xp.reshape(BATCH, SEQ, D_MODEL), xs.reshape(DEC_BATCH, DEC_SEQ, D_MODEL),
            out(ka, (H_A, 2 * D_A)), out(va, (H_A, 2 * D_A)), out(kb, (H_B, D_B)), out(vb, (H_B, D_B)),
            out(ckv, (KV_RANK,)), out(kr, (ROPE_DIM,)), out(kd, (G_D, D_D)), out(vd, (G_D, D_D)))
```

```python
import functools
import math

import jax
import jax.numpy as jnp
import numpy as np
from jax import lax
from jax.experimental import pallas as pl
from jax.experimental.pallas import tpu as pltpu

D_MODEL = 2048
BATCH = 16
SEQ = 256
DEPTH = 4
DEC_BATCH = 2
DEC_SEQ = 4096
PAST_LEN = 256
GRID_W = 64
ROWS = DEC_SEQ // GRID_W
N_BRANCH = 4
BRANCH_W = 512
H_A, D_A = 4, 64
H_B, D_B = 8, 64
NA_ROWS, NA_COLS = 8, 16
H_C, Q_RANK, KV_RANK, NOPE_DIM, ROPE_DIM, V_DIM_C = 4, 512, 256, 128, 64, 128
H_D, G_D, D_D = 8, 2, 64
ROPE_BASE = 10000.0
EPS = 1e-6
ALPHA = (2 * DEPTH) ** 0.25
IN_SIZES = (512, 512, 512, 512, 512, 512, Q_RANK, KV_RANK, ROPE_DIM, 512, 128, 128,
            N_BRANCH * BRANCH_W, N_BRANCH * D_MODEL)

BF = jnp.bfloat16
F32 = jnp.float32
LANES = 128
MXU_N = 256
LOG2E = 1.4426950408889634
VMEM_LIMIT = 56 * 1024 * 1024
NEG = -1e30

N_CTX = BATCH * SEQ
N_LAT = DEC_BATCH * DEC_SEQ
NQ = 4736
NZG = 10240

O_AQ, O_AK, O_AV, O_BQ, O_BK, O_BV = 0, 512, 1024, 1536, 2048, 2560
O_CQ, O_CKV, O_DQ, O_DK, O_DV, O_KR = 3072, 3584, 3840, 4352, 4480, 4608

TM_PROJ = 256
TM_ZG = 1024
TN_ZG = 2048
TM_MERGE = 256
MERGE_CHAIN = 128
ROW_TILE = 256
TQ_A = 512
TQ_C = 1024
TQ_D = 512
TC_REPACK = 256
NB_ROWS = 4
NB_KROWS = 12


def _cparams(sem):
    return pltpu.CompilerParams(dimension_semantics=sem, vmem_limit_bytes=VMEM_LIMIT)


def _dot(a, b):
    return jnp.dot(a, b, preferred_element_type=F32)


def _dot_nt(a, b):
    return lax.dot_general(a, b, (((1,), (1,)), ((), ())), preferred_element_type=F32)


def _sigmoid(x):
    return 1.0 / (1.0 + jnp.exp(-x))


def _lane_lt64(shape):
    return lax.broadcasted_iota(jnp.int32, shape, len(shape) - 1) < 64


def _adaln_body(c_ref, w_ref, b_ref, o_ref):
    c = c_ref[...]
    s = (c * _sigmoid(c)).astype(BF)
    o_ref[...] = _dot(s, w_ref[...].astype(BF)) + b_ref[...]


def _adaln(cond8, w_ada, b_ada):
    tn = 1536
    return pl.pallas_call(
        _adaln_body,
        grid=(DEPTH, 3 * D_MODEL // tn),
        in_specs=[pl.BlockSpec((8, D_MODEL), lambda l, j: (0, 0)),
                  pl.BlockSpec((None, D_MODEL, tn), lambda l, j: (l, 0, j)),
                  pl.BlockSpec((None, 1, tn), lambda l, j: (l, 0, j))],
        out_specs=pl.BlockSpec((None, 8, tn), lambda l, j: (l, 0, j)),
        out_shape=jax.ShapeDtypeStruct((DEPTH, 8, 3 * D_MODEL), F32),
        compiler_params=_cparams(("arbitrary", "arbitrary")),
        name="adaln",
    )(cond8, w_ada, b_ada.reshape(DEPTH, 1, 3 * D_MODEL))


def _bias_table_body(rpb_ref, o_ref):
    lh = pl.program_id(0)
    qc = lax.broadcasted_iota(jnp.int32, (GRID_W, LANES), 0)
    lane = lax.broadcasted_iota(jnp.int32, (GRID_W, LANES), 1)
    kc = jnp.bitwise_and(lane, 63)
    c0 = jnp.clip(qc - NA_COLS // 2, 0, GRID_W - NA_COLS)
    col_ok = (kc >= c0) & (kc < c0 + NA_COLS)
    dc = jnp.where(col_ok, kc - qc + (NA_COLS - 1), -1)
    right = lane >= 64
    n_dr, n_dc = 2 * NA_ROWS - 1, 2 * NA_COLS - 1
    neg = jnp.full((GRID_W, LANES), NEG, F32)
    rows = []
    for dr in range(n_dr):
        val = neg
        for d in range(n_dc):
            val = jnp.where(dc == d, rpb_ref[(lh * n_dr + dr) * n_dc + d] * LOG2E, val)
        rows.append(val)
    for u in range(n_dr + 1):
        left = rows[u - 1] if u >= 1 else neg
        o_ref[u] = jnp.where(right, rows[u] if u < n_dr else neg, left)


def _bias_table(b_rpb):
    n = DEPTH * H_B
    return pl.pallas_call(
        _bias_table_body,
        grid_spec=pltpu.PrefetchScalarGridSpec(
            num_scalar_prefetch=1, grid=(n,),
            in_specs=[],
            out_specs=pl.BlockSpec((None, 2 * NA_ROWS, GRID_W, LANES), lambda i, r: (i, 0, 0, 0))),
        out_shape=jax.ShapeDtypeStruct((n, 2 * NA_ROWS, GRID_W, LANES), F32),
        compiler_params=_cparams(("arbitrary",)),
        name="nb_bias_table",
    )(b_rpb.reshape(-1))


def _cache_mla_body(ckv_ref, kr_ref, wn_ref, wv_ref, k_ref, v_ref):
    ckv = ckv_ref[...].astype(BF)
    kn = _dot(ckv, wn_ref[...])
    kr = kr_ref[...]
    k_ref[...] = jnp.concatenate(
        [t for h in range(H_C) for t in (kn[:, h * LANES:(h + 1) * LANES], kr)], axis=1).astype(BF)
    v_ref[...] = _dot(ckv, wv_ref[...]).astype(BF)


def _cache_mla(cache_c_kv, cache_c_kr, wukv_n, wukv_v):
    return pl.pallas_call(
        _cache_mla_body,
        grid=(DEPTH, DEC_BATCH),
        in_specs=[pl.BlockSpec((None, None, PAST_LEN, KV_RANK), lambda l, b: (b, l, 0, 0)),
                  pl.BlockSpec((None, None, PAST_LEN, LANES), lambda l, b: (b, l, 0, 0)),
                  pl.BlockSpec((None, KV_RANK, 512), lambda l, b: (l, 0, 0)),
                  pl.BlockSpec((None, KV_RANK, 512), lambda l, b: (l, 0, 0))],
        out_specs=[pl.BlockSpec((None, None, PAST_LEN, 1024), lambda l, b: (l, b, 0, 0)),
                   pl.BlockSpec((None, None, PAST_LEN, 512), lambda l, b: (l, b, 0, 0))],
        out_shape=[jax.ShapeDtypeStruct((DEPTH, DEC_BATCH, PAST_LEN, 1024), BF),
                   jax.ShapeDtypeStruct((DEPTH, DEC_BATCH, PAST_LEN, 512), BF)],
        compiler_params=_cparams(("arbitrary", "arbitrary")),
        name="cache_mla",
    )(cache_c_kv, jnp.pad(cache_c_kr, ((0, 0), (0, 0), (0, 0), (0, LANES - ROPE_DIM))), wukv_n, wukv_v)


def _rope_tiles(x, cos, sa, sb):
    outs = []
    for j in range(x.shape[1] // LANES):
        t = x[:, j * LANES:(j + 1) * LANES]
        outs.append(t * cos + pltpu.roll(t, LANES - 16, 1) * sa + pltpu.roll(t, 16, 1) * sb)
    return outs[0] if len(outs) == 1 else jnp.concatenate(outs, axis=1)


def _group64_rms(x, g):
    w = x.shape[1]
    r = lax.shift_right_logical(lax.broadcasted_iota(jnp.int32, (w, w), 0), 6)
    c = lax.shift_right_logical(lax.broadcasted_iota(jnp.int32, (w, w), 1), 6)
    bd = jnp.where(r == c, 1.0, 0.0).astype(BF)
    x2 = x * x
    hi = x2.astype(BF)
    lo = (x2 - hi.astype(F32)).astype(BF)
    ms = (_dot(hi, bd) + _dot(lo, bd)) * (1.0 / 64)
    return x * lax.rsqrt(ms + EPS) * g


def _interleave_ones(v):
    ones = jnp.ones((v.shape[0], LANES), BF)
    return jnp.concatenate([t for j in range(v.shape[1] // LANES)
                            for t in (v[:, j * LANES:(j + 1) * LANES].astype(BF), ones)], axis=1)


def _full_rms(x, g):
    ms = jnp.mean(x * x, axis=-1, keepdims=True)
    return x * lax.rsqrt(ms + EPS) * g


def _proj_body(l_ref, x_ref, mod_ref, w_ref, wuq_ref, wun_ref, wuv_ref, cqn_ref, ckvn_ref, dqn_ref, dkn_ref,
               *refs, rope):
    if rope:
        cos_ref, sa_ref, sb_ref = refs[:3]
        refs = refs[3:]
        cos, sa, sb = cos_ref[...], sa_ref[...], sb_ref[...]
        rp = lambda t: _rope_tiles(t, cos, sa, sb)
    else:
        rp = lambda t: t
    x = x_ref[...]
    shift = mod_ref[:, 0:D_MODEL]
    scale = mod_ref[:, D_MODEL:2 * D_MODEL]
    h = (x * (1.0 + scale) + shift).astype(BF)

    acc = _dot_nt(h, w_ref[...])

    def col(o, n):
        return acc[:, o:o + n]

    with_ones = _interleave_ones

    qa = rp(col(O_AQ, 512)) * (D_A ** -0.5 * LOG2E)
    ka = rp(col(O_AK, 512))
    va = col(O_AV, 512)
    qb = col(O_BQ, 512) * (D_B ** -0.5 * LOG2E)
    kb = col(O_BK, 512)
    vb = col(O_BV, 512)
    cq = _full_rms(col(O_CQ, Q_RANK), cqn_ref[...]).astype(BF)
    qc_raw = _dot(cq, wuq_ref[...])
    qc_scale = (NOPE_DIM + ROPE_DIM) ** -0.5 * LOG2E
    qc = jnp.concatenate(
        [t for hh in range(H_C) for t in (qc_raw[:, 2 * hh * LANES:(2 * hh + 1) * LANES],
                                          rp(qc_raw[:, (2 * hh + 1) * LANES:(2 * hh + 2) * LANES]))],
        axis=1) * qc_scale
    ckv = _full_rms(col(O_CKV, KV_RANK), ckvn_ref[...])
    ckv_b = ckv.astype(BF)
    kn = _dot(ckv_b, wun_ref[...])
    vc = _dot(ckv_b, wuv_ref[...])
    kr_raw = col(O_KR, LANES)
    kr = rp(kr_raw)
    kc = jnp.concatenate([t for hh in range(H_C) for t in (kn[:, hh * LANES:(hh + 1) * LANES], kr)], axis=1)
    qd = rp(_group64_rms(col(O_DQ, 512), dqn_ref[...])) * (D_D ** -0.5 * LOG2E)
    kd_n = _group64_rms(col(O_DK, LANES), dkn_ref[...])
    kd = rp(kd_n)
    vd = col(O_DV, LANES)

    if rope:
        (h_o, qa_o, ka_o, va_o, qb_o, kb_o, vb_o, qc_o, kc_o, vc_o, qd_o, kd_o, vd_o) = refs
        ka_o[...] = ka.astype(BF)
        va_o[...] = with_ones(va)
        kb_o[...] = kb.astype(BF)
        vb_o[...] = with_ones(vb)
        kd_o[...] = kd.astype(BF)
        vd_o[...] = with_ones(vd)
        vc_o[...] = with_ones(vc)
    else:
        (h_o, qa_o, qb_o, qc_o, kc_o, vc_o, qd_o,
         ka_o, va_o, kb_o, vb_o, ckv_o, kr_o, kd_o, vd_o) = refs[len(CACHE_WIDTHS):]
        ka_o[...] = ka
        va_o[...] = va
        kb_o[...] = kb
        vb_o[...] = vb
        ckv_o[...] = ckv
        kr_o[...] = kr_raw[:, 0:ROPE_DIM]
        kd_o[...] = kd_n
        vd_o[...] = vd
        vc_o[...] = vc.astype(BF)
    h_o[...] = h
    qa_o[...] = qa.astype(BF)
    qb_o[...] = qb.astype(BF)
    qc_o[...] = qc.astype(BF)
    kc_o[...] = kc.astype(BF)
    qd_o[...] = qd.astype(BF)


CACHE_WIDTHS = (512, 512, 512, 512, KV_RANK, ROPE_DIM, LANES, LANES)
KV_LEN = DEC_SEQ + PAST_LEN
LAT_KV_OUTS = (2, 3, 8, 9, 11, 12)


def _proj(l, x, mod4, W, extra, *, rope):
    m = x.shape[0]
    tm = TM_PROJ
    per_b = DEC_SEQ // tm
    cond = (lambda i: 1 + i // per_b) if rope else (lambda i: 0)
    row = lambda w: pl.BlockSpec((tm, w), lambda i, lr: (i, 0))
    wfull = lambda a: pl.BlockSpec((None,) + a.shape[1:], lambda i, lr: (lr[0],) + (0,) * (a.ndim - 1),
                                   pipeline_mode=pl.Buffered(1))
    weights = [W["w_qkv"], W["wuq"], W["wukv_n"], W["wukv_v"], W["cqn"], W["ckvn"], W["dqn"], W["dkn"]]
    in_specs = [row(D_MODEL),
                pl.BlockSpec((None, None, 1, 3 * D_MODEL), lambda i, lr: (lr[0], cond(i), 0, 0))]
    in_specs += [wfull(a) for a in weights]
    args = [x, mod4] + weights
    if rope:
        in_specs += [pl.BlockSpec((tm, LANES), lambda i, lr: (i % per_b, 0))] * 3
        widths = [(D_MODEL, BF), (512, BF), (512, BF), (1024, BF), (512, BF), (512, BF), (1024, BF),
                  (1024, BF), (1024, BF), (1024, BF), (512, BF), (LANES, BF), (2 * LANES, BF)]
        aliases = {}
    else:
        assert tm == SEQ
        in_specs += [pl.BlockSpec(memory_space=pl.ANY)] * len(CACHE_WIDTHS)
        widths = [(D_MODEL, BF), (512, BF), (512, BF), (1024, BF), (1024, BF), (512, BF), (512, BF)]
        aliases = {1 + len(args) + k: len(widths) + k for k in range(len(CACHE_WIDTHS))}
    args += list(extra)
    out_specs = [row(w) for w, _ in widths]
    out_shape = [jax.ShapeDtypeStruct((m, w), d) for w, d in widths]
    if rope:
        for k in LAT_KV_OUTS:
            w, d = widths[k]
            out_specs[k] = pl.BlockSpec((None, tm, w), lambda i, lr: (i // per_b, i % per_b, 0))
            out_shape[k] = jax.ShapeDtypeStruct((DEC_BATCH, KV_LEN, w), d)
    if not rope:
        out_specs += [pl.BlockSpec((None, None, SEQ, w), lambda i, lr: (i, lr[0], 0, 0)) for w in CACHE_WIDTHS]
        out_shape += [jax.ShapeDtypeStruct((BATCH, DEPTH, SEQ, w), F32) for w in CACHE_WIDTHS]
    return pl.pallas_call(
        functools.partial(_proj_body, rope=rope),
        grid_spec=pltpu.PrefetchScalarGridSpec(
            num_scalar_prefetch=1, grid=(m // tm,), in_specs=in_specs, out_specs=out_specs),
        out_shape=out_shape,
        input_output_aliases=aliases,
        compiler_params=_cparams(("arbitrary",)),
        name="proj_lat" if rope else "proj_ctx",
    )(l, *args)


def _cache_fill_body(l_ref, cak_ref, cav_ref, kcc_ref, vcc_ref, cdk_ref, cdv_ref, *refs):
    ka_o, va_o, kc_o, vc_o, kd_o, vd_o = refs[6:]
    ka_o[...] = cak_ref[...].astype(BF)
    va_o[...] = _interleave_ones(cav_ref[...])
    kc_o[...] = kcc_ref[...]
    vc_o[...] = _interleave_ones(vcc_ref[...])
    kd_o[...] = cdk_ref[...].astype(BF)
    vd_o[...] = _interleave_ones(cdv_ref[...])


def _cache_fill(l, caches, bufs):
    ca_k, ca_v, kc_cache, vc_cache, cd_k, cd_v = caches
    by_batch = lambda a: pl.BlockSpec((None, None, PAST_LEN, a.shape[3]), lambda b, lr: (b, lr[0], 0, 0))
    by_layer = lambda a: pl.BlockSpec((None, None, PAST_LEN, a.shape[3]), lambda b, lr: (lr[0], b, 0, 0))
    in_specs = [by_batch(ca_k), by_batch(ca_v), by_layer(kc_cache), by_layer(vc_cache), by_batch(cd_k),
                by_batch(cd_v)] + [pl.BlockSpec(memory_space=pl.ANY)] * len(bufs)
    tail = DEC_SEQ // PAST_LEN
    return pl.pallas_call(
        _cache_fill_body,
        grid_spec=pltpu.PrefetchScalarGridSpec(
            num_scalar_prefetch=1, grid=(DEC_BATCH,), in_specs=in_specs,
            out_specs=[pl.BlockSpec((None, PAST_LEN, a.shape[2]), lambda b, lr: (b, tail, 0)) for a in bufs]),
        out_shape=[jax.ShapeDtypeStruct(a.shape, a.dtype) for a in bufs],
        input_output_aliases={1 + len(caches) + k: k for k in range(len(bufs))},
        compiler_params=_cparams(("arbitrary",)),
        name="cache_fill",
    )(l, *caches, *bufs)


def _zg_body(l_ref, h_ref, w_ref, o_ref):
    is_z = pl.program_id(0) < (N_BRANCH * BRANCH_W) // TN_ZG

    def run(silu):
        h = h_ref[...]
        for n in range(TN_ZG // MXU_N):
            sl = slice(n * MXU_N, (n + 1) * MXU_N)
            a = _dot_nt(h, w_ref[sl, :])
            s = _sigmoid(a)
            o_ref[:, sl] = ((a * s) if silu else s).astype(BF)

    @pl.when(is_z)
    def _():
        run(True)

    @pl.when(jnp.logical_not(is_z))
    def _():
        run(False)


def _zg(l, h, w_zg):
    m = h.shape[0]
    return pl.pallas_call(
        _zg_body,
        grid_spec=pltpu.PrefetchScalarGridSpec(
            num_scalar_prefetch=1, grid=(NZG // TN_ZG, m // TM_ZG),
            in_specs=[pl.BlockSpec((TM_ZG, D_MODEL), lambda j, i, lr: (i, 0)),
                      pl.BlockSpec((None, TN_ZG, D_MODEL), lambda j, i, lr: (lr[0], j, 0))],
            out_specs=pl.BlockSpec((TM_ZG, TN_ZG), lambda j, i, lr: (i, j))),
        out_shape=jax.ShapeDtypeStruct((m, NZG), BF),
        compiler_params=_cparams(("arbitrary", "arbitrary")),
        name="zg_proj",
    )(l, h, w_zg)


def _softmax_pv(scores, values):
    m = None
    for s in scores:
        sm = jnp.max(s, axis=-1, keepdims=True)
        m = sm if m is None else jnp.maximum(m, sm)
    acc = None
    for s, v in zip(scores, values):
        o = _dot(jnp.exp2((s - m).astype(BF)), v)
        acc = o if acc is None else acc + o
    return acc[:, :LANES] / acc[:, LANES:]


def _with_ones(v):
    return jnp.concatenate([v, jnp.ones_like(v)], axis=1)


def _diff_lambda(lam_ref, lam_init):
    la = lam_ref[...]
    s01 = jnp.sum(la[0:1] * la[1:2], axis=-1, keepdims=True)
    s23 = jnp.sum(la[2:3] * la[3:4], axis=-1, keepdims=True)
    return jnp.exp(s01) - jnp.exp(s23) + lam_init


def _diff_head(q, ks, vs, lam, subln, lam_init):
    lt = _lane_lt64(q.shape)
    zero = jnp.zeros_like(q)
    o = []
    for qm in (jnp.where(lt, q, zero), jnp.where(lt, zero, q)):
        o.append(_softmax_pv([_dot_nt(qm, k) for k in ks], vs))
    d = o[0] - lam * o[1]
    ms = jnp.mean(d * d, axis=-1, keepdims=True)
    return d * lax.rsqrt(ms + EPS) * subln * (1.0 - lam_init)


def _pair_heads(q, ks, vs):
    lt = _lane_lt64(q.shape)
    zero = jnp.zeros_like(q)
    o = [_softmax_pv([_dot_nt(qm, k) for k in ks], vs)
         for qm in (jnp.where(lt, q, zero), jnp.where(lt, zero, q))]
    return jnp.where(lt, o[0], o[1])


def _gqa_natural(tiles):
    t0, t1, t2, t3 = tiles
    lt = _lane_lt64(t0.shape)
    swap = lambda t: pltpu.roll(t, 64, 1)
    return jnp.concatenate([jnp.where(lt, t0, swap(t1)), jnp.where(lt, t2, swap(t3)),
                            jnp.where(lt, swap(t0), t1), jnp.where(lt, swap(t2), t3)], axis=1)


def _ctx_attn_body(l_ref, li_ref, qa_ref, ka_ref, va_ref, qb_ref, kb_ref, vb_ref, qc_ref, kc_ref, vc_ref,
                   qd_ref, kd_ref, vd_ref, lam_ref, subln_ref, oa_ref, ob_ref, oc_ref, od_ref):
    lam_init = li_ref[l_ref[0]]
    lam = _diff_lambda(lam_ref, lam_init)
    subln = subln_ref[...]
    for h in range(H_A):
        sl = slice(h * LANES, (h + 1) * LANES)
        oa_ref[:, sl] = _diff_head(qa_ref[:, sl], [ka_ref[:, sl].astype(BF)],
                                   [_with_ones(va_ref[:, sl].astype(BF))], lam, subln, lam_init).astype(BF)
    for j in range(H_B // 2):
        sl = slice(j * LANES, (j + 1) * LANES)
        ob_ref[:, sl] = _pair_heads(qb_ref[:, sl], [kb_ref[:, sl].astype(BF)],
                                    [_with_ones(vb_ref[:, sl].astype(BF))]).astype(BF)
    for h in range(H_C):
        oc_ref[:, h * LANES:(h + 1) * LANES] = _softmax_pv(
            [_dot_nt(qc_ref[:, 2 * h * LANES:(2 * h + 2) * LANES], kc_ref[:, 2 * h * LANES:(2 * h + 2) * LANES])],
            [_with_ones(vc_ref[:, h * LANES:(h + 1) * LANES])]).astype(BF)
    kd = kd_ref[...].astype(BF)
    vd = _with_ones(vd_ref[...].astype(BF))
    od_ref[...] = _gqa_natural(
        [_pair_heads(qd_ref[:, j * LANES:(j + 1) * LANES], [kd], [vd]) for j in range(H_D // 2)]).astype(BF)


def _ctx_attn(l, lam_tab, pc, lam_a, a_subln):
    row = lambda w: pl.BlockSpec((SEQ, w), lambda b, lr, li: (b, 0))
    ins = [pc["qa"], pc["ka"], pc["va"], pc["qb"], pc["kb"], pc["vb"], pc["qc"], pc["kc"], pc["vc"],
           pc["qd"], pc["kd"], pc["vd"]]
    layer_row = lambda w: pl.BlockSpec((None, None, SEQ, w), lambda b, lr, li: (b, lr[0], 0, 0))
    in_specs = [row(a.shape[1]) if a.ndim == 2 else layer_row(a.shape[3]) for a in ins]
    in_specs += [pl.BlockSpec((None, 4, D_A), lambda b, lr, li: (lr[0], 0, 0)),
                 pl.BlockSpec((None, 1, 2 * D_A), lambda b, lr, li: (lr[0], 0, 0))]
    return pl.pallas_call(
        _ctx_attn_body,
        grid_spec=pltpu.PrefetchScalarGridSpec(
            num_scalar_prefetch=2, grid=(BATCH,), in_specs=in_specs,
            out_specs=[row(BRANCH_W)] * N_BRANCH),
        out_shape=[jax.ShapeDtypeStruct((N_CTX, BRANCH_W), BF)] * N_BRANCH,
        compiler_params=_cparams(("arbitrary",)),
        name="ctx_attn",
    )(l, lam_tab, *ins, lam_a, a_subln)


def _lat_a_body(l_ref, li_ref, q_ref, k_ref, v_ref, lam_ref, subln_ref, o_ref):
    lam_init = li_ref[l_ref[0]]
    lam = _diff_lambda(lam_ref, lam_init)
    subln = subln_ref[...]
    for h in range(H_A):
        sl = slice(h * LANES, (h + 1) * LANES)
        ks = [k_ref[:, sl]]
        vs = [v_ref[:, 2 * h * LANES:(2 * h + 2) * LANES]]
        for r in range(TQ_A // ROW_TILE):
            rows = slice(r * ROW_TILE, (r + 1) * ROW_TILE)
            o_ref[rows, sl] = _diff_head(q_ref[rows, sl], ks, vs, lam, subln, lam_init).astype(BF)


def _lat_specs(tq, q_w, k_w, v_w):
    qo = lambda w: pl.BlockSpec((None, tq, w), lambda b, i, *pre: (b, i, 0))
    kv = lambda w: pl.BlockSpec((None, KV_LEN, w), lambda b, i, *pre: (b, 0, 0), pipeline_mode=pl.Buffered(1))
    return qo, [qo(q_w), kv(k_w), kv(v_w)]


def _lat_a(l, lam_tab, q, k, v, lam_a, a_subln):
    qo, in_specs = _lat_specs(TQ_A, 512, 512, 1024)
    in_specs += [pl.BlockSpec((None, 4, D_A), lambda b, i, lr, li: (lr[0], 0, 0)),
                 pl.BlockSpec((None, 1, 2 * D_A), lambda b, i, lr, li: (lr[0], 0, 0))]
    return pl.pallas_call(
        _lat_a_body,
        grid_spec=pltpu.PrefetchScalarGridSpec(
            num_scalar_prefetch=2, grid=(DEC_BATCH, DEC_SEQ // TQ_A), in_specs=in_specs, out_specs=qo(BRANCH_W)),
        out_shape=jax.ShapeDtypeStruct((DEC_BATCH, DEC_SEQ, BRANCH_W), BF),
        compiler_params=_cparams(("arbitrary",) * 2),
        name="lat_attn_a",
    )(l, lam_tab, q, k, v, lam_a, a_subln)


def _lat_c_body(l_ref, q_ref, k_ref, v_ref, o_ref):
    for h in range(H_C):
        sl = slice(h * LANES, (h + 1) * LANES)
        sl2 = slice(2 * h * LANES, (2 * h + 2) * LANES)
        for r in range(TQ_C // ROW_TILE):
            rows = slice(r * ROW_TILE, (r + 1) * ROW_TILE)
            o_ref[rows, sl] = _softmax_pv([_dot_nt(q_ref[rows, sl2], k_ref[:, sl2])], [v_ref[:, sl2]]).astype(BF)


def _lat_c(l, q, k, v):
    qo, in_specs = _lat_specs(TQ_C, 1024, 1024, 1024)
    return pl.pallas_call(
        _lat_c_body,
        grid_spec=pltpu.PrefetchScalarGridSpec(
            num_scalar_prefetch=1, grid=(DEC_BATCH, DEC_SEQ // TQ_C), in_specs=in_specs, out_specs=qo(BRANCH_W)),
        out_shape=jax.ShapeDtypeStruct((DEC_BATCH, DEC_SEQ, BRANCH_W), BF),
        compiler_params=_cparams(("arbitrary",) * 2),
        name="lat_attn_c",
    )(l, q, k, v)


def _lat_d_body(l_ref, q_ref, k_ref, v_ref, o_ref):
    ks = [k_ref[...]]
    vs = [v_ref[...]]
    for r in range(TQ_D // ROW_TILE):
        rows = slice(r * ROW_TILE, (r + 1) * ROW_TILE)
        o_ref[rows, :] = _gqa_natural(
            [_pair_heads(q_ref[rows, j * LANES:(j + 1) * LANES], ks, vs) for j in range(H_D // 2)]).astype(BF)


def _lat_d(l, q, k, v):
    qo, in_specs = _lat_specs(TQ_D, 512, LANES, 2 * LANES)
    return pl.pallas_call(
        _lat_d_body,
        grid_spec=pltpu.PrefetchScalarGridSpec(
            num_scalar_prefetch=1, grid=(DEC_BATCH, DEC_SEQ // TQ_D), in_specs=in_specs, out_specs=qo(BRANCH_W)),
        out_shape=jax.ShapeDtypeStruct((DEC_BATCH, DEC_SEQ, BRANCH_W), BF),
        compiler_params=_cparams(("arbitrary",) * 2),
        name="lat_attn_d",
    )(l, q, k, v)


def _lat_b_body(l_ref, q_ref, k_ref, v_ref, kc_ref, vc_ref, tab_ref, o_ref):
    i = pl.program_id(1)
    qr0 = i * NB_ROWS
    kr0 = jnp.clip(qr0 - NA_ROWS // 2, 0, ROWS - NB_KROWS)
    start = pl.multiple_of(kr0 * GRID_W, GRID_W)
    n_keys = NB_KROWS * GRID_W
    lt = _lane_lt64((GRID_W, LANES))
    for j in range(H_B // 2):
        sl = slice(j * LANES, (j + 1) * LANES)
        kwin = k_ref[pl.ds(start, n_keys), sl]
        vwin = v_ref[pl.ds(start, n_keys), 2 * j * LANES:(2 * j + 2) * LANES]
        kcat = jnp.concatenate([kwin, kc_ref[:, sl].astype(BF)], axis=0)
        vcat = jnp.concatenate([vwin, _with_ones(vc_ref[:, sl].astype(BF))], axis=0)
        q = q_ref[:, sl]
        ltq = _lane_lt64(q.shape)
        zero = jnp.zeros_like(q)
        outs = []
        for half, qm in ((0, jnp.where(ltq, q, zero)), (1, jnp.where(ltq, zero, q))):
            head = 2 * j + half
            rows = []
            for a in range(NB_ROWS):
                qr = qr0 + a
                r0 = jnp.clip(qr - NA_ROWS // 2, 0, ROWS - NA_ROWS)
                tiles = []
                for p in range(NB_KROWS // 2):
                    kr_l = kr0 + 2 * p
                    u = jnp.clip(kr_l - qr + NA_ROWS, 0, 2 * NA_ROWS - 1)
                    pen_l = jnp.where((kr_l >= r0) & (kr_l < r0 + NA_ROWS), 0.0, NEG)
                    pen_r = jnp.where((kr_l + 1 >= r0) & (kr_l + 1 < r0 + NA_ROWS), 0.0, NEG)
                    tiles.append(tab_ref[head, u] + jnp.where(lt, pen_l, pen_r))
                rows.append(jnp.concatenate(tiles, axis=1))
            bias = jnp.concatenate(rows, axis=0)
            s = _dot_nt(qm, kcat)
            s = jnp.concatenate([s[:, :n_keys] + bias, s[:, n_keys:]], axis=1)
            outs.append(_softmax_pv([s], [vcat]))
        o_ref[:, sl] = jnp.where(ltq, outs[0], outs[1]).astype(BF)


def _lat_b(l, q, k, v, cache_k, cache_v, tab):
    nq = NB_ROWS * GRID_W
    kv = pl.BlockSpec((None, DEC_SEQ, 512), lambda b, i, lr: (b, 0, 0))
    vv = pl.BlockSpec((None, DEC_SEQ, 1024), lambda b, i, lr: (b, 0, 0))
    cache = pl.BlockSpec((None, None, PAST_LEN, 512), lambda b, i, lr: (b, lr[0], 0, 0))
    qo = pl.BlockSpec((None, nq, 512), lambda b, i, lr: (b, i, 0))
    return pl.pallas_call(
        _lat_b_body,
        grid_spec=pltpu.PrefetchScalarGridSpec(
            num_scalar_prefetch=1, grid=(DEC_BATCH, DEC_SEQ // nq),
            in_specs=[qo, kv, vv, cache, cache,
                      pl.BlockSpec((None, H_B, 2 * NA_ROWS, GRID_W, LANES), lambda b, i, lr: (lr[0], 0, 0, 0, 0))],
            out_specs=qo),
        out_shape=jax.ShapeDtypeStruct((DEC_BATCH, DEC_SEQ, BRANCH_W), BF),
        compiler_params=_cparams(("arbitrary",) * 2),
        name="lat_attn_b",
    )(l, q, k, v, cache_k, cache_v, tab)


def _merge_body(l_ref, x_ref, mod_ref, oa_ref, ob_ref, oc_ref, od_ref, z_ref, g0_ref, g1_ref, g2_ref, g3_ref,
                wbr_ref, wout_ref, lng_ref, lnb_ref, o_ref):
    gate = mod_ref[:, 2 * D_MODEL:3 * D_MODEL]
    for c in range(TM_MERGE // MERGE_CHAIN):
        rows = slice(c * MERGE_CHAIN, (c + 1) * MERGE_CHAIN)
        merged = None
        for i, (o_r, g_r) in enumerate(zip((oa_ref, ob_ref, oc_ref, od_ref), (g0_ref, g1_ref, g2_ref, g3_ref))):
            u = (o_r[rows, :].astype(F32)
                 * z_ref[rows, i * BRANCH_W:(i + 1) * BRANCH_W].astype(F32)).astype(BF)
            term = g_r[rows, :].astype(F32) * _dot(u, wbr_ref[i])
            merged = term if merged is None else merged + term
        y = _dot(merged.astype(BF), wout_ref[...])
        r = ALPHA * x_ref[rows, :] + gate * y
        mu = jnp.mean(r, axis=-1, keepdims=True)
        d = r - mu
        var = jnp.mean(d * d, axis=-1, keepdims=True)
        o_ref[rows, :] = d * lax.rsqrt(var + EPS) * lng_ref[...] + lnb_ref[...]


def _merge(l, x, mod4, outs, zg, W, *, latent):
    m = x.shape[0]
    tm = TM_MERGE
    per_b = DEC_SEQ // tm
    cond = (lambda i: 1 + i // per_b) if latent else (lambda i: 0)
    row = lambda w: pl.BlockSpec((tm, w), lambda i, lr: (i, 0))
    zgb = lambda j: pl.BlockSpec((tm, D_MODEL), lambda i, lr: (i, j))
    wfull = lambda a: pl.BlockSpec((None,) + a.shape[1:], lambda i, lr: (lr[0],) + (0,) * (a.ndim - 1),
                                   pipeline_mode=pl.Buffered(1))
    in_specs = [row(D_MODEL),
                pl.BlockSpec((None, None, 1, 3 * D_MODEL), lambda i, lr: (lr[0], cond(i), 0, 0)),
                row(BRANCH_W), row(BRANCH_W), row(BRANCH_W), row(BRANCH_W),
                zgb(0), zgb(1), zgb(2), zgb(3), zgb(4),
                wfull(W["w_br"]), wfull(W["w_out"]), wfull(W["ln_g"]), wfull(W["ln_b"])]
    return pl.pallas_call(
        _merge_body,
        grid_spec=pltpu.PrefetchScalarGridSpec(
            num_scalar_prefetch=1, grid=(m // tm,), in_specs=in_specs, out_specs=row(D_MODEL)),
        out_shape=jax.ShapeDtypeStruct((m, D_MODEL), F32),
        input_output_aliases={1: 0},
        compiler_params=_cparams(("arbitrary",)),
        name="merge_lat" if latent else "merge_ctx",
    )(l, x, mod4, *outs, zg, zg, zg, zg, zg, W["w_br"], W["w_out"], W["ln_g"], W["ln_b"])


def _rope_tables():
    t = jnp.arange(DEC_SEQ)
    row = (t // GRID_W).astype(F32)
    col = (t % GRID_W).astype(F32)
    quarter = D_A // 4
    inv_freq = ROPE_BASE ** (-jnp.arange(quarter, dtype=F32) / quarter)
    ar = row[:, None] * inv_freq
    ac = col[:, None] * inv_freq
    ang = jnp.concatenate([ar, ar, ac, ac], axis=-1)
    cos, sin = jnp.cos(ang), jnp.sin(ang)
    even = (jnp.arange(D_A) // quarter) % 2 == 0
    sa = jnp.where(even, -sin, 0.0)
    sb = jnp.where(even, 0.0, sin)
    tile2 = lambda a: jnp.concatenate([a, a], axis=-1)
    return tile2(cos), tile2(sa), tile2(sb)


def _repack_plan():
    offs = [int(v) // 64 for v in np.concatenate([[0], np.cumsum(IN_SIZES)])]
    aq, ak, av, bq, bk, bv, cq, ckv, ckr, dq, dk, dv, z, g = offs[:14]
    gqa = lambda base: [base + h for j in range(H_D // 2) for h in (j, H_D // 2 + j)]
    qkv = list(range(aq, ckr)) + gqa(dq) + [dk, dk + 1, dv, dv + 1, ckr, None]
    zg = list(range(z, offs[14]))

    def runs(chunks):
        out = []
        for d, s in enumerate(chunks):
            if out and s is not None and out[-1][1] is not None and out[-1][1] + out[-1][2] == s:
                out[-1][2] += 1
            else:
                out.append([d, s, 1])
        return out
    assert len(qkv) * 64 == NQ and len(zg) * 64 == NZG
    return runs(qkv), runs(zg)


N_IN = sum(IN_SIZES)


def _repack_body(w_ref, qkv_ref, zg_ref):
    qkv_plan, zg_plan = _repack_plan()
    for plan, o_ref in ((qkv_plan, qkv_ref), (zg_plan, zg_ref)):
        for d, s, n in plan:
            rows = slice(d * 64, (d + n) * 64)
            if s is None:
                o_ref[rows, :] = jnp.zeros((n * 64, TC_REPACK), BF)
            else:
                o_ref[rows, :] = w_ref[s * 64:(s + n) * 64, :].astype(BF)


def _repack_w_in(w_in_t):
    return pl.pallas_call(
        _repack_body,
        grid=(DEPTH, D_MODEL // TC_REPACK),
        in_specs=[pl.BlockSpec((None, N_IN, TC_REPACK), lambda l, i: (l, 0, i))],
        out_specs=[pl.BlockSpec((None, NQ, TC_REPACK), lambda l, i: (l, 0, i)),
                   pl.BlockSpec((None, NZG, TC_REPACK), lambda l, i: (l, 0, i))],
        out_shape=[jax.ShapeDtypeStruct((DEPTH, NQ, D_MODEL), BF),
                   jax.ShapeDtypeStruct((DEPTH, NZG, D_MODEL), BF)],
        compiler_params=_cparams(("arbitrary", "arbitrary")),
        name="repack_w_in",
    )(w_in_t)


def _prep_weights(w_in, c_q_norm, c_kv_norm, w_c_uq, w_c_ukv, d_q_norm, d_k_norm, w_br, w_out, ln_g, ln_b):
    w_qkv, w_zg = _repack_w_in(jnp.swapaxes(w_in, 1, 2))
    uq = w_c_uq.reshape(DEPTH, Q_RANK, H_C, NOPE_DIM + ROPE_DIM)
    wuq = jnp.concatenate([uq, jnp.zeros((DEPTH, Q_RANK, H_C, 2 * LANES - NOPE_DIM - ROPE_DIM), F32)],
                          axis=3).reshape(DEPTH, Q_RANK, H_C * 2 * LANES).astype(BF)
    ukv = w_c_ukv.reshape(DEPTH, KV_RANK, H_C, NOPE_DIM + V_DIM_C)
    wukv_n = ukv[..., :NOPE_DIM].reshape(DEPTH, KV_RANK, H_C * NOPE_DIM).astype(BF)
    wukv_v = ukv[..., NOPE_DIM:].reshape(DEPTH, KV_RANK, H_C * V_DIM_C).astype(BF)
    wbr = w_br.astype(BF)
    return {
        "w_qkv": w_qkv, "w_zg": w_zg, "wuq": wuq, "wukv_n": wukv_n, "wukv_v": wukv_v,
        "cqn": c_q_norm.reshape(DEPTH, 1, Q_RANK), "ckvn": c_kv_norm.reshape(DEPTH, 1, KV_RANK),
        "dqn": jnp.tile(d_q_norm, (1, H_D)).reshape(DEPTH, 1, H_D * D_D),
        "dkn": jnp.tile(d_k_norm, (1, G_D)).reshape(DEPTH, 1, G_D * D_D),
        "w_br": wbr, "w_out": w_out.astype(BF),
        "ln_g": ln_g.reshape(DEPTH, 1, D_MODEL), "ln_b": ln_b.reshape(DEPTH, 1, D_MODEL),
    }


def kernel(x_prompt, x_sample, cache_a_k, cache_a_v, cache_b_k, cache_b_v, cache_c_kv, cache_c_kr, cache_d_k,
           cache_d_v, c, c_ctx, w_ada, b_ada, w_in, lam_a, a_subln, b_rpb, c_q_norm, c_kv_norm, w_c_uq, w_c_ukv,
           d_q_norm, d_k_norm, w_br, w_out, ln_g, ln_b):
    W = _prep_weights(w_in, c_q_norm, c_kv_norm, w_c_uq, w_c_ukv, d_q_norm, d_k_norm, w_br, w_out, ln_g, ln_b)
    cond8 = jnp.concatenate([c_ctx[None], c, jnp.zeros((5, D_MODEL), F32)], axis=0)
    mod4 = _adaln(cond8, w_ada, b_ada).reshape(DEPTH, 8, 1, 3 * D_MODEL)
    nb_tab = _bias_table(b_rpb).reshape(DEPTH, H_B, 2 * NA_ROWS, GRID_W, LANES)
    kc_cache, vc_cache = _cache_mla(cache_c_kv, cache_c_kr, W["wukv_n"], W["wukv_v"])
    rope_tabs = _rope_tables()
    lam_tab = jnp.array([0.8 - 0.6 * math.exp(-0.3 * l) for l in range(DEPTH)], F32)
    subln = a_subln.reshape(DEPTH, 1, 2 * D_A)
    flat = lambda a: a.reshape(DEC_BATCH, DEPTH, PAST_LEN, -1)
    ca_k, ca_v, cb_k, cb_v, cd_k, cd_v = (flat(a) for a in (cache_a_k, cache_a_v, cache_b_k, cache_b_v,
                                                               cache_d_k, cache_d_v))
    lat3 = lambda a: a.reshape(DEC_BATCH, DEC_SEQ, a.shape[-1])

    def layer(carry, li):
        xp, xs, bufs = carry
        l = li.reshape(1)
        (h_c, qa, qb, qc, kc, vc, qd, ka, va, kb, vb, ckv, kr, kd, vd) = _proj(l, xp, mod4, W, bufs, rope=False)
        pc = dict(qa=qa, ka=ka, va=va, qb=qb, kb=kb, vb=vb, qc=qc, kc=kc, vc=vc, qd=qd, kd=kd, vd=vd)
        outs_c = _ctx_attn(l, lam_tab, pc, lam_a, subln)
        zg_c = _zg(l, h_c, W["w_zg"])
        xp_new = _merge(l, xp, mod4, outs_c, zg_c, W, latent=False)
        (h_l, lqa, lka, lva, lqb, lkb, lvb, lqc, lkc, lvc, lqd, lkd, lvd) = _proj(l, xs, mod4, W, rope_tabs,
                                                                                   rope=True)
        lka, lva, lkc, lvc, lkd, lvd = _cache_fill(l, (ca_k, ca_v, kc_cache, vc_cache, cd_k, cd_v),
                                                   (lka, lva, lkc, lvc, lkd, lvd))
        o_a = _lat_a(l, lam_tab, lat3(lqa), lka, lva, lam_a, subln)
        o_b = _lat_b(l, lat3(lqb), lat3(lkb), lat3(lvb), cb_k, cb_v, nb_tab)
        o_c = _lat_c(l, lat3(lqc), lkc, lvc)
        o_d = _lat_d(l, lat3(lqd), lkd, lvd)
        outs_l = [o.reshape(N_LAT, BRANCH_W) for o in (o_a, o_b, o_c, o_d)]
        zg_l = _zg(l, h_l, W["w_zg"])
        xs_new = _merge(l, xs, mod4, outs_l, zg_l, W, latent=True)
        return (xp_new, xs_new, (ka, va, kb, vb, ckv, kr, kd, vd)), None

    bufs0 = tuple(jnp.zeros((BATCH, DEPTH, SEQ, w), F32) for w in CACHE_WIDTHS)
    (xp, xs, caches), _ = lax.scan(
        layer, (x_prompt.reshape(N_CTX, D_MODEL), x_sample.reshape(N_LAT, D_MODEL), bufs0),
        jnp.arange(DEPTH, dtype=jnp.int32))
    ka, va, kb, vb, ckv, kr, kd, vd = caches

    def out(a, tail):
        return a.reshape((BATCH, DEPTH, SEQ) + tail)

    return (xp.reshape(BATCH, SEQ, D_MODEL), xs.reshape(DEC_BATCH, DEC_SEQ, D_MODEL),
            out(ka, (H_A, 2 * D_A)), out(va, (H_A, 2 * D_A)), out(kb, (H_B, D_B)), out(vb, (H_B, D_B)),
            out(ckv, (KV_RANK,)), out(kr, (ROPE_DIM,)), out(kd, (G_D, D_D)), out(vd, (G_D, D_D)))
```

```python
import functools
import math

import jax
import jax.numpy as jnp
import numpy as np
from jax import lax
from jax.experimental import pallas as pl
from jax.experimental.pallas import tpu as pltpu

D_MODEL = 2048
BATCH = 16
SEQ = 256
DEPTH = 4
DEC_BATCH = 2
DEC_SEQ = 4096
PAST_LEN = 256
GRID_W = 64
ROWS = DEC_SEQ // GRID_W
N_BRANCH = 4
BRANCH_W = 512
H_A, D_A = 4, 64
H_B, D_B = 8, 64
NA_ROWS, NA_COLS = 8, 16
H_C, Q_RANK, KV_RANK, NOPE_DIM, ROPE_DIM, V_DIM_C = 4, 512, 256, 128, 64, 128
H_D, G_D, D_D = 8, 2, 64
ROPE_BASE = 10000.0
EPS = 1e-6
ALPHA = (2 * DEPTH) ** 0.25
IN_SIZES = (512, 512, 512, 512, 512, 512, Q_RANK, KV_RANK, ROPE_DIM, 512, 128, 128,
            N_BRANCH * BRANCH_W, N_BRANCH * D_MODEL)

BF = jnp.bfloat16
F32 = jnp.float32
LANES = 128
MXU_N = 256
LOG2E = 1.4426950408889634
VMEM_LIMIT = 56 * 1024 * 1024
NEG = -1e30

N_CTX = BATCH * SEQ
N_LAT = DEC_BATCH * DEC_SEQ
NQ = 4736
NZG = 10240

O_AQ, O_AK, O_AV, O_BQ, O_BK, O_BV = 0, 512, 1024, 1536, 2048, 2560
O_CQ, O_CKV, O_DQ, O_DK, O_DV, O_KR = 3072, 3584, 3840, 4352, 4480, 4608

TM_PROJ = 256
TM_ZG = 1024
TN_ZG = 2048
TM_MERGE = 256
ROW_TILE = 256
TQ_A = 512
TQ_C = 1024
TQ_D = 512
TC_REPACK = 256
NB_ROWS = 4
NB_KROWS = 12


def _cparams(sem):
    return pltpu.CompilerParams(dimension_semantics=sem, vmem_limit_bytes=VMEM_LIMIT)


def _dot(a, b):
    return jnp.dot(a, b, preferred_element_type=F32)


def _dot_nt(a, b):
    return lax.dot_general(a, b, (((1,), (1,)), ((), ())), preferred_element_type=F32)


def _sigmoid(x):
    return 1.0 / (1.0 + jnp.exp(-x))


def _lane_lt64(shape):
    return lax.broadcasted_iota(jnp.int32, shape, len(shape) - 1) < 64


def _adaln_body(c_ref, w_ref, b_ref, o_ref):
    c = c_ref[...]
    s = (c * _sigmoid(c)).astype(BF)
    o_ref[...] = _dot(s, w_ref[...].astype(BF)) + b_ref[...]


def _adaln(cond8, w_ada, b_ada):
    tn = 1536
    return pl.pallas_call(
        _adaln_body,
        grid=(DEPTH, 3 * D_MODEL // tn),
        in_specs=[pl.BlockSpec((8, D_MODEL), lambda l, j: (0, 0)),
                  pl.BlockSpec((None, D_MODEL, tn), lambda l, j: (l, 0, j)),
                  pl.BlockSpec((None, 1, tn), lambda l, j: (l, 0, j))],
        out_specs=pl.BlockSpec((None, 8, tn), lambda l, j: (l, 0, j)),
        out_shape=jax.ShapeDtypeStruct((DEPTH, 8, 3 * D_MODEL), F32),
        compiler_params=_cparams(("arbitrary", "arbitrary")),
        name="adaln",
    )(cond8, w_ada, b_ada.reshape(DEPTH, 1, 3 * D_MODEL))


def _bias_table_body(rpb_ref, o_ref):
    lh = pl.program_id(0)
    qc = lax.broadcasted_iota(jnp.int32, (GRID_W, LANES), 0)
    lane = lax.broadcasted_iota(jnp.int32, (GRID_W, LANES), 1)
    kc = jnp.bitwise_and(lane, 63)
    c0 = jnp.clip(qc - NA_COLS // 2, 0, GRID_W - NA_COLS)
    col_ok = (kc >= c0) & (kc < c0 + NA_COLS)
    dc = jnp.where(col_ok, kc - qc + (NA_COLS - 1), -1)
    right = lane >= 64
    n_dr, n_dc = 2 * NA_ROWS - 1, 2 * NA_COLS - 1
    neg = jnp.full((GRID_W, LANES), NEG, F32)
    rows = []
    for dr in range(n_dr):
        val = neg
        for d in range(n_dc):
            val = jnp.where(dc == d, rpb_ref[(lh * n_dr + dr) * n_dc + d] * LOG2E, val)
        rows.append(val)
    for u in range(n_dr + 1):
        left = rows[u - 1] if u >= 1 else neg
        o_ref[u] = jnp.where(right, rows[u] if u < n_dr else neg, left)


def _bias_table(b_rpb):
    n = DEPTH * H_B
    return pl.pallas_call(
        _bias_table_body,
        grid_spec=pltpu.PrefetchScalarGridSpec(
            num_scalar_prefetch=1, grid=(n,),
            in_specs=[],
            out_specs=pl.BlockSpec((None, 2 * NA_ROWS, GRID_W, LANES), lambda i, r: (i, 0, 0, 0))),
        out_shape=jax.ShapeDtypeStruct((n, 2 * NA_ROWS, GRID_W, LANES), F32),
        compiler_params=_cparams(("arbitrary",)),
        name="nb_bias_table",
    )(b_rpb.reshape(-1))


def _cache_mla_body(ckv_ref, kr_ref, wn_ref, wv_ref, k_ref, v_ref):
    ckv = ckv_ref[...].astype(BF)
    kn = _dot(ckv, wn_ref[...])
    kr = kr_ref[...]
    k_ref[...] = jnp.concatenate(
        [t for h in range(H_C) for t in (kn[:, h * LANES:(h + 1) * LANES], kr)], axis=1).astype(BF)
    v_ref[...] = _dot(ckv, wv_ref[...]).astype(BF)


def _cache_mla(cache_c_kv, cache_c_kr, wukv_n, wukv_v):
    return pl.pallas_call(
        _cache_mla_body,
        grid=(DEPTH, DEC_BATCH),
        in_specs=[pl.BlockSpec((None, None, PAST_LEN, KV_RANK), lambda l, b: (b, l, 0, 0)),
                  pl.BlockSpec((None, None, PAST_LEN, LANES), lambda l, b: (b, l, 0, 0)),
                  pl.BlockSpec((None, KV_RANK, 512), lambda l, b: (l, 0, 0)),
                  pl.BlockSpec((None, KV_RANK, 512), lambda l, b: (l, 0, 0))],
        out_specs=[pl.BlockSpec((None, None, PAST_LEN, 1024), lambda l, b: (l, b, 0, 0)),
                   pl.BlockSpec((None, None, PAST_LEN, 512), lambda l, b: (l, b, 0, 0))],
        out_shape=[jax.ShapeDtypeStruct((DEPTH, DEC_BATCH, PAST_LEN, 1024), BF),
                   jax.ShapeDtypeStruct((DEPTH, DEC_BATCH, PAST_LEN, 512), BF)],
        compiler_params=_cparams(("arbitrary", "arbitrary")),
        name="cache_mla",
    )(cache_c_kv, jnp.pad(cache_c_kr, ((0, 0), (0, 0), (0, 0), (0, LANES - ROPE_DIM))), wukv_n, wukv_v)


def _rope_tiles(x, cos, sa, sb):
    outs = []
    for j in range(x.shape[1] // LANES):
        t = x[:, j * LANES:(j + 1) * LANES]
        outs.append(t * cos + pltpu.roll(t, LANES - 16, 1) * sa + pltpu.roll(t, 16, 1) * sb)
    return outs[0] if len(outs) == 1 else jnp.concatenate(outs, axis=1)


def _group64_rms(x, g):
    w = x.shape[1]
    r = lax.shift_right_logical(lax.broadcasted_iota(jnp.int32, (w, w), 0), 6)
    c = lax.shift_right_logical(lax.broadcasted_iota(jnp.int32, (w, w), 1), 6)
    bd = jnp.where(r == c, 1.0, 0.0).astype(BF)
    x2 = x * x
    hi = x2.astype(BF)
    lo = (x2 - hi.astype(F32)).astype(BF)
    ms = (_dot(hi, bd) + _dot(lo, bd)) * (1.0 / 64)
    return x * lax.rsqrt(ms + EPS) * g


def _interleave_ones(v):
    ones = jnp.ones((v.shape[0], LANES), BF)
    return jnp.concatenate([t for j in range(v.shape[1] // LANES)
                            for t in (v[:, j * LANES:(j + 1) * LANES].astype(BF), ones)], axis=1)


def _full_rms(x, g):
    ms = jnp.mean(x * x, axis=-1, keepdims=True)
    return x * lax.rsqrt(ms + EPS) * g


def _proj_body(l_ref, x_ref, mod_ref, w_ref, wuq_ref, wun_ref, wuv_ref, cqn_ref, ckvn_ref, dqn_ref, dkn_ref,
               *refs, rope):
    if rope:
        cos_ref, sa_ref, sb_ref = refs[:3]
        refs = refs[3:]
        cos, sa, sb = cos_ref[...], sa_ref[...], sb_ref[...]
        rp = lambda t: _rope_tiles(t, cos, sa, sb)
    else:
        rp = lambda t: t
    x = x_ref[...]
    shift = mod_ref[:, 0:D_MODEL]
    scale = mod_ref[:, D_MODEL:2 * D_MODEL]
    h = (x * (1.0 + scale) + shift).astype(BF)

    acc = _dot_nt(h, w_ref[...])

    def col(o, n):
        return acc[:, o:o + n]

    with_ones = _interleave_ones

    qa = rp(col(O_AQ, 512)) * (D_A ** -0.5 * LOG2E)
    ka = rp(col(O_AK, 512))
    va = col(O_AV, 512)
    qb = col(O_BQ, 512) * (D_B ** -0.5 * LOG2E)
    kb = col(O_BK, 512)
    vb = col(O_BV, 512)
    cq = _full_rms(col(O_CQ, Q_RANK), cqn_ref[...]).astype(BF)
    qc_raw = _dot(cq, wuq_ref[...])
    qc_scale = (NOPE_DIM + ROPE_DIM) ** -0.5 * LOG2E
    qc = jnp.concatenate(
        [t for hh in range(H_C) for t in (qc_raw[:, 2 * hh * LANES:(2 * hh + 1) * LANES],
                                          rp(qc_raw[:, (2 * hh + 1) * LANES:(2 * hh + 2) * LANES]))],
        axis=1) * qc_scale
    ckv = _full_rms(col(O_CKV, KV_RANK), ckvn_ref[...])
    ckv_b = ckv.astype(BF)
    kn = _dot(ckv_b, wun_ref[...])
    vc = _dot(ckv_b, wuv_ref[...])
    kr_raw = col(O_KR, LANES)
    kr = rp(kr_raw)
    kc = jnp.concatenate([t for hh in range(H_C) for t in (kn[:, hh * LANES:(hh + 1) * LANES], kr)], axis=1)
    qd = rp(_group64_rms(col(O_DQ, 512), dqn_ref[...])) * (D_D ** -0.5 * LOG2E)
    kd_n = _group64_rms(col(O_DK, LANES), dkn_ref[...])
    kd = rp(kd_n)
    vd = col(O_DV, LANES)

    if rope:
        (h_o, qa_o, ka_o, va_o, qb_o, kb_o, vb_o, qc_o, kc_o, vc_o, qd_o, kd_o, vd_o) = refs
        ka_o[...] = ka.astype(BF)
        va_o[...] = with_ones(va)
        kb_o[...] = kb.astype(BF)
        vb_o[...] = with_ones(vb)
        kd_o[...] = kd.astype(BF)
        vd_o[...] = with_ones(vd)
        vc_o[...] = with_ones(vc)
    else:
        (h_o, qa_o, qb_o, qc_o, kc_o, vc_o, qd_o,
         ka_o, va_o, kb_o, vb_o, ckv_o, kr_o, kd_o, vd_o) = refs[len(CACHE_WIDTHS):]
        ka_o[...] = ka
        va_o[...] = va
        kb_o[...] = kb
        vb_o[...] = vb
        ckv_o[...] = ckv
        kr_o[...] = kr_raw[:, 0:ROPE_DIM]
        kd_o[...] = kd_n
        vd_o[...] = vd
        vc_o[...] = vc.astype(BF)
    h_o[...] = h
    qa_o[...] = qa.astype(BF)
    qb_o[...] = qb.astype(BF)
    qc_o[...] = qc.astype(BF)
    kc_o[...] = kc.astype(BF)
    qd_o[...] = qd.astype(BF)


CACHE_WIDTHS = (512, 512, 512, 512, KV_RANK, ROPE_DIM, LANES, LANES)
KV_LEN = DEC_SEQ + PAST_LEN
LAT_KV_OUTS = (2, 3, 8, 9, 11, 12)


def _proj(l, x, mod4, W, extra, *, rope):
    m = x.shape[0]
    tm = TM_PROJ
    per_b = DEC_SEQ // tm
    cond = (lambda i: 1 + i // per_b) if rope else (lambda i: 0)
    row = lambda w: pl.BlockSpec((tm, w), lambda i, lr: (i, 0))
    wfull = lambda a: pl.BlockSpec((None,) + a.shape[1:], lambda i, lr: (lr[0],) + (0,) * (a.ndim - 1),
                                   pipeline_mode=pl.Buffered(1))
    weights = [W["w_qkv"], W["wuq"], W["wukv_n"], W["wukv_v"], W["cqn"], W["ckvn"], W["dqn"], W["dkn"]]
    in_specs = [row(D_MODEL),
                pl.BlockSpec((None, None, 1, 3 * D_MODEL), lambda i, lr: (lr[0], cond(i), 0, 0))]
    in_specs += [wfull(a) for a in weights]
    args = [x, mod4] + weights
    if rope:
        in_specs += [pl.BlockSpec((tm, LANES), lambda i, lr: (i % per_b, 0))] * 3
        widths = [(D_MODEL, BF), (512, BF), (512, BF), (1024, BF), (512, BF), (512, BF), (1024, BF),
                  (1024, BF), (1024, BF), (1024, BF), (512, BF), (LANES, BF), (2 * LANES, BF)]
        aliases = {}
    else:
        assert tm == SEQ
        in_specs += [pl.BlockSpec(memory_space=pl.ANY)] * len(CACHE_WIDTHS)
        widths = [(D_MODEL, BF), (512, BF), (512, BF), (1024, BF), (1024, BF), (512, BF), (512, BF)]
        aliases = {1 + len(args) + k: len(widths) + k for k in range(len(CACHE_WIDTHS))}
    args += list(extra)
    out_specs = [row(w) for w, _ in widths]
    out_shape = [jax.ShapeDtypeStruct((m, w), d) for w, d in widths]
    if rope:
        for k in LAT_KV_OUTS:
            w, d = widths[k]
            out_specs[k] = pl.BlockSpec((None, tm, w), lambda i, lr: (i // per_b, i % per_b, 0))
            out_shape[k] = jax.ShapeDtypeStruct((DEC_BATCH, KV_LEN, w), d)
    if not rope:
        out_specs += [pl.BlockSpec((None, None, SEQ, w), lambda i, lr: (i, lr[0], 0, 0)) for w in CACHE_WIDTHS]
        out_shape += [jax.ShapeDtypeStruct((BATCH, DEPTH, SEQ, w), F32) for w in CACHE_WIDTHS]
    return pl.pallas_call(
        functools.partial(_proj_body, rope=rope),
        grid_spec=pltpu.PrefetchScalarGridSpec(
            num_scalar_prefetch=1, grid=(m // tm,), in_specs=in_specs, out_specs=out_specs),
        out_shape=out_shape,
        input_output_aliases=aliases,
        compiler_params=_cparams(("arbitrary",)),
        name="proj_lat" if rope else "proj_ctx",
    )(l, *args)


def _cache_fill_body(l_ref, cak_ref, cav_ref, kcc_ref, vcc_ref, cdk_ref, cdv_ref, *refs):
    ka_o, va_o, kc_o, vc_o, kd_o, vd_o = refs[6:]
    ka_o[...] = cak_ref[...].astype(BF)
    va_o[...] = _interleave_ones(cav_ref[...])
    kc_o[...] = kcc_ref[...]
    vc_o[...] = _interleave_ones(vcc_ref[...])
    kd_o[...] = cdk_ref[...].astype(BF)
    vd_o[...] = _interleave_ones(cdv_ref[...])


def _cache_fill(l, caches, bufs):
    ca_k, ca_v, kc_cache, vc_cache, cd_k, cd_v = caches
    by_batch = lambda a: pl.BlockSpec((None, None, PAST_LEN, a.shape[3]), lambda b, lr: (b, lr[0], 0, 0))
    by_layer = lambda a: pl.BlockSpec((None, None, PAST_LEN, a.shape[3]), lambda b, lr: (lr[0], b, 0, 0))
    in_specs = [by_batch(ca_k), by_batch(ca_v), by_layer(kc_cache), by_layer(vc_cache), by_batch(cd_k),
                by_batch(cd_v)] + [pl.BlockSpec(memory_space=pl.ANY)] * len(bufs)
    tail = DEC_SEQ // PAST_LEN
    return pl.pallas_call(
        _cache_fill_body,
        grid_spec=pltpu.PrefetchScalarGridSpec(
            num_scalar_prefetch=1, grid=(DEC_BATCH,), in_specs=in_specs,
            out_specs=[pl.BlockSpec((None, PAST_LEN, a.shape[2]), lambda b, lr: (b, tail, 0)) for a in bufs]),
        out_shape=[jax.ShapeDtypeStruct(a.shape, a.dtype) for a in bufs],
        input_output_aliases={1 + len(caches) + k: k for k in range(len(bufs))},
        compiler_params=_cparams(("arbitrary",)),
        name="cache_fill",
    )(l, *caches, *bufs)


def _zg_body(l_ref, h_ref, w_ref, o_ref):
    is_z = pl.program_id(1) < (N_BRANCH * BRANCH_W) // TN_ZG

    def run(silu):
        h = h_ref[...]
        for n in range(TN_ZG // MXU_N):
            sl = slice(n * MXU_N, (n + 1) * MXU_N)
            a = _dot_nt(h, w_ref[sl, :])
            s = _sigmoid(a)
            o_ref[:, sl] = ((a * s) if silu else s).astype(BF)

    @pl.when(is_z)
    def _():
        run(True)

    @pl.when(jnp.logical_not(is_z))
    def _():
        run(False)


def _zg(l, h, w_zg):
    m = h.shape[0]
    return pl.pallas_call(
        _zg_body,
        grid_spec=pltpu.PrefetchScalarGridSpec(
            num_scalar_prefetch=1, grid=(m // TM_ZG, NZG // TN_ZG),
            in_specs=[pl.BlockSpec((TM_ZG, D_MODEL), lambda i, j, lr: (i, 0)),
                      pl.BlockSpec((None, TN_ZG, D_MODEL), lambda i, j, lr: (lr[0], j, 0))],
            out_specs=pl.BlockSpec((TM_ZG, TN_ZG), lambda i, j, lr: (i, j))),
        out_shape=jax.ShapeDtypeStruct((m, NZG), BF),
        compiler_params=_cparams(("arbitrary", "arbitrary")),
        name="zg_proj",
    )(l, h, w_zg)


def _softmax_pv(scores, values):
    m = None
    for s in scores:
        sm = jnp.max(s, axis=-1, keepdims=True)
        m = sm if m is None else jnp.maximum(m, sm)
    acc = None
    for s, v in zip(scores, values):
        o = _dot(jnp.exp2((s - m).astype(BF)), v)
        acc = o if acc is None else acc + o
    return acc[:, :LANES] / acc[:, LANES:]


def _with_ones(v):
    return jnp.concatenate([v, jnp.ones_like(v)], axis=1)


def _diff_lambda(lam_ref, lam_init):
    la = lam_ref[...]
    s01 = jnp.sum(la[0:1] * la[1:2], axis=-1, keepdims=True)
    s23 = jnp.sum(la[2:3] * la[3:4], axis=-1, keepdims=True)
    return jnp.exp(s01) - jnp.exp(s23) + lam_init


def _diff_head(q, ks, vs, lam, subln, lam_init):
    lt = _lane_lt64(q.shape)
    zero = jnp.zeros_like(q)
    o = []
    for qm in (jnp.where(lt, q, zero), jnp.where(lt, zero, q)):
        o.append(_softmax_pv([_dot_nt(qm, k) for k in ks], vs))
    d = o[0] - lam * o[1]
    ms = jnp.mean(d * d, axis=-1, keepdims=True)
    return d * lax.rsqrt(ms + EPS) * subln * (1.0 - lam_init)


def _pair_heads(q, ks, vs):
    lt = _lane_lt64(q.shape)
    zero = jnp.zeros_like(q)
    o = [_softmax_pv([_dot_nt(qm, k) for k in ks], vs)
         for qm in (jnp.where(lt, q, zero), jnp.where(lt, zero, q))]
    return jnp.where(lt, o[0], o[1])


def _gqa_natural(tiles):
    t0, t1, t2, t3 = tiles
    lt = _lane_lt64(t0.shape)
    swap = lambda t: pltpu.roll(t, 64, 1)
    return jnp.concatenate([jnp.where(lt, t0, swap(t1)), jnp.where(lt, t2, swap(t3)),
                            jnp.where(lt, swap(t0), t1), jnp.where(lt, swap(t2), t3)], axis=1)


def _ctx_attn_body(l_ref, li_ref, qa_ref, ka_ref, va_ref, qb_ref, kb_ref, vb_ref, qc_ref, kc_ref, vc_ref,
                   qd_ref, kd_ref, vd_ref, lam_ref, subln_ref, oa_ref, ob_ref, oc_ref, od_ref):
    lam_init = li_ref[l_ref[0]]
    lam = _diff_lambda(lam_ref, lam_init)
    subln = subln_ref[...]
    for h in range(H_A):
        sl = slice(h * LANES, (h + 1) * LANES)
        oa_ref[:, sl] = _diff_head(qa_ref[:, sl], [ka_ref[:, sl].astype(BF)],
                                   [_with_ones(va_ref[:, sl].astype(BF))], lam, subln, lam_init).astype(BF)
    for j in range(H_B // 2):
        sl = slice(j * LANES, (j + 1) * LANES)
        ob_ref[:, sl] = _pair_heads(qb_ref[:, sl], [kb_ref[:, sl].astype(BF)],
                                    [_with_ones(vb_ref[:, sl].astype(BF))]).astype(BF)
    for h in range(H_C):
        oc_ref[:, h * LANES:(h + 1) * LANES] = _softmax_pv(
            [_dot_nt(qc_ref[:, 2 * h * LANES:(2 * h + 2) * LANES], kc_ref[:, 2 * h * LANES:(2 * h + 2) * LANES])],
            [_with_ones(vc_ref[:, h * LANES:(h + 1) * LANES])]).astype(BF)
    kd = kd_ref[...].astype(BF)
    vd = _with_ones(vd_ref[...].astype(BF))
    od_ref[...] = _gqa_natural(
        [_pair_heads(qd_ref[:, j * LANES:(j + 1) * LANES], [kd], [vd]) for j in range(H_D // 2)]).astype(BF)


def _ctx_attn(l, lam_tab, pc, lam_a, a_subln):
    row = lambda w: pl.BlockSpec((SEQ, w), lambda b, lr, li: (b, 0))
    ins = [pc["qa"], pc["ka"], pc["va"], pc["qb"], pc["kb"], pc["vb"], pc["qc"], pc["kc"], pc["vc"],
           pc["qd"], pc["kd"], pc["vd"]]
    layer_row = lambda w: pl.BlockSpec((None, None, SEQ, w), lambda b, lr, li: (b, lr[0], 0, 0))
    in_specs = [row(a.shape[1]) if a.ndim == 2 else layer_row(a.shape[3]) for a in ins]
    in_specs += [pl.BlockSpec((None, 4, D_A), lambda b, lr, li: (lr[0], 0, 0)),
                 pl.BlockSpec((None, 1, 2 * D_A), lambda b, lr, li: (lr[0], 0, 0))]
    return pl.pallas_call(
        _ctx_attn_body,
        grid_spec=pltpu.PrefetchScalarGridSpec(
            num_scalar_prefetch=2, grid=(BATCH,), in_specs=in_specs,
            out_specs=[row(BRANCH_W)] * N_BRANCH),
        out_shape=[jax.ShapeDtypeStruct((N_CTX, BRANCH_W), BF)] * N_BRANCH,
        compiler_params=_cparams(("arbitrary",)),
        name="ctx_attn",
    )(l, lam_tab, *ins, lam_a, a_subln)


def _lat_a_body(l_ref, li_ref, q_ref, k_ref, v_ref, lam_ref, subln_ref, o_ref):
    lam_init = li_ref[l_ref[0]]
    lam = _diff_lambda(lam_ref, lam_init)
    subln = subln_ref[...]
    for h in range(H_A):
        sl = slice(h * LANES, (h + 1) * LANES)
        ks = [k_ref[:, sl]]
        vs = [v_ref[:, 2 * h * LANES:(2 * h + 2) * LANES]]
        for r in range(TQ_A // ROW_TILE):
            rows = slice(r * ROW_TILE, (r + 1) * ROW_TILE)
            o_ref[rows, sl] = _diff_head(q_ref[rows, sl], ks, vs, lam, subln, lam_init).astype(BF)


def _lat_specs(tq, q_w, k_w, v_w):
    qo = lambda w: pl.BlockSpec((None, tq, w), lambda b, i, *pre: (b, i, 0))
    kv = lambda w: pl.BlockSpec((None, KV_LEN, w), lambda b, i, *pre: (b, 0, 0), pipeline_mode=pl.Buffered(1))
    return qo, [qo(q_w), kv(k_w), kv(v_w)]


def _lat_a(l, lam_tab, q, k, v, lam_a, a_subln):
    qo, in_specs = _lat_specs(TQ_A, 512, 512, 1024)
    in_specs += [pl.BlockSpec((None, 4, D_A), lambda b, i, lr, li: (lr[0], 0, 0)),
                 pl.BlockSpec((None, 1, 2 * D_A), lambda b, i, lr, li: (lr[0], 0, 0))]
    return pl.pallas_call(
        _lat_a_body,
        grid_spec=pltpu.PrefetchScalarGridSpec(
            num_scalar_prefetch=2, grid=(DEC_BATCH, DEC_SEQ // TQ_A), in_specs=in_specs, out_specs=qo(BRANCH_W)),
        out_shape=jax.ShapeDtypeStruct((DEC_BATCH, DEC_SEQ, BRANCH_W), BF),
        compiler_params=_cparams(("arbitrary",) * 2),
        name="lat_attn_a",
    )(l, lam_tab, q, k, v, lam_a, a_subln)


def _lat_c_body(l_ref, q_ref, k_ref, v_ref, o_ref):
    for h in range(H_C):
        sl = slice(h * LANES, (h + 1) * LANES)
        sl2 = slice(2 * h * LANES, (2 * h + 2) * LANES)
        for r in range(TQ_C // ROW_TILE):
            rows = slice(r * ROW_TILE, (r + 1) * ROW_TILE)
            o_ref[rows, sl] = _softmax_pv([_dot_nt(q_ref[rows, sl2], k_ref[:, sl2])], [v_ref[:, sl2]]).astype(BF)


def _lat_c(l, q, k, v):
    qo, in_specs = _lat_specs(TQ_C, 1024, 1024, 1024)
    return pl.pallas_call(
        _lat_c_body,
        grid_spec=pltpu.PrefetchScalarGridSpec(
            num_scalar_prefetch=1, grid=(DEC_BATCH, DEC_SEQ // TQ_C), in_specs=in_specs, out_specs=qo(BRANCH_W)),
        out_shape=jax.ShapeDtypeStruct((DEC_BATCH, DEC_SEQ, BRANCH_W), BF),
        compiler_params=_cparams(("arbitrary",) * 2),
        name="lat_attn_c",
    )(l, q, k, v)


def _lat_d_body(l_ref, q_ref, k_ref, v_ref, o_ref):
    ks = [k_ref[...]]
    vs = [v_ref[...]]
    for r in range(TQ_D // ROW_TILE):
        rows = slice(r * ROW_TILE, (r + 1) * ROW_TILE)
        o_ref[rows, :] = _gqa_natural(
            [_pair_heads(q_ref[rows, j * LANES:(j + 1) * LANES], ks, vs) for j in range(H_D // 2)]).astype(BF)


def _lat_d(l, q, k, v):
    qo, in_specs = _lat_specs(TQ_D, 512, LANES, 2 * LANES)
    return pl.pallas_call(
        _lat_d_body,
        grid_spec=pltpu.PrefetchScalarGridSpec(
            num_scalar_prefetch=1, grid=(DEC_BATCH, DEC_SEQ // TQ_D), in_specs=in_specs, out_specs=qo(BRANCH_W)),
        out_shape=jax.ShapeDtypeStruct((DEC_BATCH, DEC_SEQ, BRANCH_W), BF),
        compiler_params=_cparams(("arbitrary",) * 2),
        name="lat_attn_d",
    )(l, q, k, v)


def _lat_b_body(l_ref, q_ref, k_ref, v_ref, kc_ref, vc_ref, tab_ref, o_ref):
    i = pl.program_id(1)
    qr0 = i * NB_ROWS
    kr0 = jnp.clip(qr0 - NA_ROWS // 2, 0, ROWS - NB_KROWS)
    start = pl.multiple_of(kr0 * GRID_W, GRID_W)
    n_keys = NB_KROWS * GRID_W
    lt = _lane_lt64((GRID_W, LANES))
    for j in range(H_B // 2):
        sl = slice(j * LANES, (j + 1) * LANES)
        kwin = k_ref[pl.ds(start, n_keys), sl]
        vwin = v_ref[pl.ds(start, n_keys), 2 * j * LANES:(2 * j + 2) * LANES]
        kcat = jnp.concatenate([kwin, kc_ref[:, sl].astype(BF)], axis=0)
        vcat = jnp.concatenate([vwin, _with_ones(vc_ref[:, sl].astype(BF))], axis=0)
        q = q_ref[:, sl]
        ltq = _lane_lt64(q.shape)
        zero = jnp.zeros_like(q)
        outs = []
        for half, qm in ((0, jnp.where(ltq, q, zero)), (1, jnp.where(ltq, zero, q))):
            head = 2 * j + half
            rows = []
            for a in range(NB_ROWS):
                qr = qr0 + a
                r0 = jnp.clip(qr - NA_ROWS // 2, 0, ROWS - NA_ROWS)
                tiles = []
                for p in range(NB_KROWS // 2):
                    kr_l = kr0 + 2 * p
                    u = jnp.clip(kr_l - qr + NA_ROWS, 0, 2 * NA_ROWS - 1)
                    pen_l = jnp.where((kr_l >= r0) & (kr_l < r0 + NA_ROWS), 0.0, NEG)
                    pen_r = jnp.where((kr_l + 1 >= r0) & (kr_l + 1 < r0 + NA_ROWS), 0.0, NEG)
                    tiles.append(tab_ref[head, u] + jnp.where(lt, pen_l, pen_r))
                rows.append(jnp.concatenate(tiles, axis=1))
            bias = jnp.concatenate(rows, axis=0)
            s = _dot_nt(qm, kcat)
            s = jnp.concatenate([s[:, :n_keys] + bias, s[:, n_keys:]], axis=1)
            outs.append(_softmax_pv([s], [vcat]))
        o_ref[:, sl] = jnp.where(ltq, outs[0], outs[1]).astype(BF)


def _lat_b(l, q, k, v, cache_k, cache_v, tab):
    nq = NB_ROWS * GRID_W
    kv = pl.BlockSpec((None, DEC_SEQ, 512), lambda b, i, lr: (b, 0, 0))
    vv = pl.BlockSpec((None, DEC_SEQ, 1024), lambda b, i, lr: (b, 0, 0))
    cache = pl.BlockSpec((None, None, PAST_LEN, 512), lambda b, i, lr: (b, lr[0], 0, 0))
    qo = pl.BlockSpec((None, nq, 512), lambda b, i, lr: (b, i, 0))
    return pl.pallas_call(
        _lat_b_body,
        grid_spec=pltpu.PrefetchScalarGridSpec(
            num_scalar_prefetch=1, grid=(DEC_BATCH, DEC_SEQ // nq),
            in_specs=[qo, kv, vv, cache, cache,
                      pl.BlockSpec((None, H_B, 2 * NA_ROWS, GRID_W, LANES), lambda b, i, lr: (lr[0], 0, 0, 0, 0))],
            out_specs=qo),
        out_shape=jax.ShapeDtypeStruct((DEC_BATCH, DEC_SEQ, BRANCH_W), BF),
        compiler_params=_cparams(("arbitrary",) * 2),
        name="lat_attn_b",
    )(l, q, k, v, cache_k, cache_v, tab)


def _merge_body(l_ref, x_ref, mod_ref, oa_ref, ob_ref, oc_ref, od_ref, z_ref, g0_ref, g1_ref, g2_ref, g3_ref,
                wbr_ref, wout_ref, lng_ref, lnb_ref, o_ref):
    merged = None
    for i, (o_r, g_r) in enumerate(zip((oa_ref, ob_ref, oc_ref, od_ref), (g0_ref, g1_ref, g2_ref, g3_ref))):
        u = (o_r[...].astype(F32) * z_ref[:, i * BRANCH_W:(i + 1) * BRANCH_W].astype(F32)).astype(BF)
        term = g_r[...].astype(F32) * _dot(u, wbr_ref[i])
        merged = term if merged is None else merged + term
    y = _dot(merged.astype(BF), wout_ref[...])
    gate = mod_ref[:, 2 * D_MODEL:3 * D_MODEL]
    r = ALPHA * x_ref[...] + gate * y
    mu = jnp.mean(r, axis=-1, keepdims=True)
    d = r - mu
    var = jnp.mean(d * d, axis=-1, keepdims=True)
    o_ref[...] = d * lax.rsqrt(var + EPS) * lng_ref[...] + lnb_ref[...]


def _merge(l, x, mod4, outs, zg, W, *, latent):
    m = x.shape[0]
    tm = TM_MERGE
    per_b = DEC_SEQ // tm
    cond = (lambda i: 1 + i // per_b) if latent else (lambda i: 0)
    row = lambda w: pl.BlockSpec((tm, w), lambda i, lr: (i, 0))
    zgb = lambda j: pl.BlockSpec((tm, D_MODEL), lambda i, lr: (i, j))
    wfull = lambda a: pl.BlockSpec((None,) + a.shape[1:], lambda i, lr: (lr[0],) + (0,) * (a.ndim - 1),
                                   pipeline_mode=pl.Buffered(1))
    in_specs = [row(D_MODEL),
                pl.BlockSpec((None, None, 1, 3 * D_MODEL), lambda i, lr: (lr[0], cond(i), 0, 0)),
                row(BRANCH_W), row(BRANCH_W), row(BRANCH_W), row(BRANCH_W),
                zgb(0), zgb(1), zgb(2), zgb(3), zgb(4),
                wfull(W["w_br"]), wfull(W["w_out"]), wfull(W["ln_g"]), wfull(W["ln_b"])]
    return pl.pallas_call(
        _merge_body,
        grid_spec=pltpu.PrefetchScalarGridSpec(
            num_scalar_prefetch=1, grid=(m // tm,), in_specs=in_specs, out_specs=row(D_MODEL)),
        out_shape=jax.ShapeDtypeStruct((m, D_MODEL), F32),
        input_output_aliases={1: 0},
        compiler_params=_cparams(("arbitrary",)),
        name="merge_lat" if latent else "merge_ctx",
    )(l, x, mod4, *outs, zg, zg, zg, zg, zg, W["w_br"], W["w_out"], W["ln_g"], W["ln_b"])


def _rope_tables():
    t = jnp.arange(DEC_SEQ)
    row = (t // GRID_W).astype(F32)
    col = (t % GRID_W).astype(F32)
    quarter = D_A // 4
    inv_freq = ROPE_BASE ** (-jnp.arange(quarter, dtype=F32) / quarter)
    ar = row[:, None] * inv_freq
    ac = col[:, None] * inv_freq
    ang = jnp.concatenate([ar, ar, ac, ac], axis=-1)
    cos, sin = jnp.cos(ang), jnp.sin(ang)
    even = (jnp.arange(D_A) // quarter) % 2 == 0
    sa = jnp.where(even, -sin, 0.0)
    sb = jnp.where(even, 0.0, sin)
    tile2 = lambda a: jnp.concatenate([a, a], axis=-1)
    return tile2(cos), tile2(sa), tile2(sb)


def _repack_plan():
    offs = [int(v) // 64 for v in np.concatenate([[0], np.cumsum(IN_SIZES)])]
    aq, ak, av, bq, bk, bv, cq, ckv, ckr, dq, dk, dv, z, g = offs[:14]
    gqa = lambda base: [base + h for j in range(H_D // 2) for h in (j, H_D // 2 + j)]
    qkv = list(range(aq, ckr)) + gqa(dq) + [dk, dk + 1, dv, dv + 1, ckr, None]
    zg = list(range(z, offs[14]))

    def runs(chunks):
        out = []
        for d, s in enumerate(chunks):
            if out and s is not None and out[-1][1] is not None and out[-1][1] + out[-1][2] == s:
                out[-1][2] += 1
            else:
                out.append([d, s, 1])
        return out
    assert len(qkv) * 64 == NQ and len(zg) * 64 == NZG
    return runs(qkv), runs(zg)


N_IN = sum(IN_SIZES)


def _repack_body(w_ref, qkv_ref, zg_ref):
    qkv_plan, zg_plan = _repack_plan()
    for plan, o_ref in ((qkv_plan, qkv_ref), (zg_plan, zg_ref)):
        for d, s, n in plan:
            rows = slice(d * 64, (d + n) * 64)
            if s is None:
                o_ref[rows, :] = jnp.zeros((n * 64, TC_REPACK), BF)
            else:
                o_ref[rows, :] = w_ref[s * 64:(s + n) * 64, :].astype(BF)


def _repack_w_in(w_in_t):
    return pl.pallas_call(
        _repack_body,
        grid=(DEPTH, D_MODEL // TC_REPACK),
        in_specs=[pl.BlockSpec((None, N_IN, TC_REPACK), lambda l, i: (l, 0, i))],
        out_specs=[pl.BlockSpec((None, NQ, TC_REPACK), lambda l, i: (l, 0, i)),
                   pl.BlockSpec((None, NZG, TC_REPACK), lambda l, i: (l, 0, i))],
        out_shape=[jax.ShapeDtypeStruct((DEPTH, NQ, D_MODEL), BF),
                   jax.ShapeDtypeStruct((DEPTH, NZG, D_MODEL), BF)],
        compiler_params=_cparams(("arbitrary", "arbitrary")),
        name="repack_w_in",
    )(w_in_t)


def _prep_weights(w_in, c_q_norm, c_kv_norm, w_c_uq, w_c_ukv, d_q_norm, d_k_norm, w_br, w_out, ln_g, ln_b):
    w_qkv, w_zg = _repack_w_in(jnp.swapaxes(w_in, 1, 2))
    uq = w_c_uq.reshape(DEPTH, Q_RANK, H_C, NOPE_DIM + ROPE_DIM)
    wuq = jnp.concatenate([uq, jnp.zeros((DEPTH, Q_RANK, H_C, 2 * LANES - NOPE_DIM - ROPE_DIM), F32)],
                          axis=3).reshape(DEPTH, Q_RANK, H_C * 2 * LANES).astype(BF)
    ukv = w_c_ukv.reshape(DEPTH, KV_RANK, H_C, NOPE_DIM + V_DIM_C)
    wukv_n = ukv[..., :NOPE_DIM].reshape(DEPTH, KV_RANK, H_C * NOPE_DIM).astype(BF)
    wukv_v = ukv[..., NOPE_DIM:].reshape(DEPTH, KV_RANK, H_C * V_DIM_C).astype(BF)
    wbr = w_br.astype(BF)
    return {
        "w_qkv": w_qkv, "w_zg": w_zg, "wuq": wuq, "wukv_n": wukv_n, "wukv_v": wukv_v,
        "cqn": c_q_norm.reshape(DEPTH, 1, Q_RANK), "ckvn": c_kv_norm.reshape(DEPTH, 1, KV_RANK),
        "dqn": jnp.tile(d_q_norm, (1, H_D)).reshape(DEPTH, 1, H_D * D_D),
        "dkn": jnp.tile(d_k_norm, (1, G_D)).reshape(DEPTH, 1, G_D * D_D),
        "w_br": wbr, "w_out": w_out.astype(BF),
        "ln_g": ln_g.reshape(DEPTH, 1, D_MODEL), "ln_b": ln_b.reshape(DEPTH, 1, D_MODEL),
    }


def kernel(x_prompt, x_sample, cache_a_k, cache_a_v, cache_b_k, cache_b_v, cache_c_kv, cache_c_kr, cache_d_k,
           cache_d_v, c, c_ctx, w_ada, b_ada, w_in, lam_a, a_subln, b_rpb, c_q_norm, c_kv_norm, w_c_uq, w_c_ukv,
           d_q_norm, d_k_norm, w_br, w_out, ln_g, ln_b):
    W = _prep_weights(w_in, c_q_norm, c_kv_norm, w_c_uq, w_c_ukv, d_q_norm, d_k_norm, w_br, w_out, ln_g, ln_b)
    cond8 = jnp.concatenate([c_ctx[None], c, jnp.zeros((5, D_MODEL), F32)], axis=0)
    mod4 = _adaln(cond8, w_ada, b_ada).reshape(DEPTH, 8, 1, 3 * D_MODEL)
    nb_tab = _bias_table(b_rpb).reshape(DEPTH, H_B, 2 * NA_ROWS, GRID_W, LANES)
    kc_cache, vc_cache = _cache_mla(cache_c_kv, cache_c_kr, W["wukv_n"], W["wukv_v"])
    rope_tabs = _rope_tables()
    lam_tab = jnp.array([0.8 - 0.6 * math.exp(-0.3 * l) for l in range(DEPTH)], F32)
    subln = a_subln.reshape(DEPTH, 1, 2 * D_A)
    flat = lambda a: a.reshape(DEC_BATCH, DEPTH, PAST_LEN, -1)
    ca_k, ca_v, cb_k, cb_v, cd_k, cd_v = (flat(a) for a in (cache_a_k, cache_a_v, cache_b_k, cache_b_v,
                                                               cache_d_k, cache_d_v))
    lat3 = lambda a: a.reshape(DEC_BATCH, DEC_SEQ, a.shape[-1])

    def layer(carry, li):
        xp, xs, bufs = carry
        l = li.reshape(1)
        (h_c, qa, qb, qc, kc, vc, qd, ka, va, kb, vb, ckv, kr, kd, vd) = _proj(l, xp, mod4, W, bufs, rope=False)
        pc = dict(qa=qa, ka=ka, va=va, qb=qb, kb=kb, vb=vb, qc=qc, kc=kc, vc=vc, qd=qd, kd=kd, vd=vd)
        outs_c = _ctx_attn(l, lam_tab, pc, lam_a, subln)
        zg_c = _zg(l, h_c, W["w_zg"])
        xp_new = _merge(l, xp, mod4, outs_c, zg_c, W, latent=False)
        (h_l, lqa, lka, lva, lqb, lkb, lvb, lqc, lkc, lvc, lqd, lkd, lvd) = _proj(l, xs, mod4, W, rope_tabs,
                                                                                   rope=True)
        lka, lva, lkc, lvc, lkd, lvd = _cache_fill(l, (ca_k, ca_v, kc_cache, vc_cache, cd_k, cd_v),
                                                   (lka, lva, lkc, lvc, lkd, lvd))
        o_a = _lat_a(l, lam_tab, lat3(lqa), lka, lva, lam_a, subln)
        o_b = _lat_b(l, lat3(lqb), lat3(lkb), lat3(lvb), cb_k, cb_v, nb_tab)
        o_c = _lat_c(l, lat3(lqc), lkc, lvc)
        o_d = _lat_d(l, lat3(lqd), lkd, lvd)
        outs_l = [o.reshape(N_LAT, BRANCH_W) for o in (o_a, o_b, o_c, o_d)]
        zg_l = _zg(l, h_l, W["w_zg"])
        xs_new = _merge(l, xs, mod4, outs_l, zg_l, W, latent=True)
        return (xp_new, xs_new, (ka, va, kb, vb, ckv, kr, kd, vd)), None

    bufs0 = tuple(jnp.zeros((BATCH, DEPTH, SEQ, w), F32) for w in CACHE_WIDTHS)
    (xp, xs, caches), _ = lax.scan(
        layer, (x_prompt.reshape(N_CTX, D_MODEL), x_sample.reshape(N_LAT, D_MODEL), bufs0),
        jnp.arange(DEPTH, dtype=jnp.int32))
    ka, va, kb, vb, ckv, kr, kd, vd = caches

    def out(a, tail):
        return a.reshape((BATCH, DEPTH, SEQ) + tail)

    return (xp.reshape(BATCH, SEQ, D_MODEL), xs.reshape(DEC_BATCH, DEC_SEQ, D_MODEL),
            out(ka, (H_A, 2 * D_A)), out(va, (H_A, 2 * D_A)), out(kb, (H_B, D_B)), out(vb, (H_B, D_B)),
            out(ckv, (KV_RANK,)), out(kr, (ROPE_DIM,)), out(kd, (G_D, D_D)), out(vd, (G_D, D_D)))
```
